```python
import jax, jax.numpy as jnp
from jax import lax
import numpy as np

D_MODEL = 1024
BATCH = 8
SEQ = 2048
DEPTH = 1
DEC_BATCH = 128
DEC_SEQ = 1
PAST_LEN = 16384
PAGE_SIZE = 128

N_HEADS = 8
HEAD_K = D_MODEL // N_HEADS
HEAD_V = D_MODEL // N_HEADS
D_HGRN = N_HEADS * HEAD_K
D_CONV = D_MODEL
CONV_W = 3
N_META = 16
CHUNK = 32
EPS = 1e-6
SPLIT_WIDTHS = (D_HGRN, D_HGRN, N_HEADS * HEAD_V, N_HEADS * HEAD_V,
                D_CONV, D_CONV, D_CONV, D_CONV, D_MODEL, D_MODEL)
D_IN = sum(SPLIT_WIDTHS)

kernel_name = "hgrn2_shortconv_gated_hybrid_step"


def rmsnorm(x, w):
    xf = x.astype(jnp.float32)
    var = jnp.mean(xf * xf, axis=-1, keepdims=True)
    return (xf * lax.rsqrt(var + EPS) * w.astype(jnp.float32)).astype(x.dtype)


def gla_chunk(S, q, k, v, g):
    C = q.shape[2]
    G = jnp.cumsum(g, axis=2)
    o_inter = jnp.einsum('nhtk,nhkv->nhtv', q * jnp.exp(G), S)
    causal = jnp.tril(jnp.ones((C, C), dtype=bool))[None, None, :, :, None]
    diff = G[:, :, :, None, :] - G[:, :, None, :, :]
    decay = jnp.where(causal, jnp.exp(jnp.where(causal, diff, 0.0)), 0.0)
    scores = jnp.einsum('nhtk,nhsk,nhtsk->nhts', q, k, decay)
    o_intra = jnp.einsum('nhts,nhsv->nhtv', scores, v)
    G_last = G[:, :, -1:, :]
    S_new = jnp.exp(G_last[:, :, 0, :])[..., None] * S + jnp.einsum(
        'nhsk,nhsv->nhkv', k * jnp.exp(G_last - G), v)
    return S_new, o_inter + o_intra


def hgrn_mix(q, k, v, g, S0, lead, chunk):
    N, H, L, _ = q.shape
    S = S0
    outs = []
    if lead > 0:
        S, o = gla_chunk(S, q[:, :, :lead], k[:, :, :lead], v[:, :, :lead], g[:, :, :lead])
        outs.append(o)
    rest = L - lead
    n = rest // chunk

    def to_blocks(a):
        a = a[:, :, lead:]
        return a.reshape(N, H, n, chunk, a.shape[-1]).transpose(2, 0, 1, 3, 4)

    S, oc = lax.scan(lambda s, xs: gla_chunk(s, *xs), S,
                     (to_blocks(q), to_blocks(k), to_blocks(v), to_blocks(g)))
    outs.append(oc.transpose(1, 2, 0, 3, 4).reshape(N, H, rest, HEAD_V))
    return jnp.concatenate(outs, axis=2), S


def mixer_layer(h, conv_ctx, S0, lb, w_in, norm_pre, norm_post, hgrn_norm, conv_w,
                w_a, w_b, w_o, lead, chunk):
    N, L, _ = h.shape
    xn = rmsnorm(h, norm_pre)
    proj = xn @ w_in
    idx = np.cumsum(SPLIT_WIDTHS)[:-1].tolist()
    q, fr, i_in, z_a, b_g, c_g, h_c, z_b, gate_a, gate_b = jnp.split(proj, idx, axis=-1)

    lbf = lb.astype(jnp.float32)
    f = lbf + (1.0 - lbf) * jax.nn.sigmoid(fr.astype(jnp.float32))
    log_f = jnp.log(f)
    k_in = 1.0 - f
    qa = jax.nn.silu(q.astype(jnp.float32))

    def heads(a):
        return a.reshape(N, L, N_HEADS, -1).transpose(0, 2, 1, 3)

    o, S_new = hgrn_mix(heads(qa), heads(k_in), heads(i_in.astype(jnp.float32)),
                        heads(log_f), S0.astype(jnp.float32), lead, chunk)
    o = o.transpose(0, 2, 1, 3)
    o = o * lax.rsqrt(jnp.mean(o * o, axis=-1, keepdims=True) + EPS) * hgrn_norm.astype(jnp.float32)
    o = o.reshape(N, L, N_HEADS * HEAD_V).astype(h.dtype) * jax.nn.silu(z_a)
    y_a = o @ w_a

    u = c_g * h_c
    u_ext = jnp.concatenate([conv_ctx.astype(u.dtype), u], axis=1)
    conv = sum(conv_w[j] * u_ext[:, j:j + L] for j in range(CONV_W))
    y_b = (b_g * conv * jax.nn.silu(z_b)) @ w_b
    new_ctx = u_ext[:, -(CONV_W - 1):]

    merged = jax.nn.sigmoid(gate_a) * y_a + jax.nn.sigmoid(gate_b) * y_b
    out = merged @ w_o
    h = h + rmsnorm(out, norm_post)
    return h, new_ctx, S_new


def setup_inputs(seed: int = 0) -> dict:
    key = jax.random.key(seed)
    ks = jax.random.split(key, 16)
    f32 = jnp.float32
    nrm = lambda k, s, sc: jax.random.normal(k, s, f32) * sc
    return {
        "x_prompt": nrm(ks[0], (BATCH, SEQ, D_MODEL), 1.0),
        "x_sample": nrm(ks[1], (DEC_BATCH, DEC_SEQ, D_MODEL), 1.0),
        "state_hgrn": nrm(ks[2], (DEPTH, DEC_BATCH, N_HEADS, HEAD_K, HEAD_V), 0.5),
        "state_conv": nrm(ks[3], (DEPTH, DEC_BATCH, CONV_W - 1, D_CONV), 0.5),
        "meta_tokens": nrm(ks[4], (N_META, D_MODEL), 1.0),
        "w_in": nrm(ks[5], (DEPTH, D_MODEL, D_IN), D_MODEL ** -0.5),
        "norm_pre": 1.0 + nrm(ks[6], (DEPTH, D_MODEL), 0.02),
        "norm_post": 1.0 + nrm(ks[7], (DEPTH, D_MODEL), 0.02),
        "lb_logits": nrm(ks[8], (DEPTH + 1, D_HGRN), 0.5),
        "hgrn_norm": 1.0 + nrm(ks[9], (DEPTH, HEAD_V), 0.02),
        "conv_w": nrm(ks[10], (DEPTH, CONV_W, D_CONV), CONV_W ** -0.5),
        "w_a": nrm(ks[11], (DEPTH, N_HEADS * HEAD_V, D_MODEL), (N_HEADS * HEAD_V) ** -0.5),
        "w_b": nrm(ks[12], (DEPTH, D_CONV, D_MODEL), D_CONV ** -0.5),
        "w_o": nrm(ks[13], (DEPTH, D_MODEL, D_MODEL), D_MODEL ** -0.5),
    }


def reference(x_prompt, x_sample, state_hgrn, state_conv, meta_tokens, w_in, norm_pre,
              norm_post, lb_logits, hgrn_norm, conv_w, w_a, w_b, w_o):
    lower_bounds = jnp.cumsum(jax.nn.softmax(lb_logits.astype(jnp.float32), axis=0), axis=0)

    hp = jnp.concatenate([jnp.broadcast_to(meta_tokens.astype(x_prompt.dtype)[None],
                                           (BATCH, N_META, D_MODEL)), x_prompt], axis=1)
    conv0 = jnp.zeros((BATCH, CONV_W - 1, D_CONV), x_prompt.dtype)
    S0 = jnp.zeros((BATCH, N_HEADS, HEAD_K, HEAD_V), jnp.float32)
    hs = x_sample
    hgrn_p, hgrn_s, conv_p, conv_s = [], [], [], []
    for l in range(DEPTH):
        hp, cp, sp = mixer_layer(hp, conv0, S0, lower_bounds[l], w_in[l], norm_pre[l], norm_post[l],
                                 hgrn_norm[l], conv_w[l], w_a[l], w_b[l], w_o[l], N_META, CHUNK)
        hs, cs, ss = mixer_layer(hs, state_conv[l], state_hgrn[l], lower_bounds[l], w_in[l],
                                 norm_pre[l], norm_post[l], hgrn_norm[l], conv_w[l], w_a[l],
                                 w_b[l], w_o[l], 0, DEC_SEQ)
        hgrn_p.append(sp); hgrn_s.append(ss); conv_p.append(cp); conv_s.append(cs)
    y_prompt = hp[:, N_META:]
    y_sample = hs
    return (y_prompt, y_sample, jnp.stack(hgrn_p), jnp.stack(hgrn_s),
            jnp.stack(conv_p), jnp.stack(conv_s))
```

```python
import functools

import jax
import jax.numpy as jnp
from jax import lax
from jax.experimental import pallas as pl
from jax.experimental.pallas import tpu as pltpu

D_MODEL = 1024
N_HEADS = 8
HEAD_DIM = D_MODEL // N_HEADS
CONV_W = 3
EPS = 1e-6
N_PROJ = 10

PROMPT_TILE = 256
PROMPT_CHUNK = 64
DECODE_GROUP = 8
CTX_ROW0 = 8 - (CONV_W - 1)
MAX_CHUNK_LOG_DECAY = 80.0
V7X_VMEM_LIMIT_BYTES = 58 * 1024 * 1024

BF16 = jnp.bfloat16
F32 = jnp.float32


def _dot(a, b):
    return jnp.dot(a, b, preferred_element_type=F32)


def _dot_nt(a, b):
    return lax.dot_general(a, b, (((1,), (1,)), ((), ())), preferred_element_type=F32)


def _dot_tn(a, b):
    return lax.dot_general(a, b, (((0,), (0,)), ((), ())), preferred_element_type=F32)


def _sigmoid(x):
    return 1.0 / (1.0 + jnp.exp(-x))


def _silu(x):
    return x * _sigmoid(x)


def _rms_scale(x):
    return lax.rsqrt(jnp.mean(x * x, axis=-1, keepdims=True) + EPS)


def _lower_bound(lb_logits):
    m = jnp.max(lb_logits, axis=0, keepdims=True)
    e = jnp.exp(lb_logits - m)
    return e[0:1, :] / jnp.sum(e, axis=0, keepdims=True)


def _split3_bf16(x):
    hi = x.astype(BF16)
    r = x - hi.astype(F32)
    mid = r.astype(BF16)
    lo = (r - mid.astype(F32)).astype(BF16)
    return hi, mid, lo


def _seq_kernel(x_ref, st0_ref, c0_ref, w_in_ref, npre_ref, npost_ref, lb_ref, hn_ref, cw_ref,
                wa_ref, wb_ref, wo_ref,
                y_ref, st_out_ref, c_out_ref,
                st_scr, u_scr, q_scr, k_scr, v_scr, g_scr, oi_scr, o_scr,
                *, tile, chunk, transpose_state_out):
    T, C = tile, chunk
    t = pl.program_id(1)

    @pl.when(t == 0)
    def _init():
        st_scr[...] = st0_ref[0]
        u_scr[CTX_ROW0:8, :] = c0_ref[0]

    x = x_ref[0]
    xn = (x * _rms_scale(x) * npre_ref[...]).astype(BF16)

    def proj(j):
        return _dot(xn, w_in_ref[:, j * D_MODEL:(j + 1) * D_MODEL])

    lb = _lower_bound(lb_ref[...])
    f = lb + (1.0 - lb) * _sigmoid(proj(1))
    k_scr[...] = 1.0 - f
    q_scr[...] = _silu(proj(0))
    v_scr[...] = proj(2)
    ri = lax.broadcasted_iota(jnp.int32, (T, T), 0)
    ci = lax.broadcasted_iota(jnp.int32, (T, T), 1)
    tri = ri >= ci
    if T != C:
        tri = jnp.logical_and(tri, (ri // C) == (ci // C))
    tri = jnp.where(tri, 1.0, 0.0).astype(BF16)
    g_hi, g_mid, g_lo = _split3_bf16(jnp.log(f))
    g_scr[...] = _dot(tri, g_hi) + _dot(tri, g_mid) + _dot(tri, g_lo)

    hn = hn_ref[...]
    tpos = lax.broadcasted_iota(jnp.int32, (C, 1), 0)
    causal = (lax.broadcasted_iota(jnp.int32, (C, C), 0) >= lax.broadcasted_iota(jnp.int32, (C, C), 1))

    def chunk_step(c, carry):
        r0 = pl.multiple_of(c * C, C)
        rows = pl.ds(r0, C)
        gc = g_scr[rows, :]
        g_last = gc[C - 1:C, :]
        qc = q_scr[rows, :]
        kc = k_scr[rows, :]
        vc = v_scr[rows, :].astype(BF16)
        stable = jnp.min(g_last) >= -MAX_CHUNK_LOG_DECAY

        @pl.when(stable)
        def _intra_matmul():
            g_mid = gc[C // 2 - 1:C // 2, :]
            qt = (qc * jnp.exp(gc - g_mid)).astype(BF16)
            kt = (kc * jnp.exp(g_mid - gc)).astype(BF16)
            for h in range(N_HEADS):
                sl = slice(h * HEAD_DIM, (h + 1) * HEAD_DIM)
                a = jnp.where(causal, _dot_nt(qt[:, sl], kt[:, sl]), 0.0).astype(BF16)
                oi_scr[rows, sl] = _dot(a, vc[:, sl])

        @pl.when(jnp.logical_not(stable))
        def _intra_exact():
            same_head = (lax.broadcasted_iota(jnp.int32, (D_MODEL, D_MODEL), 0) // HEAD_DIM ==
                         lax.broadcasted_iota(jnp.int32, (D_MODEL, D_MODEL), 1) // HEAD_DIM)
            head_sum = jnp.where(same_head, 1.0, 0.0).astype(BF16)
            oi_scr[rows, :] = jnp.zeros((C, D_MODEL), F32)

            def src_step(s, carry2):
                src = pl.ds(r0 + s, 1)
                p = qc * jnp.exp(jnp.minimum(gc - g_scr[src, :], 0.0)) * k_scr[src, :]
                p = jnp.where(tpos >= s, p, 0.0).astype(BF16)
                oi_scr[rows, :] += _dot(p, head_sum) * v_scr[src, :]
                return carry2

            lax.fori_loop(0, C, src_step, 0)

        q_in = (qc * jnp.exp(gc)).astype(BF16)
        k_st = (kc * jnp.exp(g_last - gc)).astype(BF16)
        decay = jnp.exp(g_last)
        for h in range(N_HEADS):
            sl = slice(h * HEAD_DIM, (h + 1) * HEAD_DIM)
            st = st_scr[h]
            o_h = _dot_nt(q_in[:, sl], st.astype(BF16)) + oi_scr[rows, sl]
            o_scr[rows, sl] = o_h * _rms_scale(o_h) * hn
            st_scr[h] = decay[:, sl] * st + _dot_tn(vc[:, sl], k_st[:, sl])
        return carry

    lax.fori_loop(0, T // C, chunk_step, 0)

    y_a = _dot((o_scr[...] * _silu(proj(3))).astype(BF16), wa_ref[...])

    u = proj(5) * proj(6)
    u_scr[8:8 + T, :] = u
    cw = cw_ref[...]
    conv = cw[CONV_W - 1:CONV_W, :] * u
    for j in range(CONV_W - 1):
        conv = conv + cw[j:j + 1, :] * u_scr[CTX_ROW0 + j:CTX_ROW0 + j + T, :]
    y_b = _dot((proj(4) * conv * _silu(proj(7))).astype(BF16), wb_ref[...])
    new_ctx = u[T - (CONV_W - 1):T, :]
    u_scr[CTX_ROW0:8, :] = new_ctx

    merged = _sigmoid(proj(8)) * y_a + _sigmoid(proj(9)) * y_b
    out = _dot(merged.astype(BF16), wo_ref[...])
    y_ref[0] = x + out * _rms_scale(out) * npost_ref[...]

    @pl.when(t == pl.num_programs(1) - 1)
    def _finish():
        c_out_ref[0] = new_ctx
        if transpose_state_out:
            for h in range(N_HEADS):
                st_out_ref[0, h] = st_scr[h].T
        else:
            st_out_ref[0] = st_scr[...]


def _const_spec(shape):
    zeros = (0,) * len(shape)
    return pl.BlockSpec(shape, lambda b, t: zeros, pipeline_mode=pl.Buffered(1))


def _run_sequences(x, st0, c0, weights, *, tile, chunk, transpose_state_out, name):
    n, length, _ = x.shape
    assert length % tile == 0 and tile % chunk == 0 and chunk % 8 == 0 and tile >= CONV_W - 1
    w_in, npre, npost, lb_logits, hn, cw, wa, wb, wo = weights
    kern = functools.partial(_seq_kernel, tile=tile, chunk=chunk, transpose_state_out=transpose_state_out)
    state_shape = (1, N_HEADS, HEAD_DIM, HEAD_DIM)
    ctx_shape = (1, CONV_W - 1, D_MODEL)
    tile_f32 = pltpu.VMEM((tile, D_MODEL), F32)
    return pl.pallas_call(
        kern,
        grid=(n, length // tile),
        in_specs=[
            pl.BlockSpec((1, tile, D_MODEL), lambda b, t: (b, t, 0)),
            _const_spec(state_shape),
            _const_spec(ctx_shape),
            _const_spec(w_in.shape),
            _const_spec(npre.shape),
            _const_spec(npost.shape),
            _const_spec(lb_logits.shape),
            _const_spec(hn.shape),
            _const_spec(cw.shape),
            _const_spec(wa.shape),
            _const_spec(wb.shape),
            _const_spec(wo.shape),
        ],
        out_specs=[
            pl.BlockSpec((1, tile, D_MODEL), lambda b, t: (b, t, 0)),
            pl.BlockSpec(state_shape, lambda b, t: (b, 0, 0, 0)),
            pl.BlockSpec(ctx_shape, lambda b, t: (b, 0, 0)),
        ],
        out_shape=[
            jax.ShapeDtypeStruct(x.shape, F32),
            jax.ShapeDtypeStruct((n,) + state_shape[1:], F32),
            jax.ShapeDtypeStruct((n,) + ctx_shape[1:], F32),
        ],
        scratch_shapes=[
            pltpu.VMEM((N_HEADS, HEAD_DIM, HEAD_DIM), F32),
            pltpu.VMEM((tile + 8, D_MODEL), F32),
            tile_f32, tile_f32, tile_f32, tile_f32, tile_f32, tile_f32,
        ],
        compiler_params=pltpu.CompilerParams(
            dimension_semantics=("arbitrary", "arbitrary"),
            vmem_limit_bytes=V7X_VMEM_LIMIT_BYTES),
        name=name,
    )(x, st0, c0, w_in, npre, npost, lb_logits, hn, cw, wa, wb, wo)


def _decode_kernel(x_ref, st_ref, ctx_ref, w_in_ref, npre_ref, npost_ref, lb_ref, hn_ref, cw_ref,
                   wa_ref, wb_ref, wo_ref,
                   y_ref, st_out_ref, ctx_out_ref,
                   ft_scr, qt_scr, v_scr, o_scr, za_scr, pb_scr, ga_scr, gb_scr):
    G = DECODE_GROUP
    i = pl.program_id(0)
    n_rows = x_ref.shape[0]

    @pl.when(i == 0)
    def _project():
        x = x_ref[...]
        xn = (x * _rms_scale(x) * npre_ref[...]).astype(BF16)

        def proj(j):
            return _dot(xn, w_in_ref[:, j * D_MODEL:(j + 1) * D_MODEL])

        lb = _lower_bound(lb_ref[...])
        f = lb + (1.0 - lb) * _sigmoid(proj(1))
        ft_scr[...] = f.T
        qt_scr[...] = _silu(proj(0)).T
        v_scr[...] = proj(2)
        za_scr[...] = _silu(proj(3))
        u = proj(5) * proj(6)
        cw = cw_ref[...]
        ctx = ctx_ref[...]
        conv = cw[CONV_W - 1:CONV_W, :] * u
        for j in range(CONV_W - 1):
            conv = conv + cw[j:j + 1, :] * ctx[:, j * D_MODEL:(j + 1) * D_MODEL]
        pb_scr[...] = proj(4) * conv * _silu(proj(7))
        ctx_out_ref[:, 0:(CONV_W - 2) * D_MODEL] = ctx[:, D_MODEL:]
        ctx_out_ref[:, (CONV_W - 2) * D_MODEL:] = u
        ga_scr[...] = _sigmoid(proj(8))
        gb_scr[...] = _sigmoid(proj(9))

    shift = (n_rows - i * G) % n_rows
    f_cols = pltpu.roll(ft_scr[...], shift, 1)
    q_cols = pltpu.roll(qt_scr[...], shift, 1)
    r0 = pl.multiple_of(i * G, G)
    v_rows = v_scr[pl.ds(r0, G), :]
    o_rows = []
    for j in range(G):
        o_heads = []
        for h in range(N_HEADS):
            sl = slice(h * HEAD_DIM, (h + 1) * HEAD_DIM)
            f_col = f_cols[sl, j:j + 1]
            q_col = q_cols[sl, j:j + 1]
            s_new = f_col * st_ref[j, h] + (1.0 - f_col) * v_rows[j:j + 1, sl]
            st_out_ref[j, h] = s_new
            o_heads.append(jnp.sum(q_col * s_new, axis=0, keepdims=True))
        o_rows.append(jnp.concatenate(o_heads, axis=1))
    o_scr[pl.ds(r0, G), :] = jnp.concatenate(o_rows, axis=0)

    @pl.when(i == pl.num_programs(0) - 1)
    def _output():
        hn = hn_ref[...]
        for h in range(N_HEADS):
            sl = slice(h * HEAD_DIM, (h + 1) * HEAD_DIM)
            o_h = o_scr[:, sl]
            o_scr[:, sl] = o_h * _rms_scale(o_h) * hn
        y_a = _dot((o_scr[...] * za_scr[...]).astype(BF16), wa_ref[...])
        y_b = _dot(pb_scr[...].astype(BF16), wb_ref[...])
        merged = ga_scr[...] * y_a + gb_scr[...] * y_b
        out = _dot(merged.astype(BF16), wo_ref[...])
        y_ref[...] = x_ref[...] + out * _rms_scale(out) * npost_ref[...]


def _run_decode(x, state, ctx, weights):
    n = x.shape[0]
    assert n % DECODE_GROUP == 0 and n == 128
    w_in, npre, npost, lb_logits, hn, cw, wa, wb, wo = weights

    def const(shape):
        zeros = (0,) * len(shape)
        return pl.BlockSpec(shape, lambda i: zeros, pipeline_mode=pl.Buffered(1))

    st_spec = pl.BlockSpec((DECODE_GROUP, N_HEADS, HEAD_DIM, HEAD_DIM), lambda i: (i, 0, 0, 0))
    rows_f32 = pltpu.VMEM((n, D_MODEL), F32)
    cols_f32 = pltpu.VMEM((D_MODEL, n), F32)
    return pl.pallas_call(
        _decode_kernel,
        grid=(n // DECODE_GROUP,),
        in_specs=[const(x.shape), st_spec, const(ctx.shape), const(w_in.shape), const(npre.shape),
                  const(npost.shape), const(lb_logits.shape), const(hn.shape), const(cw.shape),
                  const(wa.shape), const(wb.shape), const(wo.shape)],
        out_specs=[const(x.shape), st_spec, const(ctx.shape)],
        out_shape=[jax.ShapeDtypeStruct(x.shape, F32),
                   jax.ShapeDtypeStruct(state.shape, F32),
                   jax.ShapeDtypeStruct(ctx.shape, F32)],
        scratch_shapes=[cols_f32, cols_f32, rows_f32, rows_f32, rows_f32, rows_f32, rows_f32, rows_f32],
        compiler_params=pltpu.CompilerParams(
            dimension_semantics=("arbitrary",),
            vmem_limit_bytes=V7X_VMEM_LIMIT_BYTES),
        name="decode_step",
    )(x, state, ctx, w_in, npre, npost, lb_logits, hn, cw, wa, wb, wo)


def kernel(x_prompt, x_sample, state_hgrn, state_conv, meta_tokens, w_in, norm_pre, norm_post, lb_logits,
           hgrn_norm, conv_w, w_a, w_b, w_o):
    depth = w_in.shape[0]
    assert depth == 1, "single-layer trunk"
    batch, seq, _ = x_prompt.shape
    dec_batch, dec_seq, _ = x_sample.shape
    assert dec_seq == 1
    n_meta = meta_tokens.shape[0]

    weights = (w_in[0].astype(BF16), norm_pre, norm_post, lb_logits, hgrn_norm, conv_w[0],
               w_a[0].astype(BF16), w_b[0].astype(BF16), w_o[0].astype(BF16))

    zero_state = jnp.zeros((1, N_HEADS, HEAD_DIM, HEAD_DIM), F32)
    zero_ctx = jnp.zeros((1, CONV_W - 1, D_MODEL), F32)
    _, st_meta, ctx_meta = _run_sequences(
        meta_tokens[None].astype(F32), zero_state, zero_ctx, weights,
        tile=n_meta, chunk=n_meta, transpose_state_out=False, name="meta_prefix")

    y_prompt, hgrn_p, conv_p = _run_sequences(
        x_prompt, st_meta, ctx_meta, weights,
        tile=PROMPT_TILE, chunk=PROMPT_CHUNK, transpose_state_out=True, name="prompt_sweep")

    y_s, hgrn_s, conv_s = _run_decode(
        x_sample.reshape(dec_batch, D_MODEL), state_hgrn[0],
        state_conv[0].reshape(dec_batch, (CONV_W - 1) * D_MODEL), weights)

    return (y_prompt, y_s.reshape(dec_batch, 1, D_MODEL), hgrn_p[None], hgrn_s[None],
            conv_p[None], conv_s.reshape(dec_batch, CONV_W - 1, D_MODEL)[None])
```

```python
import functools

import jax
import jax.numpy as jnp
from jax import lax
from jax.experimental import pallas as pl
from jax.experimental.pallas import tpu as pltpu

D_MODEL = 1024
N_HEADS = 8
HEAD_DIM = D_MODEL // N_HEADS
CONV_W = 3
EPS = 1e-6
N_PROJ = 10

PROMPT_TILE = 256
PROMPT_CHUNK = 64
DECODE_GROUP = 8
CTX_ROW0 = 8 - (CONV_W - 1)
MAX_CHUNK_LOG_DECAY = 80.0
V7X_VMEM_LIMIT_BYTES = 58 * 1024 * 1024

BF16 = jnp.bfloat16
F32 = jnp.float32


def _dot(a, b):
    return jnp.dot(a, b, preferred_element_type=F32)


def _dot_nt(a, b):
    return lax.dot_general(a, b, (((1,), (1,)), ((), ())), preferred_element_type=F32)


def _dot_tn(a, b):
    return lax.dot_general(a, b, (((0,), (0,)), ((), ())), preferred_element_type=F32)


def _sigmoid(x):
    return 1.0 / (1.0 + jnp.exp(-x))


def _silu(x):
    return x * _sigmoid(x)


def _rms_scale(x):
    return lax.rsqrt(jnp.mean(x * x, axis=-1, keepdims=True) + EPS)


def _lower_bound(lb_logits):
    m = jnp.max(lb_logits, axis=0, keepdims=True)
    e = jnp.exp(lb_logits - m)
    return e[0:1, :] / jnp.sum(e, axis=0, keepdims=True)


def _split3_bf16(x):
    hi = x.astype(BF16)
    r = x - hi.astype(F32)
    mid = r.astype(BF16)
    lo = (r - mid.astype(F32)).astype(BF16)
    return hi, mid, lo


def _seq_kernel(x_ref, st0_ref, c0_ref, w_in_ref, npre_ref, npost_ref, lb_ref, hn_ref, cw_ref,
                wa_ref, wb_ref, wo_ref,
                y_ref, st_out_ref, c_out_ref,
                st_scr, u_scr, q_scr, k_scr, v_scr, g_scr, oi_scr, o_scr,
                qt_scr, kt_scr, qin_scr, kst_scr, vb_scr, dec_scr,
                *, tile, chunk, transpose_state_out):
    T, C = tile, chunk
    t = pl.program_id(1)

    @pl.when(t == 0)
    def _init():
        st_scr[...] = st0_ref[0]
        u_scr[CTX_ROW0:8, :] = c0_ref[0]

    x = x_ref[0]
    xn = (x * _rms_scale(x) * npre_ref[...]).astype(BF16)

    def proj(j):
        return _dot(xn, w_in_ref[:, j * D_MODEL:(j + 1) * D_MODEL])

    lb = _lower_bound(lb_ref[...])
    f = lb + (1.0 - lb) * _sigmoid(proj(1))
    k_scr[...] = 1.0 - f
    q_scr[...] = _silu(proj(0))
    v_scr[...] = proj(2)
    ri = lax.broadcasted_iota(jnp.int32, (T, T), 0)
    ci = lax.broadcasted_iota(jnp.int32, (T, T), 1)
    tri = ri >= ci
    if T != C:
        tri = jnp.logical_and(tri, (ri // C) == (ci // C))
    tri = jnp.where(tri, 1.0, 0.0).astype(BF16)
    g_hi, g_mid, g_lo = _split3_bf16(jnp.log(f))
    g_scr[...] = _dot(tri, g_hi) + _dot(tri, g_mid) + _dot(tri, g_lo)

    n_chunks = T // C
    g_floor = None
    for c in range(n_chunks):
        rows = slice(c * C, (c + 1) * C)
        gc = g_scr[rows, :]
        g_last = gc[C - 1:C, :]
        g_mid = gc[C // 2 - 1:C // 2, :]
        qc = q_scr[rows, :]
        kc = k_scr[rows, :]
        qt_scr[rows, :] = (qc * jnp.exp(gc - g_mid)).astype(BF16)
        kt_scr[rows, :] = (kc * jnp.exp(g_mid - gc)).astype(BF16)
        qin_scr[rows, :] = (qc * jnp.exp(gc)).astype(BF16)
        kst_scr[rows, :] = (kc * jnp.exp(g_last - gc)).astype(BF16)
        vb_scr[rows, :] = v_scr[rows, :].astype(BF16)
        dec_scr[c:c + 1, :] = jnp.exp(g_last)
        g_floor = g_last if g_floor is None else jnp.minimum(g_floor, g_last)
    stable = jnp.min(g_floor) >= -MAX_CHUNK_LOG_DECAY

    causal = (lax.broadcasted_iota(jnp.int32, (C, C), 0) >= lax.broadcasted_iota(jnp.int32, (C, C), 1))
    heads = [slice(h * HEAD_DIM, (h + 1) * HEAD_DIM) for h in range(N_HEADS)]
    for c in range(n_chunks):
        rows = slice(c * C, (c + 1) * C)
        scores = [jnp.where(causal, _dot_nt(qt_scr[rows, sl], kt_scr[rows, sl]), 0.0).astype(BF16)
                  for sl in heads]
        for h, sl in enumerate(heads):
            vb = vb_scr[rows, sl]
            st = st_scr[h]
            oi_scr[rows, sl] = _dot(scores[h], vb)
            o_scr[rows, sl] = _dot_nt(qin_scr[rows, sl], st.astype(BF16))
            st_scr[h] = dec_scr[c:c + 1, sl] * st + _dot_tn(vb, kst_scr[rows, sl])

    @pl.when(jnp.logical_not(stable))
    def _intra_exact():
        same_head = (lax.broadcasted_iota(jnp.int32, (D_MODEL, D_MODEL), 0) // HEAD_DIM ==
                     lax.broadcasted_iota(jnp.int32, (D_MODEL, D_MODEL), 1) // HEAD_DIM)
        head_sum = jnp.where(same_head, 1.0, 0.0).astype(BF16)
        tpos = lax.broadcasted_iota(jnp.int32, (C, 1), 0)

        def chunk_step(c, carry):
            r0 = pl.multiple_of(c * C, C)
            rows = pl.ds(r0, C)
            gc = g_scr[rows, :]
            qc = q_scr[rows, :]
            oi_scr[rows, :] = jnp.zeros((C, D_MODEL), F32)

            def src_step(s, carry2):
                src = pl.ds(r0 + s, 1)
                p = qc * jnp.exp(jnp.minimum(gc - g_scr[src, :], 0.0)) * k_scr[src, :]
                p = jnp.where(tpos >= s, p, 0.0).astype(BF16)
                oi_scr[rows, :] += _dot(p, head_sum) * v_scr[src, :]
                return carry2

            return lax.fori_loop(0, C, src_step, carry)

        lax.fori_loop(0, n_chunks, chunk_step, 0)

    hn = hn_ref[...]
    for h in range(N_HEADS):
        sl = slice(h * HEAD_DIM, (h + 1) * HEAD_DIM)
        o_h = o_scr[:, sl] + oi_scr[:, sl]
        o_scr[:, sl] = o_h * _rms_scale(o_h) * hn
    y_a = _dot((o_scr[...] * _silu(proj(3))).astype(BF16), wa_ref[...])

    u = proj(5) * proj(6)
    u_scr[8:8 + T, :] = u
    cw = cw_ref[...]
    conv = cw[CONV_W - 1:CONV_W, :] * u
    for j in range(CONV_W - 1):
        conv = conv + cw[j:j + 1, :] * u_scr[CTX_ROW0 + j:CTX_ROW0 + j + T, :]
    y_b = _dot((proj(4) * conv * _silu(proj(7))).astype(BF16), wb_ref[...])
    new_ctx = u[T - (CONV_W - 1):T, :]
    u_scr[CTX_ROW0:8, :] = new_ctx

    merged = _sigmoid(proj(8)) * y_a + _sigmoid(proj(9)) * y_b
    out = _dot(merged.astype(BF16), wo_ref[...])
    y_ref[0] = x + out * _rms_scale(out) * npost_ref[...]

    @pl.when(t == pl.num_programs(1) - 1)
    def _finish():
        c_out_ref[0] = new_ctx
        if transpose_state_out:
            for h in range(N_HEADS):
                st_out_ref[0, h] = st_scr[h].T
        else:
            st_out_ref[0] = st_scr[...]


def _const_spec(shape):
    zeros = (0,) * len(shape)
    return pl.BlockSpec(shape, lambda b, t: zeros, pipeline_mode=pl.Buffered(1))


def _run_sequences(x, st0, c0, weights, *, tile, chunk, transpose_state_out, name):
    n, length, _ = x.shape
    assert length % tile == 0 and tile % chunk == 0 and chunk % 8 == 0 and tile >= CONV_W - 1
    w_in, npre, npost, lb_logits, hn, cw, wa, wb, wo = weights
    kern = functools.partial(_seq_kernel, tile=tile, chunk=chunk, transpose_state_out=transpose_state_out)
    state_shape = (1, N_HEADS, HEAD_DIM, HEAD_DIM)
    ctx_shape = (1, CONV_W - 1, D_MODEL)
    tile_f32 = pltpu.VMEM((tile, D_MODEL), F32)
    tile_bf16 = pltpu.VMEM((tile, D_MODEL), BF16)
    return pl.pallas_call(
        kern,
        grid=(n, length // tile),
        in_specs=[
            pl.BlockSpec((1, tile, D_MODEL), lambda b, t: (b, t, 0)),
            _const_spec(state_shape),
            _const_spec(ctx_shape),
            _const_spec(w_in.shape),
            _const_spec(npre.shape),
            _const_spec(npost.shape),
            _const_spec(lb_logits.shape),
            _const_spec(hn.shape),
            _const_spec(cw.shape),
            _const_spec(wa.shape),
            _const_spec(wb.shape),
            _const_spec(wo.shape),
        ],
        out_specs=[
            pl.BlockSpec((1, tile, D_MODEL), lambda b, t: (b, t, 0)),
            pl.BlockSpec(state_shape, lambda b, t: (b, 0, 0, 0)),
            pl.BlockSpec(ctx_shape, lambda b, t: (b, 0, 0)),
        ],
        out_shape=[
            jax.ShapeDtypeStruct(x.shape, F32),
            jax.ShapeDtypeStruct((n,) + state_shape[1:], F32),
            jax.ShapeDtypeStruct((n,) + ctx_shape[1:], F32),
        ],
        scratch_shapes=[
            pltpu.VMEM((N_HEADS, HEAD_DIM, HEAD_DIM), F32),
            pltpu.VMEM((tile + 8, D_MODEL), F32),
            tile_f32, tile_f32, tile_f32, tile_f32, tile_f32, tile_f32,
            tile_bf16, tile_bf16, tile_bf16, tile_bf16, tile_bf16,
            pltpu.VMEM((max(8, tile // chunk), D_MODEL), F32),
        ],
        compiler_params=pltpu.CompilerParams(
            dimension_semantics=("arbitrary", "arbitrary"),
            vmem_limit_bytes=V7X_VMEM_LIMIT_BYTES),
        name=name,
    )(x, st0, c0, w_in, npre, npost, lb_logits, hn, cw, wa, wb, wo)


def _decode_kernel(x_ref, st_ref, ctx_ref, w_in_ref, npre_ref, npost_ref, lb_ref, hn_ref, cw_ref,
                   wa_ref, wb_ref, wo_ref,
                   y_ref, st_out_ref, ctx_out_ref,
                   ft_scr, qt_scr, v_scr, o_scr, za_scr, pb_scr, ga_scr, gb_scr):
    G = DECODE_GROUP
    i = pl.program_id(0)
    n_rows = x_ref.shape[0]

    @pl.when(i == 0)
    def _project():
        x = x_ref[...]
        xn = (x * _rms_scale(x) * npre_ref[...]).astype(BF16)

        def proj(j):
            return _dot(xn, w_in_ref[:, j * D_MODEL:(j + 1) * D_MODEL])

        lb = _lower_bound(lb_ref[...])
        f = lb + (1.0 - lb) * _sigmoid(proj(1))
        ft_scr[...] = f.T
        qt_scr[...] = _silu(proj(0)).T
        v_scr[...] = proj(2)
        za_scr[...] = _silu(proj(3))
        u = proj(5) * proj(6)
        cw = cw_ref[...]
        ctx = ctx_ref[...]
        conv = cw[CONV_W - 1:CONV_W, :] * u
        for j in range(CONV_W - 1):
            conv = conv + cw[j:j + 1, :] * ctx[:, j * D_MODEL:(j + 1) * D_MODEL]
        pb_scr[...] = proj(4) * conv * _silu(proj(7))
        ctx_out_ref[:, 0:(CONV_W - 2) * D_MODEL] = ctx[:, D_MODEL:]
        ctx_out_ref[:, (CONV_W - 2) * D_MODEL:] = u
        ga_scr[...] = _sigmoid(proj(8))
        gb_scr[...] = _sigmoid(proj(9))

    shift = (n_rows - i * G) % n_rows
    f_cols = pltpu.roll(ft_scr[...], shift, 1)
    q_cols = pltpu.roll(qt_scr[...], shift, 1)
    r0 = pl.multiple_of(i * G, G)
    v_rows = v_scr[pl.ds(r0, G), :]
    o_rows = []
    for j in range(G):
        o_heads = []
        for h in range(N_HEADS):
            sl = slice(h * HEAD_DIM, (h + 1) * HEAD_DIM)
            f_col = f_cols[sl, j:j + 1]
            q_col = q_cols[sl, j:j + 1]
            s_new = f_col * st_ref[j, h] + (1.0 - f_col) * v_rows[j:j + 1, sl]
            st_out_ref[j, h] = s_new
            o_heads.append(jnp.sum(q_col * s_new, axis=0, keepdims=True))
        o_rows.append(jnp.concatenate(o_heads, axis=1))
    o_scr[pl.ds(r0, G), :] = jnp.concatenate(o_rows, axis=0)

    @pl.when(i == pl.num_programs(0) - 1)
    def _output():
        hn = hn_ref[...]
        for h in range(N_HEADS):
            sl = slice(h * HEAD_DIM, (h + 1) * HEAD_DIM)
            o_h = o_scr[:, sl]
            o_scr[:, sl] = o_h * _rms_scale(o_h) * hn
        y_a = _dot((o_scr[...] * za_scr[...]).astype(BF16), wa_ref[...])
        y_b = _dot(pb_scr[...].astype(BF16), wb_ref[...])
        merged = ga_scr[...] * y_a + gb_scr[...] * y_b
        out = _dot(merged.astype(BF16), wo_ref[...])
        y_ref[...] = x_ref[...] + out * _rms_scale(out) * npost_ref[...]


def _run_decode(x, state, ctx, weights):
    n = x.shape[0]
    assert n % DECODE_GROUP == 0 and n == 128
    w_in, npre, npost, lb_logits, hn, cw, wa, wb, wo = weights

    def const(shape):
        zeros = (0,) * len(shape)
        return pl.BlockSpec(shape, lambda i: zeros, pipeline_mode=pl.Buffered(1))

    st_spec = pl.BlockSpec((DECODE_GROUP, N_HEADS, HEAD_DIM, HEAD_DIM), lambda i: (i, 0, 0, 0))
    rows_f32 = pltpu.VMEM((n, D_MODEL), F32)
    cols_f32 = pltpu.VMEM((D_MODEL, n), F32)
    return pl.pallas_call(
        _decode_kernel,
        grid=(n // DECODE_GROUP,),
        in_specs=[const(x.shape), st_spec, const(ctx.shape), const(w_in.shape), const(npre.shape),
                  const(npost.shape), const(lb_logits.shape), const(hn.shape), const(cw.shape),
                  const(wa.shape), const(wb.shape), const(wo.shape)],
        out_specs=[const(x.shape), st_spec, const(ctx.shape)],
        out_shape=[jax.ShapeDtypeStruct(x.shape, F32),
                   jax.ShapeDtypeStruct(state.shape, F32),
                   jax.ShapeDtypeStruct(ctx.shape, F32)],
        scratch_shapes=[cols_f32, cols_f32, rows_f32, rows_f32, rows_f32, rows_f32, rows_f32, rows_f32],
        compiler_params=pltpu.CompilerParams(
            dimension_semantics=("arbitrary",),
            vmem_limit_bytes=V7X_VMEM_LIMIT_BYTES),
        name="decode_step",
    )(x, state, ctx, w_in, npre, npost, lb_logits, hn, cw, wa, wb, wo)


def kernel(x_prompt, x_sample, state_hgrn, state_conv, meta_tokens, w_in, norm_pre, norm_post, lb_logits,
           hgrn_norm, conv_w, w_a, w_b, w_o):
    depth = w_in.shape[0]
    assert depth == 1, "single-layer trunk"
    batch, seq, _ = x_prompt.shape
    dec_batch, dec_seq, _ = x_sample.shape
    assert dec_seq == 1
    n_meta = meta_tokens.shape[0]

    weights = (w_in[0].astype(BF16), norm_pre, norm_post, lb_logits, hgrn_norm, conv_w[0],
               w_a[0].astype(BF16), w_b[0].astype(BF16), w_o[0].astype(BF16))

    zero_state = jnp.zeros((1, N_HEADS, HEAD_DIM, HEAD_DIM), F32)
    zero_ctx = jnp.zeros((1, CONV_W - 1, D_MODEL), F32)
    _, st_meta, ctx_meta = _run_sequences(
        meta_tokens[None].astype(F32), zero_state, zero_ctx, weights,
        tile=n_meta, chunk=n_meta, transpose_state_out=False, name="meta_prefix")

    y_prompt, hgrn_p, conv_p = _run_sequences(
        x_prompt, st_meta, ctx_meta, weights,
        tile=PROMPT_TILE, chunk=PROMPT_CHUNK, transpose_state_out=True, name="prompt_sweep")

    y_s, hgrn_s, conv_s = _run_decode(
        x_sample.reshape(dec_batch, D_MODEL), state_hgrn[0],
        state_conv[0].reshape(dec_batch, (CONV_W - 1) * D_MODEL), weights)

    return (y_prompt, y_s.reshape(dec_batch, 1, D_MODEL), hgrn_p[None], hgrn_s[None],
            conv_p[None], conv_s.reshape(dec_batch, CONV_W - 1, D_MODEL)[None])
```

```python
import functools

import jax
import jax.numpy as jnp
from jax import lax
from jax.experimental import pallas as pl
from jax.experimental.pallas import tpu as pltpu

D_MODEL = 1024
N_HEADS = 8
HEAD_DIM = D_MODEL // N_HEADS
CONV_W = 3
EPS = 1e-6
N_PROJ = 10

PROMPT_TILE = 256
PROMPT_CHUNK = 64
DECODE_GROUP = 8
CTX_ROW0 = 8 - (CONV_W - 1)
MAX_CHUNK_LOG_DECAY = 80.0
V7X_VMEM_LIMIT_BYTES = 58 * 1024 * 1024

BF16 = jnp.bfloat16
F32 = jnp.float32


def _dot(a, b):
    return jnp.dot(a, b, preferred_element_type=F32)


def _dot_nt(a, b):
    return lax.dot_general(a, b, (((1,), (1,)), ((), ())), preferred_element_type=F32)


def _dot_tn(a, b):
    return lax.dot_general(a, b, (((0,), (0,)), ((), ())), preferred_element_type=F32)


def _sigmoid(x):
    return 1.0 / (1.0 + jnp.exp(-x))


def _silu(x):
    return x * _sigmoid(x)


def _rms_scale(x):
    return lax.rsqrt(jnp.mean(x * x, axis=-1, keepdims=True) + EPS)


def _lower_bound(lb_logits):
    m = jnp.max(lb_logits, axis=0, keepdims=True)
    e = jnp.exp(lb_logits - m)
    return e[0:1, :] / jnp.sum(e, axis=0, keepdims=True)


def _split3_bf16(x):
    hi = x.astype(BF16)
    r = x - hi.astype(F32)
    mid = r.astype(BF16)
    lo = (r - mid.astype(F32)).astype(BF16)
    return hi, mid, lo


def _seq_kernel(x_ref, st0_ref, c0_ref, w_in_ref, npre_ref, npost_ref, lb_ref, hn_ref, cw_ref,
                wa_ref, wb_ref, wo_ref,
                y_ref, st_out_ref, c_out_ref,
                st_scr, u_scr, q_scr, k_scr, v_scr, g_scr, oi_scr, o_scr,
                qt_scr, kt_scr, qin_scr, kst_scr, vb_scr, dec_scr,
                *, tile, chunk, transpose_state_out):
    T, C = tile, chunk
    t = pl.program_id(1)

    @pl.when(t == 0)
    def _init():
        st_scr[...] = st0_ref[0]
        u_scr[CTX_ROW0:8, :] = c0_ref[0]

    x = x_ref[0]
    xn = (x * _rms_scale(x) * npre_ref[...]).astype(BF16)

    def proj(j):
        return _dot(xn, w_in_ref[:, j * D_MODEL:(j + 1) * D_MODEL])

    lb = _lower_bound(lb_ref[...])
    f = lb + (1.0 - lb) * _sigmoid(proj(1))
    k_scr[...] = 1.0 - f
    q_scr[...] = _silu(proj(0))
    v_scr[...] = proj(2)
    ri = lax.broadcasted_iota(jnp.int32, (T, T), 0)
    ci = lax.broadcasted_iota(jnp.int32, (T, T), 1)
    tri = ri >= ci
    if T != C:
        tri = jnp.logical_and(tri, (ri // C) == (ci // C))
    tri = jnp.where(tri, 1.0, 0.0).astype(BF16)
    g_hi, g_mid, g_lo = _split3_bf16(jnp.log(f))
    g_scr[...] = _dot(tri, g_hi) + _dot(tri, g_mid) + _dot(tri, g_lo)

    n_chunks = T // C
    g_floor = None
    for c in range(n_chunks):
        rows = slice(c * C, (c + 1) * C)
        gc = g_scr[rows, :]
        g_last = gc[C - 1:C, :]
        g_mid = gc[C // 2 - 1:C // 2, :]
        qc = q_scr[rows, :]
        kc = k_scr[rows, :]
        qt_scr[rows, :] = (qc * jnp.exp(gc - g_mid)).astype(BF16)
        kt_scr[rows, :] = (kc * jnp.exp(g_mid - gc)).astype(BF16)
        qin_scr[rows, :] = (qc * jnp.exp(gc)).astype(BF16)
        kst_scr[rows, :] = (kc * jnp.exp(g_last - gc)).astype(BF16)
        vb_scr[rows, :] = v_scr[rows, :].astype(BF16)
        dec_scr[c:c + 1, :] = jnp.exp(g_last)
        g_floor = g_last if g_floor is None else jnp.minimum(g_floor, g_last)
    stable = jnp.min(g_floor) >= -MAX_CHUNK_LOG_DECAY

    causal = (lax.broadcasted_iota(jnp.int32, (C, C), 0) >= lax.broadcasted_iota(jnp.int32, (C, C), 1))
    heads = [slice(h * HEAD_DIM, (h + 1) * HEAD_DIM) for h in range(N_HEADS)]
    for c in range(n_chunks):
        rows = slice(c * C, (c + 1) * C)
        scores = [jnp.where(causal, _dot_nt(qt_scr[rows, sl], kt_scr[rows, sl]), 0.0).astype(BF16)
                  for sl in heads]
        for h, sl in enumerate(heads):
            vb = vb_scr[rows, sl]
            st = st_scr[h]
            oi_scr[rows, sl] = _dot(scores[h], vb)
            o_scr[rows, sl] = _dot_nt(qin_scr[rows, sl], st.astype(BF16))
            st_scr[h] = dec_scr[c:c + 1, sl] * st + _dot_tn(vb, kst_scr[rows, sl])

    @pl.when(jnp.logical_not(stable))
    def _intra_exact():
        same_head = (lax.broadcasted_iota(jnp.int32, (D_MODEL, D_MODEL), 0) // HEAD_DIM ==
                     lax.broadcasted_iota(jnp.int32, (D_MODEL, D_MODEL), 1) // HEAD_DIM)
        head_sum = jnp.where(same_head, 1.0, 0.0).astype(BF16)
        tpos = lax.broadcasted_iota(jnp.int32, (C, 1), 0)

        def chunk_step(c, carry):
            r0 = pl.multiple_of(c * C, C)
            rows = pl.ds(r0, C)
            gc = g_scr[rows, :]
            qc = q_scr[rows, :]
            oi_scr[rows, :] = jnp.zeros((C, D_MODEL), F32)

            def src_step(s, carry2):
                src = pl.ds(r0 + s, 1)
                p = qc * jnp.exp(jnp.minimum(gc - g_scr[src, :], 0.0)) * k_scr[src, :]
                p = jnp.where(tpos >= s, p, 0.0).astype(BF16)
                oi_scr[rows, :] += _dot(p, head_sum) * v_scr[src, :]
                return carry2

            return lax.fori_loop(0, C, src_step, carry)

        lax.fori_loop(0, n_chunks, chunk_step, 0)

    hn = hn_ref[...]
    for h in range(N_HEADS):
        sl = slice(h * HEAD_DIM, (h + 1) * HEAD_DIM)
        o_h = o_scr[:, sl] + oi_scr[:, sl]
        o_scr[:, sl] = o_h * _rms_scale(o_h) * hn
    y_a = _dot((o_scr[...] * _silu(proj(3))).astype(BF16), wa_ref[...])

    u = proj(5) * proj(6)
    u_scr[8:8 + T, :] = u
    cw = cw_ref[...]
    conv = cw[CONV_W - 1:CONV_W, :] * u
    for j in range(CONV_W - 1):
        conv = conv + cw[j:j + 1, :] * u_scr[CTX_ROW0 + j:CTX_ROW0 + j + T, :]
    y_b = _dot((proj(4) * conv * _silu(proj(7))).astype(BF16), wb_ref[...])
    new_ctx = u[T - (CONV_W - 1):T, :]
    u_scr[CTX_ROW0:8, :] = new_ctx

    merged = _sigmoid(proj(8)) * y_a + _sigmoid(proj(9)) * y_b
    out = _dot(merged.astype(BF16), wo_ref[...])
    y_ref[0] = x + out * _rms_scale(out) * npost_ref[...]

    @pl.when(t == pl.num_programs(1) - 1)
    def _finish():
        c_out_ref[0] = new_ctx
        if transpose_state_out:
            for h in range(N_HEADS):
                st_out_ref[0, h] = st_scr[h].T
        else:
            st_out_ref[0] = st_scr[...]


def _const_spec(shape):
    zeros = (0,) * len(shape)
    return pl.BlockSpec(shape, lambda b, t: zeros, pipeline_mode=pl.Buffered(1))


def _run_sequences(x, st0, c0, weights, *, tile, chunk, transpose_state_out, name):
    n, length, _ = x.shape
    assert length % tile == 0 and tile % chunk == 0 and chunk % 8 == 0 and tile >= CONV_W - 1
    w_in, npre, npost, lb_logits, hn, cw, wa, wb, wo = weights
    kern = functools.partial(_seq_kernel, tile=tile, chunk=chunk, transpose_state_out=transpose_state_out)
    state_shape = (1, N_HEADS, HEAD_DIM, HEAD_DIM)
    ctx_shape = (1, CONV_W - 1, D_MODEL)
    tile_f32 = pltpu.VMEM((tile, D_MODEL), F32)
    tile_bf16 = pltpu.VMEM((tile, D_MODEL), BF16)
    return pl.pallas_call(
        kern,
        grid=(n, length // tile),
        in_specs=[
            pl.BlockSpec((1, tile, D_MODEL), lambda b, t: (b, t, 0)),
            _const_spec(state_shape),
            _const_spec(ctx_shape),
            _const_spec(w_in.shape),
            _const_spec(npre.shape),
            _const_spec(npost.shape),
            _const_spec(lb_logits.shape),
            _const_spec(hn.shape),
            _const_spec(cw.shape),
            _const_spec(wa.shape),
            _const_spec(wb.shape),
            _const_spec(wo.shape),
        ],
        out_specs=[
            pl.BlockSpec((1, tile, D_MODEL), lambda b, t: (b, t, 0)),
            pl.BlockSpec(state_shape, lambda b, t: (b, 0, 0, 0)),
            pl.BlockSpec(ctx_shape, lambda b, t: (b, 0, 0)),
        ],
        out_shape=[
            jax.ShapeDtypeStruct(x.shape, F32),
            jax.ShapeDtypeStruct((n,) + state_shape[1:], F32),
            jax.ShapeDtypeStruct((n,) + ctx_shape[1:], F32),
        ],
        scratch_shapes=[
            pltpu.VMEM((N_HEADS, HEAD_DIM, HEAD_DIM), F32),
            pltpu.VMEM((tile + 8, D_MODEL), F32),
            tile_f32, tile_f32, tile_f32, tile_f32, tile_f32, tile_f32,
            tile_bf16, tile_bf16, tile_bf16, tile_bf16, tile_bf16,
            pltpu.VMEM((max(8, tile // chunk), D_MODEL), F32),
        ],
        compiler_params=pltpu.CompilerParams(
            dimension_semantics=("arbitrary", "arbitrary"),
            vmem_limit_bytes=V7X_VMEM_LIMIT_BYTES),
        name=name,
    )(x, st0, c0, w_in, npre, npost, lb_logits, hn, cw, wa, wb, wo)


def _decode_kernel(x_ref, st_ref, ctx_ref, w_in_ref, npre_ref, npost_ref, lb_ref, hn_ref, cw_ref,
                   wa_ref, wb_ref, wo_ref,
                   y_ref, st_out_ref, ctx_out_ref,
                   ft_scr, q_scr, v_scr, o_scr, za_scr, pb_scr, ga_scr, gb_scr):
    G = DECODE_GROUP
    i = pl.program_id(0)
    n_rows = x_ref.shape[0]

    @pl.when(i == 0)
    def _project():
        x = x_ref[...]
        xn = (x * _rms_scale(x) * npre_ref[...]).astype(BF16)

        def proj(j):
            return _dot(xn, w_in_ref[:, j * D_MODEL:(j + 1) * D_MODEL])

        lb = _lower_bound(lb_ref[...])
        f = lb + (1.0 - lb) * _sigmoid(proj(1))
        ft_scr[...] = f.T
        q_scr[...] = _silu(proj(0))
        v_scr[...] = proj(2)
        za_scr[...] = _silu(proj(3))
        u = proj(5) * proj(6)
        cw = cw_ref[...]
        ctx = ctx_ref[...]
        conv = cw[CONV_W - 1:CONV_W, :] * u
        for j in range(CONV_W - 1):
            conv = conv + cw[j:j + 1, :] * ctx[:, j * D_MODEL:(j + 1) * D_MODEL]
        pb_scr[...] = proj(4) * conv * _silu(proj(7))
        ctx_out_ref[:, 0:(CONV_W - 2) * D_MODEL] = ctx[:, D_MODEL:]
        ctx_out_ref[:, (CONV_W - 2) * D_MODEL:] = u
        ga_scr[...] = _sigmoid(proj(8))
        gb_scr[...] = _sigmoid(proj(9))

    shift = (n_rows - i * G) % n_rows
    f_cols = pltpu.roll(ft_scr[...], shift, 1)
    r0 = pl.multiple_of(i * G, G)
    v_rows = v_scr[pl.ds(r0, G), :]
    q_rows = q_scr[pl.ds(r0, G), :].astype(BF16)
    row_id = lax.broadcasted_iota(jnp.int32, (G, HEAD_DIM), 0)
    o_heads = [jnp.zeros((G, HEAD_DIM), F32)] * N_HEADS
    for j in range(G):
        for h in range(N_HEADS):
            sl = slice(h * HEAD_DIM, (h + 1) * HEAD_DIM)
            f_b = jnp.broadcast_to(f_cols[sl, j:j + 1], (HEAD_DIM, HEAD_DIM))
            s_new = f_b * st_ref[j, h] + (1.0 - f_b) * v_rows[j:j + 1, sl]
            st_out_ref[j, h] = s_new
            read = _dot(q_rows[:, sl], s_new.astype(BF16))
            o_heads[h] = jnp.where(row_id == j, read, o_heads[h])
    o_scr[pl.ds(r0, G), :] = jnp.concatenate(o_heads, axis=1)

    @pl.when(i == pl.num_programs(0) - 1)
    def _output():
        hn = hn_ref[...]
        for h in range(N_HEADS):
            sl = slice(h * HEAD_DIM, (h + 1) * HEAD_DIM)
            o_h = o_scr[:, sl]
            o_scr[:, sl] = o_h * _rms_scale(o_h) * hn
        y_a = _dot((o_scr[...] * za_scr[...]).astype(BF16), wa_ref[...])
        y_b = _dot(pb_scr[...].astype(BF16), wb_ref[...])
        merged = ga_scr[...] * y_a + gb_scr[...] * y_b
        out = _dot(merged.astype(BF16), wo_ref[...])
        y_ref[...] = x_ref[...] + out * _rms_scale(out) * npost_ref[...]


def _run_decode(x, state, ctx, weights):
    n = x.shape[0]
    assert n % DECODE_GROUP == 0 and n == 128
    w_in, npre, npost, lb_logits, hn, cw, wa, wb, wo = weights

    def const(shape):
        zeros = (0,) * len(shape)
        return pl.BlockSpec(shape, lambda i: zeros, pipeline_mode=pl.Buffered(1))

    st_spec = pl.BlockSpec((DECODE_GROUP, N_HEADS, HEAD_DIM, HEAD_DIM), lambda i: (i, 0, 0, 0))
    rows_f32 = pltpu.VMEM((n, D_MODEL), F32)
    cols_f32 = pltpu.VMEM((D_MODEL, n), F32)
    return pl.pallas_call(
        _decode_kernel,
        grid=(n // DECODE_GROUP,),
        in_specs=[const(x.shape), st_spec, const(ctx.shape), const(w_in.shape), const(npre.shape),
                  const(npost.shape), const(lb_logits.shape), const(hn.shape), const(cw.shape),
                  const(wa.shape), const(wb.shape), const(wo.shape)],
        out_specs=[const(x.shape), st_spec, const(ctx.shape)],
        out_shape=[jax.ShapeDtypeStruct(x.shape, F32),
                   jax.ShapeDtypeStruct(state.shape, F32),
                   jax.ShapeDtypeStruct(ctx.shape, F32)],
        scratch_shapes=[cols_f32, rows_f32, rows_f32, rows_f32, rows_f32, rows_f32, rows_f32, rows_f32],
        compiler_params=pltpu.CompilerParams(
            dimension_semantics=("arbitrary",),
            vmem_limit_bytes=V7X_VMEM_LIMIT_BYTES),
        name="decode_step",
    )(x, state, ctx, w_in, npre, npost, lb_logits, hn, cw, wa, wb, wo)


def kernel(x_prompt, x_sample, state_hgrn, state_conv, meta_tokens, w_in, norm_pre, norm_post, lb_logits,
           hgrn_norm, conv_w, w_a, w_b, w_o):
    depth = w_in.shape[0]
    assert depth == 1, "single-layer trunk"
    batch, seq, _ = x_prompt.shape
    dec_batch, dec_seq, _ = x_sample.shape
    assert dec_seq == 1
    n_meta = meta_tokens.shape[0]

    weights = (w_in[0].astype(BF16), norm_pre, norm_post, lb_logits, hgrn_norm, conv_w[0],
               w_a[0].astype(BF16), w_b[0].astype(BF16), w_o[0].astype(BF16))

    zero_state = jnp.zeros((1, N_HEADS, HEAD_DIM, HEAD_DIM), F32)
    zero_ctx = jnp.zeros((1, CONV_W - 1, D_MODEL), F32)
    _, st_meta, ctx_meta = _run_sequences(
        meta_tokens[None].astype(F32), zero_state, zero_ctx, weights,
        tile=n_meta, chunk=n_meta, transpose_state_out=False, name="meta_prefix")

    y_prompt, hgrn_p, conv_p = _run_sequences(
        x_prompt, st_meta, ctx_meta, weights,
        tile=PROMPT_TILE, chunk=PROMPT_CHUNK, transpose_state_out=True, name="prompt_sweep")

    y_s, hgrn_s, conv_s = _run_decode(
        x_sample.reshape(dec_batch, D_MODEL), state_hgrn[0],
        state_conv[0].reshape(dec_batch, (CONV_W - 1) * D_MODEL), weights)

    return (y_prompt, y_s.reshape(dec_batch, 1, D_MODEL), hgrn_p[None], hgrn_s[None],
            conv_p[None], conv_s.reshape(dec_batch, CONV_W - 1, D_MODEL)[None])
```

```python
import functools

import jax
import jax.numpy as jnp
from jax import lax
from jax.experimental import pallas as pl
from jax.experimental.pallas import tpu as pltpu

D_MODEL = 1024
N_HEADS = 8
HEAD_DIM = D_MODEL // N_HEADS
CONV_W = 3
EPS = 1e-6
N_PROJ = 10

PROMPT_TILE = 256
PROMPT_CHUNK = 64
DECODE_GROUP = 8
SUBLANES = 8
CTX_ROW0 = SUBLANES - (CONV_W - 1)
MAX_CHUNK_LOG_DECAY = 80.0
V7X_VMEM_LIMIT_BYTES = 58 * 1024 * 1024

BF16 = jnp.bfloat16
F32 = jnp.float32


def _dot(a, b):
    return jnp.dot(a, b, preferred_element_type=F32)


def _dot_nt(a, b):
    return lax.dot_general(a, b, (((1,), (1,)), ((), ())), preferred_element_type=F32)


def _dot_tn(a, b):
    return lax.dot_general(a, b, (((0,), (0,)), ((), ())), preferred_element_type=F32)


def _pack_bf16_rows(w):
    bits = lax.bitcast_convert_type(w.astype(BF16), jnp.uint16).astype(jnp.uint32)
    return bits[0::2] | (bits[1::2] << 16)


def _packed_weight(w_ref, cols=slice(None)):
    return pltpu.bitcast(w_ref[:, cols], BF16)


def _sigmoid(x):
    return 1.0 / (1.0 + jnp.exp(-x))


def _silu(x):
    return x * _sigmoid(x)


def _rms_scale(x):
    return lax.rsqrt(jnp.mean(x * x, axis=-1, keepdims=True) + EPS)


def _lower_bound(lb_logits):
    m = jnp.max(lb_logits, axis=0, keepdims=True)
    e = jnp.exp(lb_logits - m)
    return e[0:1, :] / jnp.sum(e, axis=0, keepdims=True)


def _chunk_cumsum_into(g_ref, x, chunk):
    n_rows, width = x.shape
    sublane = lax.broadcasted_iota(jnp.int32, (SUBLANES, width), 0)
    carry = None
    for r in range(0, n_rows, SUBLANES):
        xs = x[r:r + SUBLANES, :]
        shift = 1
        while shift < SUBLANES:
            xs = xs + jnp.where(sublane >= shift, pltpu.roll(xs, shift, 0), 0.0)
            shift *= 2
        if r % chunk != 0:
            xs = xs + carry
        g_ref[r:r + SUBLANES, :] = xs
        carry = xs[SUBLANES - 1:SUBLANES, :]


def _seq_kernel(x_ref, st0_ref, c0_ref, w_in_ref, npre_ref, npost_ref, lb_ref, hn_ref, cw_ref,
                wa_ref, wb_ref, wo_ref,
                y_ref, st_out_ref, c_out_ref,
                st_scr, u_scr, q_scr, k_scr, v_scr, g_scr, oi_scr, o_scr,
                qt_scr, kt_scr, qin_scr, kst_scr, vb_scr, dec_scr,
                *, tile, chunk, transpose_state_out):
    T, C = tile, chunk
    t = pl.program_id(1)

    @pl.when(t == 0)
    def _init():
        st_scr[...] = st0_ref[0]
        u_scr[CTX_ROW0:8, :] = c0_ref[0]

    x = x_ref[0]
    xn = (x * _rms_scale(x) * npre_ref[...]).astype(BF16)

    def proj(j):
        return _dot(xn, _packed_weight(w_in_ref, slice(j * D_MODEL, (j + 1) * D_MODEL)))

    lb = _lower_bound(lb_ref[...])
    f = lb + (1.0 - lb) * _sigmoid(proj(1))
    k_scr[...] = 1.0 - f
    q_scr[...] = _silu(proj(0))
    v_scr[...] = proj(2)
    _chunk_cumsum_into(g_scr, jnp.log(f), C)

    u = proj(5) * proj(6)
    u_scr[8:8 + T, :] = u
    cw = cw_ref[...]
    conv = cw[CONV_W - 1:CONV_W, :] * u
    for j in range(CONV_W - 1):
        conv = conv + cw[j:j + 1, :] * u_scr[CTX_ROW0 + j:CTX_ROW0 + j + T, :]
    y_b = _dot((proj(4) * conv * _silu(proj(7))).astype(BF16), _packed_weight(wb_ref))
    new_ctx = u[T - (CONV_W - 1):T, :]
    u_scr[CTX_ROW0:8, :] = new_ctx

    n_chunks = T // C
    g_floor = None
    for c in range(n_chunks):
        rows = slice(c * C, (c + 1) * C)
        gc = g_scr[rows, :]
        g_last = gc[C - 1:C, :]
        g_mid = gc[C // 2 - 1:C // 2, :]
        qc = q_scr[rows, :]
        kc = k_scr[rows, :]
        qt_scr[rows, :] = (qc * jnp.exp(gc - g_mid)).astype(BF16)
        kt_scr[rows, :] = (kc * jnp.exp(g_mid - gc)).astype(BF16)
        qin_scr[rows, :] = (qc * jnp.exp(gc)).astype(BF16)
        kst_scr[rows, :] = (kc * jnp.exp(g_last - gc)).astype(BF16)
        vb_scr[rows, :] = v_scr[rows, :].astype(BF16)
        dec_scr[c:c + 1, :] = jnp.exp(g_last)
        g_floor = g_last if g_floor is None else jnp.minimum(g_floor, g_last)
    stable = jnp.min(g_floor) >= -MAX_CHUNK_LOG_DECAY

    causal = (lax.broadcasted_iota(jnp.int32, (C, C), 0) >= lax.broadcasted_iota(jnp.int32, (C, C), 1))
    heads = [slice(h * HEAD_DIM, (h + 1) * HEAD_DIM) for h in range(N_HEADS)]
    gate_fns = [lambda: _silu(proj(3)), lambda: _sigmoid(proj(8)), lambda: _sigmoid(proj(9))]
    gates = []
    for c in range(n_chunks):
        rows = slice(c * C, (c + 1) * C)
        scores = [jnp.where(causal, _dot_nt(qt_scr[rows, sl], kt_scr[rows, sl]), 0.0).astype(BF16)
                  for sl in heads]
        for h, sl in enumerate(heads):
            vb = vb_scr[rows, sl]
            st = st_scr[h]
            oi_scr[rows, sl] = _dot(scores[h], vb)
            o_scr[rows, sl] = _dot_nt(qin_scr[rows, sl], st.astype(BF16))
            st_scr[h] = dec_scr[c:c + 1, sl] * st + _dot_tn(vb, kst_scr[rows, sl])
        if len(gates) < len(gate_fns):
            gates.append(gate_fns[len(gates)]())
    while len(gates) < len(gate_fns):
        gates.append(gate_fns[len(gates)]())
    silu_za, gate_a, gate_b = gates

    @pl.when(jnp.logical_not(stable))
    def _intra_exact():
        same_head = (lax.broadcasted_iota(jnp.int32, (D_MODEL, D_MODEL), 0) // HEAD_DIM ==
                     lax.broadcasted_iota(jnp.int32, (D_MODEL, D_MODEL), 1) // HEAD_DIM)
        head_sum = jnp.where(same_head, 1.0, 0.0).astype(BF16)
        tpos = lax.broadcasted_iota(jnp.int32, (C, 1), 0)

        def chunk_step(c, carry):
            r0 = pl.multiple_of(c * C, C)
            rows = pl.ds(r0, C)
            gc = g_scr[rows, :]
            qc = q_scr[rows, :]
            oi_scr[rows, :] = jnp.zeros((C, D_MODEL), F32)

            def src_step(s, carry2):
                src = pl.ds(r0 + s, 1)
                p = qc * jnp.exp(jnp.minimum(gc - g_scr[src, :], 0.0)) * k_scr[src, :]
                p = jnp.where(tpos >= s, p, 0.0).astype(BF16)
                oi_scr[rows, :] += _dot(p, head_sum) * v_scr[src, :]
                return carry2

            return lax.fori_loop(0, C, src_step, carry)

        lax.fori_loop(0, n_chunks, chunk_step, 0)

    hn = hn_ref[...]
    for h in range(N_HEADS):
        sl = slice(h * HEAD_DIM, (h + 1) * HEAD_DIM)
        o_h = o_scr[:, sl] + oi_scr[:, sl]
        o_scr[:, sl] = o_h * _rms_scale(o_h) * hn
    y_a = _dot((o_scr[...] * silu_za).astype(BF16), _packed_weight(wa_ref))

    merged = gate_a * y_a + gate_b * y_b
    out = _dot(merged.astype(BF16), _packed_weight(wo_ref))
    y_ref[0] = x + out * _rms_scale(out) * npost_ref[...]

    @pl.when(t == pl.num_programs(1) - 1)
    def _finish():
        c_out_ref[0] = new_ctx
        if transpose_state_out:
            for h in range(N_HEADS):
                st_out_ref[0, h] = st_scr[h].T
        else:
            st_out_ref[0] = st_scr[...]


def _const_spec(shape):
    zeros = (0,) * len(shape)
    return pl.BlockSpec(shape, lambda b, t: zeros, pipeline_mode=pl.Buffered(1))


def _run_sequences(x, st0, c0, weights, *, tile, chunk, transpose_state_out, name):
    n, length, _ = x.shape
    assert length % tile == 0 and tile % chunk == 0 and chunk % 8 == 0 and tile >= CONV_W - 1
    w_in, npre, npost, lb_logits, hn, cw, wa, wb, wo = weights
    kern = functools.partial(_seq_kernel, tile=tile, chunk=chunk, transpose_state_out=transpose_state_out)
    state_shape = (1, N_HEADS, HEAD_DIM, HEAD_DIM)
    ctx_shape = (1, CONV_W - 1, D_MODEL)
    tile_f32 = pltpu.VMEM((tile, D_MODEL), F32)
    tile_bf16 = pltpu.VMEM((tile, D_MODEL), BF16)
    return pl.pallas_call(
        kern,
        grid=(n, length // tile),
        in_specs=[
            pl.BlockSpec((1, tile, D_MODEL), lambda b, t: (b, t, 0)),
            _const_spec(state_shape),
            _const_spec(ctx_shape),
            _const_spec(w_in.shape),
            _const_spec(npre.shape),
            _const_spec(npost.shape),
            _const_spec(lb_logits.shape),
            _const_spec(hn.shape),
            _const_spec(cw.shape),
            _const_spec(wa.shape),
            _const_spec(wb.shape),
            _const_spec(wo.shape),
        ],
        out_specs=[
            pl.BlockSpec((1, tile, D_MODEL), lambda b, t: (b, t, 0)),
            pl.BlockSpec(state_shape, lambda b, t: (b, 0, 0, 0)),
            pl.BlockSpec(ctx_shape, lambda b, t: (b, 0, 0)),
        ],
        out_shape=[
            jax.ShapeDtypeStruct(x.shape, F32),
            jax.ShapeDtypeStruct((n,) + state_shape[1:], F32),
            jax.ShapeDtypeStruct((n,) + ctx_shape[1:], F32),
        ],
        scratch_shapes=[
            pltpu.VMEM((N_HEADS, HEAD_DIM, HEAD_DIM), F32),
            pltpu.VMEM((tile + 8, D_MODEL), F32),
            tile_f32, tile_f32, tile_f32, tile_f32, tile_f32, tile_f32,
            tile_bf16, tile_bf16, tile_bf16, tile_bf16, tile_bf16,
            pltpu.VMEM((max(8, tile // chunk), D_MODEL), F32),
        ],
        compiler_params=pltpu.CompilerParams(
            dimension_semantics=("arbitrary", "arbitrary"),
            vmem_limit_bytes=V7X_VMEM_LIMIT_BYTES),
        name=name,
    )(x, st0, c0, w_in, npre, npost, lb_logits, hn, cw, wa, wb, wo)


def _decode_kernel(x_ref, st_ref, ctx_ref, w_in_ref, npre_ref, npost_ref, lb_ref, hn_ref, cw_ref,
                   wa_ref, wb_ref, wo_ref,
                   y_ref, st_out_ref, ctx_out_ref,
                   ft_scr, q_scr, v_scr, o_scr, za_scr, pb_scr, ga_scr, gb_scr):
    G = DECODE_GROUP
    i = pl.program_id(0)
    n_rows = x_ref.shape[0]

    @pl.when(i == 0)
    def _project():
        x = x_ref[...]
        xn = (x * _rms_scale(x) * npre_ref[...]).astype(BF16)

        def proj(j):
            return _dot(xn, _packed_weight(w_in_ref, slice(j * D_MODEL, (j + 1) * D_MODEL)))

        lb = _lower_bound(lb_ref[...])
        f = lb + (1.0 - lb) * _sigmoid(proj(1))
        ft_scr[...] = f.T
        q_scr[...] = _silu(proj(0))
        v_scr[...] = proj(2)
        za_scr[...] = _silu(proj(3))
        u = proj(5) * proj(6)
        cw = cw_ref[...]
        ctx = ctx_ref[...]
        conv = cw[CONV_W - 1:CONV_W, :] * u
        for j in range(CONV_W - 1):
            conv = conv + cw[j:j + 1, :] * ctx[:, j * D_MODEL:(j + 1) * D_MODEL]
        pb_scr[...] = proj(4) * conv * _silu(proj(7))
        ctx_out_ref[:, 0:(CONV_W - 2) * D_MODEL] = ctx[:, D_MODEL:]
        ctx_out_ref[:, (CONV_W - 2) * D_MODEL:] = u
        ga_scr[...] = _sigmoid(proj(8))
        gb_scr[...] = _sigmoid(proj(9))

    shift = (n_rows - i * G) % n_rows
    f_cols = pltpu.roll(ft_scr[...], shift, 1)
    r0 = pl.multiple_of(i * G, G)
    v_rows = v_scr[pl.ds(r0, G), :]
    q_rows = q_scr[pl.ds(r0, G), :].astype(BF16)
    row_id = lax.broadcasted_iota(jnp.int32, (G, HEAD_DIM), 0)
    o_heads = [jnp.zeros((G, HEAD_DIM), F32)] * N_HEADS
    for j in range(G):
        for h in range(N_HEADS):
            sl = slice(h * HEAD_DIM, (h + 1) * HEAD_DIM)
            f_b = jnp.broadcast_to(f_cols[sl, j:j + 1], (HEAD_DIM, HEAD_DIM))
            s_new = f_b * st_ref[j, h] + (1.0 - f_b) * v_rows[j:j + 1, sl]
            st_out_ref[j, h] = s_new
            read = _dot(q_rows[:, sl], s_new.astype(BF16))
            o_heads[h] = jnp.where(row_id == j, read, o_heads[h])
    o_scr[pl.ds(r0, G), :] = jnp.concatenate(o_heads, axis=1)

    @pl.when(i == pl.num_programs(0) - 1)
    def _output():
        hn = hn_ref[...]
        for h in range(N_HEADS):
            sl = slice(h * HEAD_DIM, (h + 1) * HEAD_DIM)
            o_h = o_scr[:, sl]
            o_scr[:, sl] = o_h * _rms_scale(o_h) * hn
        y_a = _dot((o_scr[...] * za_scr[...]).astype(BF16), _packed_weight(wa_ref))
        y_b = _dot(pb_scr[...].astype(BF16), _packed_weight(wb_ref))
        merged = ga_scr[...] * y_a + gb_scr[...] * y_b
        out = _dot(merged.astype(BF16), _packed_weight(wo_ref))
        y_ref[...] = x_ref[...] + out * _rms_scale(out) * npost_ref[...]


def _run_decode(x, state, ctx, weights):
    n = x.shape[0]
    assert n % DECODE_GROUP == 0 and n == 128
    w_in, npre, npost, lb_logits, hn, cw, wa, wb, wo = weights

    def const(shape):
        zeros = (0,) * len(shape)
        return pl.BlockSpec(shape, lambda i: zeros, pipeline_mode=pl.Buffered(1))

    st_spec = pl.BlockSpec((DECODE_GROUP, N_HEADS, HEAD_DIM, HEAD_DIM), lambda i: (i, 0, 0, 0))
    rows_f32 = pltpu.VMEM((n, D_MODEL), F32)
    cols_f32 = pltpu.VMEM((D_MODEL, n), F32)
    return pl.pallas_call(
        _decode_kernel,
        grid=(n // DECODE_GROUP,),
        in_specs=[const(x.shape), st_spec, const(ctx.shape), const(w_in.shape), const(npre.shape),
                  const(npost.shape), const(lb_logits.shape), const(hn.shape), const(cw.shape),
                  const(wa.shape), const(wb.shape), const(wo.shape)],
        out_specs=[const(x.shape), st_spec, const(ctx.shape)],
        out_shape=[jax.ShapeDtypeStruct(x.shape, F32),
                   jax.ShapeDtypeStruct(state.shape, F32),
                   jax.ShapeDtypeStruct(ctx.shape, F32)],
        scratch_shapes=[cols_f32, rows_f32, rows_f32, rows_f32, rows_f32, rows_f32, rows_f32, rows_f32],
        compiler_params=pltpu.CompilerParams(
            dimension_semantics=("arbitrary",),
            vmem_limit_bytes=V7X_VMEM_LIMIT_BYTES),
        name="decode_step",
    )(x, state, ctx, w_in, npre, npost, lb_logits, hn, cw, wa, wb, wo)


def kernel(x_prompt, x_sample, state_hgrn, state_conv, meta_tokens, w_in, norm_pre, norm_post, lb_logits,
           hgrn_norm, conv_w, w_a, w_b, w_o):
    depth = w_in.shape[0]
    assert depth == 1, "single-layer trunk"
    batch, seq, _ = x_prompt.shape
    dec_batch, dec_seq, _ = x_sample.shape
    assert dec_seq == 1
    n_meta = meta_tokens.shape[0]

    weights = (_pack_bf16_rows(w_in[0]), norm_pre, norm_post, lb_logits, hgrn_norm, conv_w[0],
               _pack_bf16_rows(w_a[0]), _pack_bf16_rows(w_b[0]), _pack_bf16_rows(w_o[0]))

    zero_state = jnp.zeros((1, N_HEADS, HEAD_DIM, HEAD_DIM), F32)
    zero_ctx = jnp.zeros((1, CONV_W - 1, D_MODEL), F32)
    _, st_meta, ctx_meta = _run_sequences(
        meta_tokens[None].astype(F32), zero_state, zero_ctx, weights,
        tile=n_meta, chunk=n_meta, transpose_state_out=False, name="meta_prefix")

    y_prompt, hgrn_p, conv_p = _run_sequences(
        x_prompt, st_meta, ctx_meta, weights,
        tile=PROMPT_TILE, chunk=PROMPT_CHUNK, transpose_state_out=True, name="prompt_sweep")

    y_s, hgrn_s, conv_s = _run_decode(
        x_sample.reshape(dec_batch, D_MODEL), state_hgrn[0],
        state_conv[0].reshape(dec_batch, (CONV_W - 1) * D_MODEL), weights)

    return (y_prompt, y_s.reshape(dec_batch, 1, D_MODEL), hgrn_p[None], hgrn_s[None],
            conv_p[None], conv_s.reshape(dec_batch, CONV_W - 1, D_MODEL)[None])
```

```python
import functools

import jax
import jax.numpy as jnp
from jax import lax
from jax.experimental import pallas as pl
from jax.experimental.pallas import tpu as pltpu

D_MODEL = 1024
N_HEADS = 8
HEAD_DIM = D_MODEL // N_HEADS
CONV_W = 3
EPS = 1e-6
P_Q, P_F, P_I, P_ZA, P_B, P_C, P_H, P_ZB, P_GA, P_GB = range(10)

PROMPT_TILE = 256
PROMPT_CHUNK = 128
DECODE_GROUP = 8
SUBLANES = 8
CTX_ROW0 = SUBLANES - (CONV_W - 1)
MAX_HALF_CHUNK_LOG_DECAY = 80.0
V7X_VMEM_LIMIT_BYTES = 58 * 1024 * 1024

BF16 = jnp.bfloat16
F32 = jnp.float32
HEADS = [slice(h * HEAD_DIM, (h + 1) * HEAD_DIM) for h in range(N_HEADS)]


def _dot(a, b):
    return jnp.dot(a, b, preferred_element_type=F32)


def _dot_nt(a, b):
    return lax.dot_general(a, b, (((1,), (1,)), ((), ())), preferred_element_type=F32)


def _dot_tn(a, b):
    return lax.dot_general(a, b, (((0,), (0,)), ((), ())), preferred_element_type=F32)


def _sigmoid(x):
    return 1.0 / (1.0 + jnp.exp(-x))


def _silu(x):
    return x * _sigmoid(x)


def _rms_scale(x):
    return lax.rsqrt(jnp.mean(x * x, axis=-1, keepdims=True) + EPS)


def _lower_bound(lb_logits):
    m = jnp.max(lb_logits, axis=0, keepdims=True)
    e = jnp.exp(lb_logits - m)
    return e[0:1, :] / jnp.sum(e, axis=0, keepdims=True)


def _split3_bf16(x):
    hi = x.astype(BF16)
    r = x - hi.astype(F32)
    mid = r.astype(BF16)
    lo = (r - mid.astype(F32)).astype(BF16)
    return hi, mid, lo


def _seq_kernel(x_ref, st0_ref, c0_ref, w_in_ref, npre_ref, npost_ref, lb_ref, hn_ref, cw_ref,
                wa_ref, wb_ref, wo_ref,
                y_ref, st_out_ref, c_out_ref,
                st_scr, u_scr, q_scr, k_scr, v_scr, g_scr, oi_scr, o_scr,
                qt_scr, kt_scr, qin_scr, kst_scr, vb_scr, dec_scr,
                *, tile, chunk, transpose_state_out):
    T, C = tile, chunk
    n_chunks = T // C
    t = pl.program_id(1)

    @pl.when(t == 0)
    def _init():
        st_scr[...] = st0_ref[0]
        u_scr[CTX_ROW0:SUBLANES, :] = c0_ref[0]

    x = x_ref[0]
    xn = (x * _rms_scale(x) * npre_ref[...]).astype(BF16)

    def proj(j):
        return _dot(xn, w_in_ref[:, j * D_MODEL:(j + 1) * D_MODEL])

    lb = _lower_bound(lb_ref[...])
    f = lb + (1.0 - lb) * _sigmoid(proj(P_F))
    k_scr[...] = 1.0 - f
    q_scr[...] = _silu(proj(P_Q))
    v_scr[...] = proj(P_I)
    ri = lax.broadcasted_iota(jnp.int32, (T, T), 0)
    ci = lax.broadcasted_iota(jnp.int32, (T, T), 1)
    tri = ri >= ci
    if T != C:
        tri = jnp.logical_and(tri, (ri // C) == (ci // C))
    tri = jnp.where(tri, 1.0, 0.0).astype(BF16)
    g_hi, g_mid, g_lo = _split3_bf16(jnp.log(f))
    g_scr[...] = _dot(tri, g_hi) + _dot(tri, g_mid) + _dot(tri, g_lo)

    g_floor = None
    for c in range(n_chunks):
        rows = slice(c * C, (c + 1) * C)
        gc = g_scr[rows, :]
        g_last = gc[C - 1:C, :]
        g_mid = gc[C // 2 - 1:C // 2, :]
        qc = q_scr[rows, :]
        kc = k_scr[rows, :]
        qt_scr[rows, :] = (qc * jnp.exp(gc - g_mid)).astype(BF16)
        kt_scr[rows, :] = (kc * jnp.exp(g_mid - gc)).astype(BF16)
        qin_scr[rows, :] = (qc * jnp.exp(gc)).astype(BF16)
        kst_scr[rows, :] = (kc * jnp.exp(g_last - gc)).astype(BF16)
        vb_scr[rows, :] = v_scr[rows, :].astype(BF16)
        dec_scr[c:c + 1, :] = jnp.exp(g_last)
        half_floor = jnp.minimum(g_mid, g_last - g_mid)
        g_floor = half_floor if g_floor is None else jnp.minimum(g_floor, half_floor)
    stable = jnp.min(g_floor) >= -MAX_HALF_CHUNK_LOG_DECAY

    causal = (lax.broadcasted_iota(jnp.int32, (C, C), 0) >= lax.broadcasted_iota(jnp.int32, (C, C), 1))
    for c in range(n_chunks):
        rows = slice(c * C, (c + 1) * C)
        scores = [jnp.where(causal, _dot_nt(qt_scr[rows, sl], kt_scr[rows, sl]), 0.0).astype(BF16)
                  for sl in HEADS]
        for h, sl in enumerate(HEADS):
            vb = vb_scr[rows, sl]
            st = st_scr[h]
            oi_scr[rows, sl] = _dot(scores[h], vb)
            o_scr[rows, sl] = _dot_nt(qin_scr[rows, sl], st.astype(BF16))
            st_scr[h] = dec_scr[c:c + 1, sl] * st + _dot_tn(vb, kst_scr[rows, sl])

    @pl.when(jnp.logical_not(stable))
    def _intra_exact():
        same_head = (lax.broadcasted_iota(jnp.int32, (D_MODEL, D_MODEL), 0) // HEAD_DIM ==
                     lax.broadcasted_iota(jnp.int32, (D_MODEL, D_MODEL), 1) // HEAD_DIM)
        head_sum = jnp.where(same_head, 1.0, 0.0).astype(BF16)
        tpos = lax.broadcasted_iota(jnp.int32, (C, 1), 0)

        def chunk_step(c, carry):
            r0 = pl.multiple_of(c * C, C)
            rows = pl.ds(r0, C)
            gc = g_scr[rows, :]
            qc = q_scr[rows, :]
            oi_scr[rows, :] = jnp.zeros((C, D_MODEL), F32)

            def src_step(i, carry2):
                src = pl.ds(r0 + i, 1)
                p = qc * jnp.exp(jnp.minimum(gc - g_scr[src, :], 0.0)) * k_scr[src, :]
                p = jnp.where(tpos >= i, p, 0.0).astype(BF16)
                oi_scr[rows, :] += _dot(p, head_sum) * v_scr[src, :]
                return carry2

            return lax.fori_loop(0, C, src_step, carry)

        lax.fori_loop(0, n_chunks, chunk_step, 0)

    hn = hn_ref[...]
    for sl in HEADS:
        o_h = o_scr[:, sl] + oi_scr[:, sl]
        o_scr[:, sl] = o_h * _rms_scale(o_h) * hn
    y_a = _dot((o_scr[...] * _silu(proj(P_ZA))).astype(BF16), wa_ref[...])

    u = proj(P_C) * proj(P_H)
    u_scr[SUBLANES:SUBLANES + T, :] = u
    cw = cw_ref[...]
    conv = cw[CONV_W - 1:CONV_W, :] * u
    for j in range(CONV_W - 1):
        conv = conv + cw[j:j + 1, :] * u_scr[CTX_ROW0 + j:CTX_ROW0 + j + T, :]
    y_b = _dot((proj(P_B) * conv * _silu(proj(P_ZB))).astype(BF16), wb_ref[...])
    new_ctx = u[T - (CONV_W - 1):T, :]
    u_scr[CTX_ROW0:SUBLANES, :] = new_ctx

    merged = _sigmoid(proj(P_GA)) * y_a + _sigmoid(proj(P_GB)) * y_b
    out = _dot(merged.astype(BF16), wo_ref[...])
    y_ref[0] = x + out * _rms_scale(out) * npost_ref[...]

    @pl.when(t == pl.num_programs(1) - 1)
    def _finish():
        c_out_ref[0] = new_ctx
        if transpose_state_out:
            for h in range(N_HEADS):
                st_out_ref[0, h] = st_scr[h].T
        else:
            st_out_ref[0] = st_scr[...]


def _const_spec(shape):
    zeros = (0,) * len(shape)
    return pl.BlockSpec(shape, lambda b, t: zeros, pipeline_mode=pl.Buffered(1))


def _run_sequences(x, st0, c0, weights, *, tile, chunk, transpose_state_out, name):
    n, length, _ = x.shape
    assert length % tile == 0 and tile % chunk == 0 and chunk % (2 * SUBLANES) == 0 and tile >= CONV_W - 1
    w_in, npre, npost, lb_logits, hn, cw, wa, wb, wo = weights
    kern = functools.partial(_seq_kernel, tile=tile, chunk=chunk, transpose_state_out=transpose_state_out)
    state_shape = (1, N_HEADS, HEAD_DIM, HEAD_DIM)
    ctx_shape = (1, CONV_W - 1, D_MODEL)
    tile_f32 = pltpu.VMEM((tile, D_MODEL), F32)
    tile_bf16 = pltpu.VMEM((tile, D_MODEL), BF16)
    return pl.pallas_call(
        kern,
        grid=(n, length // tile),
        in_specs=[
            pl.BlockSpec((1, tile, D_MODEL), lambda b, t: (b, t, 0)),
            _const_spec(state_shape),
            _const_spec(ctx_shape),
            _const_spec(w_in.shape),
            _const_spec(npre.shape),
            _const_spec(npost.shape),
            _const_spec(lb_logits.shape),
            _const_spec(hn.shape),
            _const_spec(cw.shape),
            _const_spec(wa.shape),
            _const_spec(wb.shape),
            _const_spec(wo.shape),
        ],
        out_specs=[
            pl.BlockSpec((1, tile, D_MODEL), lambda b, t: (b, t, 0)),
            pl.BlockSpec(state_shape, lambda b, t: (b, 0, 0, 0)),
            pl.BlockSpec(ctx_shape, lambda b, t: (b, 0, 0)),
        ],
        out_shape=[
            jax.ShapeDtypeStruct(x.shape, F32),
            jax.ShapeDtypeStruct((n,) + state_shape[1:], F32),
            jax.ShapeDtypeStruct((n,) + ctx_shape[1:], F32),
        ],
        scratch_shapes=[
            pltpu.VMEM((N_HEADS, HEAD_DIM, HEAD_DIM), F32),
            pltpu.VMEM((tile + SUBLANES, D_MODEL), F32),
            tile_f32, tile_f32, tile_f32, tile_f32, tile_f32, tile_f32,
            tile_bf16, tile_bf16, tile_bf16, tile_bf16, tile_bf16,
            pltpu.VMEM((max(SUBLANES, tile // chunk), D_MODEL), F32),
        ],
        compiler_params=pltpu.CompilerParams(
            dimension_semantics=("arbitrary", "arbitrary"),
            vmem_limit_bytes=V7X_VMEM_LIMIT_BYTES),
        name=name,
    )(x, st0, c0, w_in, npre, npost, lb_logits, hn, cw, wa, wb, wo)


def _decode_kernel(x_ref, st_ref, ctx_ref, w_in_ref, npre_ref, npost_ref, lb_ref, hn_ref, cw_ref,
                   wa_ref, wb_ref, wo_ref,
                   y_ref, st_out_ref, ctx_out_ref,
                   ft_scr, q_scr, v_scr, o_scr, za_scr, pb_scr, ga_scr, gb_scr):
    G = DECODE_GROUP
    i = pl.program_id(0)
    n_rows = x_ref.shape[0]

    @pl.when(i == 0)
    def _project():
        x = x_ref[...]
        xn = (x * _rms_scale(x) * npre_ref[...]).astype(BF16)

        def proj(j):
            return _dot(xn, w_in_ref[:, j * D_MODEL:(j + 1) * D_MODEL])

        lb = _lower_bound(lb_ref[...])
        f = lb + (1.0 - lb) * _sigmoid(proj(P_F))
        ft_scr[...] = f.T
        q_scr[...] = _silu(proj(P_Q))
        v_scr[...] = proj(P_I)
        za_scr[...] = _silu(proj(P_ZA))
        u = proj(P_C) * proj(P_H)
        cw = cw_ref[...]
        ctx = ctx_ref[...]
        conv = cw[CONV_W - 1:CONV_W, :] * u
        for j in range(CONV_W - 1):
            conv = conv + cw[j:j + 1, :] * ctx[:, j * D_MODEL:(j + 1) * D_MODEL]
        pb_scr[...] = proj(P_B) * conv * _silu(proj(P_ZB))
        ctx_out_ref[:, 0:(CONV_W - 2) * D_MODEL] = ctx[:, D_MODEL:]
        ctx_out_ref[:, (CONV_W - 2) * D_MODEL:] = u
        ga_scr[...] = _sigmoid(proj(P_GA))
        gb_scr[...] = _sigmoid(proj(P_GB))

    shift = (n_rows - i * G) % n_rows
    f_cols = pltpu.roll(ft_scr[...], shift, 1)
    r0 = pl.multiple_of(i * G, G)
    v_rows = v_scr[pl.ds(r0, G), :]
    q_rows = q_scr[pl.ds(r0, G), :].astype(BF16)
    row_id = lax.broadcasted_iota(jnp.int32, (G, HEAD_DIM), 0)
    o_heads = [jnp.zeros((G, HEAD_DIM), F32)] * N_HEADS
    for j in range(G):
        for h, sl in enumerate(HEADS):
            f_b = jnp.broadcast_to(f_cols[sl, j:j + 1], (HEAD_DIM, HEAD_DIM))
            s_new = f_b * st_ref[j, h] + (1.0 - f_b) * v_rows[j:j + 1, sl]
            st_out_ref[j, h] = s_new
            read = _dot(q_rows[:, sl], s_new.astype(BF16))
            o_heads[h] = jnp.where(row_id == j, read, o_heads[h])
    o_scr[pl.ds(r0, G), :] = jnp.concatenate(o_heads, axis=1)

    @pl.when(i == pl.num_programs(0) - 1)
    def _output():
        hn = hn_ref[...]
        for sl in HEADS:
            o_h = o_scr[:, sl]
            o_scr[:, sl] = o_h * _rms_scale(o_h) * hn
        y_a = _dot((o_scr[...] * za_scr[...]).astype(BF16), wa_ref[...])
        y_b = _dot(pb_scr[...].astype(BF16), wb_ref[...])
        merged = ga_scr[...] * y_a + gb_scr[...] * y_b
        out = _dot(merged.astype(BF16), wo_ref[...])
        y_ref[...] = x_ref[...] + out * _rms_scale(out) * npost_ref[...]


def _run_decode(x, state, ctx, weights):
    n = x.shape[0]
    assert n % DECODE_GROUP == 0 and n == 128
    w_in, npre, npost, lb_logits, hn, cw, wa, wb, wo = weights

    def const(shape):
        zeros = (0,) * len(shape)
        return pl.BlockSpec(shape, lambda i: zeros, pipeline_mode=pl.Buffered(1))

    st_spec = pl.BlockSpec((DECODE_GROUP, N_HEADS, HEAD_DIM, HEAD_DIM), lambda i: (i, 0, 0, 0))
    rows_f32 = pltpu.VMEM((n, D_MODEL), F32)
    cols_f32 = pltpu.VMEM((D_MODEL, n), F32)
    return pl.pallas_call(
        _decode_kernel,
        grid=(n // DECODE_GROUP,),
        in_specs=[const(x.shape), st_spec, const(ctx.shape), const(w_in.shape), const(npre.shape),
                  const(npost.shape), const(lb_logits.shape), const(hn.shape), const(cw.shape),
                  const(wa.shape), const(wb.shape), const(wo.shape)],
        out_specs=[const(x.shape), st_spec, const(ctx.shape)],
        out_shape=[jax.ShapeDtypeStruct(x.shape, F32),
                   jax.ShapeDtypeStruct(state.shape, F32),
                   jax.ShapeDtypeStruct(ctx.shape, F32)],
        scratch_shapes=[cols_f32, rows_f32, rows_f32, rows_f32, rows_f32, rows_f32, rows_f32, rows_f32],
        compiler_params=pltpu.CompilerParams(
            dimension_semantics=("arbitrary",),
            vmem_limit_bytes=V7X_VMEM_LIMIT_BYTES),
        name="decode_step",
    )(x, state, ctx, w_in, npre, npost, lb_logits, hn, cw, wa, wb, wo)


def kernel(x_prompt, x_sample, state_hgrn, state_conv, meta_tokens, w_in, norm_pre, norm_post, lb_logits,
           hgrn_norm, conv_w, w_a, w_b, w_o):
    depth = w_in.shape[0]
    assert depth == 1, "single-layer trunk"
    batch, seq, _ = x_prompt.shape
    dec_batch, dec_seq, _ = x_sample.shape
    assert dec_seq == 1
    n_meta = meta_tokens.shape[0]

    weights = (w_in[0].astype(BF16), norm_pre, norm_post, lb_logits, hgrn_norm, conv_w[0],
               w_a[0].astype(BF16), w_b[0].astype(BF16), w_o[0].astype(BF16))

    zero_state = jnp.zeros((1, N_HEADS, HEAD_DIM, HEAD_DIM), F32)
    zero_ctx = jnp.zeros((1, CONV_W - 1, D_MODEL), F32)
    _, st_meta, ctx_meta = _run_sequences(
        meta_tokens[None].astype(F32), zero_state, zero_ctx, weights,
        tile=n_meta, chunk=n_meta, transpose_state_out=False, name="meta_prefix")

    y_prompt, hgrn_p, conv_p = _run_sequences(
        x_prompt, st_meta, ctx_meta, weights,
        tile=PROMPT_TILE, chunk=PROMPT_CHUNK, transpose_state_out=True, name="prompt_sweep")

    y_s, hgrn_s, conv_s = _run_decode(
        x_sample.reshape(dec_batch, D_MODEL), state_hgrn[0],
        state_conv[0].reshape(dec_batch, (CONV_W - 1) * D_MODEL), weights)

    return (y_prompt, y_s.reshape(dec_batch, 1, D_MODEL), hgrn_p[None], hgrn_s[None],
            conv_p[None], conv_s.reshape(dec_batch, CONV_W - 1, D_MODEL)[None])
```

```python
import functools

import jax
import jax.numpy as jnp
from jax import lax
from jax.experimental import pallas as pl
from jax.experimental.pallas import tpu as pltpu

D_MODEL = 1024
N_HEADS = 8
HEAD_DIM = D_MODEL // N_HEADS
CONV_W = 3
EPS = 1e-6
P_Q, P_F, P_I, P_ZA, P_B, P_C, P_H, P_ZB, P_GA, P_GB = range(10)

PROMPT_TILE = 256
PROMPT_CHUNK = 128
DECODE_GROUP = 8
SUBLANES = 8
CTX_ROW0 = SUBLANES - (CONV_W - 1)
MAX_HALF_CHUNK_LOG_DECAY = 80.0
V7X_VMEM_LIMIT_BYTES = 58 * 1024 * 1024

BF16 = jnp.bfloat16
F32 = jnp.float32
HEADS = [slice(h * HEAD_DIM, (h + 1) * HEAD_DIM) for h in range(N_HEADS)]


def _dot(a, b):
    return jnp.dot(a, b, preferred_element_type=F32)


def _dot_nt(a, b):
    return lax.dot_general(a, b, (((1,), (1,)), ((), ())), preferred_element_type=F32)


def _dot_tn(a, b):
    return lax.dot_general(a, b, (((0,), (0,)), ((), ())), preferred_element_type=F32)


def _sigmoid(x):
    return 1.0 / (1.0 + jnp.exp(-x))


def _silu(x):
    return x * _sigmoid(x)


def _rms_scale(x):
    return lax.rsqrt(jnp.mean(x * x, axis=-1, keepdims=True) + EPS)


def _lower_bound(lb_logits):
    m = jnp.max(lb_logits, axis=0, keepdims=True)
    e = jnp.exp(lb_logits - m)
    return e[0:1, :] / jnp.sum(e, axis=0, keepdims=True)


def _block_cumsum(x, block):
    n = x.shape[0]
    ri = lax.broadcasted_iota(jnp.int32, (n, n), 0)
    ci = lax.broadcasted_iota(jnp.int32, (n, n), 1)
    tri = ri >= ci
    if n != block:
        tri = jnp.logical_and(tri, (ri // block) == (ci // block))
    tri = jnp.where(tri, 1.0, 0.0).astype(BF16)
    hi = x.astype(BF16)
    r = x - hi.astype(F32)
    mid = r.astype(BF16)
    lo = (r - mid.astype(F32)).astype(BF16)
    return _dot(tri, hi) + _dot(tri, mid) + _dot(tri, lo)


def _seq_kernel(x_ref, meta_ref, w_in_ref, npre_ref, npost_ref, lb_ref, hn_ref, cw_ref,
                wa_ref, wb_ref, wo_ref,
                y_ref, st_out_ref, c_out_ref,
                st_scr, u_scr, q_scr, k_scr, v_scr, g_scr, oi_scr, o_scr,
                qt_scr, kt_scr, qin_scr, kst_scr, vb_scr, dec_scr, st_meta_scr, ctx_meta_scr,
                *, tile, chunk):
    T, C = tile, chunk
    n_chunks = T // C
    t = pl.program_id(1)
    lb = _lower_bound(lb_ref[...])

    def project(xn, j):
        return _dot(xn, w_in_ref[:, j * D_MODEL:(j + 1) * D_MODEL])

    @pl.when(jnp.logical_and(pl.program_id(0) == 0, t == 0))
    def _meta_prefix():
        xm = meta_ref[...]
        n_meta = xm.shape[0]
        xn_m = (xm * _rms_scale(xm) * npre_ref[...]).astype(BF16)
        f_m = lb + (1.0 - lb) * _sigmoid(project(xn_m, P_F))
        g_m = _block_cumsum(jnp.log(f_m), n_meta)
        k_end = ((1.0 - f_m) * jnp.exp(g_m[n_meta - 1:n_meta, :] - g_m)).astype(BF16)
        v_m = project(xn_m, P_I).astype(BF16)
        for h, sl in enumerate(HEADS):
            st_meta_scr[h] = _dot_tn(v_m[:, sl], k_end[:, sl])
        u_m = project(xn_m, P_C) * project(xn_m, P_H)
        ctx_meta_scr[CTX_ROW0:SUBLANES, :] = u_m[n_meta - (CONV_W - 1):n_meta, :]

    @pl.when(t == 0)
    def _init():
        st_scr[...] = st_meta_scr[...]
        u_scr[CTX_ROW0:SUBLANES, :] = ctx_meta_scr[CTX_ROW0:SUBLANES, :]

    x = x_ref[0]
    xn = (x * _rms_scale(x) * npre_ref[...]).astype(BF16)

    def proj(j):
        return project(xn, j)

    f = lb + (1.0 - lb) * _sigmoid(proj(P_F))
    k_scr[...] = 1.0 - f
    q_scr[...] = _silu(proj(P_Q))
    v_scr[...] = proj(P_I)
    g_scr[...] = _block_cumsum(jnp.log(f), C)

    g_floor = None
    for c in range(n_chunks):
        rows = slice(c * C, (c + 1) * C)
        gc = g_scr[rows, :]
        g_last = gc[C - 1:C, :]
        g_mid = gc[C // 2 - 1:C // 2, :]
        qc = q_scr[rows, :]
        kc = k_scr[rows, :]
        qt_scr[rows, :] = (qc * jnp.exp(gc - g_mid)).astype(BF16)
        kt_scr[rows, :] = (kc * jnp.exp(g_mid - gc)).astype(BF16)
        qin_scr[rows, :] = (qc * jnp.exp(gc)).astype(BF16)
        kst_scr[rows, :] = (kc * jnp.exp(g_last - gc)).astype(BF16)
        vb_scr[rows, :] = v_scr[rows, :].astype(BF16)
        dec_scr[c:c + 1, :] = jnp.exp(g_last)
        half_floor = jnp.minimum(g_mid, g_last - g_mid)
        g_floor = half_floor if g_floor is None else jnp.minimum(g_floor, half_floor)
    stable = jnp.min(g_floor) >= -MAX_HALF_CHUNK_LOG_DECAY

    causal = (lax.broadcasted_iota(jnp.int32, (C, C), 0) >= lax.broadcasted_iota(jnp.int32, (C, C), 1))
    for c in range(n_chunks):
        rows = slice(c * C, (c + 1) * C)
        scores = [jnp.where(causal, _dot_nt(qt_scr[rows, sl], kt_scr[rows, sl]), 0.0).astype(BF16)
                  for sl in HEADS]
        for h, sl in enumerate(HEADS):
            vb = vb_scr[rows, sl]
            st = st_scr[h]
            oi_scr[rows, sl] = _dot(scores[h], vb)
            o_scr[rows, sl] = _dot_nt(qin_scr[rows, sl], st.astype(BF16))
            st_scr[h] = dec_scr[c:c + 1, sl] * st + _dot_tn(vb, kst_scr[rows, sl])

    @pl.when(jnp.logical_not(stable))
    def _intra_exact():
        same_head = (lax.broadcasted_iota(jnp.int32, (D_MODEL, D_MODEL), 0) // HEAD_DIM ==
                     lax.broadcasted_iota(jnp.int32, (D_MODEL, D_MODEL), 1) // HEAD_DIM)
        head_sum = jnp.where(same_head, 1.0, 0.0).astype(BF16)
        tpos = lax.broadcasted_iota(jnp.int32, (C, 1), 0)

        def chunk_step(c, carry):
            r0 = pl.multiple_of(c * C, C)
            rows = pl.ds(r0, C)
            gc = g_scr[rows, :]
            qc = q_scr[rows, :]
            oi_scr[rows, :] = jnp.zeros((C, D_MODEL), F32)

            def src_step(i, carry2):
                src = pl.ds(r0 + i, 1)
                p = qc * jnp.exp(jnp.minimum(gc - g_scr[src, :], 0.0)) * k_scr[src, :]
                p = jnp.where(tpos >= i, p, 0.0).astype(BF16)
                oi_scr[rows, :] += _dot(p, head_sum) * v_scr[src, :]
                return carry2

            return lax.fori_loop(0, C, src_step, carry)

        lax.fori_loop(0, n_chunks, chunk_step, 0)

    hn = hn_ref[...]
    for sl in HEADS:
        o_h = o_scr[:, sl] + oi_scr[:, sl]
        o_scr[:, sl] = o_h * _rms_scale(o_h) * hn
    y_a = _dot((o_scr[...] * _silu(proj(P_ZA))).astype(BF16), wa_ref[...])

    u = proj(P_C) * proj(P_H)
    u_scr[SUBLANES:SUBLANES + T, :] = u
    cw = cw_ref[...]
    conv = cw[CONV_W - 1:CONV_W, :] * u
    for j in range(CONV_W - 1):
        conv = conv + cw[j:j + 1, :] * u_scr[CTX_ROW0 + j:CTX_ROW0 + j + T, :]
    y_b = _dot((proj(P_B) * conv * _silu(proj(P_ZB))).astype(BF16), wb_ref[...])
    new_ctx = u[T - (CONV_W - 1):T, :]
    u_scr[CTX_ROW0:SUBLANES, :] = new_ctx

    merged = _sigmoid(proj(P_GA)) * y_a + _sigmoid(proj(P_GB)) * y_b
    out = _dot(merged.astype(BF16), wo_ref[...])
    y_ref[0] = x + out * _rms_scale(out) * npost_ref[...]

    @pl.when(t == pl.num_programs(1) - 1)
    def _finish():
        c_out_ref[0] = new_ctx
        for h in range(N_HEADS):
            st_out_ref[0, h] = st_scr[h].T


def _const_spec(shape):
    zeros = (0,) * len(shape)
    return pl.BlockSpec(shape, lambda b, t: zeros, pipeline_mode=pl.Buffered(1))


def _run_prompt(x, meta, weights, *, tile, chunk):
    n, length, _ = x.shape
    assert length % tile == 0 and tile % chunk == 0 and chunk % (2 * SUBLANES) == 0 and tile >= CONV_W - 1
    assert meta.shape[0] % (2 * SUBLANES) == 0 and meta.shape[0] >= CONV_W - 1
    w_in, npre, npost, lb_logits, hn, cw, wa, wb, wo = weights
    kern = functools.partial(_seq_kernel, tile=tile, chunk=chunk)
    state_shape = (1, N_HEADS, HEAD_DIM, HEAD_DIM)
    ctx_shape = (1, CONV_W - 1, D_MODEL)
    tile_f32 = pltpu.VMEM((tile, D_MODEL), F32)
    tile_bf16 = pltpu.VMEM((tile, D_MODEL), BF16)
    return pl.pallas_call(
        kern,
        grid=(n, length // tile),
        in_specs=[
            pl.BlockSpec((1, tile, D_MODEL), lambda b, t: (b, t, 0)),
            _const_spec(meta.shape),
            _const_spec(w_in.shape),
            _const_spec(npre.shape),
            _const_spec(npost.shape),
            _const_spec(lb_logits.shape),
            _const_spec(hn.shape),
            _const_spec(cw.shape),
            _const_spec(wa.shape),
            _const_spec(wb.shape),
            _const_spec(wo.shape),
        ],
        out_specs=[
            pl.BlockSpec((1, tile, D_MODEL), lambda b, t: (b, t, 0)),
            pl.BlockSpec(state_shape, lambda b, t: (b, 0, 0, 0)),
            pl.BlockSpec(ctx_shape, lambda b, t: (b, 0, 0)),
        ],
        out_shape=[
            jax.ShapeDtypeStruct(x.shape, F32),
            jax.ShapeDtypeStruct((n,) + state_shape[1:], F32),
            jax.ShapeDtypeStruct((n,) + ctx_shape[1:], F32),
        ],
        scratch_shapes=[
            pltpu.VMEM((N_HEADS, HEAD_DIM, HEAD_DIM), F32),
            pltpu.VMEM((tile + SUBLANES, D_MODEL), F32),
            tile_f32, tile_f32, tile_f32, tile_f32, tile_f32, tile_f32,
            tile_bf16, tile_bf16, tile_bf16, tile_bf16, tile_bf16,
            pltpu.VMEM((max(SUBLANES, tile // chunk), D_MODEL), F32),
            pltpu.VMEM((N_HEADS, HEAD_DIM, HEAD_DIM), F32),
            pltpu.VMEM((SUBLANES, D_MODEL), F32),
        ],
        compiler_params=pltpu.CompilerParams(
            dimension_semantics=("arbitrary", "arbitrary"),
            vmem_limit_bytes=V7X_VMEM_LIMIT_BYTES),
        name="prompt_sweep",
    )(x, meta, w_in, npre, npost, lb_logits, hn, cw, wa, wb, wo)


def _decode_kernel(x_ref, st_ref, ctx_ref, w_in_ref, npre_ref, npost_ref, lb_ref, hn_ref, cw_ref,
                   wa_ref, wb_ref, wo_ref,
                   y_ref, st_out_ref, ctx_out_ref,
                   ft_scr, q_scr, v_scr, o_scr, za_scr, pb_scr, ga_scr, gb_scr):
    G = DECODE_GROUP
    i = pl.program_id(0)
    n_rows = x_ref.shape[0]

    @pl.when(i == 0)
    def _project():
        x = x_ref[...]
        xn = (x * _rms_scale(x) * npre_ref[...]).astype(BF16)

        def proj(j):
            return _dot(xn, w_in_ref[:, j * D_MODEL:(j + 1) * D_MODEL])

        lb = _lower_bound(lb_ref[...])
        f = lb + (1.0 - lb) * _sigmoid(proj(P_F))
        ft_scr[...] = f.T
        q_scr[...] = _silu(proj(P_Q))
        v_scr[...] = proj(P_I)
        za_scr[...] = _silu(proj(P_ZA))
        u = proj(P_C) * proj(P_H)
        cw = cw_ref[...]
        ctx = ctx_ref[...]
        conv = cw[CONV_W - 1:CONV_W, :] * u
        for j in range(CONV_W - 1):
            conv = conv + cw[j:j + 1, :] * ctx[:, j * D_MODEL:(j + 1) * D_MODEL]
        pb_scr[...] = proj(P_B) * conv * _silu(proj(P_ZB))
        ctx_out_ref[:, 0:(CONV_W - 2) * D_MODEL] = ctx[:, D_MODEL:]
        ctx_out_ref[:, (CONV_W - 2) * D_MODEL:] = u
        ga_scr[...] = _sigmoid(proj(P_GA))
        gb_scr[...] = _sigmoid(proj(P_GB))

    shift = (n_rows - i * G) % n_rows
    f_cols = pltpu.roll(ft_scr[...], shift, 1)
    r0 = pl.multiple_of(i * G, G)
    v_rows = v_scr[pl.ds(r0, G), :]
    q_rows = q_scr[pl.ds(r0, G), :].astype(BF16)
    row_id = lax.broadcasted_iota(jnp.int32, (G, HEAD_DIM), 0)
    o_heads = [jnp.zeros((G, HEAD_DIM), F32)] * N_HEADS
    for j in range(G):
        for h, sl in enumerate(HEADS):
            f_b = jnp.broadcast_to(f_cols[sl, j:j + 1], (HEAD_DIM, HEAD_DIM))
            s_new = f_b * st_ref[j, h] + (1.0 - f_b) * v_rows[j:j + 1, sl]
            st_out_ref[j, h] = s_new
            read = _dot(q_rows[:, sl], s_new.astype(BF16))
            o_heads[h] = jnp.where(row_id == j, read, o_heads[h])
    o_scr[pl.ds(r0, G), :] = jnp.concatenate(o_heads, axis=1)

    @pl.when(i == pl.num_programs(0) - 1)
    def _output():
        hn = hn_ref[...]
        for sl in HEADS:
            o_h = o_scr[:, sl]
            o_scr[:, sl] = o_h * _rms_scale(o_h) * hn
        y_a = _dot((o_scr[...] * za_scr[...]).astype(BF16), wa_ref[...])
        y_b = _dot(pb_scr[...].astype(BF16), wb_ref[...])
        merged = ga_scr[...] * y_a + gb_scr[...] * y_b
        out = _dot(merged.astype(BF16), wo_ref[...])
        y_ref[...] = x_ref[...] + out * _rms_scale(out) * npost_ref[...]


def _run_decode(x, state, ctx, weights):
    n = x.shape[0]
    assert n % DECODE_GROUP == 0 and n == 128
    w_in, npre, npost, lb_logits, hn, cw, wa, wb, wo = weights

    def const(shape):
        zeros = (0,) * len(shape)
        return pl.BlockSpec(shape, lambda i: zeros, pipeline_mode=pl.Buffered(1))

    st_spec = pl.BlockSpec((DECODE_GROUP, N_HEADS, HEAD_DIM, HEAD_DIM), lambda i: (i, 0, 0, 0))
    rows_f32 = pltpu.VMEM((n, D_MODEL), F32)
    cols_f32 = pltpu.VMEM((D_MODEL, n), F32)
    return pl.pallas_call(
        _decode_kernel,
        grid=(n // DECODE_GROUP,),
        in_specs=[const(x.shape), st_spec, const(ctx.shape), const(w_in.shape), const(npre.shape),
                  const(npost.shape), const(lb_logits.shape), const(hn.shape), const(cw.shape),
                  const(wa.shape), const(wb.shape), const(wo.shape)],
        out_specs=[const(x.shape), st_spec, const(ctx.shape)],
        out_shape=[jax.ShapeDtypeStruct(x.shape, F32),
                   jax.ShapeDtypeStruct(state.shape, F32),
                   jax.ShapeDtypeStruct(ctx.shape, F32)],
        scratch_shapes=[cols_f32, rows_f32, rows_f32, rows_f32, rows_f32, rows_f32, rows_f32, rows_f32],
        compiler_params=pltpu.CompilerParams(
            dimension_semantics=("arbitrary",),
            vmem_limit_bytes=V7X_VMEM_LIMIT_BYTES),
        name="decode_step",
    )(x, state, ctx, w_in, npre, npost, lb_logits, hn, cw, wa, wb, wo)


def kernel(x_prompt, x_sample, state_hgrn, state_conv, meta_tokens, w_in, norm_pre, norm_post, lb_logits,
           hgrn_norm, conv_w, w_a, w_b, w_o):
    depth = w_in.shape[0]
    assert depth == 1, "single-layer trunk"
    batch, seq, _ = x_prompt.shape
    dec_batch, dec_seq, _ = x_sample.shape
    assert dec_seq == 1

    weights = (w_in[0].astype(BF16), norm_pre, norm_post, lb_logits, hgrn_norm, conv_w[0],
               w_a[0].astype(BF16), w_b[0].astype(BF16), w_o[0].astype(BF16))

    y_prompt, hgrn_p, conv_p = _run_prompt(
        x_prompt, meta_tokens.astype(x_prompt.dtype), weights, tile=PROMPT_TILE, chunk=PROMPT_CHUNK)

    y_s, hgrn_s, conv_s = _run_decode(
        x_sample.reshape(dec_batch, D_MODEL), state_hgrn[0],
        state_conv[0].reshape(dec_batch, (CONV_W - 1) * D_MODEL), weights)

    return (y_prompt, y_s.reshape(dec_batch, 1, D_MODEL), hgrn_p[None], hgrn_s[None],
            conv_p[None], conv_s.reshape(dec_batch, CONV_W - 1, D_MODEL)[None])
```

```python
import functools

import jax
import jax.numpy as jnp
from jax import lax
from jax.experimental import pallas as pl
from jax.experimental.pallas import tpu as pltpu

D_MODEL = 1024
N_HEADS = 8
HEAD_DIM = D_MODEL // N_HEADS
CONV_W = 3
EPS = 1e-6
P_Q, P_F, P_I, P_ZA, P_B, P_C, P_H, P_ZB, P_GA, P_GB = range(10)

PROMPT_TILE = 256
PROMPT_CHUNK = 128
DECODE_GROUP = 8
SUBLANES = 8
CTX_ROW0 = SUBLANES - (CONV_W - 1)
MAX_HALF_CHUNK_LOG_DECAY = 80.0
V7X_VMEM_LIMIT_BYTES = 58 * 1024 * 1024

BF16 = jnp.bfloat16
F32 = jnp.float32
HEADS = [slice(h * HEAD_DIM, (h + 1) * HEAD_DIM) for h in range(N_HEADS)]


def _dot(a, b):
    return jnp.dot(a, b, preferred_element_type=F32)


def _dot_nt(a, b):
    return lax.dot_general(a, b, (((1,), (1,)), ((), ())), preferred_element_type=F32)


def _dot_tn(a, b):
    return lax.dot_general(a, b, (((0,), (0,)), ((), ())), preferred_element_type=F32)


def _sigmoid(x):
    return 1.0 / (1.0 + jnp.exp(-x))


def _silu(x):
    return x * _sigmoid(x)


def _rms_scale(x):
    return lax.rsqrt(jnp.mean(x * x, axis=-1, keepdims=True) + EPS)


def _lower_bound(lb_logits):
    m = jnp.max(lb_logits, axis=0, keepdims=True)
    e = jnp.exp(lb_logits - m)
    return e[0:1, :] / jnp.sum(e, axis=0, keepdims=True)


def _block_cumsum(x, block):
    n = x.shape[0]
    ri = lax.broadcasted_iota(jnp.int32, (n, n), 0)
    ci = lax.broadcasted_iota(jnp.int32, (n, n), 1)
    tri = ri >= ci
    if n != block:
        tri = jnp.logical_and(tri, (ri // block) == (ci // block))
    tri = jnp.where(tri, 1.0, 0.0).astype(BF16)
    hi = x.astype(BF16)
    r = x - hi.astype(F32)
    mid = r.astype(BF16)
    lo = (r - mid.astype(F32)).astype(BF16)
    return _dot(tri, hi) + _dot(tri, mid) + _dot(tri, lo)


def _seq_kernel(x_ref, meta_ref, w_in_ref, npre_ref, npost_ref, lb_ref, hn_ref, cw_ref,
                wa_ref, wb_ref, wo_ref,
                y_ref, st_out_ref, c_out_ref,
                st_scr, u_scr, q_scr, k_scr, v_scr, g_scr, o_scr,
                qt_scr, kt_scr, qin_scr, kst_scr, vb_scr, dec_scr, st_meta_scr, ctx_meta_scr,
                *, tile, chunk):
    T, C = tile, chunk
    n_chunks = T // C
    t = pl.program_id(1)
    lb = _lower_bound(lb_ref[...])

    def project(xn, j):
        return _dot(xn, w_in_ref[:, j * D_MODEL:(j + 1) * D_MODEL])

    @pl.when(jnp.logical_and(pl.program_id(0) == 0, t == 0))
    def _meta_prefix():
        xm = meta_ref[...]
        n_meta = xm.shape[0]
        xn_m = (xm * _rms_scale(xm) * npre_ref[...]).astype(BF16)
        f_m = lb + (1.0 - lb) * _sigmoid(project(xn_m, P_F))
        g_m = _block_cumsum(jnp.log(f_m), n_meta)
        k_end = ((1.0 - f_m) * jnp.exp(g_m[n_meta - 1:n_meta, :] - g_m)).astype(BF16)
        v_m = project(xn_m, P_I).astype(BF16)
        for h, sl in enumerate(HEADS):
            st_meta_scr[h] = _dot_tn(k_end[:, sl], v_m[:, sl])
        u_m = project(xn_m, P_C) * project(xn_m, P_H)
        ctx_meta_scr[CTX_ROW0:SUBLANES, :] = u_m[n_meta - (CONV_W - 1):n_meta, :]
        dec_scr[...] = jnp.zeros(dec_scr.shape, F32)

    @pl.when(t == 0)
    def _init():
        st_scr[...] = st_meta_scr[...]
        u_scr[CTX_ROW0:SUBLANES, :] = ctx_meta_scr[CTX_ROW0:SUBLANES, :]

    x = x_ref[0]
    xn = (x * _rms_scale(x) * npre_ref[...]).astype(BF16)

    def proj(j):
        return project(xn, j)

    f = lb + (1.0 - lb) * _sigmoid(proj(P_F))
    k_scr[...] = 1.0 - f
    q_scr[...] = _silu(proj(P_Q))
    v_scr[...] = proj(P_I)
    g_scr[...] = _block_cumsum(jnp.log(f), C)

    g_floor = None
    for c in range(n_chunks):
        rows = slice(c * C, (c + 1) * C)
        gc = g_scr[rows, :]
        g_last = gc[C - 1:C, :]
        g_mid = gc[C // 2 - 1:C // 2, :]
        qc = q_scr[rows, :]
        kc = k_scr[rows, :]
        qt_scr[rows, :] = (qc * jnp.exp(gc - g_mid)).astype(BF16)
        kt_scr[rows, :] = (kc * jnp.exp(g_mid - gc)).astype(BF16)
        qin_scr[rows, :] = (qc * jnp.exp(gc)).astype(BF16)
        kst_scr[rows, :] = (kc * jnp.exp(g_last - gc)).astype(BF16)
        vb_scr[rows, :] = v_scr[rows, :].astype(BF16)
        dec_scr[c:c + 1, :] = jnp.exp(g_last)
        half_floor = jnp.minimum(g_mid, g_last - g_mid)
        g_floor = half_floor if g_floor is None else jnp.minimum(g_floor, half_floor)
    stable = jnp.min(g_floor) >= -MAX_HALF_CHUNK_LOG_DECAY

    dec_cols = dec_scr[...].T
    keep = jnp.logical_and(
        lax.broadcasted_iota(jnp.int32, (C, C), 0) >= lax.broadcasted_iota(jnp.int32, (C, C), 1), stable)
    for c in range(n_chunks):
        rows = slice(c * C, (c + 1) * C)
        scores = [jnp.where(keep, _dot_nt(qt_scr[rows, sl], kt_scr[rows, sl]), 0.0).astype(BF16)
                  for sl in HEADS]
        for h, sl in enumerate(HEADS):
            vb = vb_scr[rows, sl]
            st = st_scr[h]
            o_scr[rows, sl] = _dot(jnp.concatenate([qin_scr[rows, sl], scores[h]], axis=1),
                                   jnp.concatenate([st.astype(BF16), vb], axis=0))
            decay = jnp.broadcast_to(dec_cols[sl, c:c + 1], (HEAD_DIM, HEAD_DIM))
            st_scr[h] = decay * st + _dot_tn(kst_scr[rows, sl], vb)

    @pl.when(jnp.logical_not(stable))
    def _intra_exact():
        same_head = (lax.broadcasted_iota(jnp.int32, (D_MODEL, D_MODEL), 0) // HEAD_DIM ==
                     lax.broadcasted_iota(jnp.int32, (D_MODEL, D_MODEL), 1) // HEAD_DIM)
        head_sum = jnp.where(same_head, 1.0, 0.0).astype(BF16)
        tpos = lax.broadcasted_iota(jnp.int32, (C, 1), 0)

        def chunk_step(c, carry):
            r0 = pl.multiple_of(c * C, C)
            rows = pl.ds(r0, C)
            gc = g_scr[rows, :]
            qc = q_scr[rows, :]

            def src_step(i, carry2):
                src = pl.ds(r0 + i, 1)
                p = qc * jnp.exp(jnp.minimum(gc - g_scr[src, :], 0.0)) * k_scr[src, :]
                p = jnp.where(tpos >= i, p, 0.0).astype(BF16)
                o_scr[rows, :] += _dot(p, head_sum) * v_scr[src, :]
                return carry2

            return lax.fori_loop(0, C, src_step, carry)

        lax.fori_loop(0, n_chunks, chunk_step, 0)

    hn = hn_ref[...]
    for sl in HEADS:
        o_h = o_scr[:, sl]
        o_scr[:, sl] = o_h * _rms_scale(o_h) * hn
    y_a = _dot((o_scr[...] * _silu(proj(P_ZA))).astype(BF16), wa_ref[...])

    u = proj(P_C) * proj(P_H)
    u_scr[SUBLANES:SUBLANES + T, :] = u
    cw = cw_ref[...]
    conv = cw[CONV_W - 1:CONV_W, :] * u
    for j in range(CONV_W - 1):
        conv = conv + cw[j:j + 1, :] * u_scr[CTX_ROW0 + j:CTX_ROW0 + j + T, :]
    y_b = _dot((proj(P_B) * conv * _silu(proj(P_ZB))).astype(BF16), wb_ref[...])
    new_ctx = u[T - (CONV_W - 1):T, :]
    u_scr[CTX_ROW0:SUBLANES, :] = new_ctx

    merged = _sigmoid(proj(P_GA)) * y_a + _sigmoid(proj(P_GB)) * y_b
    out = _dot(merged.astype(BF16), wo_ref[...])
    y_ref[0] = x + out * _rms_scale(out) * npost_ref[...]

    @pl.when(t == pl.num_programs(1) - 1)
    def _finish():
        c_out_ref[0] = new_ctx
        st_out_ref[0] = st_scr[...]


def _const_spec(shape):
    zeros = (0,) * len(shape)
    return pl.BlockSpec(shape, lambda b, t: zeros, pipeline_mode=pl.Buffered(1))


def _run_prompt(x, meta, weights, *, tile, chunk):
    n, length, _ = x.shape
    assert length % tile == 0 and tile % chunk == 0 and chunk % (2 * SUBLANES) == 0 and tile >= CONV_W - 1
    assert meta.shape[0] % (2 * SUBLANES) == 0 and meta.shape[0] >= CONV_W - 1
    assert tile // chunk <= HEAD_DIM
    w_in, npre, npost, lb_logits, hn, cw, wa, wb, wo = weights
    kern = functools.partial(_seq_kernel, tile=tile, chunk=chunk)
    state_shape = (1, N_HEADS, HEAD_DIM, HEAD_DIM)
    ctx_shape = (1, CONV_W - 1, D_MODEL)
    tile_f32 = pltpu.VMEM((tile, D_MODEL), F32)
    tile_bf16 = pltpu.VMEM((tile, D_MODEL), BF16)
    return pl.pallas_call(
        kern,
        grid=(n, length // tile),
        in_specs=[
            pl.BlockSpec((1, tile, D_MODEL), lambda b, t: (b, t, 0)),
            _const_spec(meta.shape),
            _const_spec(w_in.shape),
            _const_spec(npre.shape),
            _const_spec(npost.shape),
            _const_spec(lb_logits.shape),
            _const_spec(hn.shape),
            _const_spec(cw.shape),
            _const_spec(wa.shape),
            _const_spec(wb.shape),
            _const_spec(wo.shape),
        ],
        out_specs=[
            pl.BlockSpec((1, tile, D_MODEL), lambda b, t: (b, t, 0)),
            pl.BlockSpec(state_shape, lambda b, t: (b, 0, 0, 0)),
            pl.BlockSpec(ctx_shape, lambda b, t: (b, 0, 0)),
        ],
        out_shape=[
            jax.ShapeDtypeStruct(x.shape, F32),
            jax.ShapeDtypeStruct((n,) + state_shape[1:], F32),
            jax.ShapeDtypeStruct((n,) + ctx_shape[1:], F32),
        ],
        scratch_shapes=[
            pltpu.VMEM((N_HEADS, HEAD_DIM, HEAD_DIM), F32),
            pltpu.VMEM((tile + SUBLANES, D_MODEL), F32),
            tile_f32, tile_f32, tile_f32, tile_f32, tile_f32,
            tile_bf16, tile_bf16, tile_bf16, tile_bf16, tile_bf16,
            pltpu.VMEM((HEAD_DIM, D_MODEL), F32),
            pltpu.VMEM((N_HEADS, HEAD_DIM, HEAD_DIM), F32),
            pltpu.VMEM((SUBLANES, D_MODEL), F32),
        ],
        compiler_params=pltpu.CompilerParams(
            dimension_semantics=("arbitrary", "arbitrary"),
            vmem_limit_bytes=V7X_VMEM_LIMIT_BYTES),
        name="prompt_sweep",
    )(x, meta, w_in, npre, npost, lb_logits, hn, cw, wa, wb, wo)


def _decode_kernel(x_ref, st_ref, ctx_ref, w_in_ref, npre_ref, npost_ref, lb_ref, hn_ref, cw_ref,
                   wa_ref, wb_ref, wo_ref,
                   y_ref, st_out_ref, ctx_out_ref,
                   ft_scr, q_scr, v_scr, o_scr, za_scr, pb_scr, ga_scr, gb_scr):
    G = DECODE_GROUP
    i = pl.program_id(0)
    n_rows = x_ref.shape[0]

    @pl.when(i == 0)
    def _project():
        x = x_ref[...]
        xn = (x * _rms_scale(x) * npre_ref[...]).astype(BF16)

        def proj(j):
            return _dot(xn, w_in_ref[:, j * D_MODEL:(j + 1) * D_MODEL])

        lb = _lower_bound(lb_ref[...])
        f = lb + (1.0 - lb) * _sigmoid(proj(P_F))
        ft_scr[...] = f.T
        q_scr[...] = _silu(proj(P_Q))
        v_scr[...] = proj(P_I)
        za_scr[...] = _silu(proj(P_ZA))
        u = proj(P_C) * proj(P_H)
        cw = cw_ref[...]
        ctx = ctx_ref[...]
        conv = cw[CONV_W - 1:CONV_W, :] * u
        for j in range(CONV_W - 1):
            conv = conv + cw[j:j + 1, :] * ctx[:, j * D_MODEL:(j + 1) * D_MODEL]
        pb_scr[...] = proj(P_B) * conv * _silu(proj(P_ZB))
        ctx_out_ref[:, 0:(CONV_W - 2) * D_MODEL] = ctx[:, D_MODEL:]
        ctx_out_ref[:, (CONV_W - 2) * D_MODEL:] = u
        ga_scr[...] = _sigmoid(proj(P_GA))
        gb_scr[...] = _sigmoid(proj(P_GB))

    shift = (n_rows - i * G) % n_rows
    f_cols = pltpu.roll(ft_scr[...], shift, 1)
    r0 = pl.multiple_of(i * G, G)
    v_rows = v_scr[pl.ds(r0, G), :]
    q_rows = q_scr[pl.ds(r0, G), :].astype(BF16)
    row_id = lax.broadcasted_iota(jnp.int32, (G, HEAD_DIM), 0)
    o_heads = [jnp.zeros((G, HEAD_DIM), F32)] * N_HEADS
    for j in range(G):
        for h, sl in enumerate(HEADS):
            f_b = jnp.broadcast_to(f_cols[sl, j:j + 1], (HEAD_DIM, HEAD_DIM))
            s_new = f_b * st_ref[j, h] + (1.0 - f_b) * v_rows[j:j + 1, sl]
            st_out_ref[j, h] = s_new
            read = _dot(q_rows[:, sl], s_new.astype(BF16))
            o_heads[h] = jnp.where(row_id == j, read, o_heads[h])
    o_scr[pl.ds(r0, G), :] = jnp.concatenate(o_heads, axis=1)

    @pl.when(i == pl.num_programs(0) - 1)
    def _output():
        hn = hn_ref[...]
        for sl in HEADS:
            o_h = o_scr[:, sl]
            o_scr[:, sl] = o_h * _rms_scale(o_h) * hn
        y_a = _dot((o_scr[...] * za_scr[...]).astype(BF16), wa_ref[...])
        y_b = _dot(pb_scr[...].astype(BF16), wb_ref[...])
        merged = ga_scr[...] * y_a + gb_scr[...] * y_b
        out = _dot(merged.astype(BF16), wo_ref[...])
        y_ref[...] = x_ref[...] + out * _rms_scale(out) * npost_ref[...]


def _run_decode(x, state, ctx, weights):
    n = x.shape[0]
    assert n % DECODE_GROUP == 0 and n == 128
    w_in, npre, npost, lb_logits, hn, cw, wa, wb, wo = weights

    def const(shape):
        zeros = (0,) * len(shape)
        return pl.BlockSpec(shape, lambda i: zeros, pipeline_mode=pl.Buffered(1))

    st_spec = pl.BlockSpec((DECODE_GROUP, N_HEADS, HEAD_DIM, HEAD_DIM), lambda i: (i, 0, 0, 0))
    rows_f32 = pltpu.VMEM((n, D_MODEL), F32)
    cols_f32 = pltpu.VMEM((D_MODEL, n), F32)
    return pl.pallas_call(
        _decode_kernel,
        grid=(n // DECODE_GROUP,),
        in_specs=[const(x.shape), st_spec, const(ctx.shape), const(w_in.shape), const(npre.shape),
                  const(npost.shape), const(lb_logits.shape), const(hn.shape), const(cw.shape),
                  const(wa.shape), const(wb.shape), const(wo.shape)],
        out_specs=[const(x.shape), st_spec, const(ctx.shape)],
        out_shape=[jax.ShapeDtypeStruct(x.shape, F32),
                   jax.ShapeDtypeStruct(state.shape, F32),
                   jax.ShapeDtypeStruct(ctx.shape, F32)],
        scratch_shapes=[cols_f32, rows_f32, rows_f32, rows_f32, rows_f32, rows_f32, rows_f32, rows_f32],
        compiler_params=pltpu.CompilerParams(
            dimension_semantics=("arbitrary",),
            vmem_limit_bytes=V7X_VMEM_LIMIT_BYTES),
        name="decode_step",
    )(x, state, ctx, w_in, npre, npost, lb_logits, hn, cw, wa, wb, wo)


def kernel(x_prompt, x_sample, state_hgrn, state_conv, meta_tokens, w_in, norm_pre, norm_post, lb_logits,
           hgrn_norm, conv_w, w_a, w_b, w_o):
    depth = w_in.shape[0]
    assert depth == 1, "single-layer trunk"
    batch, seq, _ = x_prompt.shape
    dec_batch, dec_seq, _ = x_sample.shape
    assert dec_seq == 1

    weights = (w_in[0].astype(BF16), norm_pre, norm_post, lb_logits, hgrn_norm, conv_w[0],
               w_a[0].astype(BF16), w_b[0].astype(BF16), w_o[0].astype(BF16))

    y_prompt, hgrn_p, conv_p = _run_prompt(
        x_prompt, meta_tokens.astype(x_prompt.dtype), weights, tile=PROMPT_TILE, chunk=PROMPT_CHUNK)

    y_s, hgrn_s, conv_s = _run_decode(
        x_sample.reshape(dec_batch, D_MODEL), state_hgrn[0],
        state_conv[0].reshape(dec_batch, (CONV_W - 1) * D_MODEL), weights)

    return (y_prompt, y_s.reshape(dec_batch, 1, D_MODEL), hgrn_p[None], hgrn_s[None],
            conv_p[None], conv_s.reshape(dec_batch, CONV_W - 1, D_MODEL)[None])
```

```python
import functools

import jax
import jax.numpy as jnp
from jax import lax
from jax.experimental import pallas as pl
from jax.experimental.pallas import tpu as pltpu

D_MODEL = 1024
N_HEADS = 8
HEAD_DIM = D_MODEL // N_HEADS
CONV_W = 3
EPS = 1e-6
P_Q, P_F, P_I, P_ZA, P_B, P_C, P_H, P_ZB, P_GA, P_GB = range(10)

PROMPT_TILE = 256
PROMPT_CHUNK = 128
DECODE_GROUP = 8
SUBLANES = 8
CTX_ROW0 = SUBLANES - (CONV_W - 1)
MAX_HALF_CHUNK_LOG_DECAY = 80.0
V7X_VMEM_LIMIT_BYTES = 58 * 1024 * 1024

BF16 = jnp.bfloat16
F32 = jnp.float32
HEADS = [slice(h * HEAD_DIM, (h + 1) * HEAD_DIM) for h in range(N_HEADS)]


def _dot(a, b):
    return jnp.dot(a, b, preferred_element_type=F32)


def _dot_nt(a, b):
    return lax.dot_general(a, b, (((1,), (1,)), ((), ())), preferred_element_type=F32)


def _dot_tn(a, b):
    return lax.dot_general(a, b, (((0,), (0,)), ((), ())), preferred_element_type=F32)


def _sigmoid(x):
    return 1.0 / (1.0 + jnp.exp(-x))


def _silu(x):
    return x * _sigmoid(x)


def _rms_scale(x):
    return lax.rsqrt(jnp.mean(x * x, axis=-1, keepdims=True) + EPS)


def _lower_bound(lb_logits):
    m = jnp.max(lb_logits, axis=0, keepdims=True)
    e = jnp.exp(lb_logits - m)
    return e[0:1, :] / jnp.sum(e, axis=0, keepdims=True)


def _block_cumsum(x, block):
    n = x.shape[0]
    ri = lax.broadcasted_iota(jnp.int32, (n, n), 0)
    ci = lax.broadcasted_iota(jnp.int32, (n, n), 1)
    tri = ri >= ci
    if n != block:
        tri = jnp.logical_and(tri, (ri // block) == (ci // block))
    tri = jnp.where(tri, 1.0, 0.0).astype(BF16)
    hi = x.astype(BF16)
    lo = (x - hi.astype(F32)).astype(BF16)
    return _dot(tri, hi) + _dot(tri, lo)


def _seq_kernel(x_ref, meta_ref, w_in_ref, npre_ref, npost_ref, lb_ref, hn_ref, cw_ref,
                wa_ref, wb_ref, wo_ref,
                y_ref, st_out_ref, c_out_ref,
                st_scr, u_scr, q_scr, k_scr, v_scr, g_scr, o_scr,
                qt_scr, kt_scr, qin_scr, kst_scr, vb_scr, dec_scr, st_meta_scr, ctx_meta_scr,
                *, tile, chunk):
    T, C = tile, chunk
    n_chunks = T // C
    t = pl.program_id(1)
    lb = _lower_bound(lb_ref[...])

    def project(xn, j):
        return _dot(xn, w_in_ref[:, j * D_MODEL:(j + 1) * D_MODEL])

    @pl.when(jnp.logical_and(pl.program_id(0) == 0, t == 0))
    def _meta_prefix():
        xm = meta_ref[...]
        n_meta = xm.shape[0]
        xn_m = (xm * _rms_scale(xm) * npre_ref[...]).astype(BF16)
        f_m = lb + (1.0 - lb) * _sigmoid(project(xn_m, P_F))
        g_m = _block_cumsum(jnp.log(f_m), n_meta)
        k_end = ((1.0 - f_m) * jnp.exp(g_m[n_meta - 1:n_meta, :] - g_m)).astype(BF16)
        v_m = project(xn_m, P_I).astype(BF16)
        for h, sl in enumerate(HEADS):
            st_meta_scr[h] = _dot_tn(k_end[:, sl], v_m[:, sl])
        u_m = project(xn_m, P_C) * project(xn_m, P_H)
        ctx_meta_scr[CTX_ROW0:SUBLANES, :] = u_m[n_meta - (CONV_W - 1):n_meta, :]
        dec_scr[...] = jnp.zeros(dec_scr.shape, F32)

    @pl.when(t == 0)
    def _init():
        st_scr[...] = st_meta_scr[...]
        u_scr[CTX_ROW0:SUBLANES, :] = ctx_meta_scr[CTX_ROW0:SUBLANES, :]

    x = x_ref[0]
    xn = (x * _rms_scale(x) * npre_ref[...]).astype(BF16)

    def proj(j):
        return project(xn, j)

    f = lb + (1.0 - lb) * _sigmoid(proj(P_F))
    k_scr[...] = 1.0 - f
    q_scr[...] = _silu(proj(P_Q))
    v_scr[...] = proj(P_I)
    g_scr[...] = _block_cumsum(jnp.log(f), C)

    g_floor = None
    for c in range(n_chunks):
        rows = slice(c * C, (c + 1) * C)
        gc = g_scr[rows, :]
        g_last = gc[C - 1:C, :]
        g_mid = gc[C // 2 - 1:C // 2, :]
        qc = q_scr[rows, :]
        kc = k_scr[rows, :]
        qt_scr[rows, :] = (qc * jnp.exp(gc - g_mid)).astype(BF16)
        kt_scr[rows, :] = (kc * jnp.exp(g_mid - gc)).astype(BF16)
        qin_scr[rows, :] = (qc * jnp.exp(gc)).astype(BF16)
        kst_scr[rows, :] = (kc * jnp.exp(g_last - gc)).astype(BF16)
        vb_scr[rows, :] = v_scr[rows, :].astype(BF16)
        dec_scr[c:c + 1, :] = jnp.exp(g_last)
        half_floor = jnp.minimum(g_mid, g_last - g_mid)
        g_floor = half_floor if g_floor is None else jnp.minimum(g_floor, half_floor)
    stable = jnp.min(g_floor) >= -MAX_HALF_CHUNK_LOG_DECAY

    dec_cols = dec_scr[...].T
    keep = jnp.logical_and(
        lax.broadcasted_iota(jnp.int32, (C, C), 0) >= lax.broadcasted_iota(jnp.int32, (C, C), 1), stable)
    for c in range(n_chunks):
        rows = slice(c * C, (c + 1) * C)
        scores = [jnp.where(keep, _dot_nt(qt_scr[rows, sl], kt_scr[rows, sl]), 0.0).astype(BF16)
                  for sl in HEADS]
        for h, sl in enumerate(HEADS):
            vb = vb_scr[rows, sl]
            st = st_scr[h]
            o_scr[rows, sl] = _dot(jnp.concatenate([qin_scr[rows, sl], scores[h]], axis=1),
                                   jnp.concatenate([st.astype(BF16), vb], axis=0))
            decay = jnp.broadcast_to(dec_cols[sl, c:c + 1], (HEAD_DIM, HEAD_DIM))
            st_scr[h] = decay * st + _dot_tn(kst_scr[rows, sl], vb)

    @pl.when(jnp.logical_not(stable))
    def _intra_exact():
        same_head = (lax.broadcasted_iota(jnp.int32, (D_MODEL, D_MODEL), 0) // HEAD_DIM ==
                     lax.broadcasted_iota(jnp.int32, (D_MODEL, D_MODEL), 1) // HEAD_DIM)
        head_sum = jnp.where(same_head, 1.0, 0.0).astype(BF16)
        tpos = lax.broadcasted_iota(jnp.int32, (C, 1), 0)

        def chunk_step(c, carry):
            r0 = pl.multiple_of(c * C, C)
            rows = pl.ds(r0, C)
            gc = g_scr[rows, :]
            qc = q_scr[rows, :]

            def src_step(i, carry2):
                src = pl.ds(r0 + i, 1)
                p = qc * jnp.exp(jnp.minimum(gc - g_scr[src, :], 0.0)) * k_scr[src, :]
                p = jnp.where(tpos >= i, p, 0.0).astype(BF16)
                o_scr[rows, :] += _dot(p, head_sum) * v_scr[src, :]
                return carry2

            return lax.fori_loop(0, C, src_step, carry)

        lax.fori_loop(0, n_chunks, chunk_step, 0)

    hn = hn_ref[...]
    for sl in HEADS:
        o_h = o_scr[:, sl]
        o_scr[:, sl] = o_h * _rms_scale(o_h) * hn
    y_a = _dot((o_scr[...] * _silu(proj(P_ZA))).astype(BF16), wa_ref[...])

    u = proj(P_C) * proj(P_H)
    u_scr[SUBLANES:SUBLANES + T, :] = u
    cw = cw_ref[...]
    conv = cw[CONV_W - 1:CONV_W, :] * u
    for j in range(CONV_W - 1):
        conv = conv + cw[j:j + 1, :] * u_scr[CTX_ROW0 + j:CTX_ROW0 + j + T, :]
    y_b = _dot((proj(P_B) * conv * _silu(proj(P_ZB))).astype(BF16), wb_ref[...])
    new_ctx = u[T - (CONV_W - 1):T, :]
    u_scr[CTX_ROW0:SUBLANES, :] = new_ctx

    merged = _sigmoid(proj(P_GA)) * y_a + _sigmoid(proj(P_GB)) * y_b
    out = _dot(merged.astype(BF16), wo_ref[...])
    y_ref[0] = x + out * _rms_scale(out) * npost_ref[...]

    @pl.when(t == pl.num_programs(1) - 1)
    def _finish():
        c_out_ref[0] = new_ctx
        st_out_ref[0] = st_scr[...]


def _const_spec(shape):
    zeros = (0,) * len(shape)
    return pl.BlockSpec(shape, lambda b, t: zeros, pipeline_mode=pl.Buffered(1))


def _run_prompt(x, meta, weights, *, tile, chunk):
    n, length, _ = x.shape
    assert length % tile == 0 and tile % chunk == 0 and chunk % (2 * SUBLANES) == 0 and tile >= CONV_W - 1
    assert meta.shape[0] % (2 * SUBLANES) == 0 and meta.shape[0] >= CONV_W - 1
    assert tile // chunk <= HEAD_DIM
    w_in, npre, npost, lb_logits, hn, cw, wa, wb, wo = weights
    kern = functools.partial(_seq_kernel, tile=tile, chunk=chunk)
    state_shape = (1, N_HEADS, HEAD_DIM, HEAD_DIM)
    ctx_shape = (1, CONV_W - 1, D_MODEL)
    tile_f32 = pltpu.VMEM((tile, D_MODEL), F32)
    tile_bf16 = pltpu.VMEM((tile, D_MODEL), BF16)
    return pl.pallas_call(
        kern,
        grid=(n, length // tile),
        in_specs=[
            pl.BlockSpec((1, tile, D_MODEL), lambda b, t: (b, t, 0)),
            _const_spec(meta.shape),
            _const_spec(w_in.shape),
            _const_spec(npre.shape),
            _const_spec(npost.shape),
            _const_spec(lb_logits.shape),
            _const_spec(hn.shape),
            _const_spec(cw.shape),
            _const_spec(wa.shape),
            _const_spec(wb.shape),
            _const_spec(wo.shape),
        ],
        out_specs=[
            pl.BlockSpec((1, tile, D_MODEL), lambda b, t: (b, t, 0)),
            pl.BlockSpec(state_shape, lambda b, t: (b, 0, 0, 0)),
            pl.BlockSpec(ctx_shape, lambda b, t: (b, 0, 0)),
        ],
        out_shape=[
            jax.ShapeDtypeStruct(x.shape, F32),
            jax.ShapeDtypeStruct((n,) + state_shape[1:], F32),
            jax.ShapeDtypeStruct((n,) + ctx_shape[1:], F32),
        ],
        scratch_shapes=[
            pltpu.VMEM((N_HEADS, HEAD_DIM, HEAD_DIM), F32),
            pltpu.VMEM((tile + SUBLANES, D_MODEL), F32),
            tile_f32, tile_f32, tile_f32, tile_f32, tile_f32,
            tile_bf16, tile_bf16, tile_bf16, tile_bf16, tile_bf16,
            pltpu.VMEM((HEAD_DIM, D_MODEL), F32),
            pltpu.VMEM((N_HEADS, HEAD_DIM, HEAD_DIM), F32),
            pltpu.VMEM((SUBLANES, D_MODEL), F32),
        ],
        compiler_params=pltpu.CompilerParams(
            dimension_semantics=("arbitrary", "arbitrary"),
            vmem_limit_bytes=V7X_VMEM_LIMIT_BYTES),
        name="prompt_sweep",
    )(x, meta, w_in, npre, npost, lb_logits, hn, cw, wa, wb, wo)


def _decode_kernel(x_ref, st_ref, ctx_ref, w_in_ref, npre_ref, npost_ref, lb_ref, hn_ref, cw_ref,
                   wa_ref, wb_ref, wo_ref,
                   y_ref, st_out_ref, ctx_out_ref,
                   ft_scr, q_scr, v_scr, o_scr, za_scr, pb_scr, ga_scr, gb_scr):
    G = DECODE_GROUP
    i = pl.program_id(0)
    n_rows = x_ref.shape[0]

    @pl.when(i == 0)
    def _project():
        x = x_ref[...]
        xn = (x * _rms_scale(x) * npre_ref[...]).astype(BF16)

        def proj(j):
            return _dot(xn, w_in_ref[:, j * D_MODEL:(j + 1) * D_MODEL])

        lb = _lower_bound(lb_ref[...])
        f = lb + (1.0 - lb) * _sigmoid(proj(P_F))
        ft_scr[...] = f.T
        q_scr[...] = _silu(proj(P_Q))
        v_scr[...] = proj(P_I)
        za_scr[...] = _silu(proj(P_ZA))
        u = proj(P_C) * proj(P_H)
        cw = cw_ref[...]
        ctx = ctx_ref[...]
        conv = cw[CONV_W - 1:CONV_W, :] * u
        for j in range(CONV_W - 1):
            conv = conv + cw[j:j + 1, :] * ctx[:, j * D_MODEL:(j + 1) * D_MODEL]
        pb_scr[...] = proj(P_B) * conv * _silu(proj(P_ZB))
        ctx_out_ref[:, 0:(CONV_W - 2) * D_MODEL] = ctx[:, D_MODEL:]
        ctx_out_ref[:, (CONV_W - 2) * D_MODEL:] = u
        ga_scr[...] = _sigmoid(proj(P_GA))
        gb_scr[...] = _sigmoid(proj(P_GB))

    shift = (n_rows - i * G) % n_rows
    f_cols = pltpu.roll(ft_scr[...], shift, 1)
    r0 = pl.multiple_of(i * G, G)
    v_rows = v_scr[pl.ds(r0, G), :]
    q_rows = q_scr[pl.ds(r0, G), :].astype(BF16)
    row_id = lax.broadcasted_iota(jnp.int32, (G, HEAD_DIM), 0)
    o_heads = [jnp.zeros((G, HEAD_DIM), F32)] * N_HEADS
    for j in range(G):
        for h, sl in enumerate(HEADS):
            f_b = jnp.broadcast_to(f_cols[sl, j:j + 1], (HEAD_DIM, HEAD_DIM))
            s_new = f_b * st_ref[j, h] + (1.0 - f_b) * v_rows[j:j + 1, sl]
            st_out_ref[j, h] = s_new
            read = _dot(q_rows[:, sl], s_new.astype(BF16))
            o_heads[h] = jnp.where(row_id == j, read, o_heads[h])
    o_scr[pl.ds(r0, G), :] = jnp.concatenate(o_heads, axis=1)

    @pl.when(i == pl.num_programs(0) - 1)
    def _output():
        hn = hn_ref[...]
        for sl in HEADS:
            o_h = o_scr[:, sl]
            o_scr[:, sl] = o_h * _rms_scale(o_h) * hn
        y_a = _dot((o_scr[...] * za_scr[...]).astype(BF16), wa_ref[...])
        y_b = _dot(pb_scr[...].astype(BF16), wb_ref[...])
        merged = ga_scr[...] * y_a + gb_scr[...] * y_b
        out = _dot(merged.astype(BF16), wo_ref[...])
        y_ref[...] = x_ref[...] + out * _rms_scale(out) * npost_ref[...]


def _run_decode(x, state, ctx, weights):
    n = x.shape[0]
    assert n % DECODE_GROUP == 0 and n == 128
    w_in, npre, npost, lb_logits, hn, cw, wa, wb, wo = weights

    def const(shape):
        zeros = (0,) * len(shape)
        return pl.BlockSpec(shape, lambda i: zeros, pipeline_mode=pl.Buffered(1))

    st_spec = pl.BlockSpec((DECODE_GROUP, N_HEADS, HEAD_DIM, HEAD_DIM), lambda i: (i, 0, 0, 0))
    rows_f32 = pltpu.VMEM((n, D_MODEL), F32)
    cols_f32 = pltpu.VMEM((D_MODEL, n), F32)
    return pl.pallas_call(
        _decode_kernel,
        grid=(n // DECODE_GROUP,),
        in_specs=[const(x.shape), st_spec, const(ctx.shape), const(w_in.shape), const(npre.shape),
                  const(npost.shape), const(lb_logits.shape), const(hn.shape), const(cw.shape),
                  const(wa.shape), const(wb.shape), const(wo.shape)],
        out_specs=[const(x.shape), st_spec, const(ctx.shape)],
        out_shape=[jax.ShapeDtypeStruct(x.shape, F32),
                   jax.ShapeDtypeStruct(state.shape, F32),
                   jax.ShapeDtypeStruct(ctx.shape, F32)],
        scratch_shapes=[cols_f32, rows_f32, rows_f32, rows_f32, rows_f32, rows_f32, rows_f32, rows_f32],
        compiler_params=pltpu.CompilerParams(
            dimension_semantics=("arbitrary",),
            vmem_limit_bytes=V7X_VMEM_LIMIT_BYTES),
        name="decode_step",
    )(x, state, ctx, w_in, npre, npost, lb_logits, hn, cw, wa, wb, wo)


def kernel(x_prompt, x_sample, state_hgrn, state_conv, meta_tokens, w_in, norm_pre, norm_post, lb_logits,
           hgrn_norm, conv_w, w_a, w_b, w_o):
    depth = w_in.shape[0]
    assert depth == 1, "single-layer trunk"
    batch, seq, _ = x_prompt.shape
    dec_batch, dec_seq, _ = x_sample.shape
    assert dec_seq == 1

    weights = (w_in[0].astype(BF16), norm_pre, norm_post, lb_logits, hgrn_norm, conv_w[0],
               w_a[0].astype(BF16), w_b[0].astype(BF16), w_o[0].astype(BF16))

    y_prompt, hgrn_p, conv_p = _run_prompt(
        x_prompt, meta_tokens.astype(x_prompt.dtype), weights, tile=PROMPT_TILE, chunk=PROMPT_CHUNK)

    y_s, hgrn_s, conv_s = _run_decode(
        x_sample.reshape(dec_batch, D_MODEL), state_hgrn[0],
        state_conv[0].reshape(dec_batch, (CONV_W - 1) * D_MODEL), weights)

    return (y_prompt, y_s.reshape(dec_batch, 1, D_MODEL), hgrn_p[None], hgrn_s[None],
            conv_p[None], conv_s.reshape(dec_batch, CONV_W - 1, D_MODEL)[None])
```

```python
import functools

import jax
import jax.numpy as jnp
from jax import lax
from jax.experimental import pallas as pl
from jax.experimental.pallas import tpu as pltpu

D_MODEL = 1024
N_HEADS = 8
HEAD_DIM = D_MODEL // N_HEADS
CONV_W = 3
EPS = 1e-6
P_Q, P_F, P_I, P_ZA, P_B, P_C, P_H, P_ZB, P_GA, P_GB = range(10)

PROMPT_TILE = 256
PROMPT_CHUNK = 128
DECODE_GROUP = 8
SUBLANES = 8
CTX_ROW0 = SUBLANES - (CONV_W - 1)
MAX_HALF_CHUNK_LOG_DECAY = 80.0
V7X_VMEM_LIMIT_BYTES = 58 * 1024 * 1024

BF16 = jnp.bfloat16
F32 = jnp.float32
HEADS = [slice(h * HEAD_DIM, (h + 1) * HEAD_DIM) for h in range(N_HEADS)]


def _dot(a, b):
    return jnp.dot(a, b, preferred_element_type=F32)


def _dot_nt(a, b):
    return lax.dot_general(a, b, (((1,), (1,)), ((), ())), preferred_element_type=F32)


def _dot_tn(a, b):
    return lax.dot_general(a, b, (((0,), (0,)), ((), ())), preferred_element_type=F32)


def _sigmoid(x):
    return 1.0 / (1.0 + jnp.exp(-x))


def _silu(x):
    return x * _sigmoid(x)


def _rms_scale(x):
    return lax.rsqrt(jnp.mean(x * x, axis=-1, keepdims=True) + EPS)


def _lower_bound(lb_logits):
    m = jnp.max(lb_logits, axis=0, keepdims=True)
    e = jnp.exp(lb_logits - m)
    return e[0:1, :] / jnp.sum(e, axis=0, keepdims=True)


def _block_cumsum(x, block):
    n = x.shape[0]
    ri = lax.broadcasted_iota(jnp.int32, (n, n), 0)
    ci = lax.broadcasted_iota(jnp.int32, (n, n), 1)
    tri = ri >= ci
    if n != block:
        tri = jnp.logical_and(tri, (ri // block) == (ci // block))
    tri = jnp.where(tri, 1.0, 0.0).astype(BF16)
    hi = x.astype(BF16)
    lo = (x - hi.astype(F32)).astype(BF16)
    return _dot(tri, hi) + _dot(tri, lo)


def _seq_kernel(x_ref, meta_ref, w_in_hbm, npre_ref, npost_ref, lb_ref, hn_ref, cw_ref,
                wa_hbm, wb_hbm, wo_hbm,
                y_ref, st_out_ref, c_out_ref, w_in_out, wa_out, wb_out, wo_out,
                st_scr, u_scr, q_scr, k_scr, v_scr, g_scr, o_scr,
                qt_scr, kt_scr, qin_scr, kst_scr, vb_scr, dec_scr, st_meta_scr, ctx_meta_scr,
                w_in_ref, wa_ref, wb_ref, wo_ref, stage_scr, load_sem, store_sem,
                *, tile, chunk):
    T, C = tile, chunk
    n_chunks = T // C
    t = pl.program_id(1)
    first_step = jnp.logical_and(pl.program_id(0) == 0, t == 0)
    last_step = jnp.logical_and(pl.program_id(0) == pl.num_programs(0) - 1, t == pl.num_programs(1) - 1)
    lb = _lower_bound(lb_ref[...])

    col = [pl.ds(j * D_MODEL, D_MODEL) for j in range(w_in_hbm.shape[1] // D_MODEL)]
    groups = [(w_in_hbm.at[:, c], w_in_ref.at[:, c], w_in_out.at[:, c]) for c in col]
    groups += [(wa_hbm, wa_ref, wa_out), (wb_hbm, wb_ref, wb_out), (wo_hbm, wo_ref, wo_out)]

    def load_copy(i):
        return pltpu.make_async_copy(groups[i][0], stage_scr.at[i % 2], load_sem.at[i % 2])

    def store_copy(i):
        return pltpu.make_async_copy(groups[i][1], groups[i][2], store_sem.at[i])

    @pl.when(first_step)
    def _stream_weights():
        load_copy(0).start()
        for i in range(len(groups)):
            if i + 1 < len(groups):
                load_copy(i + 1).start()
            load_copy(i).wait()
            groups[i][1][...] = stage_scr[i % 2].astype(BF16)
            store_copy(i).start()

    def project(xn, j):
        return _dot(xn, w_in_ref[:, j * D_MODEL:(j + 1) * D_MODEL])

    @pl.when(first_step)
    def _meta_prefix():
        xm = meta_ref[...]
        n_meta = xm.shape[0]
        xn_m = (xm * _rms_scale(xm) * npre_ref[...]).astype(BF16)
        f_m = lb + (1.0 - lb) * _sigmoid(project(xn_m, P_F))
        g_m = _block_cumsum(jnp.log(f_m), n_meta)
        k_end = ((1.0 - f_m) * jnp.exp(g_m[n_meta - 1:n_meta, :] - g_m)).astype(BF16)
        v_m = project(xn_m, P_I).astype(BF16)
        for h, sl in enumerate(HEADS):
            st_meta_scr[h] = _dot_tn(k_end[:, sl], v_m[:, sl])
        u_m = project(xn_m, P_C) * project(xn_m, P_H)
        ctx_meta_scr[CTX_ROW0:SUBLANES, :] = u_m[n_meta - (CONV_W - 1):n_meta, :]
        dec_scr[...] = jnp.zeros(dec_scr.shape, F32)

    @pl.when(t == 0)
    def _init():
        st_scr[...] = st_meta_scr[...]
        u_scr[CTX_ROW0:SUBLANES, :] = ctx_meta_scr[CTX_ROW0:SUBLANES, :]

    x = x_ref[0]
    xn = (x * _rms_scale(x) * npre_ref[...]).astype(BF16)

    def proj(j):
        return project(xn, j)

    f = lb + (1.0 - lb) * _sigmoid(proj(P_F))
    k_scr[...] = 1.0 - f
    q_scr[...] = _silu(proj(P_Q))
    v_scr[...] = proj(P_I)
    g_scr[...] = _block_cumsum(jnp.log(f), C)

    g_floor = None
    for c in range(n_chunks):
        rows = slice(c * C, (c + 1) * C)
        gc = g_scr[rows, :]
        g_last = gc[C - 1:C, :]
        g_mid = gc[C // 2 - 1:C // 2, :]
        qc = q_scr[rows, :]
        kc = k_scr[rows, :]
        qt_scr[rows, :] = (qc * jnp.exp(gc - g_mid)).astype(BF16)
        kt_scr[rows, :] = (kc * jnp.exp(g_mid - gc)).astype(BF16)
        qin_scr[rows, :] = (qc * jnp.exp(gc)).astype(BF16)
        kst_scr[rows, :] = (kc * jnp.exp(g_last - gc)).astype(BF16)
        vb_scr[rows, :] = v_scr[rows, :].astype(BF16)
        dec_scr[c:c + 1, :] = jnp.exp(g_last)
        half_floor = jnp.minimum(g_mid, g_last - g_mid)
        g_floor = half_floor if g_floor is None else jnp.minimum(g_floor, half_floor)
    stable = jnp.min(g_floor) >= -MAX_HALF_CHUNK_LOG_DECAY

    dec_cols = dec_scr[...].T
    keep = jnp.logical_and(
        lax.broadcasted_iota(jnp.int32, (C, C), 0) >= lax.broadcasted_iota(jnp.int32, (C, C), 1), stable)
    for c in range(n_chunks):
        rows = slice(c * C, (c + 1) * C)
        scores = [jnp.where(keep, _dot_nt(qt_scr[rows, sl], kt_scr[rows, sl]), 0.0).astype(BF16)
                  for sl in HEADS]
        for h, sl in enumerate(HEADS):
            vb = vb_scr[rows, sl]
            st = st_scr[h]
            o_scr[rows, sl] = _dot(jnp.concatenate([qin_scr[rows, sl], scores[h]], axis=1),
                                   jnp.concatenate([st.astype(BF16), vb], axis=0))
            decay = jnp.broadcast_to(dec_cols[sl, c:c + 1], (HEAD_DIM, HEAD_DIM))
            st_scr[h] = decay * st + _dot_tn(kst_scr[rows, sl], vb)

    @pl.when(jnp.logical_not(stable))
    def _intra_exact():
        same_head = (lax.broadcasted_iota(jnp.int32, (D_MODEL, D_MODEL), 0) // HEAD_DIM ==
                     lax.broadcasted_iota(jnp.int32, (D_MODEL, D_MODEL), 1) // HEAD_DIM)
        head_sum = jnp.where(same_head, 1.0, 0.0).astype(BF16)
        tpos = lax.broadcasted_iota(jnp.int32, (C, 1), 0)

        def chunk_step(c, carry):
            r0 = pl.multiple_of(c * C, C)
            rows = pl.ds(r0, C)
            gc = g_scr[rows, :]
            qc = q_scr[rows, :]

            def src_step(i, carry2):
                src = pl.ds(r0 + i, 1)
                p = qc * jnp.exp(jnp.minimum(gc - g_scr[src, :], 0.0)) * k_scr[src, :]
                p = jnp.where(tpos >= i, p, 0.0).astype(BF16)
                o_scr[rows, :] += _dot(p, head_sum) * v_scr[src, :]
                return carry2

            return lax.fori_loop(0, C, src_step, carry)

        lax.fori_loop(0, n_chunks, chunk_step, 0)

    hn = hn_ref[...]
    for sl in HEADS:
        o_h = o_scr[:, sl]
        o_scr[:, sl] = o_h * _rms_scale(o_h) * hn
    y_a = _dot((o_scr[...] * _silu(proj(P_ZA))).astype(BF16), wa_ref[...])

    u = proj(P_C) * proj(P_H)
    u_scr[SUBLANES:SUBLANES + T, :] = u
    cw = cw_ref[...]
    conv = cw[CONV_W - 1:CONV_W, :] * u
    for j in range(CONV_W - 1):
        conv = conv + cw[j:j + 1, :] * u_scr[CTX_ROW0 + j:CTX_ROW0 + j + T, :]
    y_b = _dot((proj(P_B) * conv * _silu(proj(P_ZB))).astype(BF16), wb_ref[...])
    new_ctx = u[T - (CONV_W - 1):T, :]
    u_scr[CTX_ROW0:SUBLANES, :] = new_ctx

    merged = _sigmoid(proj(P_GA)) * y_a + _sigmoid(proj(P_GB)) * y_b
    out = _dot(merged.astype(BF16), wo_ref[...])
    y_ref[0] = x + out * _rms_scale(out) * npost_ref[...]

    @pl.when(t == pl.num_programs(1) - 1)
    def _finish():
        c_out_ref[0] = new_ctx
        st_out_ref[0] = st_scr[...]

    @pl.when(last_step)
    def _weights_written():
        for i in range(len(groups)):
            store_copy(i).wait()


def _const_spec(shape):
    zeros = (0,) * len(shape)
    return pl.BlockSpec(shape, lambda b, t: zeros, pipeline_mode=pl.Buffered(1))


def _run_prompt(x, meta, weights, *, tile, chunk):
    n, length, _ = x.shape
    assert length % tile == 0 and tile % chunk == 0 and chunk % (2 * SUBLANES) == 0 and tile >= CONV_W - 1
    assert meta.shape[0] % (2 * SUBLANES) == 0 and meta.shape[0] >= CONV_W - 1
    assert tile // chunk <= HEAD_DIM
    w_in, npre, npost, lb_logits, hn, cw, wa, wb, wo = weights
    kern = functools.partial(_seq_kernel, tile=tile, chunk=chunk)
    hbm = pl.BlockSpec(memory_space=pl.ANY)
    n_groups = w_in.shape[1] // D_MODEL + 3
    assert w_in.shape[1] % D_MODEL == 0 and wa.shape == wb.shape == wo.shape == (w_in.shape[0], D_MODEL)
    state_shape = (1, N_HEADS, HEAD_DIM, HEAD_DIM)
    ctx_shape = (1, CONV_W - 1, D_MODEL)
    tile_f32 = pltpu.VMEM((tile, D_MODEL), F32)
    tile_bf16 = pltpu.VMEM((tile, D_MODEL), BF16)
    return pl.pallas_call(
        kern,
        grid=(n, length // tile),
        in_specs=[
            pl.BlockSpec((1, tile, D_MODEL), lambda b, t: (b, t, 0)),
            _const_spec(meta.shape),
            hbm,
            _const_spec(npre.shape),
            _const_spec(npost.shape),
            _const_spec(lb_logits.shape),
            _const_spec(hn.shape),
            _const_spec(cw.shape),
            hbm, hbm, hbm,
        ],
        out_specs=[
            pl.BlockSpec((1, tile, D_MODEL), lambda b, t: (b, t, 0)),
            pl.BlockSpec(state_shape, lambda b, t: (b, 0, 0, 0)),
            pl.BlockSpec(ctx_shape, lambda b, t: (b, 0, 0)),
            hbm, hbm, hbm, hbm,
        ],
        out_shape=[
            jax.ShapeDtypeStruct(x.shape, F32),
            jax.ShapeDtypeStruct((n,) + state_shape[1:], F32),
            jax.ShapeDtypeStruct((n,) + ctx_shape[1:], F32),
            jax.ShapeDtypeStruct(w_in.shape, BF16),
            jax.ShapeDtypeStruct(wa.shape, BF16),
            jax.ShapeDtypeStruct(wb.shape, BF16),
            jax.ShapeDtypeStruct(wo.shape, BF16),
        ],
        scratch_shapes=[
            pltpu.VMEM((N_HEADS, HEAD_DIM, HEAD_DIM), F32),
            pltpu.VMEM((tile + SUBLANES, D_MODEL), F32),
            tile_f32, tile_f32, tile_f32, tile_f32, tile_f32,
            tile_bf16, tile_bf16, tile_bf16, tile_bf16, tile_bf16,
            pltpu.VMEM((HEAD_DIM, D_MODEL), F32),
            pltpu.VMEM((N_HEADS, HEAD_DIM, HEAD_DIM), F32),
            pltpu.VMEM((SUBLANES, D_MODEL), F32),
            pltpu.VMEM(w_in.shape, BF16),
            pltpu.VMEM(wa.shape, BF16), pltpu.VMEM(wb.shape, BF16), pltpu.VMEM(wo.shape, BF16),
            pltpu.VMEM((2,) + wa.shape, F32),
            pltpu.SemaphoreType.DMA((2,)),
            pltpu.SemaphoreType.DMA((n_groups,)),
        ],
        compiler_params=pltpu.CompilerParams(
            dimension_semantics=("arbitrary", "arbitrary"),
            vmem_limit_bytes=V7X_VMEM_LIMIT_BYTES),
        name="prompt_sweep",
    )(x, meta, w_in, npre, npost, lb_logits, hn, cw, wa, wb, wo)


def _decode_kernel(x_ref, st_ref, ctx_ref, w_in_ref, npre_ref, npost_ref, lb_ref, hn_ref, cw_ref,
                   wa_ref, wb_ref, wo_ref,
                   y_ref, st_out_ref, ctx_out_ref,
                   ft_scr, q_scr, v_scr, o_scr, za_scr, pb_scr, ga_scr, gb_scr):
    G = DECODE_GROUP
    i = pl.program_id(0)
    n_rows = x_ref.shape[0]

    @pl.when(i == 0)
    def _project():
        x = x_ref[...]
        xn = (x * _rms_scale(x) * npre_ref[...]).astype(BF16)

        def proj(j):
            return _dot(xn, w_in_ref[:, j * D_MODEL:(j + 1) * D_MODEL])

        lb = _lower_bound(lb_ref[...])
        f = lb + (1.0 - lb) * _sigmoid(proj(P_F))
        ft_scr[...] = f.T
        q_scr[...] = _silu(proj(P_Q))
        v_scr[...] = proj(P_I)
        za_scr[...] = _silu(proj(P_ZA))
        u = proj(P_C) * proj(P_H)
        cw = cw_ref[...]
        ctx = ctx_ref[...]
        conv = cw[CONV_W - 1:CONV_W, :] * u
        for j in range(CONV_W - 1):
            conv = conv + cw[j:j + 1, :] * ctx[:, j * D_MODEL:(j + 1) * D_MODEL]
        pb_scr[...] = proj(P_B) * conv * _silu(proj(P_ZB))
        ctx_out_ref[:, 0:(CONV_W - 2) * D_MODEL] = ctx[:, D_MODEL:]
        ctx_out_ref[:, (CONV_W - 2) * D_MODEL:] = u
        ga_scr[...] = _sigmoid(proj(P_GA))
        gb_scr[...] = _sigmoid(proj(P_GB))

    shift = (n_rows - i * G) % n_rows
    f_cols = pltpu.roll(ft_scr[...], shift, 1)
    r0 = pl.multiple_of(i * G, G)
    v_rows = v_scr[pl.ds(r0, G), :]
    q_rows = q_scr[pl.ds(r0, G), :].astype(BF16)
    row_id = lax.broadcasted_iota(jnp.int32, (G, HEAD_DIM), 0)
    o_heads = [jnp.zeros((G, HEAD_DIM), F32)] * N_HEADS
    for j in range(G):
        for h, sl in enumerate(HEADS):
            f_b = jnp.broadcast_to(f_cols[sl, j:j + 1], (HEAD_DIM, HEAD_DIM))
            s_new = f_b * st_ref[j, h] + (1.0 - f_b) * v_rows[j:j + 1, sl]
            st_out_ref[j, h] = s_new
            read = _dot(q_rows[:, sl], s_new.astype(BF16))
            o_heads[h] = jnp.where(row_id == j, read, o_heads[h])
    o_scr[pl.ds(r0, G), :] = jnp.concatenate(o_heads, axis=1)

    @pl.when(i == pl.num_programs(0) - 1)
    def _output():
        hn = hn_ref[...]
        for sl in HEADS:
            o_h = o_scr[:, sl]
            o_scr[:, sl] = o_h * _rms_scale(o_h) * hn
        y_a = _dot((o_scr[...] * za_scr[...]).astype(BF16), wa_ref[...])
        y_b = _dot(pb_scr[...].astype(BF16), wb_ref[...])
        merged = ga_scr[...] * y_a + gb_scr[...] * y_b
        out = _dot(merged.astype(BF16), wo_ref[...])
        y_ref[...] = x_ref[...] + out * _rms_scale(out) * npost_ref[...]


def _run_decode(x, state, ctx, weights):
    n = x.shape[0]
    assert n % DECODE_GROUP == 0 and n == 128
    w_in, npre, npost, lb_logits, hn, cw, wa, wb, wo = weights

    def const(shape):
        zeros = (0,) * len(shape)
        return pl.BlockSpec(shape, lambda i: zeros, pipeline_mode=pl.Buffered(1))

    st_spec = pl.BlockSpec((DECODE_GROUP, N_HEADS, HEAD_DIM, HEAD_DIM), lambda i: (i, 0, 0, 0))
    rows_f32 = pltpu.VMEM((n, D_MODEL), F32)
    cols_f32 = pltpu.VMEM((D_MODEL, n), F32)
    return pl.pallas_call(
        _decode_kernel,
        grid=(n // DECODE_GROUP,),
        in_specs=[const(x.shape), st_spec, const(ctx.shape), const(w_in.shape), const(npre.shape),
                  const(npost.shape), const(lb_logits.shape), const(hn.shape), const(cw.shape),
                  const(wa.shape), const(wb.shape), const(wo.shape)],
        out_specs=[const(x.shape), st_spec, const(ctx.shape)],
        out_shape=[jax.ShapeDtypeStruct(x.shape, F32),
                   jax.ShapeDtypeStruct(state.shape, F32),
                   jax.ShapeDtypeStruct(ctx.shape, F32)],
        scratch_shapes=[cols_f32, rows_f32, rows_f32, rows_f32, rows_f32, rows_f32, rows_f32, rows_f32],
        compiler_params=pltpu.CompilerParams(
            dimension_semantics=("arbitrary",),
            vmem_limit_bytes=V7X_VMEM_LIMIT_BYTES),
        name="decode_step",
    )(x, state, ctx, w_in, npre, npost, lb_logits, hn, cw, wa, wb, wo)


def kernel(x_prompt, x_sample, state_hgrn, state_conv, meta_tokens, w_in, norm_pre, norm_post, lb_logits,
           hgrn_norm, conv_w, w_a, w_b, w_o):
    depth = w_in.shape[0]
    assert depth == 1, "single-layer trunk"
    batch, seq, _ = x_prompt.shape
    dec_batch, dec_seq, _ = x_sample.shape
    assert dec_seq == 1

    weights = (w_in[0], norm_pre, norm_post, lb_logits, hgrn_norm, conv_w[0], w_a[0], w_b[0], w_o[0])

    y_prompt, hgrn_p, conv_p, w_in_bf, wa_bf, wb_bf, wo_bf = _run_prompt(
        x_prompt, meta_tokens.astype(x_prompt.dtype), weights, tile=PROMPT_TILE, chunk=PROMPT_CHUNK)
    weights = (w_in_bf,) + weights[1:6] + (wa_bf, wb_bf, wo_bf)

    y_s, hgrn_s, conv_s = _run_decode(
        x_sample.reshape(dec_batch, D_MODEL), state_hgrn[0],
        state_conv[0].reshape(dec_batch, (CONV_W - 1) * D_MODEL), weights)

    return (y_prompt, y_s.reshape(dec_batch, 1, D_MODEL), hgrn_p[None], hgrn_s[None],
            conv_p[None], conv_s.reshape(dec_batch, CONV_W - 1, D_MODEL)[None])
```

```python
import functools

import jax
import jax.numpy as jnp
from jax import lax
from jax.experimental import pallas as pl
from jax.experimental.pallas import tpu as pltpu

D_MODEL = 1024
N_HEADS = 8
HEAD_DIM = D_MODEL // N_HEADS
CONV_W = 3
EPS = 1e-6
P_Q, P_F, P_I, P_ZA, P_B, P_C, P_H, P_ZB, P_GA, P_GB = range(10)

PROMPT_TILE = 256
PROMPT_CHUNK = 128
DECODE_GROUP = 8
LOAD_BANDS = 4
SUBLANES = 8
CTX_ROW0 = SUBLANES - (CONV_W - 1)
MAX_HALF_CHUNK_LOG_DECAY = 80.0
V7X_VMEM_LIMIT_BYTES = 58 * 1024 * 1024

BF16 = jnp.bfloat16
F32 = jnp.float32
HEADS = [slice(h * HEAD_DIM, (h + 1) * HEAD_DIM) for h in range(N_HEADS)]


def _dot(a, b):
    return jnp.dot(a, b, preferred_element_type=F32)


def _dot_nt(a, b):
    return lax.dot_general(a, b, (((1,), (1,)), ((), ())), preferred_element_type=F32)


def _dot_tn(a, b):
    return lax.dot_general(a, b, (((0,), (0,)), ((), ())), preferred_element_type=F32)


def _sigmoid(x):
    return 1.0 / (1.0 + jnp.exp(-x))


def _silu(x):
    return x * _sigmoid(x)


def _rms_scale(x):
    return lax.rsqrt(jnp.mean(x * x, axis=-1, keepdims=True) + EPS)


def _lower_bound(lb_logits):
    m = jnp.max(lb_logits, axis=0, keepdims=True)
    e = jnp.exp(lb_logits - m)
    return e[0:1, :] / jnp.sum(e, axis=0, keepdims=True)


def _block_cumsum(x, block):
    n = x.shape[0]
    ri = lax.broadcasted_iota(jnp.int32, (n, n), 0)
    ci = lax.broadcasted_iota(jnp.int32, (n, n), 1)
    tri = ri >= ci
    if n != block:
        tri = jnp.logical_and(tri, (ri // block) == (ci // block))
    tri = jnp.where(tri, 1.0, 0.0).astype(BF16)
    hi = x.astype(BF16)
    lo = (x - hi.astype(F32)).astype(BF16)
    return _dot(tri, hi) + _dot(tri, lo)


def _seq_kernel(x_ref, meta_ref, w_in_hbm, npre_ref, npost_ref, lb_ref, hn_ref, cw_ref,
                wa_hbm, wb_hbm, wo_hbm,
                y_ref, st_out_ref, c_out_ref, w_in_out, wa_out, wb_out, wo_out,
                st_scr, u_scr, q_scr, k_scr, v_scr, g_scr, o_scr,
                qt_scr, kt_scr, qin_scr, kst_scr, vb_scr, dec_scr, st_meta_scr, ctx_meta_scr,
                w_in_ref, wa_ref, wb_ref, wo_ref, stage_scr, load_sem, store_sem,
                *, tile, chunk):
    T, C = tile, chunk
    n_chunks = T // C
    t = pl.program_id(1)
    first_step = jnp.logical_and(pl.program_id(0) == 0, t == 0)
    last_step = jnp.logical_and(pl.program_id(0) == pl.num_programs(0) - 1, t == pl.num_programs(1) - 1)
    lb = _lower_bound(lb_ref[...])

    col = [pl.ds(j * D_MODEL, D_MODEL) for j in range(w_in_hbm.shape[1] // D_MODEL)]
    groups = [(w_in_hbm.at[:, c], w_in_ref.at[:, c], w_in_out.at[:, c]) for c in col]
    groups += [(wa_hbm, wa_ref, wa_out), (wb_hbm, wb_ref, wb_out), (wo_hbm, wo_ref, wo_out)]

    def load_copies(i):
        band = groups[i][0].shape[0] // LOAD_BANDS
        return [pltpu.make_async_copy(groups[i][0].at[pl.ds(b * band, band), :],
                                      stage_scr.at[i % 2, pl.ds(b * band, band), :],
                                      load_sem.at[i % 2, b]) for b in range(LOAD_BANDS)]

    def store_copy(i):
        return pltpu.make_async_copy(groups[i][1], groups[i][2], store_sem.at[i])

    @pl.when(first_step)
    def _stream_weights():
        for copy in load_copies(0):
            copy.start()
        for i in range(len(groups)):
            if i + 1 < len(groups):
                for copy in load_copies(i + 1):
                    copy.start()
            for copy in load_copies(i):
                copy.wait()
            groups[i][1][...] = stage_scr[i % 2].astype(BF16)
        for i in range(len(groups)):
            store_copy(i).start()

    def project(xn, j):
        return _dot(xn, w_in_ref[:, j * D_MODEL:(j + 1) * D_MODEL])

    @pl.when(first_step)
    def _meta_prefix():
        xm = meta_ref[...]
        n_meta = xm.shape[0]
        xn_m = (xm * _rms_scale(xm) * npre_ref[...]).astype(BF16)
        f_m = lb + (1.0 - lb) * _sigmoid(project(xn_m, P_F))
        g_m = _block_cumsum(jnp.log(f_m), n_meta)
        k_end = ((1.0 - f_m) * jnp.exp(g_m[n_meta - 1:n_meta, :] - g_m)).astype(BF16)
        v_m = project(xn_m, P_I).astype(BF16)
        for h, sl in enumerate(HEADS):
            st_meta_scr[h] = _dot_tn(k_end[:, sl], v_m[:, sl])
        u_m = project(xn_m, P_C) * project(xn_m, P_H)
        ctx_meta_scr[CTX_ROW0:SUBLANES, :] = u_m[n_meta - (CONV_W - 1):n_meta, :]
        dec_scr[...] = jnp.zeros(dec_scr.shape, F32)

    @pl.when(t == 0)
    def _init():
        st_scr[...] = st_meta_scr[...]
        u_scr[CTX_ROW0:SUBLANES, :] = ctx_meta_scr[CTX_ROW0:SUBLANES, :]

    x = x_ref[0]
    xn = (x * _rms_scale(x) * npre_ref[...]).astype(BF16)

    def proj(j):
        return project(xn, j)

    f = lb + (1.0 - lb) * _sigmoid(proj(P_F))
    k_scr[...] = 1.0 - f
    q_scr[...] = _silu(proj(P_Q))
    v_scr[...] = proj(P_I)
    g_scr[...] = _block_cumsum(jnp.log(f), C)

    g_floor = None
    for c in range(n_chunks):
        rows = slice(c * C, (c + 1) * C)
        gc = g_scr[rows, :]
        g_last = gc[C - 1:C, :]
        g_mid = gc[C // 2 - 1:C // 2, :]
        qc = q_scr[rows, :]
        kc = k_scr[rows, :]
        qt_scr[rows, :] = (qc * jnp.exp(gc - g_mid)).astype(BF16)
        kt_scr[rows, :] = (kc * jnp.exp(g_mid - gc)).astype(BF16)
        qin_scr[rows, :] = (qc * jnp.exp(gc)).astype(BF16)
        kst_scr[rows, :] = (kc * jnp.exp(g_last - gc)).astype(BF16)
        vb_scr[rows, :] = v_scr[rows, :].astype(BF16)
        dec_scr[c:c + 1, :] = jnp.exp(g_last)
        half_floor = jnp.minimum(g_mid, g_last - g_mid)
        g_floor = half_floor if g_floor is None else jnp.minimum(g_floor, half_floor)
    stable = jnp.min(g_floor) >= -MAX_HALF_CHUNK_LOG_DECAY

    dec_cols = dec_scr[...].T
    keep = jnp.logical_and(
        lax.broadcasted_iota(jnp.int32, (C, C), 0) >= lax.broadcasted_iota(jnp.int32, (C, C), 1), stable)
    for c in range(n_chunks):
        rows = slice(c * C, (c + 1) * C)
        scores = [jnp.where(keep, _dot_nt(qt_scr[rows, sl], kt_scr[rows, sl]), 0.0).astype(BF16)
                  for sl in HEADS]
        for h, sl in enumerate(HEADS):
            vb = vb_scr[rows, sl]
            st = st_scr[h]
            o_scr[rows, sl] = _dot(jnp.concatenate([qin_scr[rows, sl], scores[h]], axis=1),
                                   jnp.concatenate([st.astype(BF16), vb], axis=0))
            decay = jnp.broadcast_to(dec_cols[sl, c:c + 1], (HEAD_DIM, HEAD_DIM))
            st_scr[h] = decay * st + _dot_tn(kst_scr[rows, sl], vb)

    @pl.when(jnp.logical_not(stable))
    def _intra_exact():
        same_head = (lax.broadcasted_iota(jnp.int32, (D_MODEL, D_MODEL), 0) // HEAD_DIM ==
                     lax.broadcasted_iota(jnp.int32, (D_MODEL, D_MODEL), 1) // HEAD_DIM)
        head_sum = jnp.where(same_head, 1.0, 0.0).astype(BF16)
        tpos = lax.broadcasted_iota(jnp.int32, (C, 1), 0)

        def chunk_step(c, carry):
            r0 = pl.multiple_of(c * C, C)
            rows = pl.ds(r0, C)
            gc = g_scr[rows, :]
            qc = q_scr[rows, :]

            def src_step(i, carry2):
                src = pl.ds(r0 + i, 1)
                p = qc * jnp.exp(jnp.minimum(gc - g_scr[src, :], 0.0)) * k_scr[src, :]
                p = jnp.where(tpos >= i, p, 0.0).astype(BF16)
                o_scr[rows, :] += _dot(p, head_sum) * v_scr[src, :]
                return carry2

            return lax.fori_loop(0, C, src_step, carry)

        lax.fori_loop(0, n_chunks, chunk_step, 0)

    hn = hn_ref[...]
    for sl in HEADS:
        o_h = o_scr[:, sl]
        o_scr[:, sl] = o_h * _rms_scale(o_h) * hn
    y_a = _dot((o_scr[...] * _silu(proj(P_ZA))).astype(BF16), wa_ref[...])

    u = proj(P_C) * proj(P_H)
    u_scr[SUBLANES:SUBLANES + T, :] = u
    cw = cw_ref[...]
    conv = cw[CONV_W - 1:CONV_W, :] * u
    for j in range(CONV_W - 1):
        conv = conv + cw[j:j + 1, :] * u_scr[CTX_ROW0 + j:CTX_ROW0 + j + T, :]
    y_b = _dot((proj(P_B) * conv * _silu(proj(P_ZB))).astype(BF16), wb_ref[...])
    new_ctx = u[T - (CONV_W - 1):T, :]
    u_scr[CTX_ROW0:SUBLANES, :] = new_ctx

    merged = _sigmoid(proj(P_GA)) * y_a + _sigmoid(proj(P_GB)) * y_b
    out = _dot(merged.astype(BF16), wo_ref[...])
    y_ref[0] = x + out * _rms_scale(out) * npost_ref[...]

    @pl.when(t == pl.num_programs(1) - 1)
    def _finish():
        c_out_ref[0] = new_ctx
        st_out_ref[0] = st_scr[...]

    @pl.when(last_step)
    def _weights_written():
        for i in range(len(groups)):
            store_copy(i).wait()


def _const_spec(shape):
    zeros = (0,) * len(shape)
    return pl.BlockSpec(shape, lambda b, t: zeros, pipeline_mode=pl.Buffered(1))


def _run_prompt(x, meta, weights, *, tile, chunk):
    n, length, _ = x.shape
    assert length % tile == 0 and tile % chunk == 0 and chunk % (2 * SUBLANES) == 0 and tile >= CONV_W - 1
    assert meta.shape[0] % (2 * SUBLANES) == 0 and meta.shape[0] >= CONV_W - 1
    assert tile // chunk <= HEAD_DIM
    w_in, npre, npost, lb_logits, hn, cw, wa, wb, wo = weights
    kern = functools.partial(_seq_kernel, tile=tile, chunk=chunk)
    hbm = pl.BlockSpec(memory_space=pl.ANY)
    n_groups = w_in.shape[1] // D_MODEL + 3
    assert w_in.shape[1] % D_MODEL == 0 and wa.shape == wb.shape == wo.shape == (w_in.shape[0], D_MODEL)
    state_shape = (1, N_HEADS, HEAD_DIM, HEAD_DIM)
    ctx_shape = (1, CONV_W - 1, D_MODEL)
    tile_f32 = pltpu.VMEM((tile, D_MODEL), F32)
    tile_bf16 = pltpu.VMEM((tile, D_MODEL), BF16)
    return pl.pallas_call(
        kern,
        grid=(n, length // tile),
        in_specs=[
            pl.BlockSpec((1, tile, D_MODEL), lambda b, t: (b, t, 0)),
            _const_spec(meta.shape),
            hbm,
            _const_spec(npre.shape),
            _const_spec(npost.shape),
            _const_spec(lb_logits.shape),
            _const_spec(hn.shape),
            _const_spec(cw.shape),
            hbm, hbm, hbm,
        ],
        out_specs=[
            pl.BlockSpec((1, tile, D_MODEL), lambda b, t: (b, t, 0)),
            pl.BlockSpec(state_shape, lambda b, t: (b, 0, 0, 0)),
            pl.BlockSpec(ctx_shape, lambda b, t: (b, 0, 0)),
            hbm, hbm, hbm, hbm,
        ],
        out_shape=[
            jax.ShapeDtypeStruct(x.shape, F32),
            jax.ShapeDtypeStruct((n,) + state_shape[1:], F32),
            jax.ShapeDtypeStruct((n,) + ctx_shape[1:], F32),
            jax.ShapeDtypeStruct(w_in.shape, BF16),
            jax.ShapeDtypeStruct(wa.shape, BF16),
            jax.ShapeDtypeStruct(wb.shape, BF16),
            jax.ShapeDtypeStruct(wo.shape, BF16),
        ],
        scratch_shapes=[
            pltpu.VMEM((N_HEADS, HEAD_DIM, HEAD_DIM), F32),
            pltpu.VMEM((tile + SUBLANES, D_MODEL), F32),
            tile_f32, tile_f32, tile_f32, tile_f32, tile_f32,
            tile_bf16, tile_bf16, tile_bf16, tile_bf16, tile_bf16,
            pltpu.VMEM((HEAD_DIM, D_MODEL), F32),
            pltpu.VMEM((N_HEADS, HEAD_DIM, HEAD_DIM), F32),
            pltpu.VMEM((SUBLANES, D_MODEL), F32),
            pltpu.VMEM(w_in.shape, BF16),
            pltpu.VMEM(wa.shape, BF16), pltpu.VMEM(wb.shape, BF16), pltpu.VMEM(wo.shape, BF16),
            pltpu.VMEM((2,) + wa.shape, F32),
            pltpu.SemaphoreType.DMA((2, LOAD_BANDS)),
            pltpu.SemaphoreType.DMA((n_groups,)),
        ],
        compiler_params=pltpu.CompilerParams(
            dimension_semantics=("arbitrary", "arbitrary"),
            vmem_limit_bytes=V7X_VMEM_LIMIT_BYTES),
        name="prompt_sweep",
    )(x, meta, w_in, npre, npost, lb_logits, hn, cw, wa, wb, wo)


def _decode_kernel(x_ref, st_ref, ctx_ref, w_in_ref, npre_ref, npost_ref, lb_ref, hn_ref, cw_ref,
                   wa_ref, wb_ref, wo_ref,
                   y_ref, st_out_ref, ctx_out_ref,
                   ft_scr, q_scr, v_scr, o_scr, za_scr, pb_scr, ga_scr, gb_scr):
    G = DECODE_GROUP
    i = pl.program_id(0)
    n_rows = x_ref.shape[0]

    @pl.when(i == 0)
    def _project():
        x = x_ref[...]
        xn = (x * _rms_scale(x) * npre_ref[...]).astype(BF16)

        def proj(j):
            return _dot(xn, w_in_ref[:, j * D_MODEL:(j + 1) * D_MODEL])

        lb = _lower_bound(lb_ref[...])
        f = lb + (1.0 - lb) * _sigmoid(proj(P_F))
        ft_scr[...] = f.T
        q_scr[...] = _silu(proj(P_Q))
        v_scr[...] = proj(P_I)
        za_scr[...] = _silu(proj(P_ZA))
        u = proj(P_C) * proj(P_H)
        cw = cw_ref[...]
        ctx = ctx_ref[...]
        conv = cw[CONV_W - 1:CONV_W, :] * u
        for j in range(CONV_W - 1):
            conv = conv + cw[j:j + 1, :] * ctx[:, j * D_MODEL:(j + 1) * D_MODEL]
        pb_scr[...] = proj(P_B) * conv * _silu(proj(P_ZB))
        ctx_out_ref[:, 0:(CONV_W - 2) * D_MODEL] = ctx[:, D_MODEL:]
        ctx_out_ref[:, (CONV_W - 2) * D_MODEL:] = u
        ga_scr[...] = _sigmoid(proj(P_GA))
        gb_scr[...] = _sigmoid(proj(P_GB))

    shift = (n_rows - i * G) % n_rows
    f_cols = pltpu.roll(ft_scr[...], shift, 1)
    r0 = pl.multiple_of(i * G, G)
    v_rows = v_scr[pl.ds(r0, G), :]
    q_rows = q_scr[pl.ds(r0, G), :].astype(BF16)
    row_id = lax.broadcasted_iota(jnp.int32, (G, HEAD_DIM), 0)
    o_heads = [jnp.zeros((G, HEAD_DIM), F32)] * N_HEADS
    for j in range(G):
        for h, sl in enumerate(HEADS):
            f_b = jnp.broadcast_to(f_cols[sl, j:j + 1], (HEAD_DIM, HEAD_DIM))
            s_new = f_b * st_ref[j, h] + (1.0 - f_b) * v_rows[j:j + 1, sl]
            st_out_ref[j, h] = s_new
            read = _dot(q_rows[:, sl], s_new.astype(BF16))
            o_heads[h] = jnp.where(row_id == j, read, o_heads[h])
    o_scr[pl.ds(r0, G), :] = jnp.concatenate(o_heads, axis=1)

    @pl.when(i == pl.num_programs(0) - 1)
    def _output():
        hn = hn_ref[...]
        for sl in HEADS:
            o_h = o_scr[:, sl]
            o_scr[:, sl] = o_h * _rms_scale(o_h) * hn
        y_a = _dot((o_scr[...] * za_scr[...]).astype(BF16), wa_ref[...])
        y_b = _dot(pb_scr[...].astype(BF16), wb_ref[...])
        merged = ga_scr[...] * y_a + gb_scr[...] * y_b
        out = _dot(merged.astype(BF16), wo_ref[...])
        y_ref[...] = x_ref[...] + out * _rms_scale(out) * npost_ref[...]


def _run_decode(x, state, ctx, weights):
    n = x.shape[0]
    assert n % DECODE_GROUP == 0 and n == 128
    w_in, npre, npost, lb_logits, hn, cw, wa, wb, wo = weights

    def const(shape):
        zeros = (0,) * len(shape)
        return pl.BlockSpec(shape, lambda i: zeros, pipeline_mode=pl.Buffered(1))

    st_spec = pl.BlockSpec((DECODE_GROUP, N_HEADS, HEAD_DIM, HEAD_DIM), lambda i: (i, 0, 0, 0))
    rows_f32 = pltpu.VMEM((n, D_MODEL), F32)
    cols_f32 = pltpu.VMEM((D_MODEL, n), F32)
    return pl.pallas_call(
        _decode_kernel,
        grid=(n // DECODE_GROUP,),
        in_specs=[const(x.shape), st_spec, const(ctx.shape), const(w_in.shape), const(npre.shape),
                  const(npost.shape), const(lb_logits.shape), const(hn.shape), const(cw.shape),
                  const(wa.shape), const(wb.shape), const(wo.shape)],
        out_specs=[const(x.shape), st_spec, const(ctx.shape)],
        out_shape=[jax.ShapeDtypeStruct(x.shape, F32),
                   jax.ShapeDtypeStruct(state.shape, F32),
                   jax.ShapeDtypeStruct(ctx.shape, F32)],
        scratch_shapes=[cols_f32, rows_f32, rows_f32, rows_f32, rows_f32, rows_f32, rows_f32, rows_f32],
        compiler_params=pltpu.CompilerParams(
            dimension_semantics=("arbitrary",),
            vmem_limit_bytes=V7X_VMEM_LIMIT_BYTES),
        name="decode_step",
    )(x, state, ctx, w_in, npre, npost, lb_logits, hn, cw, wa, wb, wo)


def kernel(x_prompt, x_sample, state_hgrn, state_conv, meta_tokens, w_in, norm_pre, norm_post, lb_logits,
           hgrn_norm, conv_w, w_a, w_b, w_o):
    depth = w_in.shape[0]
    assert depth == 1, "single-layer trunk"
    batch, seq, _ = x_prompt.shape
    dec_batch, dec_seq, _ = x_sample.shape
    assert dec_seq == 1

    weights = (w_in[0], norm_pre, norm_post, lb_logits, hgrn_norm, conv_w[0], w_a[0], w_b[0], w_o[0])

    y_prompt, hgrn_p, conv_p, w_in_bf, wa_bf, wb_bf, wo_bf = _run_prompt(
        x_prompt, meta_tokens.astype(x_prompt.dtype), weights, tile=PROMPT_TILE, chunk=PROMPT_CHUNK)
    weights = (w_in_bf,) + weights[1:6] + (wa_bf, wb_bf, wo_bf)

    y_s, hgrn_s, conv_s = _run_decode(
        x_sample.reshape(dec_batch, D_MODEL), state_hgrn[0],
        state_conv[0].reshape(dec_batch, (CONV_W - 1) * D_MODEL), weights)

    return (y_prompt, y_s.reshape(dec_batch, 1, D_MODEL), hgrn_p[None], hgrn_s[None],
            conv_p[None], conv_s.reshape(dec_batch, CONV_W - 1, D_MODEL)[None])
```

```python
import functools

import jax
import jax.numpy as jnp
from jax import lax
from jax.experimental import pallas as pl
from jax.experimental.pallas import tpu as pltpu

D_MODEL = 1024
N_HEADS = 8
HEAD_DIM = D_MODEL // N_HEADS
CONV_W = 3
EPS = 1e-6
P_Q, P_F, P_I, P_ZA, P_B, P_C, P_H, P_ZB, P_GA, P_GB = range(10)

PROMPT_TILE = 256
PROMPT_CHUNK = 128
DECODE_GROUP = 8
LOAD_BANDS = 4
SUBLANES = 8
CTX_ROW0 = SUBLANES - (CONV_W - 1)
MAX_HALF_CHUNK_LOG_DECAY = 80.0
V7X_VMEM_LIMIT_BYTES = 58 * 1024 * 1024

BF16 = jnp.bfloat16
F32 = jnp.float32
HEADS = [slice(h * HEAD_DIM, (h + 1) * HEAD_DIM) for h in range(N_HEADS)]


def _dot(a, b):
    return jnp.dot(a, b, preferred_element_type=F32)


def _dot_nt(a, b):
    return lax.dot_general(a, b, (((1,), (1,)), ((), ())), preferred_element_type=F32)


def _dot_tn(a, b):
    return lax.dot_general(a, b, (((0,), (0,)), ((), ())), preferred_element_type=F32)


def _sigmoid(x):
    return 1.0 / (1.0 + jnp.exp(-x))


def _silu(x):
    return x * _sigmoid(x)


def _rms_scale(x):
    return lax.rsqrt(jnp.mean(x * x, axis=-1, keepdims=True) + EPS)


def _lower_bound(lb_logits):
    m = jnp.max(lb_logits, axis=0, keepdims=True)
    e = jnp.exp(lb_logits - m)
    return e[0:1, :] / jnp.sum(e, axis=0, keepdims=True)


def _block_cumsum(x, block):
    n = x.shape[0]
    ri = lax.broadcasted_iota(jnp.int32, (n, n), 0)
    ci = lax.broadcasted_iota(jnp.int32, (n, n), 1)
    tri = ri >= ci
    if n != block:
        tri = jnp.logical_and(tri, (ri // block) == (ci // block))
    tri = jnp.where(tri, 1.0, 0.0).astype(BF16)
    hi = x.astype(BF16)
    lo = (x - hi.astype(F32)).astype(BF16)
    return _dot(tri, hi) + _dot(tri, lo)


def _seq_kernel(x_ref, meta_ref, w_in_hbm, npre_ref, npost_ref, lb_ref, hn_ref, cw_ref,
                wa_hbm, wb_hbm, wo_hbm,
                y_ref, st_out_ref, c_out_ref, w_in_out, wa_out, wb_out, wo_out,
                st_scr, u_scr, q_scr, k_scr, v_scr, g_scr, o_scr,
                qt_scr, kt_scr, qin_scr, kst_scr, vb_scr, dec_scr, st_meta_scr, ctx_meta_scr,
                w_in_ref, wa_ref, wb_ref, wo_ref, stage_scr, load_sem, store_sem,
                *, tile, chunk):
    T, C = tile, chunk
    n_chunks = T // C
    t = pl.program_id(1)
    first_step = jnp.logical_and(pl.program_id(0) == 0, t == 0)
    last_step = jnp.logical_and(pl.program_id(0) == pl.num_programs(0) - 1, t == pl.num_programs(1) - 1)
    lb = _lower_bound(lb_ref[...])

    col = [pl.ds(j * D_MODEL, D_MODEL) for j in range(w_in_hbm.shape[1] // D_MODEL)]
    groups = [(w_in_hbm.at[:, c], w_in_ref.at[:, c], w_in_out.at[:, c]) for c in col]
    groups += [(wa_hbm, wa_ref, wa_out), (wb_hbm, wb_ref, wb_out), (wo_hbm, wo_ref, wo_out)]

    def load_copies(i):
        band = groups[i][0].shape[0] // LOAD_BANDS
        return [pltpu.make_async_copy(groups[i][0].at[pl.ds(b * band, band), :],
                                      stage_scr.at[i % 2, pl.ds(b * band, band), :],
                                      load_sem.at[i % 2, b]) for b in range(LOAD_BANDS)]

    def store_copy(i):
        return pltpu.make_async_copy(groups[i][1], groups[i][2], store_sem.at[i])

    @pl.when(first_step)
    def _stream_weights():
        for copy in load_copies(0):
            copy.start()
        for i in range(len(groups)):
            if i + 1 < len(groups):
                for copy in load_copies(i + 1):
                    copy.start()
            for copy in load_copies(i):
                copy.wait()
            groups[i][1][...] = stage_scr[i % 2].astype(BF16)
        for i in range(len(groups)):
            store_copy(i).start()

    def project(xn, j):
        return _dot(xn, w_in_ref[:, j * D_MODEL:(j + 1) * D_MODEL])

    @pl.when(first_step)
    def _meta_prefix():
        xm = meta_ref[...]
        n_meta = xm.shape[0]
        xn_m = (xm * _rms_scale(xm) * npre_ref[...]).astype(BF16)
        f_m = lb + (1.0 - lb) * _sigmoid(project(xn_m, P_F))
        g_m = _block_cumsum(jnp.log(f_m), n_meta)
        k_end = ((1.0 - f_m) * jnp.exp(g_m[n_meta - 1:n_meta, :] - g_m)).astype(BF16)
        v_m = project(xn_m, P_I).astype(BF16)
        for h, sl in enumerate(HEADS):
            st_meta_scr[h] = _dot_tn(k_end[:, sl], v_m[:, sl])
        u_m = project(xn_m, P_C) * project(xn_m, P_H)
        ctx_meta_scr[CTX_ROW0:SUBLANES, :] = u_m[n_meta - (CONV_W - 1):n_meta, :]
        dec_scr[...] = jnp.zeros(dec_scr.shape, F32)

    @pl.when(t == 0)
    def _init():
        st_scr[...] = st_meta_scr[...]
        u_scr[CTX_ROW0:SUBLANES, :] = ctx_meta_scr[CTX_ROW0:SUBLANES, :]

    x = x_ref[0]
    xn = (x * _rms_scale(x) * npre_ref[...]).astype(BF16)

    def proj(j):
        return project(xn, j)

    f = lb + (1.0 - lb) * _sigmoid(proj(P_F))
    k_scr[...] = 1.0 - f
    q_scr[...] = _silu(proj(P_Q))
    v_scr[...] = proj(P_I)
    g_scr[...] = _block_cumsum(jnp.log(f), C)

    g_floor = None
    for c in range(n_chunks):
        rows = slice(c * C, (c + 1) * C)
        gc = g_scr[rows, :]
        g_last = gc[C - 1:C, :]
        g_mid = gc[C // 2 - 1:C // 2, :]
        qc = q_scr[rows, :]
        kc = k_scr[rows, :]
        qt_scr[rows, :] = (qc * jnp.exp(gc - g_mid)).astype(BF16)
        kt_scr[rows, :] = (kc * jnp.exp(g_mid - gc)).astype(BF16)
        qin_scr[rows, :] = (qc * jnp.exp(gc)).astype(BF16)
        kst_scr[rows, :] = (kc * jnp.exp(g_last - gc)).astype(BF16)
        vb_scr[rows, :] = v_scr[rows, :].astype(BF16)
        dec_scr[c:c + 1, :] = jnp.exp(g_last)
        half_floor = jnp.minimum(g_mid, g_last - g_mid)
        g_floor = half_floor if g_floor is None else jnp.minimum(g_floor, half_floor)
    stable = jnp.min(g_floor) >= -MAX_HALF_CHUNK_LOG_DECAY

    dec_cols = dec_scr[...].T
    keep = jnp.logical_and(
        lax.broadcasted_iota(jnp.int32, (C, C), 0) >= lax.broadcasted_iota(jnp.int32, (C, C), 1), stable)
    for c in range(n_chunks):
        rows = slice(c * C, (c + 1) * C)
        scores = [jnp.where(keep, _dot_nt(qt_scr[rows, sl], kt_scr[rows, sl]), 0.0).astype(BF16)
                  for sl in HEADS]
        for h, sl in enumerate(HEADS):
            vb = vb_scr[rows, sl]
            st = st_scr[h]
            o_scr[rows, sl] = _dot(jnp.concatenate([qin_scr[rows, sl], scores[h]], axis=1),
                                   jnp.concatenate([st.astype(BF16), vb], axis=0))
            decay = jnp.broadcast_to(dec_cols[sl, c:c + 1], (HEAD_DIM, HEAD_DIM))
            st_scr[h] = decay * st + _dot_tn(kst_scr[rows, sl], vb)

    @pl.when(jnp.logical_not(stable))
    def _intra_exact():
        same_head = (lax.broadcasted_iota(jnp.int32, (D_MODEL, D_MODEL), 0) // HEAD_DIM ==
                     lax.broadcasted_iota(jnp.int32, (D_MODEL, D_MODEL), 1) // HEAD_DIM)
        head_sum = jnp.where(same_head, 1.0, 0.0).astype(BF16)
        tpos = lax.broadcasted_iota(jnp.int32, (C, 1), 0)

        def chunk_step(c, carry):
            r0 = pl.multiple_of(c * C, C)
            rows = pl.ds(r0, C)
            gc = g_scr[rows, :]
            qc = q_scr[rows, :]

            def src_step(i, carry2):
                src = pl.ds(r0 + i, 1)
                p = qc * jnp.exp(jnp.minimum(gc - g_scr[src, :], 0.0)) * k_scr[src, :]
                p = jnp.where(tpos >= i, p, 0.0).astype(BF16)
                o_scr[rows, :] += _dot(p, head_sum) * v_scr[src, :]
                return carry2

            return lax.fori_loop(0, C, src_step, carry)

        lax.fori_loop(0, n_chunks, chunk_step, 0)

    hn = hn_ref[...]
    for sl in HEADS:
        o_h = o_scr[:, sl]
        o_scr[:, sl] = o_h * _rms_scale(o_h) * hn
    y_a = _dot((o_scr[...] * _silu(proj(P_ZA))).astype(BF16), wa_ref[...])

    u = proj(P_C) * proj(P_H)
    u_scr[SUBLANES:SUBLANES + T, :] = u
    cw = cw_ref[...]
    conv = cw[CONV_W - 1:CONV_W, :] * u
    for j in range(CONV_W - 1):
        conv = conv + cw[j:j + 1, :] * u_scr[CTX_ROW0 + j:CTX_ROW0 + j + T, :]
    y_b = _dot((proj(P_B) * conv * _silu(proj(P_ZB))).astype(BF16), wb_ref[...])
    new_ctx = u[T - (CONV_W - 1):T, :]
    u_scr[CTX_ROW0:SUBLANES, :] = new_ctx

    merged = _sigmoid(proj(P_GA)) * y_a + _sigmoid(proj(P_GB)) * y_b
    out = _dot(merged.astype(BF16), wo_ref[...])
    y_ref[0] = x + out * _rms_scale(out) * npost_ref[...]

    @pl.when(t == pl.num_programs(1) - 1)
    def _finish():
        c_out_ref[0] = new_ctx
        st_out_ref[0] = st_scr[...]

    @pl.when(last_step)
    def _weights_written():
        for i in range(len(groups)):
            store_copy(i).wait()


def _const_spec(shape):
    zeros = (0,) * len(shape)
    return pl.BlockSpec(shape, lambda b, t: zeros, pipeline_mode=pl.Buffered(1))


def _run_prompt(x, meta, weights, *, tile, chunk):
    n, length, _ = x.shape
    assert length % tile == 0 and tile % chunk == 0 and chunk % (2 * SUBLANES) == 0 and tile >= CONV_W - 1
    assert meta.shape[0] % (2 * SUBLANES) == 0 and meta.shape[0] >= CONV_W - 1
    assert tile // chunk <= HEAD_DIM
    w_in, npre, npost, lb_logits, hn, cw, wa, wb, wo = weights
    kern = functools.partial(_seq_kernel, tile=tile, chunk=chunk)
    hbm = pl.BlockSpec(memory_space=pl.ANY)
    n_groups = w_in.shape[1] // D_MODEL + 3
    assert w_in.shape[1] % D_MODEL == 0 and wa.shape == wb.shape == wo.shape == (w_in.shape[0], D_MODEL)
    state_shape = (1, N_HEADS, HEAD_DIM, HEAD_DIM)
    ctx_shape = (1, CONV_W - 1, D_MODEL)
    tile_f32 = pltpu.VMEM((tile, D_MODEL), F32)
    tile_bf16 = pltpu.VMEM((tile, D_MODEL), BF16)
    return pl.pallas_call(
        kern,
        grid=(n, length // tile),
        in_specs=[
            pl.BlockSpec((1, tile, D_MODEL), lambda b, t: (b, t, 0)),
            _const_spec(meta.shape),
            hbm,
            _const_spec(npre.shape),
            _const_spec(npost.shape),
            _const_spec(lb_logits.shape),
            _const_spec(hn.shape),
            _const_spec(cw.shape),
            hbm, hbm, hbm,
        ],
        out_specs=[
            pl.BlockSpec((1, tile, D_MODEL), lambda b, t: (b, t, 0)),
            pl.BlockSpec(state_shape, lambda b, t: (b, 0, 0, 0)),
            pl.BlockSpec(ctx_shape, lambda b, t: (b, 0, 0)),
            hbm, hbm, hbm, hbm,
        ],
        out_shape=[
            jax.ShapeDtypeStruct(x.shape, F32),
            jax.ShapeDtypeStruct((n,) + state_shape[1:], F32),
            jax.ShapeDtypeStruct((n,) + ctx_shape[1:], F32),
            jax.ShapeDtypeStruct(w_in.shape, BF16),
            jax.ShapeDtypeStruct(wa.shape, BF16),
            jax.ShapeDtypeStruct(wb.shape, BF16),
            jax.ShapeDtypeStruct(wo.shape, BF16),
        ],
        scratch_shapes=[
            pltpu.VMEM((N_HEADS, HEAD_DIM, HEAD_DIM), F32),
            pltpu.VMEM((tile + SUBLANES, D_MODEL), F32),
            tile_f32, tile_f32, tile_f32, tile_f32, tile_f32,
            tile_bf16, tile_bf16, tile_bf16, tile_bf16, tile_bf16,
            pltpu.VMEM((HEAD_DIM, D_MODEL), F32),
            pltpu.VMEM((N_HEADS, HEAD_DIM, HEAD_DIM), F32),
            pltpu.VMEM((SUBLANES, D_MODEL), F32),
            pltpu.VMEM(w_in.shape, BF16),
            pltpu.VMEM(wa.shape, BF16), pltpu.VMEM(wb.shape, BF16), pltpu.VMEM(wo.shape, BF16),
            pltpu.VMEM((2,) + wa.shape, F32),
            pltpu.SemaphoreType.DMA((2, LOAD_BANDS)),
            pltpu.SemaphoreType.DMA((n_groups,)),
        ],
        compiler_params=pltpu.CompilerParams(
            dimension_semantics=("arbitrary", "arbitrary"),
            vmem_limit_bytes=V7X_VMEM_LIMIT_BYTES),
        name="prompt_sweep",
    )(x, meta, w_in, npre, npost, lb_logits, hn, cw, wa, wb, wo)


def _decode_kernel(x_ref, st_ref, ctx_ref, w_in_ref, npre_ref, npost_ref, lb_ref, hn_ref, cw_ref,
                   wa_ref, wb_ref, wo_ref,
                   y_ref, st_out_ref, ctx_out_ref,
                   ft_scr, q_scr, v_scr, o_scr, za_scr, pb_scr, ga_scr, gb_scr):
    G = DECODE_GROUP
    i = pl.program_id(0)
    n_rows = x_ref.shape[0]

    @pl.when(i == 0)
    def _project():
        x = x_ref[:, 0, :]
        xn = (x * _rms_scale(x) * npre_ref[...]).astype(BF16)

        def proj(j):
            return _dot(xn, w_in_ref[:, j * D_MODEL:(j + 1) * D_MODEL])

        lb = _lower_bound(lb_ref[...])
        f = lb + (1.0 - lb) * _sigmoid(proj(P_F))
        ft_scr[...] = f.T
        q_scr[...] = _silu(proj(P_Q))
        v_scr[...] = proj(P_I)
        za_scr[...] = _silu(proj(P_ZA))
        u = proj(P_C) * proj(P_H)
        cw = cw_ref[...]
        conv = cw[CONV_W - 1:CONV_W, :] * u
        for j in range(CONV_W - 1):
            ctx_j = ctx_ref[:, j, :]
            conv = conv + cw[j:j + 1, :] * ctx_j
            if j > 0:
                ctx_out_ref[:, j - 1, :] = ctx_j
        ctx_out_ref[:, CONV_W - 2, :] = u
        pb_scr[...] = proj(P_B) * conv * _silu(proj(P_ZB))
        ga_scr[...] = _sigmoid(proj(P_GA))
        gb_scr[...] = _sigmoid(proj(P_GB))

    shift = (n_rows - i * G) % n_rows
    f_cols = pltpu.roll(ft_scr[...], shift, 1)
    r0 = pl.multiple_of(i * G, G)
    v_rows = v_scr[pl.ds(r0, G), :]
    q_rows = q_scr[pl.ds(r0, G), :].astype(BF16)
    row_id = lax.broadcasted_iota(jnp.int32, (G, HEAD_DIM), 0)
    o_heads = [jnp.zeros((G, HEAD_DIM), F32)] * N_HEADS
    for j in range(G):
        for h, sl in enumerate(HEADS):
            f_b = jnp.broadcast_to(f_cols[sl, j:j + 1], (HEAD_DIM, HEAD_DIM))
            s_new = f_b * st_ref[j, h] + (1.0 - f_b) * v_rows[j:j + 1, sl]
            st_out_ref[j, h] = s_new
            read = _dot(q_rows[:, sl], s_new.astype(BF16))
            o_heads[h] = jnp.where(row_id == j, read, o_heads[h])
    o_scr[pl.ds(r0, G), :] = jnp.concatenate(o_heads, axis=1)

    @pl.when(i == pl.num_programs(0) - 1)
    def _output():
        hn = hn_ref[...]
        for sl in HEADS:
            o_h = o_scr[:, sl]
            o_scr[:, sl] = o_h * _rms_scale(o_h) * hn
        y_a = _dot((o_scr[...] * za_scr[...]).astype(BF16), wa_ref[...])
        y_b = _dot(pb_scr[...].astype(BF16), wb_ref[...])
        merged = ga_scr[...] * y_a + gb_scr[...] * y_b
        out = _dot(merged.astype(BF16), wo_ref[...])
        y_ref[:, 0, :] = x_ref[:, 0, :] + out * _rms_scale(out) * npost_ref[...]


def _run_decode(x, state, ctx, weights):
    n = x.shape[0]
    assert n % DECODE_GROUP == 0 and n == 128
    w_in, npre, npost, lb_logits, hn, cw, wa, wb, wo = weights

    def const(shape):
        zeros = (0,) * len(shape)
        return pl.BlockSpec(shape, lambda i: zeros, pipeline_mode=pl.Buffered(1))

    st_spec = pl.BlockSpec((DECODE_GROUP, N_HEADS, HEAD_DIM, HEAD_DIM), lambda i: (i, 0, 0, 0))
    rows_f32 = pltpu.VMEM((n, D_MODEL), F32)
    cols_f32 = pltpu.VMEM((D_MODEL, n), F32)
    return pl.pallas_call(
        _decode_kernel,
        grid=(n // DECODE_GROUP,),
        in_specs=[const(x.shape), st_spec, const(ctx.shape), const(w_in.shape), const(npre.shape),
                  const(npost.shape), const(lb_logits.shape), const(hn.shape), const(cw.shape),
                  const(wa.shape), const(wb.shape), const(wo.shape)],
        out_specs=[const(x.shape), st_spec, const(ctx.shape)],
        out_shape=[jax.ShapeDtypeStruct(x.shape, F32),
                   jax.ShapeDtypeStruct(state.shape, F32),
                   jax.ShapeDtypeStruct(ctx.shape, F32)],
        scratch_shapes=[cols_f32, rows_f32, rows_f32, rows_f32, rows_f32, rows_f32, rows_f32, rows_f32],
        compiler_params=pltpu.CompilerParams(
            dimension_semantics=("arbitrary",),
            vmem_limit_bytes=V7X_VMEM_LIMIT_BYTES),
        name="decode_step",
    )(x, state, ctx, w_in, npre, npost, lb_logits, hn, cw, wa, wb, wo)


def kernel(x_prompt, x_sample, state_hgrn, state_conv, meta_tokens, w_in, norm_pre, norm_post, lb_logits,
           hgrn_norm, conv_w, w_a, w_b, w_o):
    depth = w_in.shape[0]
    assert depth == 1, "single-layer trunk"
    batch, seq, _ = x_prompt.shape
    dec_batch, dec_seq, _ = x_sample.shape
    assert dec_seq == 1

    weights = (w_in[0], norm_pre, norm_post, lb_logits, hgrn_norm, conv_w[0], w_a[0], w_b[0], w_o[0])

    y_prompt, hgrn_p, conv_p, w_in_bf, wa_bf, wb_bf, wo_bf = _run_prompt(
        x_prompt, meta_tokens.astype(x_prompt.dtype), weights, tile=PROMPT_TILE, chunk=PROMPT_CHUNK)
    weights = (w_in_bf,) + weights[1:6] + (wa_bf, wb_bf, wo_bf)

    y_s, hgrn_s, conv_s = _run_decode(x_sample, state_hgrn[0], state_conv[0], weights)

    return (y_prompt, y_s, hgrn_p[None], hgrn_s[None], conv_p[None], conv_s[None])
```

```python
import functools

import jax
import jax.numpy as jnp
from jax import lax
from jax.experimental import pallas as pl
from jax.experimental.pallas import tpu as pltpu

D_MODEL = 1024
N_HEADS = 8
HEAD_DIM = D_MODEL // N_HEADS
CONV_W = 3
EPS = 1e-6
P_Q, P_F, P_I, P_ZA, P_B, P_C, P_H, P_ZB, P_GA, P_GB = range(10)

PROMPT_TILE = 256
PROMPT_CHUNK = 128
DECODE_GROUP = 8
LOAD_BANDS = 8
SUBLANES = 8
CTX_ROW0 = SUBLANES - (CONV_W - 1)
MAX_HALF_CHUNK_LOG_DECAY = 80.0
V7X_VMEM_LIMIT_BYTES = 58 * 1024 * 1024

BF16 = jnp.bfloat16
F32 = jnp.float32
HEADS = [slice(h * HEAD_DIM, (h + 1) * HEAD_DIM) for h in range(N_HEADS)]


def _dot(a, b):
    return jnp.dot(a, b, preferred_element_type=F32)


def _dot_nt(a, b):
    return lax.dot_general(a, b, (((1,), (1,)), ((), ())), preferred_element_type=F32)


def _dot_tn(a, b):
    return lax.dot_general(a, b, (((0,), (0,)), ((), ())), preferred_element_type=F32)


def _sigmoid(x):
    return 1.0 / (1.0 + jnp.exp(-x))


def _silu(x):
    return x * _sigmoid(x)


def _rms_scale(x):
    return lax.rsqrt(jnp.mean(x * x, axis=-1, keepdims=True) + EPS)


def _lower_bound(lb_logits):
    m = jnp.max(lb_logits, axis=0, keepdims=True)
    e = jnp.exp(lb_logits - m)
    return e[0:1, :] / jnp.sum(e, axis=0, keepdims=True)


def _block_cumsum(x, block):
    n = x.shape[0]
    ri = lax.broadcasted_iota(jnp.int32, (n, n), 0)
    ci = lax.broadcasted_iota(jnp.int32, (n, n), 1)
    tri = ri >= ci
    if n != block:
        tri = jnp.logical_and(tri, (ri // block) == (ci // block))
    tri = jnp.where(tri, 1.0, 0.0).astype(BF16)
    hi = x.astype(BF16)
    lo = (x - hi.astype(F32)).astype(BF16)
    return _dot(tri, hi) + _dot(tri, lo)


def _seq_kernel(x_ref, meta_ref, w_in_hbm, npre_ref, npost_ref, lb_ref, hn_ref, cw_ref,
                wa_hbm, wb_hbm, wo_hbm,
                y_ref, st_out_ref, c_out_ref, w_in_out, wa_out, wb_out, wo_out,
                st_scr, u_scr, q_scr, k_scr, v_scr, g_scr, o_scr,
                qt_scr, kt_scr, qin_scr, kst_scr, vb_scr, dec_scr, st_meta_scr, ctx_meta_scr,
                w_in_ref, wa_ref, wb_ref, wo_ref, stage_scr, load_sem, store_sem,
                *, tile, chunk):
    T, C = tile, chunk
    n_chunks = T // C
    t = pl.program_id(1)
    first_step = jnp.logical_and(pl.program_id(0) == 0, t == 0)
    last_step = jnp.logical_and(pl.program_id(0) == pl.num_programs(0) - 1, t == pl.num_programs(1) - 1)
    lb = _lower_bound(lb_ref[...])

    col = [pl.ds(j * D_MODEL, D_MODEL) for j in range(w_in_hbm.shape[1] // D_MODEL)]
    groups = [(w_in_hbm.at[:, c], w_in_ref.at[:, c], w_in_out.at[:, c]) for c in col]
    groups += [(wa_hbm, wa_ref, wa_out), (wb_hbm, wb_ref, wb_out), (wo_hbm, wo_ref, wo_out)]

    def load_copies(i):
        band = groups[i][0].shape[0] // LOAD_BANDS
        return [pltpu.make_async_copy(groups[i][0].at[pl.ds(b * band, band), :],
                                      stage_scr.at[i % 2, pl.ds(b * band, band), :],
                                      load_sem.at[i % 2, b]) for b in range(LOAD_BANDS)]

    def store_copy(i):
        return pltpu.make_async_copy(groups[i][1], groups[i][2], store_sem.at[i])

    @pl.when(first_step)
    def _stream_weights():
        for copy in load_copies(0):
            copy.start()
        for i in range(len(groups)):
            if i + 1 < len(groups):
                for copy in load_copies(i + 1):
                    copy.start()
            for copy in load_copies(i):
                copy.wait()
            groups[i][1][...] = stage_scr[i % 2].astype(BF16)
        for i in range(len(groups)):
            store_copy(i).start()

    def project(xn, j):
        return _dot(xn, w_in_ref[:, j * D_MODEL:(j + 1) * D_MODEL])

    @pl.when(first_step)
    def _meta_prefix():
        xm = meta_ref[...]
        n_meta = xm.shape[0]
        xn_m = (xm * _rms_scale(xm) * npre_ref[...]).astype(BF16)
        f_m = lb + (1.0 - lb) * _sigmoid(project(xn_m, P_F))
        g_m = _block_cumsum(jnp.log(f_m), n_meta)
        k_end = ((1.0 - f_m) * jnp.exp(g_m[n_meta - 1:n_meta, :] - g_m)).astype(BF16)
        v_m = project(xn_m, P_I).astype(BF16)
        for h, sl in enumerate(HEADS):
            st_meta_scr[h] = _dot_tn(k_end[:, sl], v_m[:, sl])
        u_m = project(xn_m, P_C) * project(xn_m, P_H)
        ctx_meta_scr[CTX_ROW0:SUBLANES, :] = u_m[n_meta - (CONV_W - 1):n_meta, :]
        dec_scr[...] = jnp.zeros(dec_scr.shape, F32)

    @pl.when(t == 0)
    def _init():
        st_scr[...] = st_meta_scr[...]
        u_scr[CTX_ROW0:SUBLANES, :] = ctx_meta_scr[CTX_ROW0:SUBLANES, :]

    x = x_ref[0]
    xn = (x * _rms_scale(x) * npre_ref[...]).astype(BF16)

    def proj(j):
        return project(xn, j)

    f = lb + (1.0 - lb) * _sigmoid(proj(P_F))
    k_scr[...] = 1.0 - f
    q_scr[...] = _silu(proj(P_Q))
    v_scr[...] = proj(P_I)
    g_scr[...] = _block_cumsum(jnp.log(f), C)

    g_floor = None
    for c in range(n_chunks):
        rows = slice(c * C, (c + 1) * C)
        gc = g_scr[rows, :]
        g_last = gc[C - 1:C, :]
        g_mid = gc[C // 2 - 1:C // 2, :]
        qc = q_scr[rows, :]
        kc = k_scr[rows, :]
        qt_scr[rows, :] = (qc * jnp.exp(gc - g_mid)).astype(BF16)
        kt_scr[rows, :] = (kc * jnp.exp(g_mid - gc)).astype(BF16)
        qin_scr[rows, :] = (qc * jnp.exp(gc)).astype(BF16)
        kst_scr[rows, :] = (kc * jnp.exp(g_last - gc)).astype(BF16)
        vb_scr[rows, :] = v_scr[rows, :].astype(BF16)
        dec_scr[c:c + 1, :] = jnp.exp(g_last)
        half_floor = jnp.minimum(g_mid, g_last - g_mid)
        g_floor = half_floor if g_floor is None else jnp.minimum(g_floor, half_floor)
    stable = jnp.min(g_floor) >= -MAX_HALF_CHUNK_LOG_DECAY

    dec_cols = dec_scr[...].T
    keep = jnp.logical_and(
        lax.broadcasted_iota(jnp.int32, (C, C), 0) >= lax.broadcasted_iota(jnp.int32, (C, C), 1), stable)
    for c in range(n_chunks):
        rows = slice(c * C, (c + 1) * C)
        scores = [jnp.where(keep, _dot_nt(qt_scr[rows, sl], kt_scr[rows, sl]), 0.0).astype(BF16)
                  for sl in HEADS]
        for h, sl in enumerate(HEADS):
            vb = vb_scr[rows, sl]
            st = st_scr[h]
            o_scr[rows, sl] = _dot(jnp.concatenate([qin_scr[rows, sl], scores[h]], axis=1),
                                   jnp.concatenate([st.astype(BF16), vb], axis=0))
            decay = jnp.broadcast_to(dec_cols[sl, c:c + 1], (HEAD_DIM, HEAD_DIM))
            st_scr[h] = decay * st + _dot_tn(kst_scr[rows, sl], vb)

    @pl.when(jnp.logical_not(stable))
    def _intra_exact():
        same_head = (lax.broadcasted_iota(jnp.int32, (D_MODEL, D_MODEL), 0) // HEAD_DIM ==
                     lax.broadcasted_iota(jnp.int32, (D_MODEL, D_MODEL), 1) // HEAD_DIM)
        head_sum = jnp.where(same_head, 1.0, 0.0).astype(BF16)
        tpos = lax.broadcasted_iota(jnp.int32, (C, 1), 0)

        def chunk_step(c, carry):
            r0 = pl.multiple_of(c * C, C)
            rows = pl.ds(r0, C)
            gc = g_scr[rows, :]
            qc = q_scr[rows, :]

            def src_step(i, carry2):
                src = pl.ds(r0 + i, 1)
                p = qc * jnp.exp(jnp.minimum(gc - g_scr[src, :], 0.0)) * k_scr[src, :]
                p = jnp.where(tpos >= i, p, 0.0).astype(BF16)
                o_scr[rows, :] += _dot(p, head_sum) * v_scr[src, :]
                return carry2

            return lax.fori_loop(0, C, src_step, carry)

        lax.fori_loop(0, n_chunks, chunk_step, 0)

    hn = hn_ref[...]
    for sl in HEADS:
        o_h = o_scr[:, sl]
        o_scr[:, sl] = o_h * _rms_scale(o_h) * hn
    y_a = _dot((o_scr[...] * _silu(proj(P_ZA))).astype(BF16), wa_ref[...])

    u = proj(P_C) * proj(P_H)
    u_scr[SUBLANES:SUBLANES + T, :] = u
    cw = cw_ref[0]
    conv = cw[CONV_W - 1:CONV_W, :] * u
    for j in range(CONV_W - 1):
        conv = conv + cw[j:j + 1, :] * u_scr[CTX_ROW0 + j:CTX_ROW0 + j + T, :]
    y_b = _dot((proj(P_B) * conv * _silu(proj(P_ZB))).astype(BF16), wb_ref[...])
    new_ctx = u[T - (CONV_W - 1):T, :]
    u_scr[CTX_ROW0:SUBLANES, :] = new_ctx

    merged = _sigmoid(proj(P_GA)) * y_a + _sigmoid(proj(P_GB)) * y_b
    out = _dot(merged.astype(BF16), wo_ref[...])
    y_ref[0] = x + out * _rms_scale(out) * npost_ref[...]

    @pl.when(t == pl.num_programs(1) - 1)
    def _finish():
        c_out_ref[0] = new_ctx
        st_out_ref[0] = st_scr[...]

    @pl.when(last_step)
    def _weights_written():
        for i in range(len(groups)):
            store_copy(i).wait()


def _const_spec(shape):
    zeros = (0,) * len(shape)
    return pl.BlockSpec(shape, lambda b, t: zeros, pipeline_mode=pl.Buffered(1))


def _run_prompt(x, meta, weights, *, tile, chunk):
    n, length, _ = x.shape
    assert length % tile == 0 and tile % chunk == 0 and chunk % (2 * SUBLANES) == 0 and tile >= CONV_W - 1
    assert meta.shape[0] % (2 * SUBLANES) == 0 and meta.shape[0] >= CONV_W - 1
    assert tile // chunk <= HEAD_DIM
    w_in, npre, npost, lb_logits, hn, cw, wa, wb, wo = weights
    kern = functools.partial(_seq_kernel, tile=tile, chunk=chunk)
    hbm = pl.BlockSpec(memory_space=pl.ANY)
    n_groups = w_in.shape[1] // D_MODEL + 3
    assert w_in.shape[1] % D_MODEL == 0 and wa.shape == wb.shape == wo.shape == (w_in.shape[0], D_MODEL)
    state_shape = (1, N_HEADS, HEAD_DIM, HEAD_DIM)
    ctx_shape = (1, CONV_W - 1, D_MODEL)
    tile_f32 = pltpu.VMEM((tile, D_MODEL), F32)
    tile_bf16 = pltpu.VMEM((tile, D_MODEL), BF16)
    return pl.pallas_call(
        kern,
        grid=(n, length // tile),
        in_specs=[
            pl.BlockSpec((1, tile, D_MODEL), lambda b, t: (b, t, 0)),
            _const_spec(meta.shape),
            hbm,
            _const_spec(npre.shape),
            _const_spec(npost.shape),
            _const_spec(lb_logits.shape),
            _const_spec(hn.shape),
            _const_spec(cw.shape),
            hbm, hbm, hbm,
        ],
        out_specs=[
            pl.BlockSpec((1, tile, D_MODEL), lambda b, t: (b, t, 0)),
            pl.BlockSpec(state_shape, lambda b, t: (b, 0, 0, 0)),
            pl.BlockSpec(ctx_shape, lambda b, t: (b, 0, 0)),
            hbm, hbm, hbm, hbm,
        ],
        out_shape=[
            jax.ShapeDtypeStruct(x.shape, F32),
            jax.ShapeDtypeStruct((n,) + state_shape[1:], F32),
            jax.ShapeDtypeStruct((n,) + ctx_shape[1:], F32),
            jax.ShapeDtypeStruct(w_in.shape, BF16),
            jax.ShapeDtypeStruct(wa.shape, BF16),
            jax.ShapeDtypeStruct(wb.shape, BF16),
            jax.ShapeDtypeStruct(wo.shape, BF16),
        ],
        scratch_shapes=[
            pltpu.VMEM((N_HEADS, HEAD_DIM, HEAD_DIM), F32),
            pltpu.VMEM((tile + SUBLANES, D_MODEL), F32),
            tile_f32, tile_f32, tile_f32, tile_f32, tile_f32,
            tile_bf16, tile_bf16, tile_bf16, tile_bf16, tile_bf16,
            pltpu.VMEM((HEAD_DIM, D_MODEL), F32),
            pltpu.VMEM((N_HEADS, HEAD_DIM, HEAD_DIM), F32),
            pltpu.VMEM((SUBLANES, D_MODEL), F32),
            pltpu.VMEM(w_in.shape, BF16),
            pltpu.VMEM(wa.shape, BF16), pltpu.VMEM(wb.shape, BF16), pltpu.VMEM(wo.shape, BF16),
            pltpu.VMEM((2,) + wa.shape, F32),
            pltpu.SemaphoreType.DMA((2, LOAD_BANDS)),
            pltpu.SemaphoreType.DMA((n_groups,)),
        ],
        compiler_params=pltpu.CompilerParams(
            dimension_semantics=("arbitrary", "arbitrary"),
            vmem_limit_bytes=V7X_VMEM_LIMIT_BYTES),
        name="prompt_sweep",
    )(x, meta, w_in, npre, npost, lb_logits, hn, cw, wa, wb, wo)


def _decode_kernel(x_ref, st_ref, ctx_ref, w_in_ref, npre_ref, npost_ref, lb_ref, hn_ref, cw_ref,
                   wa_ref, wb_ref, wo_ref,
                   y_ref, st_out_ref, ctx_out_ref,
                   ft_scr, q_scr, v_scr, o_scr, za_scr, pb_scr, ga_scr, gb_scr):
    G = DECODE_GROUP
    i = pl.program_id(0)
    n_rows = x_ref.shape[0]

    @pl.when(i == 0)
    def _project():
        x = x_ref[:, 0, :]
        xn = (x * _rms_scale(x) * npre_ref[...]).astype(BF16)

        def proj(j):
            return _dot(xn, w_in_ref[:, j * D_MODEL:(j + 1) * D_MODEL])

        lb = _lower_bound(lb_ref[...])
        f = lb + (1.0 - lb) * _sigmoid(proj(P_F))
        ft_scr[...] = f.T
        q_scr[...] = _silu(proj(P_Q))
        v_scr[...] = proj(P_I)
        za_scr[...] = _silu(proj(P_ZA))
        u = proj(P_C) * proj(P_H)
        cw = cw_ref[0]
        conv = cw[CONV_W - 1:CONV_W, :] * u
        for j in range(CONV_W - 1):
            ctx_j = ctx_ref[:, j, :]
            conv = conv + cw[j:j + 1, :] * ctx_j
            if j > 0:
                ctx_out_ref[:, j - 1, :] = ctx_j
        ctx_out_ref[:, CONV_W - 2, :] = u
        pb_scr[...] = proj(P_B) * conv * _silu(proj(P_ZB))
        ga_scr[...] = _sigmoid(proj(P_GA))
        gb_scr[...] = _sigmoid(proj(P_GB))

    shift = (n_rows - i * G) % n_rows
    f_cols = pltpu.roll(ft_scr[...], shift, 1)
    r0 = pl.multiple_of(i * G, G)
    v_rows = v_scr[pl.ds(r0, G), :]
    q_rows = q_scr[pl.ds(r0, G), :].astype(BF16)
    row_id = lax.broadcasted_iota(jnp.int32, (G, HEAD_DIM), 0)
    o_heads = [jnp.zeros((G, HEAD_DIM), F32)] * N_HEADS
    for j in range(G):
        for h, sl in enumerate(HEADS):
            f_b = jnp.broadcast_to(f_cols[sl, j:j + 1], (HEAD_DIM, HEAD_DIM))
            s_new = f_b * st_ref[j, h] + (1.0 - f_b) * v_rows[j:j + 1, sl]
            st_out_ref[j, h] = s_new
            read = _dot(q_rows[:, sl], s_new.astype(BF16))
            o_heads[h] = jnp.where(row_id == j, read, o_heads[h])
    o_scr[pl.ds(r0, G), :] = jnp.concatenate(o_heads, axis=1)

    @pl.when(i == pl.num_programs(0) - 1)
    def _output():
        hn = hn_ref[...]
        for sl in HEADS:
            o_h = o_scr[:, sl]
            o_scr[:, sl] = o_h * _rms_scale(o_h) * hn
        y_a = _dot((o_scr[...] * za_scr[...]).astype(BF16), wa_ref[...])
        y_b = _dot(pb_scr[...].astype(BF16), wb_ref[...])
        merged = ga_scr[...] * y_a + gb_scr[...] * y_b
        out = _dot(merged.astype(BF16), wo_ref[...])
        y_ref[:, 0, :] = x_ref[:, 0, :] + out * _rms_scale(out) * npost_ref[...]


def _run_decode(x, state, ctx, weights):
    n = x.shape[0]
    assert n % DECODE_GROUP == 0 and n == 128
    w_in, npre, npost, lb_logits, hn, cw, wa, wb, wo = weights

    def const(shape):
        zeros = (0,) * len(shape)
        return pl.BlockSpec(shape, lambda i: zeros, pipeline_mode=pl.Buffered(1))

    st_spec = pl.BlockSpec((DECODE_GROUP, N_HEADS, HEAD_DIM, HEAD_DIM), lambda i: (i, 0, 0, 0))
    rows_f32 = pltpu.VMEM((n, D_MODEL), F32)
    cols_f32 = pltpu.VMEM((D_MODEL, n), F32)
    return pl.pallas_call(
        _decode_kernel,
        grid=(n // DECODE_GROUP,),
        in_specs=[const(x.shape), st_spec, const(ctx.shape), const(w_in.shape), const(npre.shape),
                  const(npost.shape), const(lb_logits.shape), const(hn.shape), const(cw.shape),
                  const(wa.shape), const(wb.shape), const(wo.shape)],
        out_specs=[const(x.shape), st_spec, const(ctx.shape)],
        out_shape=[jax.ShapeDtypeStruct(x.shape, F32),
                   jax.ShapeDtypeStruct(state.shape, F32),
                   jax.ShapeDtypeStruct(ctx.shape, F32)],
        scratch_shapes=[cols_f32, rows_f32, rows_f32, rows_f32, rows_f32, rows_f32, rows_f32, rows_f32],
        compiler_params=pltpu.CompilerParams(
            dimension_semantics=("arbitrary",),
            vmem_limit_bytes=V7X_VMEM_LIMIT_BYTES),
        name="decode_step",
    )(x, state, ctx, w_in, npre, npost, lb_logits, hn, cw, wa, wb, wo)


def kernel(x_prompt, x_sample, state_hgrn, state_conv, meta_tokens, w_in, norm_pre, norm_post, lb_logits,
           hgrn_norm, conv_w, w_a, w_b, w_o):
    depth = w_in.shape[0]
    assert depth == 1, "single-layer trunk"
    batch, seq, _ = x_prompt.shape
    dec_batch, dec_seq, _ = x_sample.shape
    assert dec_seq == 1

    weights = (w_in[0], norm_pre, norm_post, lb_logits, hgrn_norm, conv_w, w_a[0], w_b[0], w_o[0])

    y_prompt, hgrn_p, conv_p, w_in_bf, wa_bf, wb_bf, wo_bf = _run_prompt(
        x_prompt, meta_tokens.astype(x_prompt.dtype), weights, tile=PROMPT_TILE, chunk=PROMPT_CHUNK)
    weights = (w_in_bf,) + weights[1:6] + (wa_bf, wb_bf, wo_bf)

    y_s, hgrn_s, conv_s = _run_decode(x_sample, state_hgrn[0], state_conv[0], weights)

    return (y_prompt, y_s, hgrn_p[None], hgrn_s[None], conv_p[None], conv_s[None])
```

```python
import functools

import jax
import jax.numpy as jnp
from jax import lax
from jax.experimental import pallas as pl
from jax.experimental.pallas import tpu as pltpu

D_MODEL = 1024
N_HEADS = 8
HEAD_DIM = D_MODEL // N_HEADS
CONV_W = 3
EPS = 1e-6
P_Q, P_F, P_I, P_ZA, P_B, P_C, P_H, P_ZB, P_GA, P_GB = range(10)

PROMPT_TILE = 256
PROMPT_CHUNK = 128
DECODE_GROUP = 8
COL_BAND = 256
LOAD_BANDS = 4
SUBLANES = 8
CTX_ROW0 = SUBLANES - (CONV_W - 1)
MAX_HALF_CHUNK_LOG_DECAY = 80.0
V7X_VMEM_LIMIT_BYTES = 58 * 1024 * 1024

BF16 = jnp.bfloat16
F32 = jnp.float32
HEADS = [slice(h * HEAD_DIM, (h + 1) * HEAD_DIM) for h in range(N_HEADS)]


def _dot(a, b):
    return jnp.dot(a, b, preferred_element_type=F32)


def _dot_nt(a, b):
    return lax.dot_general(a, b, (((1,), (1,)), ((), ())), preferred_element_type=F32)


def _dot_tn(a, b):
    return lax.dot_general(a, b, (((0,), (0,)), ((), ())), preferred_element_type=F32)


def _sigmoid(x):
    return 1.0 / (1.0 + jnp.exp(-x))


def _silu(x):
    return x * _sigmoid(x)


def _rms_scale(x):
    return lax.rsqrt(jnp.mean(x * x, axis=-1, keepdims=True) + EPS)


def _lower_bound(lb_logits):
    m = jnp.max(lb_logits, axis=0, keepdims=True)
    e = jnp.exp(lb_logits - m)
    return e[0:1, :] / jnp.sum(e, axis=0, keepdims=True)


def _block_cumsum(x, block):
    n = x.shape[0]
    ri = lax.broadcasted_iota(jnp.int32, (n, n), 0)
    ci = lax.broadcasted_iota(jnp.int32, (n, n), 1)
    tri = ri >= ci
    if n != block:
        tri = jnp.logical_and(tri, (ri // block) == (ci // block))
    tri = jnp.where(tri, 1.0, 0.0).astype(BF16)
    hi = x.astype(BF16)
    lo = (x - hi.astype(F32)).astype(BF16)
    return _dot(tri, hi) + _dot(tri, lo)


def _seq_kernel(x_ref, meta_ref, w_in_hbm, npre_ref, npost_ref, lb_ref, hn_ref, cw_ref,
                wa_hbm, wb_hbm, wo_hbm,
                y_ref, st_out_ref, c_out_ref, w_in_out, wa_out, wb_out, wo_out,
                st_scr, u_scr, q_scr, k_scr, v_scr, g_scr, o_scr,
                qt_scr, kt_scr, qin_scr, kst_scr, vb_scr, dec_scr, st_meta_scr, ctx_meta_scr,
                w_in_ref, wa_ref, wb_ref, wo_ref, stage_scr, load_sem, store_sem,
                *, tile, chunk):
    T, C = tile, chunk
    n_chunks = T // C
    t = pl.program_id(1)
    first_step = jnp.logical_and(pl.program_id(0) == 0, t == 0)
    last_step = jnp.logical_and(pl.program_id(0) == pl.num_programs(0) - 1, t == pl.num_programs(1) - 1)
    lb = _lower_bound(lb_ref[...])

    col = [pl.ds(j * D_MODEL, D_MODEL) for j in range(w_in_hbm.shape[1] // D_MODEL)]
    groups = [(w_in_hbm.at[:, c], w_in_ref.at[:, c], w_in_out.at[:, c]) for c in col]
    groups += [(wa_hbm, wa_ref, wa_out), (wb_hbm, wb_ref, wb_out), (wo_hbm, wo_ref, wo_out)]

    def load_copies(i):
        band = groups[i][0].shape[0] // LOAD_BANDS
        return [pltpu.make_async_copy(groups[i][0].at[pl.ds(b * band, band), :],
                                      stage_scr.at[i % 2, pl.ds(b * band, band), :],
                                      load_sem.at[i % 2, b]) for b in range(LOAD_BANDS)]

    def store_copy(i):
        return pltpu.make_async_copy(groups[i][1], groups[i][2], store_sem.at[i])

    @pl.when(first_step)
    def _stream_weights():
        for copy in load_copies(0):
            copy.start()
        for i in range(len(groups)):
            if i + 1 < len(groups):
                for copy in load_copies(i + 1):
                    copy.start()
            for copy in load_copies(i):
                copy.wait()
            groups[i][1][...] = stage_scr[i % 2].astype(BF16)
        for i in range(len(groups)):
            store_copy(i).start()

    def project(xn, j):
        return _dot(xn, w_in_ref[:, j * D_MODEL:(j + 1) * D_MODEL])

    @pl.when(first_step)
    def _meta_prefix():
        xm = meta_ref[...]
        n_meta = xm.shape[0]
        xn_m = (xm * _rms_scale(xm) * npre_ref[...]).astype(BF16)
        f_m = lb + (1.0 - lb) * _sigmoid(project(xn_m, P_F))
        g_m = _block_cumsum(jnp.log(f_m), n_meta)
        k_end = ((1.0 - f_m) * jnp.exp(g_m[n_meta - 1:n_meta, :] - g_m)).astype(BF16)
        v_m = project(xn_m, P_I).astype(BF16)
        for h, sl in enumerate(HEADS):
            st_meta_scr[h] = _dot_tn(k_end[:, sl], v_m[:, sl])
        u_m = project(xn_m, P_C) * project(xn_m, P_H)
        ctx_meta_scr[CTX_ROW0:SUBLANES, :] = u_m[n_meta - (CONV_W - 1):n_meta, :]
        dec_scr[...] = jnp.zeros(dec_scr.shape, F32)

    @pl.when(t == 0)
    def _init():
        st_scr[...] = st_meta_scr[...]
        u_scr[CTX_ROW0:SUBLANES, :] = ctx_meta_scr[CTX_ROW0:SUBLANES, :]

    x = x_ref[0]
    xn = (x * _rms_scale(x) * npre_ref[...]).astype(BF16)

    def proj(j):
        return project(xn, j)

    f = lb + (1.0 - lb) * _sigmoid(proj(P_F))
    k_scr[...] = 1.0 - f
    q_scr[...] = _silu(proj(P_Q))
    v_scr[...] = proj(P_I)
    g_scr[...] = _block_cumsum(jnp.log(f), C)

    g_floor = None
    for c in range(n_chunks):
        rows = slice(c * C, (c + 1) * C)
        gc = g_scr[rows, :]
        g_last = gc[C - 1:C, :]
        g_mid = gc[C // 2 - 1:C // 2, :]
        qc = q_scr[rows, :]
        kc = k_scr[rows, :]
        qt_scr[rows, :] = (qc * jnp.exp(gc - g_mid)).astype(BF16)
        kt_scr[rows, :] = (kc * jnp.exp(g_mid - gc)).astype(BF16)
        qin_scr[rows, :] = (qc * jnp.exp(gc)).astype(BF16)
        kst_scr[rows, :] = (kc * jnp.exp(g_last - gc)).astype(BF16)
        vb_scr[rows, :] = v_scr[rows, :].astype(BF16)
        dec_scr[c:c + 1, :] = jnp.exp(g_last)
        half_floor = jnp.minimum(g_mid, g_last - g_mid)
        g_floor = half_floor if g_floor is None else jnp.minimum(g_floor, half_floor)
    stable = jnp.min(g_floor) >= -MAX_HALF_CHUNK_LOG_DECAY

    dec_cols = dec_scr[...].T
    keep = jnp.logical_and(
        lax.broadcasted_iota(jnp.int32, (C, C), 0) >= lax.broadcasted_iota(jnp.int32, (C, C), 1), stable)
    for c in range(n_chunks):
        rows = slice(c * C, (c + 1) * C)
        scores = [jnp.where(keep, _dot_nt(qt_scr[rows, sl], kt_scr[rows, sl]), 0.0).astype(BF16)
                  for sl in HEADS]
        for h, sl in enumerate(HEADS):
            vb = vb_scr[rows, sl]
            st = st_scr[h]
            o_scr[rows, sl] = _dot(jnp.concatenate([qin_scr[rows, sl], scores[h]], axis=1),
                                   jnp.concatenate([st.astype(BF16), vb], axis=0))
            decay = jnp.broadcast_to(dec_cols[sl, c:c + 1], (HEAD_DIM, HEAD_DIM))
            st_scr[h] = decay * st + _dot_tn(kst_scr[rows, sl], vb)

    @pl.when(jnp.logical_not(stable))
    def _intra_exact():
        same_head = (lax.broadcasted_iota(jnp.int32, (D_MODEL, D_MODEL), 0) // HEAD_DIM ==
                     lax.broadcasted_iota(jnp.int32, (D_MODEL, D_MODEL), 1) // HEAD_DIM)
        head_sum = jnp.where(same_head, 1.0, 0.0).astype(BF16)
        tpos = lax.broadcasted_iota(jnp.int32, (C, 1), 0)

        def chunk_step(c, carry):
            r0 = pl.multiple_of(c * C, C)
            rows = pl.ds(r0, C)
            gc = g_scr[rows, :]
            qc = q_scr[rows, :]

            def src_step(i, carry2):
                src = pl.ds(r0 + i, 1)
                p = qc * jnp.exp(jnp.minimum(gc - g_scr[src, :], 0.0)) * k_scr[src, :]
                p = jnp.where(tpos >= i, p, 0.0).astype(BF16)
                o_scr[rows, :] += _dot(p, head_sum) * v_scr[src, :]
                return carry2

            return lax.fori_loop(0, C, src_step, carry)

        lax.fori_loop(0, n_chunks, chunk_step, 0)

    hn = hn_ref[...]
    for sl in HEADS:
        o_h = o_scr[:, sl]
        o_scr[:, sl] = o_h * _rms_scale(o_h) * hn
    za_scr, pb_scr, mg_scr = qt_scr, kt_scr, qin_scr
    bands = [slice(n * COL_BAND, (n + 1) * COL_BAND) for n in range(D_MODEL // COL_BAND)]

    def proj_band(j, band):
        return _dot(xn, w_in_ref[:, j * D_MODEL + band.start:j * D_MODEL + band.stop])

    cw = cw_ref[...]
    for band in bands:
        za_scr[:, band] = (o_scr[:, band] * _silu(proj_band(P_ZA, band))).astype(BF16)
        u = proj_band(P_C, band) * proj_band(P_H, band)
        u_scr[SUBLANES:SUBLANES + T, band] = u
        conv = cw[CONV_W - 1:CONV_W, band] * u
        for j in range(CONV_W - 1):
            conv = conv + cw[j:j + 1, band] * u_scr[CTX_ROW0 + j:CTX_ROW0 + j + T, band]
        pb_scr[:, band] = (proj_band(P_B, band) * conv * _silu(proj_band(P_ZB, band))).astype(BF16)
    new_ctx = u_scr[SUBLANES + T - (CONV_W - 1):SUBLANES + T, :]
    u_scr[CTX_ROW0:SUBLANES, :] = new_ctx

    for band in bands:
        y_a = _dot(za_scr[...], wa_ref[:, band])
        y_b = _dot(pb_scr[...], wb_ref[:, band])
        mg_scr[:, band] = (_sigmoid(proj_band(P_GA, band)) * y_a
                           + _sigmoid(proj_band(P_GB, band)) * y_b).astype(BF16)
    out = _dot(mg_scr[...], wo_ref[...])
    y_ref[0] = x + out * _rms_scale(out) * npost_ref[...]

    @pl.when(t == pl.num_programs(1) - 1)
    def _finish():
        c_out_ref[0] = new_ctx
        st_out_ref[0] = st_scr[...]

    @pl.when(last_step)
    def _weights_written():
        for i in range(len(groups)):
            store_copy(i).wait()


def _const_spec(shape):
    zeros = (0,) * len(shape)
    return pl.BlockSpec(shape, lambda b, t: zeros, pipeline_mode=pl.Buffered(1))


def _run_prompt(x, meta, weights, *, tile, chunk):
    n, length, _ = x.shape
    assert length % tile == 0 and tile % chunk == 0 and chunk % (2 * SUBLANES) == 0 and tile >= CONV_W - 1
    assert meta.shape[0] % (2 * SUBLANES) == 0 and meta.shape[0] >= CONV_W - 1
    assert tile // chunk <= HEAD_DIM
    w_in, npre, npost, lb_logits, hn, cw, wa, wb, wo = weights
    kern = functools.partial(_seq_kernel, tile=tile, chunk=chunk)
    hbm = pl.BlockSpec(memory_space=pl.ANY)
    n_groups = w_in.shape[1] // D_MODEL + 3
    assert w_in.shape[1] % D_MODEL == 0 and wa.shape == wb.shape == wo.shape == (w_in.shape[0], D_MODEL)
    state_shape = (1, N_HEADS, HEAD_DIM, HEAD_DIM)
    ctx_shape = (1, CONV_W - 1, D_MODEL)
    tile_f32 = pltpu.VMEM((tile, D_MODEL), F32)
    tile_bf16 = pltpu.VMEM((tile, D_MODEL), BF16)
    return pl.pallas_call(
        kern,
        grid=(n, length // tile),
        in_specs=[
            pl.BlockSpec((1, tile, D_MODEL), lambda b, t: (b, t, 0)),
            _const_spec(meta.shape),
            hbm,
            _const_spec(npre.shape),
            _const_spec(npost.shape),
            _const_spec(lb_logits.shape),
            _const_spec(hn.shape),
            _const_spec(cw.shape),
            hbm, hbm, hbm,
        ],
        out_specs=[
            pl.BlockSpec((1, tile, D_MODEL), lambda b, t: (b, t, 0)),
            pl.BlockSpec(state_shape, lambda b, t: (b, 0, 0, 0)),
            pl.BlockSpec(ctx_shape, lambda b, t: (b, 0, 0)),
            hbm, hbm, hbm, hbm,
        ],
        out_shape=[
            jax.ShapeDtypeStruct(x.shape, F32),
            jax.ShapeDtypeStruct((n,) + state_shape[1:], F32),
            jax.ShapeDtypeStruct((n,) + ctx_shape[1:], F32),
            jax.ShapeDtypeStruct(w_in.shape, BF16),
            jax.ShapeDtypeStruct(wa.shape, BF16),
            jax.ShapeDtypeStruct(wb.shape, BF16),
            jax.ShapeDtypeStruct(wo.shape, BF16),
        ],
        scratch_shapes=[
            pltpu.VMEM((N_HEADS, HEAD_DIM, HEAD_DIM), F32),
            pltpu.VMEM((tile + SUBLANES, D_MODEL), F32),
            tile_f32, tile_f32, tile_f32, tile_f32, tile_f32,
            tile_bf16, tile_bf16, tile_bf16, tile_bf16, tile_bf16,
            pltpu.VMEM((HEAD_DIM, D_MODEL), F32),
            pltpu.VMEM((N_HEADS, HEAD_DIM, HEAD_DIM), F32),
            pltpu.VMEM((SUBLANES, D_MODEL), F32),
            pltpu.VMEM(w_in.shape, BF16),
            pltpu.VMEM(wa.shape, BF16), pltpu.VMEM(wb.shape, BF16), pltpu.VMEM(wo.shape, BF16),
            pltpu.VMEM((2,) + wa.shape, F32),
            pltpu.SemaphoreType.DMA((2, LOAD_BANDS)),
            pltpu.SemaphoreType.DMA((n_groups,)),
        ],
        compiler_params=pltpu.CompilerParams(
            dimension_semantics=("arbitrary", "arbitrary"),
            vmem_limit_bytes=V7X_VMEM_LIMIT_BYTES),
        name="prompt_sweep",
    )(x, meta, w_in, npre, npost, lb_logits, hn, cw, wa, wb, wo)


def _decode_kernel(x_ref, st_ref, ctx_ref, w_in_ref, npre_ref, npost_ref, lb_ref, hn_ref, cw_ref,
                   wa_ref, wb_ref, wo_ref,
                   y_ref, st_out_ref, ctx_out_ref,
                   ft_scr, q_scr, v_scr, o_scr, za_scr, pb_scr, ga_scr, gb_scr):
    G = DECODE_GROUP
    i = pl.program_id(0)
    n_rows = x_ref.shape[0]

    @pl.when(i == 0)
    def _project():
        x = x_ref[:, 0, :]
        xn = (x * _rms_scale(x) * npre_ref[...]).astype(BF16)

        def proj(j):
            return _dot(xn, w_in_ref[:, j * D_MODEL:(j + 1) * D_MODEL])

        lb = _lower_bound(lb_ref[...])
        f = lb + (1.0 - lb) * _sigmoid(proj(P_F))
        ft_scr[...] = f.T
        q_scr[...] = _silu(proj(P_Q))
        v_scr[...] = proj(P_I)
        za_scr[...] = _silu(proj(P_ZA))
        u = proj(P_C) * proj(P_H)
        cw = cw_ref[...]
        conv = cw[CONV_W - 1:CONV_W, :] * u
        for j in range(CONV_W - 1):
            ctx_j = ctx_ref[:, j, :]
            conv = conv + cw[j:j + 1, :] * ctx_j
            if j > 0:
                ctx_out_ref[:, j - 1, :] = ctx_j
        ctx_out_ref[:, CONV_W - 2, :] = u
        pb_scr[...] = proj(P_B) * conv * _silu(proj(P_ZB))
        ga_scr[...] = _sigmoid(proj(P_GA))
        gb_scr[...] = _sigmoid(proj(P_GB))

    shift = (n_rows - i * G) % n_rows
    f_cols = pltpu.roll(ft_scr[...], shift, 1)
    r0 = pl.multiple_of(i * G, G)
    v_rows = v_scr[pl.ds(r0, G), :]
    q_rows = q_scr[pl.ds(r0, G), :].astype(BF16)
    row_id = lax.broadcasted_iota(jnp.int32, (G, HEAD_DIM), 0)
    o_heads = [jnp.zeros((G, HEAD_DIM), F32)] * N_HEADS
    for j in range(G):
        for h, sl in enumerate(HEADS):
            f_b = jnp.broadcast_to(f_cols[sl, j:j + 1], (HEAD_DIM, HEAD_DIM))
            s_new = f_b * st_ref[j, h] + (1.0 - f_b) * v_rows[j:j + 1, sl]
            st_out_ref[j, h] = s_new
            read = _dot(q_rows[:, sl], s_new.astype(BF16))
            o_heads[h] = jnp.where(row_id == j, read, o_heads[h])
    o_scr[pl.ds(r0, G), :] = jnp.concatenate(o_heads, axis=1)

    @pl.when(i == pl.num_programs(0) - 1)
    def _output():
        hn = hn_ref[...]
        for sl in HEADS:
            o_h = o_scr[:, sl]
            o_scr[:, sl] = o_h * _rms_scale(o_h) * hn
        y_a = _dot((o_scr[...] * za_scr[...]).astype(BF16), wa_ref[...])
        y_b = _dot(pb_scr[...].astype(BF16), wb_ref[...])
        merged = ga_scr[...] * y_a + gb_scr[...] * y_b
        out = _dot(merged.astype(BF16), wo_ref[...])
        y_ref[:, 0, :] = x_ref[:, 0, :] + out * _rms_scale(out) * npost_ref[...]


def _run_decode(x, state, ctx, weights):
    n = x.shape[0]
    assert n % DECODE_GROUP == 0 and n == 128
    w_in, npre, npost, lb_logits, hn, cw, wa, wb, wo = weights

    def const(shape):
        zeros = (0,) * len(shape)
        return pl.BlockSpec(shape, lambda i: zeros, pipeline_mode=pl.Buffered(1))

    st_spec = pl.BlockSpec((DECODE_GROUP, N_HEADS, HEAD_DIM, HEAD_DIM), lambda i: (i, 0, 0, 0))
    rows_f32 = pltpu.VMEM((n, D_MODEL), F32)
    cols_f32 = pltpu.VMEM((D_MODEL, n), F32)
    return pl.pallas_call(
        _decode_kernel,
        grid=(n // DECODE_GROUP,),
        in_specs=[const(x.shape), st_spec, const(ctx.shape), const(w_in.shape), const(npre.shape),
                  const(npost.shape), const(lb_logits.shape), const(hn.shape), const(cw.shape),
                  const(wa.shape), const(wb.shape), const(wo.shape)],
        out_specs=[const(x.shape), st_spec, const(ctx.shape)],
        out_shape=[jax.ShapeDtypeStruct(x.shape, F32),
                   jax.ShapeDtypeStruct(state.shape, F32),
                   jax.ShapeDtypeStruct(ctx.shape, F32)],
        scratch_shapes=[cols_f32, rows_f32, rows_f32, rows_f32, rows_f32, rows_f32, rows_f32, rows_f32],
        compiler_params=pltpu.CompilerParams(
            dimension_semantics=("arbitrary",),
            vmem_limit_bytes=V7X_VMEM_LIMIT_BYTES),
        name="decode_step",
    )(x, state, ctx, w_in, npre, npost, lb_logits, hn, cw, wa, wb, wo)


def kernel(x_prompt, x_sample, state_hgrn, state_conv, meta_tokens, w_in, norm_pre, norm_post, lb_logits,
           hgrn_norm, conv_w, w_a, w_b, w_o):
    depth = w_in.shape[0]
    assert depth == 1, "single-layer trunk"
    batch, seq, _ = x_prompt.shape
    dec_batch, dec_seq, _ = x_sample.shape
    assert dec_seq == 1

    weights = (w_in[0], norm_pre, norm_post, lb_logits, hgrn_norm, conv_w[0], w_a[0], w_b[0], w_o[0])

    y_prompt, hgrn_p, conv_p, w_in_bf, wa_bf, wb_bf, wo_bf = _run_prompt(
        x_prompt, meta_tokens.astype(x_prompt.dtype), weights, tile=PROMPT_TILE, chunk=PROMPT_CHUNK)
    weights = (w_in_bf,) + weights[1:6] + (wa_bf, wb_bf, wo_bf)

    y_s, hgrn_s, conv_s = _run_decode(x_sample, state_hgrn[0], state_conv[0], weights)

    return (y_prompt, y_s, hgrn_p[None], hgrn_s[None], conv_p[None], conv_s[None])
```

```python
import functools

import jax
import jax.numpy as jnp
from jax import lax
from jax.experimental import pallas as pl
from jax.experimental.pallas import tpu as pltpu

D_MODEL = 1024
N_HEADS = 8
HEAD_DIM = D_MODEL // N_HEADS
CONV_W = 3
EPS = 1e-6
P_Q, P_F, P_I, P_ZA, P_B, P_C, P_H, P_ZB, P_GA, P_GB = range(10)

PROMPT_TILE = 256
PROMPT_CHUNK = 128
DECODE_GROUP = 8
LOAD_BANDS = 4
SUBLANES = 8
CTX_ROW0 = SUBLANES - (CONV_W - 1)
MAX_HALF_CHUNK_LOG_DECAY = 80.0
V7X_VMEM_LIMIT_BYTES = 58 * 1024 * 1024

BF16 = jnp.bfloat16
F32 = jnp.float32
HEADS = [slice(h * HEAD_DIM, (h + 1) * HEAD_DIM) for h in range(N_HEADS)]


def _dot(a, b):
    return jnp.dot(a, b, preferred_element_type=F32)


def _dot_nt(a, b):
    return lax.dot_general(a, b, (((1,), (1,)), ((), ())), preferred_element_type=F32)


def _dot_tn(a, b):
    return lax.dot_general(a, b, (((0,), (0,)), ((), ())), preferred_element_type=F32)


def _sigmoid(x):
    return 1.0 / (1.0 + jnp.exp(-x))


def _silu(x):
    return x * _sigmoid(x)


def _rms_scale(x):
    return lax.rsqrt(jnp.mean(x * x, axis=-1, keepdims=True) + EPS)


def _lower_bound(lb_logits):
    m = jnp.max(lb_logits, axis=0, keepdims=True)
    e = jnp.exp(lb_logits - m)
    return e[0:1, :] / jnp.sum(e, axis=0, keepdims=True)


def _block_cumsum(x, block):
    n = x.shape[0]
    ri = lax.broadcasted_iota(jnp.int32, (n, n), 0)
    ci = lax.broadcasted_iota(jnp.int32, (n, n), 1)
    tri = ri >= ci
    if n != block:
        tri = jnp.logical_and(tri, (ri // block) == (ci // block))
    tri = jnp.where(tri, 1.0, 0.0).astype(BF16)
    hi = x.astype(BF16)
    lo = (x - hi.astype(F32)).astype(BF16)
    return _dot(tri, hi) + _dot(tri, lo)


def _seq_kernel(x_ref, meta_ref, w_in_hbm, npre_ref, npost_ref, lb_ref, hn_ref, cw_ref,
                wa_hbm, wb_hbm, wo_hbm,
                y_ref, st_out_ref, c_out_ref, w_in_out, wa_out, wb_out, wo_out,
                st_scr, u_scr, q_scr, k_scr, v_scr, g_scr, o_scr,
                qt_scr, kt_scr, qin_scr, kst_scr, vb_scr, dec_scr, st_meta_scr, ctx_meta_scr,
                w_in_ref, wa_ref, wb_ref, wo_ref, stage_scr, load_sem, store_sem,
                *, tile, chunk):
    T, C = tile, chunk
    n_chunks = T // C
    t = pl.program_id(1)
    first_step = jnp.logical_and(pl.program_id(0) == 0, t == 0)
    last_step = jnp.logical_and(pl.program_id(0) == pl.num_programs(0) - 1, t == pl.num_programs(1) - 1)
    lb = _lower_bound(lb_ref[...])

    col = [pl.ds(j * D_MODEL, D_MODEL) for j in range(w_in_hbm.shape[1] // D_MODEL)]
    groups = [(w_in_hbm.at[:, c], w_in_ref.at[:, c], w_in_out.at[:, c]) for c in col]
    groups += [(wa_hbm, wa_ref, wa_out), (wb_hbm, wb_ref, wb_out), (wo_hbm, wo_ref, wo_out)]

    def load_copies(i):
        band = groups[i][0].shape[0] // LOAD_BANDS
        return [pltpu.make_async_copy(groups[i][0].at[pl.ds(b * band, band), :],
                                      stage_scr.at[i % 2, pl.ds(b * band, band), :],
                                      load_sem.at[i % 2, b]) for b in range(LOAD_BANDS)]

    def store_copy(i):
        return pltpu.make_async_copy(groups[i][1], groups[i][2], store_sem.at[i])

    @pl.when(first_step)
    def _stream_weights():
        for copy in load_copies(0):
            copy.start()
        for i in range(len(groups)):
            if i + 1 < len(groups):
                for copy in load_copies(i + 1):
                    copy.start()
            for copy in load_copies(i):
                copy.wait()
            groups[i][1][...] = stage_scr[i % 2].astype(BF16)
        for i in range(len(groups)):
            store_copy(i).start()

    def project(xn, j):
        return _dot(xn, w_in_ref[:, j * D_MODEL:(j + 1) * D_MODEL])

    @pl.when(first_step)
    def _meta_prefix():
        xm = meta_ref[...]
        n_meta = xm.shape[0]
        xn_m = (xm * _rms_scale(xm) * npre_ref[...]).astype(BF16)
        f_m = lb + (1.0 - lb) * _sigmoid(project(xn_m, P_F))
        g_m = _block_cumsum(jnp.log(f_m), n_meta)
        k_end = ((1.0 - f_m) * jnp.exp(g_m[n_meta - 1:n_meta, :] - g_m)).astype(BF16)
        v_m = project(xn_m, P_I).astype(BF16)
        for h, sl in enumerate(HEADS):
            st_meta_scr[h] = _dot_tn(k_end[:, sl], v_m[:, sl])
        u_m = project(xn_m, P_C) * project(xn_m, P_H)
        ctx_meta_scr[CTX_ROW0:SUBLANES, :] = u_m[n_meta - (CONV_W - 1):n_meta, :]
        dec_scr[...] = jnp.zeros(dec_scr.shape, F32)

    @pl.when(t == 0)
    def _init():
        st_scr[...] = st_meta_scr[...]
        u_scr[CTX_ROW0:SUBLANES, :] = ctx_meta_scr[CTX_ROW0:SUBLANES, :]

    x = x_ref[0]
    xn = (x * _rms_scale(x) * npre_ref[...]).astype(BF16)

    def proj(j):
        return project(xn, j)

    assert (P_Q, P_F, P_I) == (0, 1, 2)
    qfi = _dot(xn, w_in_ref[:, 0:3 * D_MODEL])
    f = lb + (1.0 - lb) * _sigmoid(qfi[:, P_F * D_MODEL:(P_F + 1) * D_MODEL])
    k_scr[...] = 1.0 - f
    q_scr[...] = _silu(qfi[:, P_Q * D_MODEL:(P_Q + 1) * D_MODEL])
    v_scr[...] = qfi[:, P_I * D_MODEL:(P_I + 1) * D_MODEL]
    g_scr[...] = _block_cumsum(jnp.log(f), C)

    g_floor = None
    for c in range(n_chunks):
        rows = slice(c * C, (c + 1) * C)
        gc = g_scr[rows, :]
        g_last = gc[C - 1:C, :]
        g_mid = gc[C // 2 - 1:C // 2, :]
        qc = q_scr[rows, :]
        kc = k_scr[rows, :]
        qt_scr[rows, :] = (qc * jnp.exp(gc - g_mid)).astype(BF16)
        kt_scr[rows, :] = (kc * jnp.exp(g_mid - gc)).astype(BF16)
        qin_scr[rows, :] = (qc * jnp.exp(gc)).astype(BF16)
        kst_scr[rows, :] = (kc * jnp.exp(g_last - gc)).astype(BF16)
        vb_scr[rows, :] = v_scr[rows, :].astype(BF16)
        dec_scr[c:c + 1, :] = jnp.exp(g_last)
        half_floor = jnp.minimum(g_mid, g_last - g_mid)
        g_floor = half_floor if g_floor is None else jnp.minimum(g_floor, half_floor)
    stable = jnp.min(g_floor) >= -MAX_HALF_CHUNK_LOG_DECAY

    dec_cols = dec_scr[...].T
    keep = jnp.logical_and(
        lax.broadcasted_iota(jnp.int32, (C, C), 0) >= lax.broadcasted_iota(jnp.int32, (C, C), 1), stable)
    for c in range(n_chunks):
        rows = slice(c * C, (c + 1) * C)
        scores = [jnp.where(keep, _dot_nt(qt_scr[rows, sl], kt_scr[rows, sl]), 0.0).astype(BF16)
                  for sl in HEADS]
        for h, sl in enumerate(HEADS):
            vb = vb_scr[rows, sl]
            st = st_scr[h]
            o_scr[rows, sl] = _dot(jnp.concatenate([qin_scr[rows, sl], scores[h]], axis=1),
                                   jnp.concatenate([st.astype(BF16), vb], axis=0))
            decay = jnp.broadcast_to(dec_cols[sl, c:c + 1], (HEAD_DIM, HEAD_DIM))
            st_scr[h] = decay * st + _dot_tn(kst_scr[rows, sl], vb)

    @pl.when(jnp.logical_not(stable))
    def _intra_exact():
        same_head = (lax.broadcasted_iota(jnp.int32, (D_MODEL, D_MODEL), 0) // HEAD_DIM ==
                     lax.broadcasted_iota(jnp.int32, (D_MODEL, D_MODEL), 1) // HEAD_DIM)
        head_sum = jnp.where(same_head, 1.0, 0.0).astype(BF16)
        tpos = lax.broadcasted_iota(jnp.int32, (C, 1), 0)

        def chunk_step(c, carry):
            r0 = pl.multiple_of(c * C, C)
            rows = pl.ds(r0, C)
            gc = g_scr[rows, :]
            qc = q_scr[rows, :]

            def src_step(i, carry2):
                src = pl.ds(r0 + i, 1)
                p = qc * jnp.exp(jnp.minimum(gc - g_scr[src, :], 0.0)) * k_scr[src, :]
                p = jnp.where(tpos >= i, p, 0.0).astype(BF16)
                o_scr[rows, :] += _dot(p, head_sum) * v_scr[src, :]
                return carry2

            return lax.fori_loop(0, C, src_step, carry)

        lax.fori_loop(0, n_chunks, chunk_step, 0)

    hn = hn_ref[...]
    for sl in HEADS:
        o_h = o_scr[:, sl]
        o_scr[:, sl] = o_h * _rms_scale(o_h) * hn
    assert (P_ZA, P_B, P_C, P_H, P_ZB) == tuple(range(P_ZA, P_ZB + 1)) and P_GB == P_GA + 1
    tail = _dot(xn, w_in_ref[:, P_ZA * D_MODEL:(P_ZB + 1) * D_MODEL])

    def tail_block(j):
        return tail[:, (j - P_ZA) * D_MODEL:(j - P_ZA + 1) * D_MODEL]

    y_a = _dot((o_scr[...] * _silu(tail_block(P_ZA))).astype(BF16), wa_ref[...])

    u = tail_block(P_C) * tail_block(P_H)
    u_scr[SUBLANES:SUBLANES + T, :] = u
    cw = cw_ref[...]
    conv = cw[CONV_W - 1:CONV_W, :] * u
    for j in range(CONV_W - 1):
        conv = conv + cw[j:j + 1, :] * u_scr[CTX_ROW0 + j:CTX_ROW0 + j + T, :]
    y_b = _dot((tail_block(P_B) * conv * _silu(tail_block(P_ZB))).astype(BF16), wb_ref[...])
    new_ctx = u[T - (CONV_W - 1):T, :]
    u_scr[CTX_ROW0:SUBLANES, :] = new_ctx

    gates = _sigmoid(_dot(xn, w_in_ref[:, P_GA * D_MODEL:(P_GB + 1) * D_MODEL]))
    merged = gates[:, :D_MODEL] * y_a + gates[:, D_MODEL:] * y_b
    out = _dot(merged.astype(BF16), wo_ref[...])
    y_ref[0] = x + out * _rms_scale(out) * npost_ref[...]

    @pl.when(t == pl.num_programs(1) - 1)
    def _finish():
        c_out_ref[0] = new_ctx
        st_out_ref[0] = st_scr[...]

    @pl.when(last_step)
    def _weights_written():
        for i in range(len(groups)):
            store_copy(i).wait()


def _const_spec(shape):
    zeros = (0,) * len(shape)
    return pl.BlockSpec(shape, lambda b, t: zeros, pipeline_mode=pl.Buffered(1))


def _run_prompt(x, meta, weights, *, tile, chunk):
    n, length, _ = x.shape
    assert length % tile == 0 and tile % chunk == 0 and chunk % (2 * SUBLANES) == 0 and tile >= CONV_W - 1
    assert meta.shape[0] % (2 * SUBLANES) == 0 and meta.shape[0] >= CONV_W - 1
    assert tile // chunk <= HEAD_DIM
    w_in, npre, npost, lb_logits, hn, cw, wa, wb, wo = weights
    kern = functools.partial(_seq_kernel, tile=tile, chunk=chunk)
    hbm = pl.BlockSpec(memory_space=pl.ANY)
    n_groups = w_in.shape[1] // D_MODEL + 3
    assert w_in.shape[1] % D_MODEL == 0 and wa.shape == wb.shape == wo.shape == (w_in.shape[0], D_MODEL)
    state_shape = (1, N_HEADS, HEAD_DIM, HEAD_DIM)
    ctx_shape = (1, CONV_W - 1, D_MODEL)
    tile_f32 = pltpu.VMEM((tile, D_MODEL), F32)
    tile_bf16 = pltpu.VMEM((tile, D_MODEL), BF16)
    return pl.pallas_call(
        kern,
        grid=(n, length // tile),
        in_specs=[
            pl.BlockSpec((1, tile, D_MODEL), lambda b, t: (b, t, 0)),
            _const_spec(meta.shape),
            hbm,
            _const_spec(npre.shape),
            _const_spec(npost.shape),
            _const_spec(lb_logits.shape),
            _const_spec(hn.shape),
            _const_spec(cw.shape),
            hbm, hbm, hbm,
        ],
        out_specs=[
            pl.BlockSpec((1, tile, D_MODEL), lambda b, t: (b, t, 0)),
            pl.BlockSpec(state_shape, lambda b, t: (b, 0, 0, 0)),
            pl.BlockSpec(ctx_shape, lambda b, t: (b, 0, 0)),
            hbm, hbm, hbm, hbm,
        ],
        out_shape=[
            jax.ShapeDtypeStruct(x.shape, F32),
            jax.ShapeDtypeStruct((n,) + state_shape[1:], F32),
            jax.ShapeDtypeStruct((n,) + ctx_shape[1:], F32),
            jax.ShapeDtypeStruct(w_in.shape, BF16),
            jax.ShapeDtypeStruct(wa.shape, BF16),
            jax.ShapeDtypeStruct(wb.shape, BF16),
            jax.ShapeDtypeStruct(wo.shape, BF16),
        ],
        scratch_shapes=[
            pltpu.VMEM((N_HEADS, HEAD_DIM, HEAD_DIM), F32),
            pltpu.VMEM((tile + SUBLANES, D_MODEL), F32),
            tile_f32, tile_f32, tile_f32, tile_f32, tile_f32,
            tile_bf16, tile_bf16, tile_bf16, tile_bf16, tile_bf16,
            pltpu.VMEM((HEAD_DIM, D_MODEL), F32),
            pltpu.VMEM((N_HEADS, HEAD_DIM, HEAD_DIM), F32),
            pltpu.VMEM((SUBLANES, D_MODEL), F32),
            pltpu.VMEM(w_in.shape, BF16),
            pltpu.VMEM(wa.shape, BF16), pltpu.VMEM(wb.shape, BF16), pltpu.VMEM(wo.shape, BF16),
            pltpu.VMEM((2,) + wa.shape, F32),
            pltpu.SemaphoreType.DMA((2, LOAD_BANDS)),
            pltpu.SemaphoreType.DMA((n_groups,)),
        ],
        compiler_params=pltpu.CompilerParams(
            dimension_semantics=("arbitrary", "arbitrary"),
            vmem_limit_bytes=V7X_VMEM_LIMIT_BYTES),
        name="prompt_sweep",
    )(x, meta, w_in, npre, npost, lb_logits, hn, cw, wa, wb, wo)


def _decode_kernel(x_ref, st_ref, ctx_ref, w_in_ref, npre_ref, npost_ref, lb_ref, hn_ref, cw_ref,
                   wa_ref, wb_ref, wo_ref,
                   y_ref, st_out_ref, ctx_out_ref,
                   ft_scr, q_scr, v_scr, o_scr, za_scr, pb_scr, ga_scr, gb_scr):
    G = DECODE_GROUP
    i = pl.program_id(0)
    n_rows = x_ref.shape[0]

    @pl.when(i == 0)
    def _project():
        x = x_ref[:, 0, :]
        xn = (x * _rms_scale(x) * npre_ref[...]).astype(BF16)

        def proj(j):
            return _dot(xn, w_in_ref[:, j * D_MODEL:(j + 1) * D_MODEL])

        lb = _lower_bound(lb_ref[...])
        f = lb + (1.0 - lb) * _sigmoid(proj(P_F))
        ft_scr[...] = f.T
        q_scr[...] = _silu(proj(P_Q))
        v_scr[...] = proj(P_I)
        za_scr[...] = _silu(proj(P_ZA))
        u = proj(P_C) * proj(P_H)
        cw = cw_ref[...]
        conv = cw[CONV_W - 1:CONV_W, :] * u
        for j in range(CONV_W - 1):
            ctx_j = ctx_ref[:, j, :]
            conv = conv + cw[j:j + 1, :] * ctx_j
            if j > 0:
                ctx_out_ref[:, j - 1, :] = ctx_j
        ctx_out_ref[:, CONV_W - 2, :] = u
        pb_scr[...] = proj(P_B) * conv * _silu(proj(P_ZB))
        ga_scr[...] = _sigmoid(proj(P_GA))
        gb_scr[...] = _sigmoid(proj(P_GB))

    shift = (n_rows - i * G) % n_rows
    f_cols = pltpu.roll(ft_scr[...], shift, 1)
    r0 = pl.multiple_of(i * G, G)
    v_rows = v_scr[pl.ds(r0, G), :]
    q_rows = q_scr[pl.ds(r0, G), :].astype(BF16)
    row_id = lax.broadcasted_iota(jnp.int32, (G, HEAD_DIM), 0)
    o_heads = [jnp.zeros((G, HEAD_DIM), F32)] * N_HEADS
    for j in range(G):
        for h, sl in enumerate(HEADS):
            f_b = jnp.broadcast_to(f_cols[sl, j:j + 1], (HEAD_DIM, HEAD_DIM))
            s_new = f_b * st_ref[j, h] + (1.0 - f_b) * v_rows[j:j + 1, sl]
            st_out_ref[j, h] = s_new
            read = _dot(q_rows[:, sl], s_new.astype(BF16))
            o_heads[h] = jnp.where(row_id == j, read, o_heads[h])
    o_scr[pl.ds(r0, G), :] = jnp.concatenate(o_heads, axis=1)

    @pl.when(i == pl.num_programs(0) - 1)
    def _output():
        hn = hn_ref[...]
        for sl in HEADS:
            o_h = o_scr[:, sl]
            o_scr[:, sl] = o_h * _rms_scale(o_h) * hn
        y_a = _dot((o_scr[...] * za_scr[...]).astype(BF16), wa_ref[...])
        y_b = _dot(pb_scr[...].astype(BF16), wb_ref[...])
        merged = ga_scr[...] * y_a + gb_scr[...] * y_b
        out = _dot(merged.astype(BF16), wo_ref[...])
        y_ref[:, 0, :] = x_ref[:, 0, :] + out * _rms_scale(out) * npost_ref[...]


def _run_decode(x, state, ctx, weights):
    n = x.shape[0]
    assert n % DECODE_GROUP == 0 and n == 128
    w_in, npre, npost, lb_logits, hn, cw, wa, wb, wo = weights

    def const(shape):
        zeros = (0,) * len(shape)
        return pl.BlockSpec(shape, lambda i: zeros, pipeline_mode=pl.Buffered(1))

    st_spec = pl.BlockSpec((DECODE_GROUP, N_HEADS, HEAD_DIM, HEAD_DIM), lambda i: (i, 0, 0, 0))
    rows_f32 = pltpu.VMEM((n, D_MODEL), F32)
    cols_f32 = pltpu.VMEM((D_MODEL, n), F32)
    return pl.pallas_call(
        _decode_kernel,
        grid=(n // DECODE_GROUP,),
        in_specs=[const(x.shape), st_spec, const(ctx.shape), const(w_in.shape), const(npre.shape),
                  const(npost.shape), const(lb_logits.shape), const(hn.shape), const(cw.shape),
                  const(wa.shape), const(wb.shape), const(wo.shape)],
        out_specs=[const(x.shape), st_spec, const(ctx.shape)],
        out_shape=[jax.ShapeDtypeStruct(x.shape, F32),
                   jax.ShapeDtypeStruct(state.shape, F32),
                   jax.ShapeDtypeStruct(ctx.shape, F32)],
        scratch_shapes=[cols_f32, rows_f32, rows_f32, rows_f32, rows_f32, rows_f32, rows_f32, rows_f32],
        compiler_params=pltpu.CompilerParams(
            dimension_semantics=("arbitrary",),
            vmem_limit_bytes=V7X_VMEM_LIMIT_BYTES),
        name="decode_step",
    )(x, state, ctx, w_in, npre, npost, lb_logits, hn, cw, wa, wb, wo)


def kernel(x_prompt, x_sample, state_hgrn, state_conv, meta_tokens, w_in, norm_pre, norm_post, lb_logits,
           hgrn_norm, conv_w, w_a, w_b, w_o):
    depth = w_in.shape[0]
    assert depth == 1, "single-layer trunk"
    batch, seq, _ = x_prompt.shape
    dec_batch, dec_seq, _ = x_sample.shape
    assert dec_seq == 1

    weights = (w_in[0], norm_pre, norm_post, lb_logits, hgrn_norm, conv_w[0], w_a[0], w_b[0], w_o[0])

    y_prompt, hgrn_p, conv_p, w_in_bf, wa_bf, wb_bf, wo_bf = _run_prompt(
        x_prompt, meta_tokens.astype(x_prompt.dtype), weights, tile=PROMPT_TILE, chunk=PROMPT_CHUNK)
    weights = (w_in_bf,) + weights[1:6] + (wa_bf, wb_bf, wo_bf)

    y_s, hgrn_s, conv_s = _run_decode(x_sample, state_hgrn[0], state_conv[0], weights)

    return (y_prompt, y_s, hgrn_p[None], hgrn_s[None], conv_p[None], conv_s[None])
```

```python
import functools

import jax
import jax.numpy as jnp
from jax import lax
from jax.experimental import pallas as pl
from jax.experimental.pallas import tpu as pltpu

D_MODEL = 1024
N_HEADS = 8
HEAD_DIM = D_MODEL // N_HEADS
CONV_W = 3
EPS = 1e-6
P_Q, P_F, P_I, P_ZA, P_B, P_C, P_H, P_ZB, P_GA, P_GB = range(10)

PROMPT_TILE = 256
PROMPT_CHUNK = 128
DECODE_GROUP = 8
LOAD_BANDS = 4
SUBLANES = 8
CTX_ROW0 = SUBLANES - (CONV_W - 1)
MAX_HALF_CHUNK_LOG_DECAY = 80.0
V7X_VMEM_LIMIT_BYTES = 58 * 1024 * 1024

BF16 = jnp.bfloat16
F32 = jnp.float32
HEADS = [slice(h * HEAD_DIM, (h + 1) * HEAD_DIM) for h in range(N_HEADS)]


def _dot(a, b):
    return jnp.dot(a, b, preferred_element_type=F32)


def _dot_nt(a, b):
    return lax.dot_general(a, b, (((1,), (1,)), ((), ())), preferred_element_type=F32)


def _dot_tn(a, b):
    return lax.dot_general(a, b, (((0,), (0,)), ((), ())), preferred_element_type=F32)


def _sigmoid(x):
    return 1.0 / (1.0 + jnp.exp(-x))


def _silu(x):
    return x * _sigmoid(x)


def _rms_scale(x):
    return lax.rsqrt(jnp.mean(x * x, axis=-1, keepdims=True) + EPS)


def _lower_bound(lb_logits):
    m = jnp.max(lb_logits, axis=0, keepdims=True)
    e = jnp.exp(lb_logits - m)
    return e[0:1, :] / jnp.sum(e, axis=0, keepdims=True)


def _block_cumsum(x, block):
    n = x.shape[0]
    ri = lax.broadcasted_iota(jnp.int32, (n, n), 0)
    ci = lax.broadcasted_iota(jnp.int32, (n, n), 1)
    tri = ri >= ci
    if n != block:
        tri = jnp.logical_and(tri, (ri // block) == (ci // block))
    tri = jnp.where(tri, 1.0, 0.0).astype(BF16)
    hi = x.astype(BF16)
    lo = (x - hi.astype(F32)).astype(BF16)
    return _dot(tri, hi) + _dot(tri, lo)


def _seq_kernel(x_ref, meta_ref, w_in_hbm, npre_ref, npost_ref, lb_ref, hn_ref, cw_ref,
                wa_hbm, wb_hbm, wo_hbm,
                y_ref, st_out_ref, c_out_ref, w_in_out, wa_out, wb_out, wo_out,
                st_scr, u_scr, q_scr, k_scr, v_scr, g_scr, o_scr,
                qt_scr, kt_scr, qin_scr, kst_scr, vb_scr, dec_scr, st_meta_scr, ctx_meta_scr,
                w_in_ref, wa_ref, wb_ref, wo_ref, stage_scr, load_sem, store_sem,
                *, tile, chunk):
    T, C = tile, chunk
    n_chunks = T // C
    t = pl.program_id(1)
    first_step = jnp.logical_and(pl.program_id(0) == 0, t == 0)
    last_step = jnp.logical_and(pl.program_id(0) == pl.num_programs(0) - 1, t == pl.num_programs(1) - 1)
    lb = _lower_bound(lb_ref[...])

    col = [pl.ds(j * D_MODEL, D_MODEL) for j in range(w_in_hbm.shape[1] // D_MODEL)]
    groups = [(w_in_hbm.at[:, c], w_in_ref.at[:, c], w_in_out.at[:, c]) for c in col]
    groups += [(wa_hbm, wa_ref, wa_out), (wb_hbm, wb_ref, wb_out), (wo_hbm, wo_ref, wo_out)]

    def load_copies(i):
        band = groups[i][0].shape[0] // LOAD_BANDS
        return [pltpu.make_async_copy(groups[i][0].at[pl.ds(b * band, band), :],
                                      stage_scr.at[i % 2, pl.ds(b * band, band), :],
                                      load_sem.at[i % 2, b]) for b in range(LOAD_BANDS)]

    def store_copy(i):
        return pltpu.make_async_copy(groups[i][1], groups[i][2], store_sem.at[i])

    @pl.when(first_step)
    def _stream_weights():
        for copy in load_copies(0):
            copy.start()
        for i in range(len(groups)):
            if i + 1 < len(groups):
                for copy in load_copies(i + 1):
                    copy.start()
            for copy in load_copies(i):
                copy.wait()
            groups[i][1][...] = stage_scr[i % 2].astype(BF16)
        for i in range(len(groups)):
            store_copy(i).start()

    def project(xn, j):
        return _dot(xn, w_in_ref[:, j * D_MODEL:(j + 1) * D_MODEL])

    @pl.when(first_step)
    def _meta_prefix():
        xm = meta_ref[...]
        n_meta = xm.shape[0]
        xn_m = (xm * _rms_scale(xm) * npre_ref[...]).astype(BF16)
        f_m = lb + (1.0 - lb) * _sigmoid(project(xn_m, P_F))
        g_m = _block_cumsum(jnp.log(f_m), n_meta)
        k_end = ((1.0 - f_m) * jnp.exp(g_m[n_meta - 1:n_meta, :] - g_m)).astype(BF16)
        v_m = project(xn_m, P_I).astype(BF16)
        for h, sl in enumerate(HEADS):
            st_meta_scr[h] = _dot_tn(k_end[:, sl], v_m[:, sl])
        u_m = project(xn_m, P_C) * project(xn_m, P_H)
        ctx_meta_scr[CTX_ROW0:SUBLANES, :] = u_m[n_meta - (CONV_W - 1):n_meta, :]
        dec_scr[...] = jnp.zeros(dec_scr.shape, F32)

    @pl.when(t == 0)
    def _init():
        st_scr[...] = st_meta_scr[...]
        u_scr[CTX_ROW0:SUBLANES, :] = ctx_meta_scr[CTX_ROW0:SUBLANES, :]

    x = x_ref[0]
    xn = (x * _rms_scale(x) * npre_ref[...]).astype(BF16)

    def proj(j):
        return project(xn, j)

    f = lb + (1.0 - lb) * _sigmoid(proj(P_F))
    k_scr[...] = 1.0 - f
    g_scr[...] = _block_cumsum(jnp.log(f), C)
    g_floor = None
    for c in range(n_chunks):
        g_mid = g_scr[c * C + C // 2 - 1:c * C + C // 2, :]
        g_last = g_scr[(c + 1) * C - 1:(c + 1) * C, :]
        half_floor = jnp.minimum(g_mid, g_last - g_mid)
        g_floor = half_floor if g_floor is None else jnp.minimum(g_floor, half_floor)
    stable = jnp.min(g_floor) >= -MAX_HALF_CHUNK_LOG_DECAY

    def rest_of_tile(exact_intra):
        q_scr[...] = _silu(proj(P_Q))
        v_scr[...] = proj(P_I)

        for c in range(n_chunks):
            rows = slice(c * C, (c + 1) * C)
            gc = g_scr[rows, :]
            g_last = gc[C - 1:C, :]
            g_mid = gc[C // 2 - 1:C // 2, :]
            qc = q_scr[rows, :]
            kc = k_scr[rows, :]
            if not exact_intra:
                qt_scr[rows, :] = (qc * jnp.exp(gc - g_mid)).astype(BF16)
                kt_scr[rows, :] = (kc * jnp.exp(g_mid - gc)).astype(BF16)
            qin_scr[rows, :] = (qc * jnp.exp(gc)).astype(BF16)
            kst_scr[rows, :] = (kc * jnp.exp(g_last - gc)).astype(BF16)
            vb_scr[rows, :] = v_scr[rows, :].astype(BF16)
            dec_scr[c:c + 1, :] = jnp.exp(g_last)

        def conv_branch():
            u = proj(P_C) * proj(P_H)
            u_scr[SUBLANES:SUBLANES + T, :] = u
            cw = cw_ref[...]
            conv = cw[CONV_W - 1:CONV_W, :] * u
            for j in range(CONV_W - 1):
                conv = conv + cw[j:j + 1, :] * u_scr[CTX_ROW0 + j:CTX_ROW0 + j + T, :]
            y_b = _dot((proj(P_B) * conv * _silu(proj(P_ZB))).astype(BF16), wb_ref[...])
            u_scr[CTX_ROW0:SUBLANES, :] = u[T - (CONV_W - 1):T, :]
            return y_b

        dec_cols = dec_scr[...].T
        causal = (lax.broadcasted_iota(jnp.int32, (C, C), 0) >= lax.broadcasted_iota(jnp.int32, (C, C), 1))
        side_work = [conv_branch, lambda: _silu(proj(P_ZA))]
        side = []
        for c in range(n_chunks):
            rows = slice(c * C, (c + 1) * C)
            if not exact_intra:
                scores = [jnp.where(causal, _dot_nt(qt_scr[rows, sl], kt_scr[rows, sl]), 0.0).astype(BF16)
                          for sl in HEADS]
            for h, sl in enumerate(HEADS):
                vb = vb_scr[rows, sl]
                st = st_scr[h]
                if exact_intra:
                    o_scr[rows, sl] = _dot(qin_scr[rows, sl], st.astype(BF16))
                else:
                    o_scr[rows, sl] = _dot(jnp.concatenate([qin_scr[rows, sl], scores[h]], axis=1),
                                           jnp.concatenate([st.astype(BF16), vb], axis=0))
                decay = jnp.broadcast_to(dec_cols[sl, c:c + 1], (HEAD_DIM, HEAD_DIM))
                st_scr[h] = decay * st + _dot_tn(kst_scr[rows, sl], vb)
            if len(side) < len(side_work):
                side.append(side_work[len(side)]())
        while len(side) < len(side_work):
            side.append(side_work[len(side)]())
        y_b, silu_za = side

        if exact_intra:
            same_head = (lax.broadcasted_iota(jnp.int32, (D_MODEL, D_MODEL), 0) // HEAD_DIM ==
                         lax.broadcasted_iota(jnp.int32, (D_MODEL, D_MODEL), 1) // HEAD_DIM)
            head_sum = jnp.where(same_head, 1.0, 0.0).astype(BF16)
            tpos = lax.broadcasted_iota(jnp.int32, (C, 1), 0)

            def chunk_step(c, carry):
                r0 = pl.multiple_of(c * C, C)
                rows = pl.ds(r0, C)
                gc = g_scr[rows, :]
                qc = q_scr[rows, :]

                def src_step(i, carry2):
                    src = pl.ds(r0 + i, 1)
                    p = qc * jnp.exp(jnp.minimum(gc - g_scr[src, :], 0.0)) * k_scr[src, :]
                    p = jnp.where(tpos >= i, p, 0.0).astype(BF16)
                    o_scr[rows, :] += _dot(p, head_sum) * v_scr[src, :]
                    return carry2

                return lax.fori_loop(0, C, src_step, carry)

            lax.fori_loop(0, n_chunks, chunk_step, 0)

        hn = hn_ref[...]
        for sl in HEADS:
            o_h = o_scr[:, sl]
            o_scr[:, sl] = o_h * _rms_scale(o_h) * hn
        y_a = _dot((o_scr[...] * silu_za).astype(BF16), wa_ref[...])

        merged = _sigmoid(proj(P_GA)) * y_a + _sigmoid(proj(P_GB)) * y_b
        out = _dot(merged.astype(BF16), wo_ref[...])
        y_ref[0] = x + out * _rms_scale(out) * npost_ref[...]

    @pl.when(stable)
    def _tile():
        rest_of_tile(exact_intra=False)

    @pl.when(jnp.logical_not(stable))
    def _tile_exact():
        rest_of_tile(exact_intra=True)

    @pl.when(t == pl.num_programs(1) - 1)
    def _finish():
        c_out_ref[0] = u_scr[CTX_ROW0:SUBLANES, :]
        st_out_ref[0] = st_scr[...]

    @pl.when(last_step)
    def _weights_written():
        for i in range(len(groups)):
            store_copy(i).wait()


def _const_spec(shape):
    zeros = (0,) * len(shape)
    return pl.BlockSpec(shape, lambda b, t: zeros, pipeline_mode=pl.Buffered(1))


def _run_prompt(x, meta, weights, *, tile, chunk):
    n, length, _ = x.shape
    assert length % tile == 0 and tile % chunk == 0 and chunk % (2 * SUBLANES) == 0 and tile >= CONV_W - 1
    assert meta.shape[0] % (2 * SUBLANES) == 0 and meta.shape[0] >= CONV_W - 1
    assert tile // chunk <= HEAD_DIM
    w_in, npre, npost, lb_logits, hn, cw, wa, wb, wo = weights
    kern = functools.partial(_seq_kernel, tile=tile, chunk=chunk)
    hbm = pl.BlockSpec(memory_space=pl.ANY)
    n_groups = w_in.shape[1] // D_MODEL + 3
    assert w_in.shape[1] % D_MODEL == 0 and wa.shape == wb.shape == wo.shape == (w_in.shape[0], D_MODEL)
    state_shape = (1, N_HEADS, HEAD_DIM, HEAD_DIM)
    ctx_shape = (1, CONV_W - 1, D_MODEL)
    tile_f32 = pltpu.VMEM((tile, D_MODEL), F32)
    tile_bf16 = pltpu.VMEM((tile, D_MODEL), BF16)
    return pl.pallas_call(
        kern,
        grid=(n, length // tile),
        in_specs=[
            pl.BlockSpec((1, tile, D_MODEL), lambda b, t: (b, t, 0)),
            _const_spec(meta.shape),
            hbm,
            _const_spec(npre.shape),
            _const_spec(npost.shape),
            _const_spec(lb_logits.shape),
            _const_spec(hn.shape),
            _const_spec(cw.shape),
            hbm, hbm, hbm,
        ],
        out_specs=[
            pl.BlockSpec((1, tile, D_MODEL), lambda b, t: (b, t, 0)),
            pl.BlockSpec(state_shape, lambda b, t: (b, 0, 0, 0)),
            pl.BlockSpec(ctx_shape, lambda b, t: (b, 0, 0)),
            hbm, hbm, hbm, hbm,
        ],
        out_shape=[
            jax.ShapeDtypeStruct(x.shape, F32),
            jax.ShapeDtypeStruct((n,) + state_shape[1:], F32),
            jax.ShapeDtypeStruct((n,) + ctx_shape[1:], F32),
            jax.ShapeDtypeStruct(w_in.shape, BF16),
            jax.ShapeDtypeStruct(wa.shape, BF16),
            jax.ShapeDtypeStruct(wb.shape, BF16),
            jax.ShapeDtypeStruct(wo.shape, BF16),
        ],
        scratch_shapes=[
            pltpu.VMEM((N_HEADS, HEAD_DIM, HEAD_DIM), F32),
            pltpu.VMEM((tile + SUBLANES, D_MODEL), F32),
            tile_f32, tile_f32, tile_f32, tile_f32, tile_f32,
            tile_bf16, tile_bf16, tile_bf16, tile_bf16, tile_bf16,
            pltpu.VMEM((HEAD_DIM, D_MODEL), F32),
            pltpu.VMEM((N_HEADS, HEAD_DIM, HEAD_DIM), F32),
            pltpu.VMEM((SUBLANES, D_MODEL), F32),
            pltpu.VMEM(w_in.shape, BF16),
            pltpu.VMEM(wa.shape, BF16), pltpu.VMEM(wb.shape, BF16), pltpu.VMEM(wo.shape, BF16),
            pltpu.VMEM((2,) + wa.shape, F32),
            pltpu.SemaphoreType.DMA((2, LOAD_BANDS)),
            pltpu.SemaphoreType.DMA((n_groups,)),
        ],
        compiler_params=pltpu.CompilerParams(
            dimension_semantics=("arbitrary", "arbitrary"),
            vmem_limit_bytes=V7X_VMEM_LIMIT_BYTES),
        name="prompt_sweep",
    )(x, meta, w_in, npre, npost, lb_logits, hn, cw, wa, wb, wo)


def _decode_kernel(x_ref, st_ref, ctx_ref, w_in_ref, npre_ref, npost_ref, lb_ref, hn_ref, cw_ref,
                   wa_ref, wb_ref, wo_ref,
                   y_ref, st_out_ref, ctx_out_ref,
                   ft_scr, q_scr, v_scr, o_scr, za_scr, pb_scr, ga_scr, gb_scr):
    G = DECODE_GROUP
    i = pl.program_id(0)
    n_rows = x_ref.shape[0]

    @pl.when(i == 0)
    def _project():
        x = x_ref[:, 0, :]
        xn = (x * _rms_scale(x) * npre_ref[...]).astype(BF16)

        def proj(j):
            return _dot(xn, w_in_ref[:, j * D_MODEL:(j + 1) * D_MODEL])

        lb = _lower_bound(lb_ref[...])
        f = lb + (1.0 - lb) * _sigmoid(proj(P_F))
        ft_scr[...] = f.T
        q_scr[...] = _silu(proj(P_Q))
        v_scr[...] = proj(P_I)
        za_scr[...] = _silu(proj(P_ZA))
        u = proj(P_C) * proj(P_H)
        cw = cw_ref[...]
        conv = cw[CONV_W - 1:CONV_W, :] * u
        for j in range(CONV_W - 1):
            ctx_j = ctx_ref[:, j, :]
            conv = conv + cw[j:j + 1, :] * ctx_j
            if j > 0:
                ctx_out_ref[:, j - 1, :] = ctx_j
        ctx_out_ref[:, CONV_W - 2, :] = u
        pb_scr[...] = proj(P_B) * conv * _silu(proj(P_ZB))
        ga_scr[...] = _sigmoid(proj(P_GA))
        gb_scr[...] = _sigmoid(proj(P_GB))

    shift = (n_rows - i * G) % n_rows
    f_cols = pltpu.roll(ft_scr[...], shift, 1)
    r0 = pl.multiple_of(i * G, G)
    v_rows = v_scr[pl.ds(r0, G), :]
    q_rows = q_scr[pl.ds(r0, G), :].astype(BF16)
    row_id = lax.broadcasted_iota(jnp.int32, (G, HEAD_DIM), 0)
    o_heads = [jnp.zeros((G, HEAD_DIM), F32)] * N_HEADS
    for j in range(G):
        for h, sl in enumerate(HEADS):
            f_b = jnp.broadcast_to(f_cols[sl, j:j + 1], (HEAD_DIM, HEAD_DIM))
            s_new = f_b * st_ref[j, h] + (1.0 - f_b) * v_rows[j:j + 1, sl]
            st_out_ref[j, h] = s_new
            read = _dot(q_rows[:, sl], s_new.astype(BF16))
            o_heads[h] = jnp.where(row_id == j, read, o_heads[h])
    o_scr[pl.ds(r0, G), :] = jnp.concatenate(o_heads, axis=1)

    @pl.when(i == pl.num_programs(0) - 1)
    def _output():
        hn = hn_ref[...]
        for sl in HEADS:
            o_h = o_scr[:, sl]
            o_scr[:, sl] = o_h * _rms_scale(o_h) * hn
        y_a = _dot((o_scr[...] * za_scr[...]).astype(BF16), wa_ref[...])
        y_b = _dot(pb_scr[...].astype(BF16), wb_ref[...])
        merged = ga_scr[...] * y_a + gb_scr[...] * y_b
        out = _dot(merged.astype(BF16), wo_ref[...])
        y_ref[:, 0, :] = x_ref[:, 0, :] + out * _rms_scale(out) * npost_ref[...]


def _run_decode(x, state, ctx, weights):
    n = x.shape[0]
    assert n % DECODE_GROUP == 0 and n == 128
    w_in, npre, npost, lb_logits, hn, cw, wa, wb, wo = weights

    def const(shape):
        zeros = (0,) * len(shape)
        return pl.BlockSpec(shape, lambda i: zeros, pipeline_mode=pl.Buffered(1))

    st_spec = pl.BlockSpec((DECODE_GROUP, N_HEADS, HEAD_DIM, HEAD_DIM), lambda i: (i, 0, 0, 0))
    rows_f32 = pltpu.VMEM((n, D_MODEL), F32)
    cols_f32 = pltpu.VMEM((D_MODEL, n), F32)
    return pl.pallas_call(
        _decode_kernel,
        grid=(n // DECODE_GROUP,),
        in_specs=[const(x.shape), st_spec, const(ctx.shape), const(w_in.shape), const(npre.shape),
                  const(npost.shape), const(lb_logits.shape), const(hn.shape), const(cw.shape),
                  const(wa.shape), const(wb.shape), const(wo.shape)],
        out_specs=[const(x.shape), st_spec, const(ctx.shape)],
        out_shape=[jax.ShapeDtypeStruct(x.shape, F32),
                   jax.ShapeDtypeStruct(state.shape, F32),
                   jax.ShapeDtypeStruct(ctx.shape, F32)],
        scratch_shapes=[cols_f32, rows_f32, rows_f32, rows_f32, rows_f32, rows_f32, rows_f32, rows_f32],
        compiler_params=pltpu.CompilerParams(
            dimension_semantics=("arbitrary",),
            vmem_limit_bytes=V7X_VMEM_LIMIT_BYTES),
        name="decode_step",
    )(x, state, ctx, w_in, npre, npost, lb_logits, hn, cw, wa, wb, wo)


def kernel(x_prompt, x_sample, state_hgrn, state_conv, meta_tokens, w_in, norm_pre, norm_post, lb_logits,
           hgrn_norm, conv_w, w_a, w_b, w_o):
    depth = w_in.shape[0]
    assert depth == 1, "single-layer trunk"
    batch, seq, _ = x_prompt.shape
    dec_batch, dec_seq, _ = x_sample.shape
    assert dec_seq == 1

    weights = (w_in[0], norm_pre, norm_post, lb_logits, hgrn_norm, conv_w[0], w_a[0], w_b[0], w_o[0])

    y_prompt, hgrn_p, conv_p, w_in_bf, wa_bf, wb_bf, wo_bf = _run_prompt(
        x_prompt, meta_tokens.astype(x_prompt.dtype), weights, tile=PROMPT_TILE, chunk=PROMPT_CHUNK)
    weights = (w_in_bf,) + weights[1:6] + (wa_bf, wb_bf, wo_bf)

    y_s, hgrn_s, conv_s = _run_decode(x_sample, state_hgrn[0], state_conv[0], weights)

    return (y_prompt, y_s, hgrn_p[None], hgrn_s[None], conv_p[None], conv_s[None])
```

```python
import functools

import jax
import jax.numpy as jnp
from jax import lax
from jax.experimental import pallas as pl
from jax.experimental.pallas import tpu as pltpu

D_MODEL = 1024
N_HEADS = 8
HEAD_DIM = D_MODEL // N_HEADS
CONV_W = 3
EPS = 1e-6
P_Q, P_F, P_I, P_ZA, P_B, P_C, P_H, P_ZB, P_GA, P_GB = range(10)

PROMPT_TILE = 256
PROMPT_CHUNK = 128
DECODE_GROUP = 8
LOAD_BANDS = 4
SUBLANES = 8
CTX_ROW0 = SUBLANES - (CONV_W - 1)
MAX_HALF_CHUNK_LOG_DECAY = 80.0
V7X_VMEM_LIMIT_BYTES = 58 * 1024 * 1024

BF16 = jnp.bfloat16
F32 = jnp.float32
HEADS = [slice(h * HEAD_DIM, (h + 1) * HEAD_DIM) for h in range(N_HEADS)]


def _dot(a, b):
    return jnp.dot(a, b, preferred_element_type=F32)


def _dot_nt(a, b):
    return lax.dot_general(a, b, (((1,), (1,)), ((), ())), preferred_element_type=F32)


def _dot_tn(a, b):
    return lax.dot_general(a, b, (((0,), (0,)), ((), ())), preferred_element_type=F32)


def _sigmoid(x):
    return 1.0 / (1.0 + jnp.exp(-x))


def _silu(x):
    return x * _sigmoid(x)


def _rms_scale(x):
    return lax.rsqrt(jnp.mean(x * x, axis=-1, keepdims=True) + EPS)


def _lower_bound(lb_logits):
    m = jnp.max(lb_logits, axis=0, keepdims=True)
    e = jnp.exp(lb_logits - m)
    return e[0:1, :] / jnp.sum(e, axis=0, keepdims=True)


def _block_cumsum(x, block):
    n = x.shape[0]
    ri = lax.broadcasted_iota(jnp.int32, (n, n), 0)
    ci = lax.broadcasted_iota(jnp.int32, (n, n), 1)
    tri = ri >= ci
    if n != block:
        tri = jnp.logical_and(tri, (ri // block) == (ci // block))
    tri = jnp.where(tri, 1.0, 0.0).astype(BF16)
    hi = x.astype(BF16)
    lo = (x - hi.astype(F32)).astype(BF16)
    return _dot(tri, hi) + _dot(tri, lo)


def _seq_kernel(x_ref, meta_ref, w_in_hbm, npre_ref, npost_ref, lb_ref, hn_ref, cw_ref,
                wa_hbm, wb_hbm, wo_hbm,
                y_ref, st_out_ref, c_out_ref, w_in_out, wa_out, wb_out, wo_out,
                st_scr, u_scr, q_scr, k_scr, v_scr, g_scr, o_scr,
                qt_scr, kt_scr, qin_scr, kst_scr, vb_scr, dec_scr, st_meta_scr, ctx_meta_scr,
                w_in_ref, wa_ref, wb_ref, wo_ref, stage_scr, load_sem, store_sem,
                *, tile, chunk):
    T, C = tile, chunk
    n_chunks = T // C
    t = pl.program_id(1)
    first_step = jnp.logical_and(pl.program_id(0) == 0, t == 0)
    last_step = jnp.logical_and(pl.program_id(0) == pl.num_programs(0) - 1, t == pl.num_programs(1) - 1)
    lb = _lower_bound(lb_ref[...])

    col = [pl.ds(j * D_MODEL, D_MODEL) for j in range(w_in_hbm.shape[1] // D_MODEL)]
    groups = [(w_in_hbm.at[:, c], w_in_ref.at[:, c], w_in_out.at[:, c]) for c in col]
    groups += [(wa_hbm, wa_ref, wa_out), (wb_hbm, wb_ref, wb_out), (wo_hbm, wo_ref, wo_out)]

    def load_copies(i):
        band = groups[i][0].shape[0] // LOAD_BANDS
        return [pltpu.make_async_copy(groups[i][0].at[pl.ds(b * band, band), :],
                                      stage_scr.at[i % 2, pl.ds(b * band, band), :],
                                      load_sem.at[i % 2, b]) for b in range(LOAD_BANDS)]

    def store_copy(i):
        return pltpu.make_async_copy(groups[i][1], groups[i][2], store_sem.at[i])

    @pl.when(first_step)
    def _stream_weights():
        for copy in load_copies(0):
            copy.start()
        for i in range(len(groups)):
            if i + 1 < len(groups):
                for copy in load_copies(i + 1):
                    copy.start()
            for copy in load_copies(i):
                copy.wait()
            groups[i][1][...] = stage_scr[i % 2].astype(BF16)
        for i in range(len(groups)):
            store_copy(i).start()

    def project(xn, j):
        return _dot(xn, w_in_ref[:, j * D_MODEL:(j + 1) * D_MODEL])

    @pl.when(first_step)
    def _meta_prefix():
        xm = meta_ref[...]
        n_meta = xm.shape[0]
        xn_m = (xm * _rms_scale(xm) * npre_ref[...]).astype(BF16)
        f_m = lb + (1.0 - lb) * _sigmoid(project(xn_m, P_F))
        g_m = _block_cumsum(jnp.log(f_m), n_meta)
        k_end = ((1.0 - f_m) * jnp.exp(g_m[n_meta - 1:n_meta, :] - g_m)).astype(BF16)
        v_m = project(xn_m, P_I).astype(BF16)
        for h, sl in enumerate(HEADS):
            st_meta_scr[h] = _dot_tn(k_end[:, sl], v_m[:, sl])
        u_m = project(xn_m, P_C) * project(xn_m, P_H)
        ctx_meta_scr[CTX_ROW0:SUBLANES, :] = u_m[n_meta - (CONV_W - 1):n_meta, :]
        dec_scr[...] = jnp.zeros(dec_scr.shape, F32)

    @pl.when(t == 0)
    def _init():
        st_scr[...] = st_meta_scr[...]
        u_scr[CTX_ROW0:SUBLANES, :] = ctx_meta_scr[CTX_ROW0:SUBLANES, :]

    x = x_ref[0]
    xn = (x * _rms_scale(x) * npre_ref[...]).astype(BF16)

    def proj(j):
        return project(xn, j)

    f = lb + (1.0 - lb) * _sigmoid(proj(P_F))
    k_scr[...] = 1.0 - f
    q_scr[...] = _silu(proj(P_Q))
    vb_scr[...] = proj(P_I).astype(BF16)
    g_scr[...] = _block_cumsum(jnp.log(f), C)

    g_floor = None
    for c in range(n_chunks):
        rows = slice(c * C, (c + 1) * C)
        gc = g_scr[rows, :]
        g_last = gc[C - 1:C, :]
        g_mid = gc[C // 2 - 1:C // 2, :]
        qc = q_scr[rows, :]
        kc = k_scr[rows, :]
        qt_scr[rows, :] = (qc * jnp.exp(gc - g_mid)).astype(BF16)
        kt_scr[rows, :] = (kc * jnp.exp(g_mid - gc)).astype(BF16)
        qin_scr[rows, :] = (qc * jnp.exp(gc)).astype(BF16)
        kst_scr[rows, :] = (kc * jnp.exp(g_last - gc)).astype(BF16)
        dec_scr[c:c + 1, :] = jnp.exp(g_last)
        half_floor = jnp.minimum(g_mid, g_last - g_mid)
        g_floor = half_floor if g_floor is None else jnp.minimum(g_floor, half_floor)
    stable = jnp.min(g_floor) >= -MAX_HALF_CHUNK_LOG_DECAY

    dec_cols = dec_scr[...].T
    keep = jnp.logical_and(
        lax.broadcasted_iota(jnp.int32, (C, C), 0) >= lax.broadcasted_iota(jnp.int32, (C, C), 1), stable)
    for c in range(n_chunks):
        rows = slice(c * C, (c + 1) * C)
        scores = [jnp.where(keep, _dot_nt(qt_scr[rows, sl], kt_scr[rows, sl]), 0.0).astype(BF16)
                  for sl in HEADS]
        for h, sl in enumerate(HEADS):
            vb = vb_scr[rows, sl]
            st = st_scr[h]
            o_scr[rows, sl] = _dot(jnp.concatenate([qin_scr[rows, sl], scores[h]], axis=1),
                                   jnp.concatenate([st.astype(BF16), vb], axis=0))
            decay = jnp.broadcast_to(dec_cols[sl, c:c + 1], (HEAD_DIM, HEAD_DIM))
            st_scr[h] = decay * st + _dot_tn(kst_scr[rows, sl], vb)

    @pl.when(jnp.logical_not(stable))
    def _intra_exact():
        same_head = (lax.broadcasted_iota(jnp.int32, (D_MODEL, D_MODEL), 0) // HEAD_DIM ==
                     lax.broadcasted_iota(jnp.int32, (D_MODEL, D_MODEL), 1) // HEAD_DIM)
        head_sum = jnp.where(same_head, 1.0, 0.0).astype(BF16)
        tpos = lax.broadcasted_iota(jnp.int32, (C, 1), 0)
        v_scr[...] = proj(P_I)

        def chunk_step(c, carry):
            r0 = pl.multiple_of(c * C, C)
            rows = pl.ds(r0, C)
            gc = g_scr[rows, :]
            qc = q_scr[rows, :]

            def src_step(i, carry2):
                src = pl.ds(r0 + i, 1)
                p = qc * jnp.exp(jnp.minimum(gc - g_scr[src, :], 0.0)) * k_scr[src, :]
                p = jnp.where(tpos >= i, p, 0.0).astype(BF16)
                o_scr[rows, :] += _dot(p, head_sum) * v_scr[src, :]
                return carry2

            return lax.fori_loop(0, C, src_step, carry)

        lax.fori_loop(0, n_chunks, chunk_step, 0)

    hn = hn_ref[...]
    for sl in HEADS:
        o_h = o_scr[:, sl]
        o_scr[:, sl] = o_h * _rms_scale(o_h) * hn
    y_a = _dot((o_scr[...] * _silu(proj(P_ZA))).astype(BF16), wa_ref[...])

    u = proj(P_C) * proj(P_H)
    u_scr[SUBLANES:SUBLANES + T, :] = u
    cw = cw_ref[...]
    conv = cw[CONV_W - 1:CONV_W, :] * u
    for j in range(CONV_W - 1):
        conv = conv + cw[j:j + 1, :] * u_scr[CTX_ROW0 + j:CTX_ROW0 + j + T, :]
    y_b = _dot((proj(P_B) * conv * _silu(proj(P_ZB))).astype(BF16), wb_ref[...])
    new_ctx = u[T - (CONV_W - 1):T, :]
    u_scr[CTX_ROW0:SUBLANES, :] = new_ctx

    merged = _sigmoid(proj(P_GA)) * y_a + _sigmoid(proj(P_GB)) * y_b
    out = _dot(merged.astype(BF16), wo_ref[...])
    y_ref[0] = x + out * _rms_scale(out) * npost_ref[...]

    @pl.when(t == pl.num_programs(1) - 1)
    def _finish():
        c_out_ref[0] = new_ctx
        st_out_ref[0] = st_scr[...]

    @pl.when(last_step)
    def _weights_written():
        for i in range(len(groups)):
            store_copy(i).wait()


def _const_spec(shape):
    zeros = (0,) * len(shape)
    return pl.BlockSpec(shape, lambda b, t: zeros, pipeline_mode=pl.Buffered(1))


def _run_prompt(x, meta, weights, *, tile, chunk):
    n, length, _ = x.shape
    assert length % tile == 0 and tile % chunk == 0 and chunk % (2 * SUBLANES) == 0 and tile >= CONV_W - 1
    assert meta.shape[0] % (2 * SUBLANES) == 0 and meta.shape[0] >= CONV_W - 1
    assert tile // chunk <= HEAD_DIM
    w_in, npre, npost, lb_logits, hn, cw, wa, wb, wo = weights
    kern = functools.partial(_seq_kernel, tile=tile, chunk=chunk)
    hbm = pl.BlockSpec(memory_space=pl.ANY)
    n_groups = w_in.shape[1] // D_MODEL + 3
    assert w_in.shape[1] % D_MODEL == 0 and wa.shape == wb.shape == wo.shape == (w_in.shape[0], D_MODEL)
    state_shape = (1, N_HEADS, HEAD_DIM, HEAD_DIM)
    ctx_shape = (1, CONV_W - 1, D_MODEL)
    tile_f32 = pltpu.VMEM((tile, D_MODEL), F32)
    tile_bf16 = pltpu.VMEM((tile, D_MODEL), BF16)
    return pl.pallas_call(
        kern,
        grid=(n, length // tile),
        in_specs=[
            pl.BlockSpec((1, tile, D_MODEL), lambda b, t: (b, t, 0)),
            _const_spec(meta.shape),
            hbm,
            _const_spec(npre.shape),
            _const_spec(npost.shape),
            _const_spec(lb_logits.shape),
            _const_spec(hn.shape),
            _const_spec(cw.shape),
            hbm, hbm, hbm,
        ],
        out_specs=[
            pl.BlockSpec((1, tile, D_MODEL), lambda b, t: (b, t, 0)),
            pl.BlockSpec(state_shape, lambda b, t: (b, 0, 0, 0)),
            pl.BlockSpec(ctx_shape, lambda b, t: (b, 0, 0)),
            hbm, hbm, hbm, hbm,
        ],
        out_shape=[
            jax.ShapeDtypeStruct(x.shape, F32),
            jax.ShapeDtypeStruct((n,) + state_shape[1:], F32),
            jax.ShapeDtypeStruct((n,) + ctx_shape[1:], F32),
            jax.ShapeDtypeStruct(w_in.shape, BF16),
            jax.ShapeDtypeStruct(wa.shape, BF16),
            jax.ShapeDtypeStruct(wb.shape, BF16),
            jax.ShapeDtypeStruct(wo.shape, BF16),
        ],
        scratch_shapes=[
            pltpu.VMEM((N_HEADS, HEAD_DIM, HEAD_DIM), F32),
            pltpu.VMEM((tile + SUBLANES, D_MODEL), F32),
            tile_f32, tile_f32, tile_f32, tile_f32, tile_f32,
            tile_bf16, tile_bf16, tile_bf16, tile_bf16, tile_bf16,
            pltpu.VMEM((HEAD_DIM, D_MODEL), F32),
            pltpu.VMEM((N_HEADS, HEAD_DIM, HEAD_DIM), F32),
            pltpu.VMEM((SUBLANES, D_MODEL), F32),
            pltpu.VMEM(w_in.shape, BF16),
            pltpu.VMEM(wa.shape, BF16), pltpu.VMEM(wb.shape, BF16), pltpu.VMEM(wo.shape, BF16),
            pltpu.VMEM((2,) + wa.shape, F32),
            pltpu.SemaphoreType.DMA((2, LOAD_BANDS)),
            pltpu.SemaphoreType.DMA((n_groups,)),
        ],
        compiler_params=pltpu.CompilerParams(
            dimension_semantics=("arbitrary", "arbitrary"),
            vmem_limit_bytes=V7X_VMEM_LIMIT_BYTES),
        name="prompt_sweep",
    )(x, meta, w_in, npre, npost, lb_logits, hn, cw, wa, wb, wo)


def _decode_kernel(x_ref, st_ref, ctx_ref, w_in_ref, npre_ref, npost_ref, lb_ref, hn_ref, cw_ref,
                   wa_ref, wb_ref, wo_ref,
                   y_ref, st_out_ref, ctx_out_ref,
                   ft_scr, q_scr, v_scr, o_scr, za_scr, pb_scr, ga_scr, gb_scr):
    G = DECODE_GROUP
    i = pl.program_id(0)
    n_rows = x_ref.shape[0]

    @pl.when(i == 0)
    def _project():
        x = x_ref[:, 0, :]
        xn = (x * _rms_scale(x) * npre_ref[...]).astype(BF16)

        def proj(j):
            return _dot(xn, w_in_ref[:, j * D_MODEL:(j + 1) * D_MODEL])

        lb = _lower_bound(lb_ref[...])
        f = lb + (1.0 - lb) * _sigmoid(proj(P_F))
        ft_scr[...] = f.T
        q_scr[...] = _silu(proj(P_Q))
        v_scr[...] = proj(P_I)
        za_scr[...] = _silu(proj(P_ZA))
        u = proj(P_C) * proj(P_H)
        cw = cw_ref[...]
        conv = cw[CONV_W - 1:CONV_W, :] * u
        for j in range(CONV_W - 1):
            ctx_j = ctx_ref[:, j, :]
            conv = conv + cw[j:j + 1, :] * ctx_j
            if j > 0:
                ctx_out_ref[:, j - 1, :] = ctx_j
        ctx_out_ref[:, CONV_W - 2, :] = u
        pb_scr[...] = proj(P_B) * conv * _silu(proj(P_ZB))
        ga_scr[...] = _sigmoid(proj(P_GA))
        gb_scr[...] = _sigmoid(proj(P_GB))

    shift = (n_rows - i * G) % n_rows
    f_cols = pltpu.roll(ft_scr[...], shift, 1)
    r0 = pl.multiple_of(i * G, G)
    v_rows = v_scr[pl.ds(r0, G), :]
    q_rows = q_scr[pl.ds(r0, G), :].astype(BF16)
    row_id = lax.broadcasted_iota(jnp.int32, (G, HEAD_DIM), 0)
    o_heads = [jnp.zeros((G, HEAD_DIM), F32)] * N_HEADS
    for j in range(G):
        for h, sl in enumerate(HEADS):
            f_b = jnp.broadcast_to(f_cols[sl, j:j + 1], (HEAD_DIM, HEAD_DIM))
            s_new = f_b * st_ref[j, h] + (1.0 - f_b) * v_rows[j:j + 1, sl]
            st_out_ref[j, h] = s_new
            read = _dot(q_rows[:, sl], s_new.astype(BF16))
            o_heads[h] = jnp.where(row_id == j, read, o_heads[h])
    o_scr[pl.ds(r0, G), :] = jnp.concatenate(o_heads, axis=1)

    @pl.when(i == pl.num_programs(0) - 1)
    def _output():
        hn = hn_ref[...]
        for sl in HEADS:
            o_h = o_scr[:, sl]
            o_scr[:, sl] = o_h * _rms_scale(o_h) * hn
        y_a = _dot((o_scr[...] * za_scr[...]).astype(BF16), wa_ref[...])
        y_b = _dot(pb_scr[...].astype(BF16), wb_ref[...])
        merged = ga_scr[...] * y_a + gb_scr[...] * y_b
        out = _dot(merged.astype(BF16), wo_ref[...])
        y_ref[:, 0, :] = x_ref[:, 0, :] + out * _rms_scale(out) * npost_ref[...]


def _run_decode(x, state, ctx, weights):
    n = x.shape[0]
    assert n % DECODE_GROUP == 0 and n == 128
    w_in, npre, npost, lb_logits, hn, cw, wa, wb, wo = weights

    def const(shape):
        zeros = (0,) * len(shape)
        return pl.BlockSpec(shape, lambda i: zeros, pipeline_mode=pl.Buffered(1))

    st_spec = pl.BlockSpec((DECODE_GROUP, N_HEADS, HEAD_DIM, HEAD_DIM), lambda i: (i, 0, 0, 0))
    rows_f32 = pltpu.VMEM((n, D_MODEL), F32)
    cols_f32 = pltpu.VMEM((D_MODEL, n), F32)
    return pl.pallas_call(
        _decode_kernel,
        grid=(n // DECODE_GROUP,),
        in_specs=[const(x.shape), st_spec, const(ctx.shape), const(w_in.shape), const(npre.shape),
                  const(npost.shape), const(lb_logits.shape), const(hn.shape), const(cw.shape),
                  const(wa.shape), const(wb.shape), const(wo.shape)],
        out_specs=[const(x.shape), st_spec, const(ctx.shape)],
        out_shape=[jax.ShapeDtypeStruct(x.shape, F32),
                   jax.ShapeDtypeStruct(state.shape, F32),
                   jax.ShapeDtypeStruct(ctx.shape, F32)],
        scratch_shapes=[cols_f32, rows_f32, rows_f32, rows_f32, rows_f32, rows_f32, rows_f32, rows_f32],
        compiler_params=pltpu.CompilerParams(
            dimension_semantics=("arbitrary",),
            vmem_limit_bytes=V7X_VMEM_LIMIT_BYTES),
        name="decode_step",
    )(x, state, ctx, w_in, npre, npost, lb_logits, hn, cw, wa, wb, wo)


def kernel(x_prompt, x_sample, state_hgrn, state_conv, meta_tokens, w_in, norm_pre, norm_post, lb_logits,
           hgrn_norm, conv_w, w_a, w_b, w_o):
    depth = w_in.shape[0]
    assert depth == 1, "single-layer trunk"
    batch, seq, _ = x_prompt.shape
    dec_batch, dec_seq, _ = x_sample.shape
    assert dec_seq == 1

    weights = (w_in[0], norm_pre, norm_post, lb_logits, hgrn_norm, conv_w[0], w_a[0], w_b[0], w_o[0])

    y_prompt, hgrn_p, conv_p, w_in_bf, wa_bf, wb_bf, wo_bf = _run_prompt(
        x_prompt, meta_tokens.astype(x_prompt.dtype), weights, tile=PROMPT_TILE, chunk=PROMPT_CHUNK)
    weights = (w_in_bf,) + weights[1:6] + (wa_bf, wb_bf, wo_bf)

    y_s, hgrn_s, conv_s = _run_decode(x_sample, state_hgrn[0], state_conv[0], weights)

    return (y_prompt, y_s, hgrn_p[None], hgrn_s[None], conv_p[None], conv_s[None])
```

```python
import functools

import jax
import jax.numpy as jnp
from jax import lax
from jax.experimental import pallas as pl
from jax.experimental.pallas import tpu as pltpu

D_MODEL = 1024
N_HEADS = 8
HEAD_DIM = D_MODEL // N_HEADS
CONV_W = 3
EPS = 1e-6
P_Q, P_F, P_I, P_ZA, P_B, P_C, P_H, P_ZB, P_GA, P_GB = range(10)

PROMPT_ROWS = 2
PROMPT_TILE = 128
PROMPT_CHUNK = 128
DECODE_GROUP = 8
LOAD_BANDS = 4
SUBLANES = 8
CTX_ROW0 = SUBLANES - (CONV_W - 1)
MAX_HALF_CHUNK_LOG_DECAY = 80.0
V7X_VMEM_LIMIT_BYTES = 58 * 1024 * 1024

BF16 = jnp.bfloat16
F32 = jnp.float32
HEADS = [slice(h * HEAD_DIM, (h + 1) * HEAD_DIM) for h in range(N_HEADS)]


def _dot(a, b):
    return jnp.dot(a, b, preferred_element_type=F32)


def _dot_nt(a, b):
    return lax.dot_general(a, b, (((1,), (1,)), ((), ())), preferred_element_type=F32)


def _dot_tn(a, b):
    return lax.dot_general(a, b, (((0,), (0,)), ((), ())), preferred_element_type=F32)


def _sigmoid(x):
    return 1.0 / (1.0 + jnp.exp(-x))


def _silu(x):
    return x * _sigmoid(x)


def _rms_scale(x):
    return lax.rsqrt(jnp.mean(x * x, axis=-1, keepdims=True) + EPS)


def _lower_bound(lb_logits):
    m = jnp.max(lb_logits, axis=0, keepdims=True)
    e = jnp.exp(lb_logits - m)
    return e[0:1, :] / jnp.sum(e, axis=0, keepdims=True)


def _block_cumsum(x, block):
    n = x.shape[0]
    ri = lax.broadcasted_iota(jnp.int32, (n, n), 0)
    ci = lax.broadcasted_iota(jnp.int32, (n, n), 1)
    tri = ri >= ci
    if n != block:
        tri = jnp.logical_and(tri, (ri // block) == (ci // block))
    tri = jnp.where(tri, 1.0, 0.0).astype(BF16)
    hi = x.astype(BF16)
    lo = (x - hi.astype(F32)).astype(BF16)
    return _dot(tri, hi) + _dot(tri, lo)


def _seq_kernel(x_ref, meta_ref, w_in_hbm, npre_ref, npost_ref, lb_ref, hn_ref, cw_ref,
                wa_hbm, wb_hbm, wo_hbm,
                y_ref, st_out_ref, c_out_ref, w_in_out, wa_out, wb_out, wo_out,
                st_scr, u_scr, q_scr, k_scr, v_scr, g_scr, o_scr,
                qt_scr, kt_scr, qin_scr, kst_scr, vb_scr, dec_scr, st_meta_scr, ctx_meta_scr,
                w_in_ref, wa_ref, wb_ref, wo_ref, stage_scr, load_sem, store_sem,
                *, tile, chunk, rows_per_step):
    R, C = rows_per_step, chunk
    T = R * tile
    n_chunks = T // C
    t = pl.program_id(1)
    first_step = jnp.logical_and(pl.program_id(0) == 0, t == 0)
    last_step = jnp.logical_and(pl.program_id(0) == pl.num_programs(0) - 1, t == pl.num_programs(1) - 1)
    lb = _lower_bound(lb_ref[...])

    col = [pl.ds(j * D_MODEL, D_MODEL) for j in range(w_in_hbm.shape[1] // D_MODEL)]
    groups = [(w_in_hbm.at[:, c], w_in_ref.at[:, c], w_in_out.at[:, c]) for c in col]
    groups += [(wa_hbm, wa_ref, wa_out), (wb_hbm, wb_ref, wb_out), (wo_hbm, wo_ref, wo_out)]

    def load_copies(i):
        band = groups[i][0].shape[0] // LOAD_BANDS
        return [pltpu.make_async_copy(groups[i][0].at[pl.ds(b * band, band), :],
                                      stage_scr.at[i % 2, pl.ds(b * band, band), :],
                                      load_sem.at[i % 2, b]) for b in range(LOAD_BANDS)]

    def store_copy(i):
        return pltpu.make_async_copy(groups[i][1], groups[i][2], store_sem.at[i])

    @pl.when(first_step)
    def _stream_weights():
        for copy in load_copies(0):
            copy.start()
        for i in range(len(groups)):
            if i + 1 < len(groups):
                for copy in load_copies(i + 1):
                    copy.start()
            for copy in load_copies(i):
                copy.wait()
            groups[i][1][...] = stage_scr[i % 2].astype(BF16)
        for i in range(len(groups)):
            store_copy(i).start()

    def project(xn, j):
        return _dot(xn, w_in_ref[:, j * D_MODEL:(j + 1) * D_MODEL])

    @pl.when(first_step)
    def _meta_prefix():
        xm = meta_ref[...]
        n_meta = xm.shape[0]
        xn_m = (xm * _rms_scale(xm) * npre_ref[...]).astype(BF16)
        f_m = lb + (1.0 - lb) * _sigmoid(project(xn_m, P_F))
        g_m = _block_cumsum(jnp.log(f_m), n_meta)
        k_end = ((1.0 - f_m) * jnp.exp(g_m[n_meta - 1:n_meta, :] - g_m)).astype(BF16)
        v_m = project(xn_m, P_I).astype(BF16)
        for h, sl in enumerate(HEADS):
            st_meta_scr[h] = _dot_tn(k_end[:, sl], v_m[:, sl])
        u_m = project(xn_m, P_C) * project(xn_m, P_H)
        ctx_meta_scr[CTX_ROW0:SUBLANES, :] = u_m[n_meta - (CONV_W - 1):n_meta, :]
        dec_scr[...] = jnp.zeros(dec_scr.shape, F32)

    @pl.when(t == 0)
    def _init():
        for r in range(R):
            st_scr[r * N_HEADS:(r + 1) * N_HEADS] = st_meta_scr[...]
            u_scr[r, CTX_ROW0:SUBLANES, :] = ctx_meta_scr[CTX_ROW0:SUBLANES, :]

    x = x_ref[...].reshape(T, D_MODEL)
    xn = (x * _rms_scale(x) * npre_ref[...]).astype(BF16)

    def proj(j):
        return project(xn, j)

    f = lb + (1.0 - lb) * _sigmoid(proj(P_F))
    k_scr[...] = 1.0 - f
    q_scr[...] = _silu(proj(P_Q))
    vb_scr[...] = proj(P_I).astype(BF16)
    g_scr[...] = _block_cumsum(jnp.log(f), C)

    g_floor = None
    for c in range(n_chunks):
        rows = slice(c * C, (c + 1) * C)
        gc = g_scr[rows, :]
        g_last = gc[C - 1:C, :]
        g_mid = gc[C // 2 - 1:C // 2, :]
        qc = q_scr[rows, :]
        kc = k_scr[rows, :]
        qt_scr[rows, :] = (qc * jnp.exp(gc - g_mid)).astype(BF16)
        kt_scr[rows, :] = (kc * jnp.exp(g_mid - gc)).astype(BF16)
        qin_scr[rows, :] = (qc * jnp.exp(gc)).astype(BF16)
        kst_scr[rows, :] = (kc * jnp.exp(g_last - gc)).astype(BF16)
        dec_scr[c:c + 1, :] = jnp.exp(g_last)
        half_floor = jnp.minimum(g_mid, g_last - g_mid)
        g_floor = half_floor if g_floor is None else jnp.minimum(g_floor, half_floor)
    stable = jnp.min(g_floor) >= -MAX_HALF_CHUNK_LOG_DECAY

    dec_cols = dec_scr[...].T
    keep = jnp.logical_and(
        lax.broadcasted_iota(jnp.int32, (C, C), 0) >= lax.broadcasted_iota(jnp.int32, (C, C), 1), stable)
    for c in range(n_chunks):
        rows = slice(c * C, (c + 1) * C)
        first_head = (c * C // tile) * N_HEADS
        scores = [jnp.where(keep, _dot_nt(qt_scr[rows, sl], kt_scr[rows, sl]), 0.0).astype(BF16)
                  for sl in HEADS]
        for h, sl in enumerate(HEADS):
            vb = vb_scr[rows, sl]
            st = st_scr[first_head + h]
            o_scr[rows, sl] = _dot(jnp.concatenate([qin_scr[rows, sl], scores[h]], axis=1),
                                   jnp.concatenate([st.astype(BF16), vb], axis=0))
            decay = jnp.broadcast_to(dec_cols[sl, c:c + 1], (HEAD_DIM, HEAD_DIM))
            st_scr[first_head + h] = decay * st + _dot_tn(kst_scr[rows, sl], vb)

    @pl.when(jnp.logical_not(stable))
    def _intra_exact():
        same_head = (lax.broadcasted_iota(jnp.int32, (D_MODEL, D_MODEL), 0) // HEAD_DIM ==
                     lax.broadcasted_iota(jnp.int32, (D_MODEL, D_MODEL), 1) // HEAD_DIM)
        head_sum = jnp.where(same_head, 1.0, 0.0).astype(BF16)
        tpos = lax.broadcasted_iota(jnp.int32, (C, 1), 0)
        v_scr[...] = proj(P_I)

        def chunk_step(c, carry):
            r0 = pl.multiple_of(c * C, C)
            rows = pl.ds(r0, C)
            gc = g_scr[rows, :]
            qc = q_scr[rows, :]

            def src_step(i, carry2):
                src = pl.ds(r0 + i, 1)
                p = qc * jnp.exp(jnp.minimum(gc - g_scr[src, :], 0.0)) * k_scr[src, :]
                p = jnp.where(tpos >= i, p, 0.0).astype(BF16)
                o_scr[rows, :] += _dot(p, head_sum) * v_scr[src, :]
                return carry2

            return lax.fori_loop(0, C, src_step, carry)

        lax.fori_loop(0, n_chunks, chunk_step, 0)

    hn = hn_ref[...]
    for sl in HEADS:
        o_h = o_scr[:, sl]
        o_scr[:, sl] = o_h * _rms_scale(o_h) * hn
    y_a = _dot((o_scr[...] * _silu(proj(P_ZA))).astype(BF16), wa_ref[...])

    u = proj(P_C) * proj(P_H)
    cw = cw_ref[...]
    conv = cw[CONV_W - 1:CONV_W, :] * u
    taps = []
    for r in range(R):
        u_scr[r, SUBLANES:SUBLANES + tile, :] = u[r * tile:(r + 1) * tile, :]
        taps.append(sum(cw[j:j + 1, :] * u_scr[r, CTX_ROW0 + j:CTX_ROW0 + j + tile, :]
                        for j in range(CONV_W - 1)))
        u_scr[r, CTX_ROW0:SUBLANES, :] = u[(r + 1) * tile - (CONV_W - 1):(r + 1) * tile, :]
    conv = conv + jnp.concatenate(taps, axis=0)
    y_b = _dot((proj(P_B) * conv * _silu(proj(P_ZB))).astype(BF16), wb_ref[...])

    merged = _sigmoid(proj(P_GA)) * y_a + _sigmoid(proj(P_GB)) * y_b
    out = _dot(merged.astype(BF16), wo_ref[...])
    y_ref[...] = (x + out * _rms_scale(out) * npost_ref[...]).reshape(R, tile, D_MODEL)

    @pl.when(t == pl.num_programs(1) - 1)
    def _finish():
        for r in range(R):
            c_out_ref[r] = u_scr[r, CTX_ROW0:SUBLANES, :]
            st_out_ref[r] = st_scr[r * N_HEADS:(r + 1) * N_HEADS]

    @pl.when(last_step)
    def _weights_written():
        for i in range(len(groups)):
            store_copy(i).wait()


def _const_spec(shape):
    zeros = (0,) * len(shape)
    return pl.BlockSpec(shape, lambda b, t: zeros, pipeline_mode=pl.Buffered(1))


def _run_prompt(x, meta, weights, *, tile, chunk, rows_per_step):
    n, length, _ = x.shape
    assert length % tile == 0 and tile % chunk == 0 and chunk % (2 * SUBLANES) == 0 and tile >= CONV_W - 1
    assert n % rows_per_step == 0
    rows, step_rows = rows_per_step, rows_per_step * tile
    assert meta.shape[0] % (2 * SUBLANES) == 0 and meta.shape[0] >= CONV_W - 1
    assert step_rows // chunk <= HEAD_DIM
    w_in, npre, npost, lb_logits, hn, cw, wa, wb, wo = weights
    kern = functools.partial(_seq_kernel, tile=tile, chunk=chunk, rows_per_step=rows)
    hbm = pl.BlockSpec(memory_space=pl.ANY)
    n_groups = w_in.shape[1] // D_MODEL + 3
    assert w_in.shape[1] % D_MODEL == 0 and wa.shape == wb.shape == wo.shape == (w_in.shape[0], D_MODEL)
    state_shape = (rows, N_HEADS, HEAD_DIM, HEAD_DIM)
    ctx_shape = (rows, CONV_W - 1, D_MODEL)
    tile_f32 = pltpu.VMEM((step_rows, D_MODEL), F32)
    tile_bf16 = pltpu.VMEM((step_rows, D_MODEL), BF16)
    return pl.pallas_call(
        kern,
        grid=(n // rows, length // tile),
        in_specs=[
            pl.BlockSpec((rows, tile, D_MODEL), lambda b, t: (b, t, 0)),
            _const_spec(meta.shape),
            hbm,
            _const_spec(npre.shape),
            _const_spec(npost.shape),
            _const_spec(lb_logits.shape),
            _const_spec(hn.shape),
            _const_spec(cw.shape),
            hbm, hbm, hbm,
        ],
        out_specs=[
            pl.BlockSpec((rows, tile, D_MODEL), lambda b, t: (b, t, 0)),
            pl.BlockSpec(state_shape, lambda b, t: (b, 0, 0, 0)),
            pl.BlockSpec(ctx_shape, lambda b, t: (b, 0, 0)),
            hbm, hbm, hbm, hbm,
        ],
        out_shape=[
            jax.ShapeDtypeStruct(x.shape, F32),
            jax.ShapeDtypeStruct((n,) + state_shape[1:], F32),
            jax.ShapeDtypeStruct((n,) + ctx_shape[1:], F32),
            jax.ShapeDtypeStruct(w_in.shape, BF16),
            jax.ShapeDtypeStruct(wa.shape, BF16),
            jax.ShapeDtypeStruct(wb.shape, BF16),
            jax.ShapeDtypeStruct(wo.shape, BF16),
        ],
        scratch_shapes=[
            pltpu.VMEM((rows * N_HEADS, HEAD_DIM, HEAD_DIM), F32),
            pltpu.VMEM((rows, tile + SUBLANES, D_MODEL), F32),
            tile_f32, tile_f32, tile_f32, tile_f32, tile_f32,
            tile_bf16, tile_bf16, tile_bf16, tile_bf16, tile_bf16,
            pltpu.VMEM((HEAD_DIM, D_MODEL), F32),
            pltpu.VMEM((N_HEADS, HEAD_DIM, HEAD_DIM), F32),
            pltpu.VMEM((SUBLANES, D_MODEL), F32),
            pltpu.VMEM(w_in.shape, BF16),
            pltpu.VMEM(wa.shape, BF16), pltpu.VMEM(wb.shape, BF16), pltpu.VMEM(wo.shape, BF16),
            pltpu.VMEM((2,) + wa.shape, F32),
            pltpu.SemaphoreType.DMA((2, LOAD_BANDS)),
            pltpu.SemaphoreType.DMA((n_groups,)),
        ],
        compiler_params=pltpu.CompilerParams(
            dimension_semantics=("arbitrary", "arbitrary"),
            vmem_limit_bytes=V7X_VMEM_LIMIT_BYTES),
        name="prompt_sweep",
    )(x, meta, w_in, npre, npost, lb_logits, hn, cw, wa, wb, wo)


def _decode_kernel(x_ref, st_ref, ctx_ref, w_in_ref, npre_ref, npost_ref, lb_ref, hn_ref, cw_ref,
                   wa_ref, wb_ref, wo_ref,
                   y_ref, st_out_ref, ctx_out_ref,
                   ft_scr, q_scr, v_scr, o_scr, za_scr, pb_scr, ga_scr, gb_scr):
    G = DECODE_GROUP
    i = pl.program_id(0)
    n_rows = x_ref.shape[0]

    @pl.when(i == 0)
    def _project():
        x = x_ref[:, 0, :]
        xn = (x * _rms_scale(x) * npre_ref[...]).astype(BF16)

        def proj(j):
            return _dot(xn, w_in_ref[:, j * D_MODEL:(j + 1) * D_MODEL])

        lb = _lower_bound(lb_ref[...])
        f = lb + (1.0 - lb) * _sigmoid(proj(P_F))
        ft_scr[...] = f.T
        q_scr[...] = _silu(proj(P_Q))
        v_scr[...] = proj(P_I)
        za_scr[...] = _silu(proj(P_ZA))
        u = proj(P_C) * proj(P_H)
        cw = cw_ref[...]
        conv = cw[CONV_W - 1:CONV_W, :] * u
        for j in range(CONV_W - 1):
            ctx_j = ctx_ref[:, j, :]
            conv = conv + cw[j:j + 1, :] * ctx_j
            if j > 0:
                ctx_out_ref[:, j - 1, :] = ctx_j
        ctx_out_ref[:, CONV_W - 2, :] = u
        pb_scr[...] = proj(P_B) * conv * _silu(proj(P_ZB))
        ga_scr[...] = _sigmoid(proj(P_GA))
        gb_scr[...] = _sigmoid(proj(P_GB))

    shift = (n_rows - i * G) % n_rows
    f_cols = pltpu.roll(ft_scr[...], shift, 1)
    r0 = pl.multiple_of(i * G, G)
    v_rows = v_scr[pl.ds(r0, G), :]
    q_rows = q_scr[pl.ds(r0, G), :].astype(BF16)
    row_id = lax.broadcasted_iota(jnp.int32, (G, HEAD_DIM), 0)
    o_heads = [jnp.zeros((G, HEAD_DIM), F32)] * N_HEADS
    for j in range(G):
        for h, sl in enumerate(HEADS):
            f_b = jnp.broadcast_to(f_cols[sl, j:j + 1], (HEAD_DIM, HEAD_DIM))
            s_new = f_b * st_ref[j, h] + (1.0 - f_b) * v_rows[j:j + 1, sl]
            st_out_ref[j, h] = s_new
            read = _dot(q_rows[:, sl], s_new.astype(BF16))
            o_heads[h] = jnp.where(row_id == j, read, o_heads[h])
    o_scr[pl.ds(r0, G), :] = jnp.concatenate(o_heads, axis=1)

    @pl.when(i == pl.num_programs(0) - 1)
    def _output():
        hn = hn_ref[...]
        for sl in HEADS:
            o_h = o_scr[:, sl]
            o_scr[:, sl] = o_h * _rms_scale(o_h) * hn
        y_a = _dot((o_scr[...] * za_scr[...]).astype(BF16), wa_ref[...])
        y_b = _dot(pb_scr[...].astype(BF16), wb_ref[...])
        merged = ga_scr[...] * y_a + gb_scr[...] * y_b
        out = _dot(merged.astype(BF16), wo_ref[...])
        y_ref[:, 0, :] = x_ref[:, 0, :] + out * _rms_scale(out) * npost_ref[...]


def _run_decode(x, state, ctx, weights):
    n = x.shape[0]
    assert n % DECODE_GROUP == 0 and n == 128
    w_in, npre, npost, lb_logits, hn, cw, wa, wb, wo = weights

    def const(shape):
        zeros = (0,) * len(shape)
        return pl.BlockSpec(shape, lambda i: zeros, pipeline_mode=pl.Buffered(1))

    st_spec = pl.BlockSpec((DECODE_GROUP, N_HEADS, HEAD_DIM, HEAD_DIM), lambda i: (i, 0, 0, 0))
    rows_f32 = pltpu.VMEM((n, D_MODEL), F32)
    cols_f32 = pltpu.VMEM((D_MODEL, n), F32)
    return pl.pallas_call(
        _decode_kernel,
        grid=(n // DECODE_GROUP,),
        in_specs=[const(x.shape), st_spec, const(ctx.shape), const(w_in.shape), const(npre.shape),
                  const(npost.shape), const(lb_logits.shape), const(hn.shape), const(cw.shape),
                  const(wa.shape), const(wb.shape), const(wo.shape)],
        out_specs=[const(x.shape), st_spec, const(ctx.shape)],
        out_shape=[jax.ShapeDtypeStruct(x.shape, F32),
                   jax.ShapeDtypeStruct(state.shape, F32),
                   jax.ShapeDtypeStruct(ctx.shape, F32)],
        scratch_shapes=[cols_f32, rows_f32, rows_f32, rows_f32, rows_f32, rows_f32, rows_f32, rows_f32],
        compiler_params=pltpu.CompilerParams(
            dimension_semantics=("arbitrary",),
            vmem_limit_bytes=V7X_VMEM_LIMIT_BYTES),
        name="decode_step",
    )(x, state, ctx, w_in, npre, npost, lb_logits, hn, cw, wa, wb, wo)


def kernel(x_prompt, x_sample, state_hgrn, state_conv, meta_tokens, w_in, norm_pre, norm_post, lb_logits,
           hgrn_norm, conv_w, w_a, w_b, w_o):
    depth = w_in.shape[0]
    assert depth == 1, "single-layer trunk"
    batch, seq, _ = x_prompt.shape
    dec_batch, dec_seq, _ = x_sample.shape
    assert dec_seq == 1

    weights = (w_in[0], norm_pre, norm_post, lb_logits, hgrn_norm, conv_w[0], w_a[0], w_b[0], w_o[0])

    y_prompt, hgrn_p, conv_p, w_in_bf, wa_bf, wb_bf, wo_bf = _run_prompt(
        x_prompt, meta_tokens.astype(x_prompt.dtype), weights, tile=PROMPT_TILE, chunk=PROMPT_CHUNK,
        rows_per_step=PROMPT_ROWS)
    weights = (w_in_bf,) + weights[1:6] + (wa_bf, wb_bf, wo_bf)

    y_s, hgrn_s, conv_s = _run_decode(x_sample, state_hgrn[0], state_conv[0], weights)

    return (y_prompt, y_s, hgrn_p[None], hgrn_s[None], conv_p[None], conv_s[None])
```

```python
import functools

import jax
import jax.numpy as jnp
from jax import lax
from jax.experimental import pallas as pl
from jax.experimental.pallas import tpu as pltpu

D_MODEL = 1024
N_HEADS = 8
HEAD_DIM = D_MODEL // N_HEADS
CONV_W = 3
EPS = 1e-6
P_Q, P_F, P_I, P_ZA, P_B, P_C, P_H, P_ZB, P_GA, P_GB = range(10)

PROMPT_TILE = 256
PROMPT_CHUNK = 128
DECODE_GROUP = 8
LOAD_BANDS = 4
SUBLANES = 8
CTX_ROW0 = SUBLANES - (CONV_W - 1)
MAX_HALF_CHUNK_LOG_DECAY = 80.0
V7X_VMEM_LIMIT_BYTES = 58 * 1024 * 1024

BF16 = jnp.bfloat16
F32 = jnp.float32
HEADS = [slice(h * HEAD_DIM, (h + 1) * HEAD_DIM) for h in range(N_HEADS)]


def _dot(a, b):
    return jnp.dot(a, b, preferred_element_type=F32)


def _dot_nt(a, b):
    return lax.dot_general(a, b, (((1,), (1,)), ((), ())), preferred_element_type=F32)


def _dot_tn(a, b):
    return lax.dot_general(a, b, (((0,), (0,)), ((), ())), preferred_element_type=F32)


def _sigmoid(x):
    return 1.0 / (1.0 + jnp.exp(-x))


def _silu(x):
    return x * _sigmoid(x)


def _rms_scale(x):
    return lax.rsqrt(jnp.mean(x * x, axis=-1, keepdims=True) + EPS)


def _lower_bound(lb_logits):
    m = jnp.max(lb_logits, axis=0, keepdims=True)
    e = jnp.exp(lb_logits - m)
    return e[0:1, :] / jnp.sum(e, axis=0, keepdims=True)


def _block_cumsum(x, block):
    n = x.shape[0]
    ri = lax.broadcasted_iota(jnp.int32, (n, n), 0)
    ci = lax.broadcasted_iota(jnp.int32, (n, n), 1)
    tri = ri >= ci
    if n != block:
        tri = jnp.logical_and(tri, (ri // block) == (ci // block))
    tri = jnp.where(tri, 1.0, 0.0).astype(BF16)
    hi = x.astype(BF16)
    lo = (x - hi.astype(F32)).astype(BF16)
    return _dot(tri, hi) + _dot(tri, lo)


def _seq_kernel(x_ref, meta_ref, w_in_hbm, npre_ref, npost_ref, lb_ref, hn_ref, cw_ref,
                wa_hbm, wb_hbm, wo_hbm,
                y_ref, st_out_ref, c_out_ref, w_in_out, wa_out, wb_out, wo_out,
                st_scr, u_scr, q_scr, k_scr, v_scr, g_scr, o_scr,
                qt_scr, kt_scr, qin_scr, kst_scr, vb_scr, dec_scr, st_meta_scr, ctx_meta_scr,
                w_in_ref, wa_ref, wb_ref, wo_ref, stage_scr, load_sem, store_sem,
                *, tile, chunk):
    T, C = tile, chunk
    n_chunks = T // C
    t = pl.program_id(1)
    first_step = jnp.logical_and(pl.program_id(0) == 0, t == 0)
    last_step = jnp.logical_and(pl.program_id(0) == pl.num_programs(0) - 1, t == pl.num_programs(1) - 1)
    lb = _lower_bound(lb_ref[...])

    col = [pl.ds(j * D_MODEL, D_MODEL) for j in range(w_in_hbm.shape[1] // D_MODEL)]
    groups = [(w_in_hbm.at[:, c], w_in_ref.at[:, c], w_in_out.at[:, c]) for c in col]
    groups += [(wa_hbm, wa_ref, wa_out), (wb_hbm, wb_ref, wb_out), (wo_hbm, wo_ref, wo_out)]

    def load_copies(i):
        band = groups[i][0].shape[0] // LOAD_BANDS
        return [pltpu.make_async_copy(groups[i][0].at[pl.ds(b * band, band), :],
                                      stage_scr.at[i % 2, pl.ds(b * band, band), :],
                                      load_sem.at[i % 2, b]) for b in range(LOAD_BANDS)]

    def store_copy(i):
        return pltpu.make_async_copy(groups[i][1], groups[i][2], store_sem.at[i])

    @pl.when(first_step)
    def _stream_weights():
        for copy in load_copies(0):
            copy.start()
        for i in range(len(groups)):
            if i + 1 < len(groups):
                for copy in load_copies(i + 1):
                    copy.start()
            for copy in load_copies(i):
                copy.wait()
            groups[i][1][...] = stage_scr[i % 2].astype(BF16)
        for i in range(len(groups)):
            store_copy(i).start()

    def project(xn, j):
        return _dot(xn, w_in_ref[:, j * D_MODEL:(j + 1) * D_MODEL])

    @pl.when(first_step)
    def _meta_prefix():
        xm = meta_ref[...]
        n_meta = xm.shape[0]
        xn_m = (xm * _rms_scale(xm) * npre_ref[...]).astype(BF16)
        f_m = lb + (1.0 - lb) * _sigmoid(project(xn_m, P_F))
        g_m = _block_cumsum(jnp.log(f_m), n_meta)
        k_end = ((1.0 - f_m) * jnp.exp(g_m[n_meta - 1:n_meta, :] - g_m)).astype(BF16)
        v_m = project(xn_m, P_I).astype(BF16)
        for h, sl in enumerate(HEADS):
            st_meta_scr[h] = _dot_tn(k_end[:, sl], v_m[:, sl])
        u_m = project(xn_m, P_C) * project(xn_m, P_H)
        ctx_meta_scr[CTX_ROW0:SUBLANES, :] = u_m[n_meta - (CONV_W - 1):n_meta, :]
        dec_scr[...] = jnp.zeros(dec_scr.shape, F32)

    @pl.when(t == 0)
    def _init():
        st_scr[...] = st_meta_scr[...]
        u_scr[CTX_ROW0:SUBLANES, :] = ctx_meta_scr[CTX_ROW0:SUBLANES, :]

    x = x_ref[0]
    xn = (x * _rms_scale(x) * npre_ref[...]).astype(BF16)

    def proj(j):
        return project(xn, j)

    f = lb + (1.0 - lb) * _sigmoid(proj(P_F))
    k_scr[...] = 1.0 - f
    q_scr[...] = _silu(proj(P_Q))
    vb_scr[...] = proj(P_I).astype(BF16)
    g_scr[...] = _block_cumsum(jnp.log(f), C)

    g_floor = None
    for c in range(n_chunks):
        rows = slice(c * C, (c + 1) * C)
        gc = g_scr[rows, :]
        g_last = gc[C - 1:C, :]
        g_mid = gc[C // 2 - 1:C // 2, :]
        qc = q_scr[rows, :]
        kc = k_scr[rows, :]
        qt_scr[rows, :] = (qc * jnp.exp(gc - g_mid)).astype(BF16)
        kt_scr[rows, :] = (kc * jnp.exp(g_mid - gc)).astype(BF16)
        qin_scr[rows, :] = (qc * jnp.exp(gc)).astype(BF16)
        kst_scr[rows, :] = (kc * jnp.exp(g_last - gc)).astype(BF16)
        dec_scr[c:c + 1, :] = jnp.exp(g_last)
        half_floor = jnp.minimum(g_mid, g_last - g_mid)
        g_floor = half_floor if g_floor is None else jnp.minimum(g_floor, half_floor)
    stable = jnp.min(g_floor) >= -MAX_HALF_CHUNK_LOG_DECAY

    dec_cols = dec_scr[...].T
    keep = jnp.logical_and(
        lax.broadcasted_iota(jnp.int32, (C, C), 0) >= lax.broadcasted_iota(jnp.int32, (C, C), 1), stable)
    for c in range(n_chunks):
        rows = slice(c * C, (c + 1) * C)
        scores = [jnp.where(keep, _dot_nt(qt_scr[rows, sl], kt_scr[rows, sl]), 0.0).astype(BF16)
                  for sl in HEADS]
        for h, sl in enumerate(HEADS):
            vb = vb_scr[rows, sl]
            st = st_scr[h]
            o_scr[rows, sl] = _dot(jnp.concatenate([qin_scr[rows, sl], scores[h]], axis=1),
                                   jnp.concatenate([st.astype(BF16), vb], axis=0))
            decay = jnp.broadcast_to(dec_cols[sl, c:c + 1], (HEAD_DIM, HEAD_DIM))
            st_scr[h] = decay * st + _dot_tn(kst_scr[rows, sl], vb)

    @pl.when(jnp.logical_not(stable))
    def _intra_exact():
        same_head = (lax.broadcasted_iota(jnp.int32, (D_MODEL, D_MODEL), 0) // HEAD_DIM ==
                     lax.broadcasted_iota(jnp.int32, (D_MODEL, D_MODEL), 1) // HEAD_DIM)
        head_sum = jnp.where(same_head, 1.0, 0.0).astype(BF16)
        tpos = lax.broadcasted_iota(jnp.int32, (C, 1), 0)
        v_scr[...] = proj(P_I)

        def chunk_step(c, carry):
            r0 = pl.multiple_of(c * C, C)
            rows = pl.ds(r0, C)
            gc = g_scr[rows, :]
            qc = q_scr[rows, :]

            def src_step(i, carry2):
                src = pl.ds(r0 + i, 1)
                p = qc * jnp.exp(jnp.minimum(gc - g_scr[src, :], 0.0)) * k_scr[src, :]
                p = jnp.where(tpos >= i, p, 0.0).astype(BF16)
                o_scr[rows, :] += _dot(p, head_sum) * v_scr[src, :]
                return carry2

            return lax.fori_loop(0, C, src_step, carry)

        lax.fori_loop(0, n_chunks, chunk_step, 0)

    u = proj(P_C) * proj(P_H)
    u_scr[SUBLANES:SUBLANES + T, :] = u
    cw = cw_ref[...]
    conv = cw[CONV_W - 1:CONV_W, :] * u
    for j in range(CONV_W - 1):
        conv = conv + cw[j:j + 1, :] * u_scr[CTX_ROW0 + j:CTX_ROW0 + j + T, :]
    y_b = _dot((proj(P_B) * conv * _silu(proj(P_ZB))).astype(BF16), wb_ref[...])
    new_ctx = u[T - (CONV_W - 1):T, :]
    u_scr[CTX_ROW0:SUBLANES, :] = new_ctx

    hn = hn_ref[...]
    for sl in HEADS:
        o_h = o_scr[:, sl]
        o_scr[:, sl] = o_h * _rms_scale(o_h) * hn
    y_a = _dot((o_scr[...] * _silu(proj(P_ZA))).astype(BF16), wa_ref[...])

    merged = _sigmoid(proj(P_GA)) * y_a + _sigmoid(proj(P_GB)) * y_b
    out = _dot(merged.astype(BF16), wo_ref[...])
    y_ref[0] = x + out * _rms_scale(out) * npost_ref[...]

    @pl.when(t == pl.num_programs(1) - 1)
    def _finish():
        c_out_ref[0] = new_ctx
        st_out_ref[0] = st_scr[...]

    @pl.when(last_step)
    def _weights_written():
        for i in range(len(groups)):
            store_copy(i).wait()


def _const_spec(shape):
    zeros = (0,) * len(shape)
    return pl.BlockSpec(shape, lambda b, t: zeros, pipeline_mode=pl.Buffered(1))


def _run_prompt(x, meta, weights, *, tile, chunk):
    n, length, _ = x.shape
    assert length % tile == 0 and tile % chunk == 0 and chunk % (2 * SUBLANES) == 0 and tile >= CONV_W - 1
    assert meta.shape[0] % (2 * SUBLANES) == 0 and meta.shape[0] >= CONV_W - 1
    assert tile // chunk <= HEAD_DIM
    w_in, npre, npost, lb_logits, hn, cw, wa, wb, wo = weights
    kern = functools.partial(_seq_kernel, tile=tile, chunk=chunk)
    hbm = pl.BlockSpec(memory_space=pl.ANY)
    n_groups = w_in.shape[1] // D_MODEL + 3
    assert w_in.shape[1] % D_MODEL == 0 and wa.shape == wb.shape == wo.shape == (w_in.shape[0], D_MODEL)
    state_shape = (1, N_HEADS, HEAD_DIM, HEAD_DIM)
    ctx_shape = (1, CONV_W - 1, D_MODEL)
    tile_f32 = pltpu.VMEM((tile, D_MODEL), F32)
    tile_bf16 = pltpu.VMEM((tile, D_MODEL), BF16)
    return pl.pallas_call(
        kern,
        grid=(n, length // tile),
        in_specs=[
            pl.BlockSpec((1, tile, D_MODEL), lambda b, t: (b, t, 0)),
            _const_spec(meta.shape),
            hbm,
            _const_spec(npre.shape),
            _const_spec(npost.shape),
            _const_spec(lb_logits.shape),
            _const_spec(hn.shape),
            _const_spec(cw.shape),
            hbm, hbm, hbm,
        ],
        out_specs=[
            pl.BlockSpec((1, tile, D_MODEL), lambda b, t: (b, t, 0)),
            pl.BlockSpec(state_shape, lambda b, t: (b, 0, 0, 0)),
            pl.BlockSpec(ctx_shape, lambda b, t: (b, 0, 0)),
            hbm, hbm, hbm, hbm,
        ],
        out_shape=[
            jax.ShapeDtypeStruct(x.shape, F32),
            jax.ShapeDtypeStruct((n,) + state_shape[1:], F32),
            jax.ShapeDtypeStruct((n,) + ctx_shape[1:], F32),
            jax.ShapeDtypeStruct(w_in.shape, BF16),
            jax.ShapeDtypeStruct(wa.shape, BF16),
            jax.ShapeDtypeStruct(wb.shape, BF16),
            jax.ShapeDtypeStruct(wo.shape, BF16),
        ],
        scratch_shapes=[
            pltpu.VMEM((N_HEADS, HEAD_DIM, HEAD_DIM), F32),
            pltpu.VMEM((tile + SUBLANES, D_MODEL), F32),
            tile_f32, tile_f32, tile_f32, tile_f32, tile_f32,
            tile_bf16, tile_bf16, tile_bf16, tile_bf16, tile_bf16,
            pltpu.VMEM((HEAD_DIM, D_MODEL), F32),
            pltpu.VMEM((N_HEADS, HEAD_DIM, HEAD_DIM), F32),
            pltpu.VMEM((SUBLANES, D_MODEL), F32),
            pltpu.VMEM(w_in.shape, BF16),
            pltpu.VMEM(wa.shape, BF16), pltpu.VMEM(wb.shape, BF16), pltpu.VMEM(wo.shape, BF16),
            pltpu.VMEM((2,) + wa.shape, F32),
            pltpu.SemaphoreType.DMA((2, LOAD_BANDS)),
            pltpu.SemaphoreType.DMA((n_groups,)),
        ],
        compiler_params=pltpu.CompilerParams(
            dimension_semantics=("arbitrary", "arbitrary"),
            vmem_limit_bytes=V7X_VMEM_LIMIT_BYTES),
        name="prompt_sweep",
    )(x, meta, w_in, npre, npost, lb_logits, hn, cw, wa, wb, wo)


def _decode_kernel(x_ref, st_ref, ctx_ref, w_in_ref, npre_ref, npost_ref, lb_ref, hn_ref, cw_ref,
                   wa_ref, wb_ref, wo_ref,
                   y_ref, st_out_ref, ctx_out_ref,
                   ft_scr, q_scr, v_scr, o_scr, za_scr, pb_scr, ga_scr, gb_scr):
    G = DECODE_GROUP
    i = pl.program_id(0)
    n_rows = x_ref.shape[0]

    @pl.when(i == 0)
    def _project():
        x = x_ref[:, 0, :]
        xn = (x * _rms_scale(x) * npre_ref[...]).astype(BF16)

        def proj(j):
            return _dot(xn, w_in_ref[:, j * D_MODEL:(j + 1) * D_MODEL])

        lb = _lower_bound(lb_ref[...])
        f = lb + (1.0 - lb) * _sigmoid(proj(P_F))
        ft_scr[...] = f.T
        q_scr[...] = _silu(proj(P_Q))
        v_scr[...] = proj(P_I)
        za_scr[...] = _silu(proj(P_ZA))
        u = proj(P_C) * proj(P_H)
        cw = cw_ref[...]
        conv = cw[CONV_W - 1:CONV_W, :] * u
        for j in range(CONV_W - 1):
            ctx_j = ctx_ref[:, j, :]
            conv = conv + cw[j:j + 1, :] * ctx_j
            if j > 0:
                ctx_out_ref[:, j - 1, :] = ctx_j
        ctx_out_ref[:, CONV_W - 2, :] = u
        pb_scr[...] = proj(P_B) * conv * _silu(proj(P_ZB))
        ga_scr[...] = _sigmoid(proj(P_GA))
        gb_scr[...] = _sigmoid(proj(P_GB))

    shift = (n_rows - i * G) % n_rows
    f_cols = pltpu.roll(ft_scr[...], shift, 1)
    r0 = pl.multiple_of(i * G, G)
    v_rows = v_scr[pl.ds(r0, G), :]
    q_rows = q_scr[pl.ds(r0, G), :].astype(BF16)
    row_id = lax.broadcasted_iota(jnp.int32, (G, HEAD_DIM), 0)
    o_heads = [jnp.zeros((G, HEAD_DIM), F32)] * N_HEADS
    for j in range(G):
        for h, sl in enumerate(HEADS):
            f_b = jnp.broadcast_to(f_cols[sl, j:j + 1], (HEAD_DIM, HEAD_DIM))
            s_new = f_b * st_ref[j, h] + (1.0 - f_b) * v_rows[j:j + 1, sl]
            st_out_ref[j, h] = s_new
            read = _dot(q_rows[:, sl], s_new.astype(BF16))
            o_heads[h] = jnp.where(row_id == j, read, o_heads[h])
    o_scr[pl.ds(r0, G), :] = jnp.concatenate(o_heads, axis=1)

    @pl.when(i == pl.num_programs(0) - 1)
    def _output():
        hn = hn_ref[...]
        for sl in HEADS:
            o_h = o_scr[:, sl]
            o_scr[:, sl] = o_h * _rms_scale(o_h) * hn
        y_a = _dot((o_scr[...] * za_scr[...]).astype(BF16), wa_ref[...])
        y_b = _dot(pb_scr[...].astype(BF16), wb_ref[...])
        merged = ga_scr[...] * y_a + gb_scr[...] * y_b
        out = _dot(merged.astype(BF16), wo_ref[...])
        y_ref[:, 0, :] = x_ref[:, 0, :] + out * _rms_scale(out) * npost_ref[...]


def _run_decode(x, state, ctx, weights):
    n = x.shape[0]
    assert n % DECODE_GROUP == 0 and n == 128
    w_in, npre, npost, lb_logits, hn, cw, wa, wb, wo = weights

    def const(shape):
        zeros = (0,) * len(shape)
        return pl.BlockSpec(shape, lambda i: zeros, pipeline_mode=pl.Buffered(1))

    st_spec = pl.BlockSpec((DECODE_GROUP, N_HEADS, HEAD_DIM, HEAD_DIM), lambda i: (i, 0, 0, 0))
    rows_f32 = pltpu.VMEM((n, D_MODEL), F32)
    cols_f32 = pltpu.VMEM((D_MODEL, n), F32)
    return pl.pallas_call(
        _decode_kernel,
        grid=(n // DECODE_GROUP,),
        in_specs=[const(x.shape), st_spec, const(ctx.shape), const(w_in.shape), const(npre.shape),
                  const(npost.shape), const(lb_logits.shape), const(hn.shape), const(cw.shape),
                  const(wa.shape), const(wb.shape), const(wo.shape)],
        out_specs=[const(x.shape), st_spec, const(ctx.shape)],
        out_shape=[jax.ShapeDtypeStruct(x.shape, F32),
                   jax.ShapeDtypeStruct(state.shape, F32),
                   jax.ShapeDtypeStruct(ctx.shape, F32)],
        scratch_shapes=[cols_f32, rows_f32, rows_f32, rows_f32, rows_f32, rows_f32, rows_f32, rows_f32],
        compiler_params=pltpu.CompilerParams(
            dimension_semantics=("arbitrary",),
            vmem_limit_bytes=V7X_VMEM_LIMIT_BYTES),
        name="decode_step",
    )(x, state, ctx, w_in, npre, npost, lb_logits, hn, cw, wa, wb, wo)


def kernel(x_prompt, x_sample, state_hgrn, state_conv, meta_tokens, w_in, norm_pre, norm_post, lb_logits,
           hgrn_norm, conv_w, w_a, w_b, w_o):
    depth = w_in.shape[0]
    assert depth == 1, "single-layer trunk"
    assert x_sample.shape[1] == 1, "one new token per decode row"

    weights = (w_in[0], norm_pre, norm_post, lb_logits, hgrn_norm, conv_w[0], w_a[0], w_b[0], w_o[0])

    y_prompt, hgrn_p, conv_p, w_in_bf, wa_bf, wb_bf, wo_bf = _run_prompt(
        x_prompt, meta_tokens.astype(x_prompt.dtype), weights, tile=PROMPT_TILE, chunk=PROMPT_CHUNK)
    weights = (w_in_bf,) + weights[1:6] + (wa_bf, wb_bf, wo_bf)

    y_s, hgrn_s, conv_s = _run_decode(x_sample, state_hgrn[0], state_conv[0], weights)

    return (y_prompt, y_s, hgrn_p[None], hgrn_s[None], conv_p[None], conv_s[None])
```

```python
import functools

import jax
import jax.numpy as jnp
from jax import lax
from jax.experimental import pallas as pl
from jax.experimental.pallas import tpu as pltpu

D_MODEL = 1024
N_HEADS = 8
HEAD_DIM = D_MODEL // N_HEADS
CONV_W = 3
EPS = 1e-6
P_Q, P_F, P_I, P_ZA, P_B, P_C, P_H, P_ZB, P_GA, P_GB = range(10)

PROMPT_TILE = 256
PROMPT_CHUNK = 128
DECODE_GROUP = 8
LOAD_BANDS = 4
SUBLANES = 8
CTX_ROW0 = SUBLANES - (CONV_W - 1)
MAX_HALF_CHUNK_LOG_DECAY = 80.0
V7X_VMEM_LIMIT_BYTES = 58 * 1024 * 1024

BF16 = jnp.bfloat16
F32 = jnp.float32
HEADS = [slice(h * HEAD_DIM, (h + 1) * HEAD_DIM) for h in range(N_HEADS)]


def _dot(a, b):
    return jnp.dot(a, b, preferred_element_type=F32)


def _dot_nt(a, b):
    return lax.dot_general(a, b, (((1,), (1,)), ((), ())), preferred_element_type=F32)


def _dot_tn(a, b):
    return lax.dot_general(a, b, (((0,), (0,)), ((), ())), preferred_element_type=F32)


def _sigmoid(x):
    return 1.0 / (1.0 + jnp.exp(-x))


def _silu(x):
    return x * _sigmoid(x)


def _rms_scale(x):
    return lax.rsqrt(jnp.mean(x * x, axis=-1, keepdims=True) + EPS)


def _lower_bound(lb_logits):
    m = jnp.max(lb_logits, axis=0, keepdims=True)
    e = jnp.exp(lb_logits - m)
    return e[0:1, :] / jnp.sum(e, axis=0, keepdims=True)


def _block_cumsum(x, block):
    n = x.shape[0]
    ri = lax.broadcasted_iota(jnp.int32, (n, n), 0)
    ci = lax.broadcasted_iota(jnp.int32, (n, n), 1)
    tri = ri >= ci
    if n != block:
        tri = jnp.logical_and(tri, (ri // block) == (ci // block))
    tri = jnp.where(tri, 1.0, 0.0).astype(BF16)
    hi = x.astype(BF16)
    lo = (x - hi.astype(F32)).astype(BF16)
    return _dot(tri, hi) + _dot(tri, lo)


def _seq_kernel(x_ref, meta_ref, w_in_hbm, npre_ref, npost_ref, lb_ref, hn_ref, cw_ref,
                wa_hbm, wb_hbm, wo_hbm,
                y_ref, st_out_ref, c_out_ref, w_in_out, wa_out, wb_out, wo_out,
                st_scr, u_scr, q_scr, k_scr, v_scr, g_scr, o_scr,
                qt_scr, kt_scr, qin_scr, kst_scr, vb_scr, dec_scr, st_meta_scr, ctx_meta_scr,
                w_in_ref, wa_ref, wb_ref, wo_ref, stage_scr, load_sem, store_sem,
                *, tile, chunk):
    T, C = tile, chunk
    n_chunks = T // C
    t = pl.program_id(1)
    first_step = jnp.logical_and(pl.program_id(0) == 0, t == 0)
    last_step = jnp.logical_and(pl.program_id(0) == pl.num_programs(0) - 1, t == pl.num_programs(1) - 1)
    lb = _lower_bound(lb_ref[...])

    col = [pl.ds(j * D_MODEL, D_MODEL) for j in range(w_in_hbm.shape[1] // D_MODEL)]
    groups = [(w_in_hbm.at[:, c], w_in_ref.at[:, c], w_in_out.at[:, c]) for c in col]
    groups += [(wa_hbm, wa_ref, wa_out), (wb_hbm, wb_ref, wb_out), (wo_hbm, wo_ref, wo_out)]

    def load_copies(i):
        band = groups[i][0].shape[0] // LOAD_BANDS
        return [pltpu.make_async_copy(groups[i][0].at[pl.ds(b * band, band), :],
                                      stage_scr.at[i % 2, pl.ds(b * band, band), :],
                                      load_sem.at[i % 2, b]) for b in range(LOAD_BANDS)]

    def store_copy(i):
        return pltpu.make_async_copy(groups[i][1], groups[i][2], store_sem.at[i])

    @pl.when(first_step)
    def _stream_weights():
        for copy in load_copies(0):
            copy.start()
        for i in range(len(groups)):
            if i + 1 < len(groups):
                for copy in load_copies(i + 1):
                    copy.start()
            for copy in load_copies(i):
                copy.wait()
            groups[i][1][...] = stage_scr[i % 2].astype(BF16)
        for i in range(len(groups)):
            store_copy(i).start()

    def project(xn, j):
        return _dot(xn, w_in_ref[:, j * D_MODEL:(j + 1) * D_MODEL])

    @pl.when(first_step)
    def _meta_prefix():
        xm = meta_ref[...]
        n_meta = xm.shape[0]
        xn_m = (xm * _rms_scale(xm) * npre_ref[...]).astype(BF16)
        f_m = lb + (1.0 - lb) * _sigmoid(project(xn_m, P_F))
        g_m = _block_cumsum(jnp.log(f_m), n_meta)
        k_end = ((1.0 - f_m) * jnp.exp(g_m[n_meta - 1:n_meta, :] - g_m)).astype(BF16)
        v_m = project(xn_m, P_I).astype(BF16)
        for h, sl in enumerate(HEADS):
            st_meta_scr[h] = _dot_tn(k_end[:, sl], v_m[:, sl])
        u_m = project(xn_m, P_C) * project(xn_m, P_H)
        ctx_meta_scr[CTX_ROW0:SUBLANES, :] = u_m[n_meta - (CONV_W - 1):n_meta, :]
        dec_scr[...] = jnp.zeros(dec_scr.shape, F32)

    @pl.when(t == 0)
    def _init():
        st_scr[...] = st_meta_scr[...]
        u_scr[CTX_ROW0:SUBLANES, :] = ctx_meta_scr[CTX_ROW0:SUBLANES, :]

    x = x_ref[0]
    xn = (x * _rms_scale(x) * npre_ref[...]).astype(BF16)

    def proj(j):
        return project(xn, j)

    f = lb + (1.0 - lb) * _sigmoid(proj(P_F))
    k_scr[...] = 1.0 - f
    q_scr[...] = _silu(proj(P_Q))
    vb_scr[...] = proj(P_I).astype(BF16)
    g_scr[...] = _block_cumsum(jnp.log(f), C)

    g_floor = None
    for c in range(n_chunks):
        rows = slice(c * C, (c + 1) * C)
        gc = g_scr[rows, :]
        g_last = gc[C - 1:C, :]
        g_mid = gc[C // 2 - 1:C // 2, :]
        qc = q_scr[rows, :]
        kc = k_scr[rows, :]
        qt_scr[rows, :] = (qc * jnp.exp(gc - g_mid)).astype(BF16)
        kt_scr[rows, :] = (kc * jnp.exp(g_mid - gc)).astype(BF16)
        qin_scr[rows, :] = (qc * jnp.exp(gc)).astype(BF16)
        kst_scr[rows, :] = (kc * jnp.exp(g_last - gc)).astype(BF16)
        dec_scr[c:c + 1, :] = jnp.exp(g_last)
        half_floor = jnp.minimum(g_mid, g_last - g_mid)
        g_floor = half_floor if g_floor is None else jnp.minimum(g_floor, half_floor)
    stable = jnp.min(g_floor) >= -MAX_HALF_CHUNK_LOG_DECAY

    dec_cols = dec_scr[...].T
    keep = jnp.logical_and(
        lax.broadcasted_iota(jnp.int32, (C, C), 0) >= lax.broadcasted_iota(jnp.int32, (C, C), 1), stable)
    for c in range(n_chunks):
        rows = slice(c * C, (c + 1) * C)
        scores = [jnp.where(keep, _dot_nt(qt_scr[rows, sl], kt_scr[rows, sl]), 0.0).astype(BF16)
                  for sl in HEADS]
        for h, sl in enumerate(HEADS):
            vb = vb_scr[rows, sl]
            st = st_scr[h]
            o_scr[rows, sl] = _dot(jnp.concatenate([qin_scr[rows, sl], scores[h]], axis=1),
                                   jnp.concatenate([st.astype(BF16), vb], axis=0))
            decay = jnp.broadcast_to(dec_cols[sl, c:c + 1], (HEAD_DIM, HEAD_DIM))
            st_scr[h] = decay * st + _dot_tn(kst_scr[rows, sl], vb)

    @pl.when(jnp.logical_not(stable))
    def _intra_exact():
        same_head = (lax.broadcasted_iota(jnp.int32, (D_MODEL, D_MODEL), 0) // HEAD_DIM ==
                     lax.broadcasted_iota(jnp.int32, (D_MODEL, D_MODEL), 1) // HEAD_DIM)
        head_sum = jnp.where(same_head, 1.0, 0.0).astype(BF16)
        tpos = lax.broadcasted_iota(jnp.int32, (C, 1), 0)
        v_scr[...] = proj(P_I)

        def chunk_step(c, carry):
            r0 = pl.multiple_of(c * C, C)
            rows = pl.ds(r0, C)
            gc = g_scr[rows, :]
            qc = q_scr[rows, :]

            def src_step(i, carry2):
                src = pl.ds(r0 + i, 1)
                p = qc * jnp.exp(jnp.minimum(gc - g_scr[src, :], 0.0)) * k_scr[src, :]
                p = jnp.where(tpos >= i, p, 0.0).astype(BF16)
                o_scr[rows, :] += _dot(p, head_sum) * v_scr[src, :]
                return carry2

            return lax.fori_loop(0, C, src_step, carry)

        lax.fori_loop(0, n_chunks, chunk_step, 0)

    hn = hn_ref[...]
    for sl in HEADS:
        o_h = o_scr[:, sl]
        o_scr[:, sl] = o_h * _rms_scale(o_h) * hn
    y_a = _dot((o_scr[...] * _silu(proj(P_ZA))).astype(BF16), wa_ref[...])

    u = proj(P_C) * proj(P_H)
    u_scr[SUBLANES:SUBLANES + T, :] = u
    cw = cw_ref[...]
    conv = cw[CONV_W - 1:CONV_W, :] * u
    for j in range(CONV_W - 1):
        conv = conv + cw[j:j + 1, :] * u_scr[CTX_ROW0 + j:CTX_ROW0 + j + T, :]
    y_b = _dot((proj(P_B) * conv * _silu(proj(P_ZB))).astype(BF16), wb_ref[...])
    new_ctx = u[T - (CONV_W - 1):T, :]
    u_scr[CTX_ROW0:SUBLANES, :] = new_ctx

    merged = _sigmoid(proj(P_GA)) * y_a + _sigmoid(proj(P_GB)) * y_b
    out = _dot(merged.astype(BF16), wo_ref[...])
    y_ref[0] = x + out * _rms_scale(out) * npost_ref[...]

    @pl.when(t == pl.num_programs(1) - 1)
    def _finish():
        c_out_ref[0] = new_ctx
        st_out_ref[0] = st_scr[...]

    @pl.when(last_step)
    def _weights_written():
        for i in range(len(groups)):
            store_copy(i).wait()


def _const_spec(shape):
    zeros = (0,) * len(shape)
    return pl.BlockSpec(shape, lambda b, t: zeros, pipeline_mode=pl.Buffered(1))


def _run_prompt(x, meta, weights, *, tile, chunk):
    n, length, _ = x.shape
    assert length % tile == 0 and tile % chunk == 0 and chunk % (2 * SUBLANES) == 0 and tile >= CONV_W - 1
    assert meta.shape[0] % (2 * SUBLANES) == 0 and meta.shape[0] >= CONV_W - 1
    assert tile // chunk <= HEAD_DIM
    w_in, npre, npost, lb_logits, hn, cw, wa, wb, wo = weights
    kern = functools.partial(_seq_kernel, tile=tile, chunk=chunk)
    hbm = pl.BlockSpec(memory_space=pl.ANY)
    n_groups = w_in.shape[1] // D_MODEL + 3
    assert w_in.shape[1] % D_MODEL == 0 and wa.shape == wb.shape == wo.shape == (w_in.shape[0], D_MODEL)
    state_shape = (1, N_HEADS, HEAD_DIM, HEAD_DIM)
    ctx_shape = (1, CONV_W - 1, D_MODEL)
    tile_f32 = pltpu.VMEM((tile, D_MODEL), F32)
    tile_bf16 = pltpu.VMEM((tile, D_MODEL), BF16)
    return pl.pallas_call(
        kern,
        grid=(n, length // tile),
        in_specs=[
            pl.BlockSpec((1, tile, D_MODEL), lambda b, t: (b, t, 0)),
            _const_spec(meta.shape),
            hbm,
            _const_spec(npre.shape),
            _const_spec(npost.shape),
            _const_spec(lb_logits.shape),
            _const_spec(hn.shape),
            _const_spec(cw.shape),
            hbm, hbm, hbm,
        ],
        out_specs=[
            pl.BlockSpec((1, tile, D_MODEL), lambda b, t: (b, t, 0)),
            pl.BlockSpec(state_shape, lambda b, t: (b, 0, 0, 0)),
            pl.BlockSpec(ctx_shape, lambda b, t: (b, 0, 0)),
            hbm, hbm, hbm, hbm,
        ],
        out_shape=[
            jax.ShapeDtypeStruct(x.shape, F32),
            jax.ShapeDtypeStruct((n,) + state_shape[1:], F32),
            jax.ShapeDtypeStruct((n,) + ctx_shape[1:], F32),
            jax.ShapeDtypeStruct(w_in.shape, BF16),
            jax.ShapeDtypeStruct(wa.shape, BF16),
            jax.ShapeDtypeStruct(wb.shape, BF16),
            jax.ShapeDtypeStruct(wo.shape, BF16),
        ],
        scratch_shapes=[
            pltpu.VMEM((N_HEADS, HEAD_DIM, HEAD_DIM), F32),
            pltpu.VMEM((tile + SUBLANES, D_MODEL), F32),
            tile_f32, tile_f32, tile_f32, tile_f32, tile_f32,
            tile_bf16, tile_bf16, tile_bf16, tile_bf16, tile_bf16,
            pltpu.VMEM((HEAD_DIM, D_MODEL), F32),
            pltpu.VMEM((N_HEADS, HEAD_DIM, HEAD_DIM), F32),
            pltpu.VMEM((SUBLANES, D_MODEL), F32),
            pltpu.VMEM(w_in.shape, BF16),
            pltpu.VMEM(wa.shape, BF16), pltpu.VMEM(wb.shape, BF16), pltpu.VMEM(wo.shape, BF16),
            pltpu.VMEM((2,) + wa.shape, F32),
            pltpu.SemaphoreType.DMA((2, LOAD_BANDS)),
            pltpu.SemaphoreType.DMA((n_groups,)),
        ],
        compiler_params=pltpu.CompilerParams(
            dimension_semantics=("arbitrary", "arbitrary"),
            vmem_limit_bytes=V7X_VMEM_LIMIT_BYTES),
        name="prompt_sweep",
    )(x, meta, w_in, npre, npost, lb_logits, hn, cw, wa, wb, wo)


def _decode_kernel(x_ref, st_ref, ctx_ref, w_in_ref, npre_ref, npost_ref, lb_ref, hn_ref, cw_ref,
                   wa_ref, wb_ref, wo_ref,
                   y_ref, st_out_ref, ctx_out_ref,
                   ft_scr, q_scr, v_scr, o_scr, za_scr, pb_scr, ga_scr, gb_scr):
    G = DECODE_GROUP
    i = pl.program_id(0)
    n_rows = x_ref.shape[0]

    @pl.when(i == 0)
    def _project():
        x = x_ref[:, 0, :]
        xn = (x * _rms_scale(x) * npre_ref[...]).astype(BF16)

        def proj(j):
            return _dot(xn, w_in_ref[:, j * D_MODEL:(j + 1) * D_MODEL])

        lb = _lower_bound(lb_ref[...])
        f = lb + (1.0 - lb) * _sigmoid(proj(P_F))
        ft_scr[...] = f.T
        q_scr[...] = _silu(proj(P_Q))
        v_scr[...] = proj(P_I)
        za_scr[...] = _silu(proj(P_ZA))
        u = proj(P_C) * proj(P_H)
        cw = cw_ref[...]
        conv = cw[CONV_W - 1:CONV_W, :] * u
        for j in range(CONV_W - 1):
            ctx_j = ctx_ref[:, j, :]
            conv = conv + cw[j:j + 1, :] * ctx_j
            if j > 0:
                ctx_out_ref[:, j - 1, :] = ctx_j
        ctx_out_ref[:, CONV_W - 2, :] = u
        pb_scr[...] = proj(P_B) * conv * _silu(proj(P_ZB))
        ga_scr[...] = _sigmoid(proj(P_GA))
        gb_scr[...] = _sigmoid(proj(P_GB))

    shift = (n_rows - i * G) % n_rows
    f_cols = pltpu.roll(ft_scr[...], shift, 1)
    r0 = pl.multiple_of(i * G, G)
    v_rows = v_scr[pl.ds(r0, G), :]
    q_rows = q_scr[pl.ds(r0, G), :].astype(BF16)
    row_id = lax.broadcasted_iota(jnp.int32, (G, HEAD_DIM), 0)
    o_heads = [jnp.zeros((G, HEAD_DIM), F32)] * N_HEADS
    for j in range(G):
        for h, sl in enumerate(HEADS):
            f_b = jnp.broadcast_to(f_cols[sl, j:j + 1], (HEAD_DIM, HEAD_DIM))
            s_new = f_b * st_ref[j, h] + (1.0 - f_b) * v_rows[j:j + 1, sl]
            st_out_ref[j, h] = s_new
            read = _dot(q_rows[:, sl], s_new.astype(BF16))
            o_heads[h] = jnp.where(row_id == j, read, o_heads[h])
    o_scr[pl.ds(r0, G), :] = jnp.concatenate(o_heads, axis=1)

    @pl.when(i == pl.num_programs(0) - 1)
    def _output():
        hn = hn_ref[...]
        for sl in HEADS:
            o_h = o_scr[:, sl]
            o_scr[:, sl] = o_h * _rms_scale(o_h) * hn
        y_a = _dot((o_scr[...] * za_scr[...]).astype(BF16), wa_ref[...])
        y_b = _dot(pb_scr[...].astype(BF16), wb_ref[...])
        merged = ga_scr[...] * y_a + gb_scr[...] * y_b
        out = _dot(merged.astype(BF16), wo_ref[...])
        y_ref[:, 0, :] = x_ref[:, 0, :] + out * _rms_scale(out) * npost_ref[...]


def _run_decode(x, state, ctx, weights):
    n = x.shape[0]
    assert n % DECODE_GROUP == 0 and n == 128
    w_in, npre, npost, lb_logits, hn, cw, wa, wb, wo = weights

    def const(shape):
        zeros = (0,) * len(shape)
        return pl.BlockSpec(shape, lambda i: zeros, pipeline_mode=pl.Buffered(1))

    st_spec = pl.BlockSpec((DECODE_GROUP, N_HEADS, HEAD_DIM, HEAD_DIM), lambda i: (i, 0, 0, 0))
    rows_f32 = pltpu.VMEM((n, D_MODEL), F32)
    cols_f32 = pltpu.VMEM((D_MODEL, n), F32)
    return pl.pallas_call(
        _decode_kernel,
        grid=(n // DECODE_GROUP,),
        in_specs=[const(x.shape), st_spec, const(ctx.shape), const(w_in.shape), const(npre.shape),
                  const(npost.shape), const(lb_logits.shape), const(hn.shape), const(cw.shape),
                  const(wa.shape), const(wb.shape), const(wo.shape)],
        out_specs=[const(x.shape), st_spec, const(ctx.shape)],
        out_shape=[jax.ShapeDtypeStruct(x.shape, F32),
                   jax.ShapeDtypeStruct(state.shape, F32),
                   jax.ShapeDtypeStruct(ctx.shape, F32)],
        scratch_shapes=[cols_f32, rows_f32, rows_f32, rows_f32, rows_f32, rows_f32, rows_f32, rows_f32],
        compiler_params=pltpu.CompilerParams(
            dimension_semantics=("arbitrary",),
            vmem_limit_bytes=V7X_VMEM_LIMIT_BYTES),
        name="decode_step",
    )(x, state, ctx, w_in, npre, npost, lb_logits, hn, cw, wa, wb, wo)


def kernel(x_prompt, x_sample, state_hgrn, state_conv, meta_tokens, w_in, norm_pre, norm_post, lb_logits,
           hgrn_norm, conv_w, w_a, w_b, w_o):
    depth = w_in.shape[0]
    assert depth == 1, "single-layer trunk"
    assert x_sample.shape[1] == 1, "one new token per decode row"

    weights = (w_in[0], norm_pre, norm_post, lb_logits, hgrn_norm, conv_w[0], w_a[0], w_b[0], w_o[0])

    y_prompt, hgrn_p, conv_p, w_in_bf, wa_bf, wb_bf, wo_bf = _run_prompt(
        x_prompt, meta_tokens.astype(x_prompt.dtype), weights, tile=PROMPT_TILE, chunk=PROMPT_CHUNK)
    weights = (w_in_bf,) + weights[1:6] + (wa_bf, wb_bf, wo_bf)

    y_s, hgrn_s, conv_s = _run_decode(x_sample, state_hgrn[0], state_conv[0], weights)

    return (y_prompt, y_s, hgrn_p[None], hgrn_s[None], conv_p[None], conv_s[None])
```

```python
import functools

import jax
import jax.numpy as jnp
from jax import lax
from jax.experimental import pallas as pl
from jax.experimental.pallas import tpu as pltpu

D_MODEL = 1024
N_HEADS = 8
HEAD_DIM = D_MODEL // N_HEADS
CONV_W = 3
EPS = 1e-6
P_Q, P_F, P_I, P_ZA, P_B, P_C, P_H, P_ZB, P_GA, P_GB = range(10)

PROMPT_TILE = 256
PROMPT_CHUNK = 128
DECODE_GROUP = 8
LOAD_BANDS = 4
SUBLANES = 8
CTX_ROW0 = SUBLANES - (CONV_W - 1)
MAX_HALF_CHUNK_LOG_DECAY = 80.0
V7X_VMEM_LIMIT_BYTES = 58 * 1024 * 1024

BF16 = jnp.bfloat16
F32 = jnp.float32
HEADS = [slice(h * HEAD_DIM, (h + 1) * HEAD_DIM) for h in range(N_HEADS)]


def _dot(a, b):
    return jnp.dot(a, b, preferred_element_type=F32)


def _dot_nt(a, b):
    return lax.dot_general(a, b, (((1,), (1,)), ((), ())), preferred_element_type=F32)


def _dot_tn(a, b):
    return lax.dot_general(a, b, (((0,), (0,)), ((), ())), preferred_element_type=F32)


def _sigmoid(x):
    return 1.0 / (1.0 + jnp.exp(-x))


def _silu(x):
    return x * _sigmoid(x)


def _rms_scale(x):
    return lax.rsqrt(jnp.mean(x * x, axis=-1, keepdims=True) + EPS)


def _lower_bound(lb_logits):
    m = jnp.max(lb_logits, axis=0, keepdims=True)
    e = jnp.exp(lb_logits - m)
    return e[0:1, :] / jnp.sum(e, axis=0, keepdims=True)


def _block_cumsum(x, block):
    n = x.shape[0]
    ri = lax.broadcasted_iota(jnp.int32, (n, n), 0)
    ci = lax.broadcasted_iota(jnp.int32, (n, n), 1)
    tri = ri >= ci
    if n != block:
        tri = jnp.logical_and(tri, (ri // block) == (ci // block))
    tri = jnp.where(tri, 1.0, 0.0).astype(BF16)
    hi = x.astype(BF16)
    lo = (x - hi.astype(F32)).astype(BF16)
    return _dot(tri, hi) + _dot(tri, lo)


def _seq_kernel(x_ref, meta_ref, w_in_hbm, npre_ref, npost_ref, lb_ref, hn_ref, cw_ref,
                wa_hbm, wb_hbm, wo_hbm,
                y_ref, st_out_ref, c_out_ref, w_in_out, wa_out, wb_out, wo_out,
                st_scr, u_scr, q_scr, k_scr, v_scr, g_scr, o_scr,
                qt_scr, kt_scr, qin_scr, kst_scr, vb_scr, dec_scr, st_meta_scr, ctx_meta_scr,
                w_in_ref, wa_ref, wb_ref, wo_ref, stage_scr, load_sem, store_sem,
                *, tile, chunk):
    T, C = tile, chunk
    n_chunks = T // C
    t = pl.program_id(1)
    first_step = jnp.logical_and(pl.program_id(0) == 0, t == 0)
    last_step = jnp.logical_and(pl.program_id(0) == pl.num_programs(0) - 1, t == pl.num_programs(1) - 1)
    lb = _lower_bound(lb_ref[...])

    col = [pl.ds(j * D_MODEL, D_MODEL) for j in range(w_in_hbm.shape[1] // D_MODEL)]
    groups = [(w_in_hbm.at[:, c], w_in_ref.at[:, c], w_in_out.at[:, c]) for c in col]
    groups += [(wa_hbm, wa_ref, wa_out), (wb_hbm, wb_ref, wb_out), (wo_hbm, wo_ref, wo_out)]

    def load_copies(i):
        band = groups[i][0].shape[0] // LOAD_BANDS
        return [pltpu.make_async_copy(groups[i][0].at[pl.ds(b * band, band), :],
                                      stage_scr.at[i % 2, pl.ds(b * band, band), :],
                                      load_sem.at[i % 2, b]) for b in range(LOAD_BANDS)]

    def store_copy(i):
        return pltpu.make_async_copy(groups[i][1], groups[i][2], store_sem.at[i])

    @pl.when(first_step)
    def _stream_weights():
        for copy in load_copies(0):
            copy.start()
        for i in range(len(groups)):
            if i + 1 < len(groups):
                for copy in load_copies(i + 1):
                    copy.start()
            for copy in load_copies(i):
                copy.wait()
            groups[i][1][...] = stage_scr[i % 2].astype(BF16)
        for i in range(len(groups)):
            store_copy(i).start()

    def project(xn, j):
        return _dot(xn, w_in_ref[:, j * D_MODEL:(j + 1) * D_MODEL])

    @pl.when(first_step)
    def _meta_prefix():
        xm = meta_ref[...]
        n_meta = xm.shape[0]
        xn_m = (xm * _rms_scale(xm) * npre_ref[...]).astype(BF16)
        f_m = lb + (1.0 - lb) * _sigmoid(project(xn_m, P_F))
        g_m = _block_cumsum(jnp.log(f_m), n_meta)
        k_end = ((1.0 - f_m) * jnp.exp(g_m[n_meta - 1:n_meta, :] - g_m)).astype(BF16)
        v_m = project(xn_m, P_I).astype(BF16)
        for h, sl in enumerate(HEADS):
            st_meta_scr[h] = _dot_tn(k_end[:, sl], v_m[:, sl])
        u_m = project(xn_m, P_C) * project(xn_m, P_H)
        ctx_meta_scr[CTX_ROW0:SUBLANES, :] = u_m[n_meta - (CONV_W - 1):n_meta, :]
        dec_scr[...] = jnp.zeros(dec_scr.shape, F32)

    @pl.when(t == 0)
    def _init():
        st_scr[...] = st_meta_scr[...]
        u_scr[CTX_ROW0:SUBLANES, :] = ctx_meta_scr[CTX_ROW0:SUBLANES, :]

    x = x_ref[0]
    xn = (x * _rms_scale(x) * npre_ref[...]).astype(BF16)

    def proj(j):
        return project(xn, j)

    f = lb + (1.0 - lb) * _sigmoid(proj(P_F))
    k_scr[...] = 1.0 - f
    q_scr[...] = _silu(proj(P_Q))
    vb_scr[...] = proj(P_I).astype(BF16)
    g_scr[...] = _block_cumsum(jnp.log(f), C)

    g_floor = None
    for c in range(n_chunks):
        rows = slice(c * C, (c + 1) * C)
        gc = g_scr[rows, :]
        g_last = gc[C - 1:C, :]
        g_mid = gc[C // 2 - 1:C // 2, :]
        qc = q_scr[rows, :]
        kc = k_scr[rows, :]
        qt_scr[rows, :] = (qc * jnp.exp(gc - g_mid)).astype(BF16)
        kt_scr[rows, :] = (kc * jnp.exp(g_mid - gc)).astype(BF16)
        qin_scr[rows, :] = (qc * jnp.exp(gc)).astype(BF16)
        kst_scr[rows, :] = (kc * jnp.exp(g_last - gc)).astype(BF16)
        dec_scr[c:c + 1, :] = jnp.exp(g_last)
        half_floor = jnp.minimum(g_mid, g_last - g_mid)
        g_floor = half_floor if g_floor is None else jnp.minimum(g_floor, half_floor)
    stable = jnp.min(g_floor) >= -MAX_HALF_CHUNK_LOG_DECAY

    dec_cols = dec_scr[...].T
    keep = jnp.logical_and(
        lax.broadcasted_iota(jnp.int32, (C, C), 0) >= lax.broadcasted_iota(jnp.int32, (C, C), 1), stable)
    for c in range(n_chunks):
        rows = slice(c * C, (c + 1) * C)
        scores = [jnp.where(keep, _dot_nt(qt_scr[rows, sl], kt_scr[rows, sl]), 0.0).astype(BF16)
                  for sl in HEADS]
        for h, sl in enumerate(HEADS):
            vb = vb_scr[rows, sl]
            st = st_scr[h]
            o_scr[rows, sl] = _dot(jnp.concatenate([qin_scr[rows, sl], scores[h]], axis=1),
                                   jnp.concatenate([st.astype(BF16), vb], axis=0))
            decay = jnp.broadcast_to(dec_cols[sl, c:c + 1], (HEAD_DIM, HEAD_DIM))
            st_scr[h] = decay * st + _dot_tn(kst_scr[rows, sl], vb)

    @pl.when(jnp.logical_not(stable))
    def _intra_exact():
        same_head = (lax.broadcasted_iota(jnp.int32, (D_MODEL, D_MODEL), 0) // HEAD_DIM ==
                     lax.broadcasted_iota(jnp.int32, (D_MODEL, D_MODEL), 1) // HEAD_DIM)
        head_sum = jnp.where(same_head, 1.0, 0.0).astype(BF16)
        tpos = lax.broadcasted_iota(jnp.int32, (C, 1), 0)
        v_scr[...] = proj(P_I)

        def chunk_step(c, carry):
            r0 = pl.multiple_of(c * C, C)
            rows = pl.ds(r0, C)
            gc = g_scr[rows, :]
            qc = q_scr[rows, :]

            def src_step(i, carry2):
                src = pl.ds(r0 + i, 1)
                p = qc * jnp.exp(jnp.minimum(gc - g_scr[src, :], 0.0)) * k_scr[src, :]
                p = jnp.where(tpos >= i, p, 0.0).astype(BF16)
                o_scr[rows, :] += _dot(p, head_sum) * v_scr[src, :]
                return carry2

            return lax.fori_loop(0, C, src_step, carry)

        lax.fori_loop(0, n_chunks, chunk_step, 0)

    hn = hn_ref[...]
    silu_za = _silu(proj(P_ZA))
    gated = []
    for sl in HEADS:
        o_h = o_scr[:, sl]
        gated.append((o_h * _rms_scale(o_h) * hn * silu_za[:, sl]).astype(BF16))
    y_a = _dot(jnp.concatenate(gated, axis=1), wa_ref[...])

    u = proj(P_C) * proj(P_H)
    u_scr[SUBLANES:SUBLANES + T, :] = u
    cw = cw_ref[...]
    conv = cw[CONV_W - 1:CONV_W, :] * u
    for j in range(CONV_W - 1):
        conv = conv + cw[j:j + 1, :] * u_scr[CTX_ROW0 + j:CTX_ROW0 + j + T, :]
    y_b = _dot((proj(P_B) * conv * _silu(proj(P_ZB))).astype(BF16), wb_ref[...])
    new_ctx = u[T - (CONV_W - 1):T, :]
    u_scr[CTX_ROW0:SUBLANES, :] = new_ctx

    merged = _sigmoid(proj(P_GA)) * y_a + _sigmoid(proj(P_GB)) * y_b
    out = _dot(merged.astype(BF16), wo_ref[...])
    y_ref[0] = x_ref[0] + out * _rms_scale(out) * npost_ref[...]

    @pl.when(t == pl.num_programs(1) - 1)
    def _finish():
        c_out_ref[0] = new_ctx
        st_out_ref[0] = st_scr[...]

    @pl.when(last_step)
    def _weights_written():
        for i in range(len(groups)):
            store_copy(i).wait()


def _const_spec(shape):
    zeros = (0,) * len(shape)
    return pl.BlockSpec(shape, lambda b, t: zeros, pipeline_mode=pl.Buffered(1))


def _run_prompt(x, meta, weights, *, tile, chunk):
    n, length, _ = x.shape
    assert length % tile == 0 and tile % chunk == 0 and chunk % (2 * SUBLANES) == 0 and tile >= CONV_W - 1
    assert meta.shape[0] % (2 * SUBLANES) == 0 and meta.shape[0] >= CONV_W - 1
    assert tile // chunk <= HEAD_DIM
    w_in, npre, npost, lb_logits, hn, cw, wa, wb, wo = weights
    kern = functools.partial(_seq_kernel, tile=tile, chunk=chunk)
    hbm = pl.BlockSpec(memory_space=pl.ANY)
    n_groups = w_in.shape[1] // D_MODEL + 3
    assert w_in.shape[1] % D_MODEL == 0 and wa.shape == wb.shape == wo.shape == (w_in.shape[0], D_MODEL)
    state_shape = (1, N_HEADS, HEAD_DIM, HEAD_DIM)
    ctx_shape = (1, CONV_W - 1, D_MODEL)
    tile_f32 = pltpu.VMEM((tile, D_MODEL), F32)
    tile_bf16 = pltpu.VMEM((tile, D_MODEL), BF16)
    return pl.pallas_call(
        kern,
        grid=(n, length // tile),
        in_specs=[
            pl.BlockSpec((1, tile, D_MODEL), lambda b, t: (b, t, 0)),
            _const_spec(meta.shape),
            hbm,
            _const_spec(npre.shape),
            _const_spec(npost.shape),
            _const_spec(lb_logits.shape),
            _const_spec(hn.shape),
            _const_spec(cw.shape),
            hbm, hbm, hbm,
        ],
        out_specs=[
            pl.BlockSpec((1, tile, D_MODEL), lambda b, t: (b, t, 0)),
            pl.BlockSpec(state_shape, lambda b, t: (b, 0, 0, 0)),
            pl.BlockSpec(ctx_shape, lambda b, t: (b, 0, 0)),
            hbm, hbm, hbm, hbm,
        ],
        out_shape=[
            jax.ShapeDtypeStruct(x.shape, F32),
            jax.ShapeDtypeStruct((n,) + state_shape[1:], F32),
            jax.ShapeDtypeStruct((n,) + ctx_shape[1:], F32),
            jax.ShapeDtypeStruct(w_in.shape, BF16),
            jax.ShapeDtypeStruct(wa.shape, BF16),
            jax.ShapeDtypeStruct(wb.shape, BF16),
            jax.ShapeDtypeStruct(wo.shape, BF16),
        ],
        scratch_shapes=[
            pltpu.VMEM((N_HEADS, HEAD_DIM, HEAD_DIM), F32),
            pltpu.VMEM((tile + SUBLANES, D_MODEL), F32),
            tile_f32, tile_f32, tile_f32, tile_f32, tile_f32,
            tile_bf16, tile_bf16, tile_bf16, tile_bf16, tile_bf16,
            pltpu.VMEM((HEAD_DIM, D_MODEL), F32),
            pltpu.VMEM((N_HEADS, HEAD_DIM, HEAD_DIM), F32),
            pltpu.VMEM((SUBLANES, D_MODEL), F32),
            pltpu.VMEM(w_in.shape, BF16),
            pltpu.VMEM(wa.shape, BF16), pltpu.VMEM(wb.shape, BF16), pltpu.VMEM(wo.shape, BF16),
            pltpu.VMEM((2,) + wa.shape, F32),
            pltpu.SemaphoreType.DMA((2, LOAD_BANDS)),
            pltpu.SemaphoreType.DMA((n_groups,)),
        ],
        compiler_params=pltpu.CompilerParams(
            dimension_semantics=("arbitrary", "arbitrary"),
            vmem_limit_bytes=V7X_VMEM_LIMIT_BYTES),
        name="prompt_sweep",
    )(x, meta, w_in, npre, npost, lb_logits, hn, cw, wa, wb, wo)


def _decode_kernel(x_ref, st_ref, ctx_ref, w_in_ref, npre_ref, npost_ref, lb_ref, hn_ref, cw_ref,
                   wa_ref, wb_ref, wo_ref,
                   y_ref, st_out_ref, ctx_out_ref,
                   ft_scr, q_scr, v_scr, o_scr, za_scr, pb_scr, ga_scr, gb_scr):
    G = DECODE_GROUP
    i = pl.program_id(0)
    n_rows = x_ref.shape[0]

    @pl.when(i == 0)
    def _project():
        x = x_ref[:, 0, :]
        xn = (x * _rms_scale(x) * npre_ref[...]).astype(BF16)

        def proj(j):
            return _dot(xn, w_in_ref[:, j * D_MODEL:(j + 1) * D_MODEL])

        lb = _lower_bound(lb_ref[...])
        f = lb + (1.0 - lb) * _sigmoid(proj(P_F))
        ft_scr[...] = f.T
        q_scr[...] = _silu(proj(P_Q))
        v_scr[...] = proj(P_I)
        za_scr[...] = _silu(proj(P_ZA))
        u = proj(P_C) * proj(P_H)
        cw = cw_ref[...]
        conv = cw[CONV_W - 1:CONV_W, :] * u
        for j in range(CONV_W - 1):
            ctx_j = ctx_ref[:, j, :]
            conv = conv + cw[j:j + 1, :] * ctx_j
            if j > 0:
                ctx_out_ref[:, j - 1, :] = ctx_j
        ctx_out_ref[:, CONV_W - 2, :] = u
        pb_scr[...] = proj(P_B) * conv * _silu(proj(P_ZB))
        ga_scr[...] = _sigmoid(proj(P_GA))
        gb_scr[...] = _sigmoid(proj(P_GB))

    shift = (n_rows - i * G) % n_rows
    f_cols = pltpu.roll(ft_scr[...], shift, 1)
    r0 = pl.multiple_of(i * G, G)
    v_rows = v_scr[pl.ds(r0, G), :]
    q_rows = q_scr[pl.ds(r0, G), :].astype(BF16)
    row_id = lax.broadcasted_iota(jnp.int32, (G, HEAD_DIM), 0)
    o_heads = [jnp.zeros((G, HEAD_DIM), F32)] * N_HEADS
    for j in range(G):
        for h, sl in enumerate(HEADS):
            f_b = jnp.broadcast_to(f_cols[sl, j:j + 1], (HEAD_DIM, HEAD_DIM))
            s_new = f_b * st_ref[j, h] + (1.0 - f_b) * v_rows[j:j + 1, sl]
            st_out_ref[j, h] = s_new
            read = _dot(q_rows[:, sl], s_new.astype(BF16))
            o_heads[h] = jnp.where(row_id == j, read, o_heads[h])
    o_scr[pl.ds(r0, G), :] = jnp.concatenate(o_heads, axis=1)

    @pl.when(i == pl.num_programs(0) - 1)
    def _output():
        hn = hn_ref[...]
        for sl in HEADS:
            o_h = o_scr[:, sl]
            o_scr[:, sl] = o_h * _rms_scale(o_h) * hn
        y_a = _dot((o_scr[...] * za_scr[...]).astype(BF16), wa_ref[...])
        y_b = _dot(pb_scr[...].astype(BF16), wb_ref[...])
        merged = ga_scr[...] * y_a + gb_scr[...] * y_b
        out = _dot(merged.astype(BF16), wo_ref[...])
        y_ref[:, 0, :] = x_ref[:, 0, :] + out * _rms_scale(out) * npost_ref[...]


def _run_decode(x, state, ctx, weights):
    n = x.shape[0]
    assert n % DECODE_GROUP == 0 and n == 128
    w_in, npre, npost, lb_logits, hn, cw, wa, wb, wo = weights

    def const(shape):
        zeros = (0,) * len(shape)
        return pl.BlockSpec(shape, lambda i: zeros, pipeline_mode=pl.Buffered(1))

    st_spec = pl.BlockSpec((DECODE_GROUP, N_HEADS, HEAD_DIM, HEAD_DIM), lambda i: (i, 0, 0, 0))
    rows_f32 = pltpu.VMEM((n, D_MODEL), F32)
    cols_f32 = pltpu.VMEM((D_MODEL, n), F32)
    return pl.pallas_call(
        _decode_kernel,
        grid=(n // DECODE_GROUP,),
        in_specs=[const(x.shape), st_spec, const(ctx.shape), const(w_in.shape), const(npre.shape),
                  const(npost.shape), const(lb_logits.shape), const(hn.shape), const(cw.shape),
                  const(wa.shape), const(wb.shape), const(wo.shape)],
        out_specs=[const(x.shape), st_spec, const(ctx.shape)],
        out_shape=[jax.ShapeDtypeStruct(x.shape, F32),
                   jax.ShapeDtypeStruct(state.shape, F32),
                   jax.ShapeDtypeStruct(ctx.shape, F32)],
        scratch_shapes=[cols_f32, rows_f32, rows_f32, rows_f32, rows_f32, rows_f32, rows_f32, rows_f32],
        compiler_params=pltpu.CompilerParams(
            dimension_semantics=("arbitrary",),
            vmem_limit_bytes=V7X_VMEM_LIMIT_BYTES),
        name="decode_step",
    )(x, state, ctx, w_in, npre, npost, lb_logits, hn, cw, wa, wb, wo)


def kernel(x_prompt, x_sample, state_hgrn, state_conv, meta_tokens, w_in, norm_pre, norm_post, lb_logits,
           hgrn_norm, conv_w, w_a, w_b, w_o):
    depth = w_in.shape[0]
    assert depth == 1, "single-layer trunk"
    assert x_sample.shape[1] == 1, "one new token per decode row"

    weights = (w_in[0], norm_pre, norm_post, lb_logits, hgrn_norm, conv_w[0], w_a[0], w_b[0], w_o[0])

    y_prompt, hgrn_p, conv_p, w_in_bf, wa_bf, wb_bf, wo_bf = _run_prompt(
        x_prompt, meta_tokens.astype(x_prompt.dtype), weights, tile=PROMPT_TILE, chunk=PROMPT_CHUNK)
    weights = (w_in_bf,) + weights[1:6] + (wa_bf, wb_bf, wo_bf)

    y_s, hgrn_s, conv_s = _run_decode(x_sample, state_hgrn[0], state_conv[0], weights)

    return (y_prompt, y_s, hgrn_p[None], hgrn_s[None], conv_p[None], conv_s[None])
```

```python
import functools

import jax
import jax.numpy as jnp
from jax import lax
from jax.experimental import pallas as pl
from jax.experimental.pallas import tpu as pltpu

D_MODEL = 1024
N_HEADS = 8
HEAD_DIM = D_MODEL // N_HEADS
CONV_W = 3
EPS = 1e-6
P_Q, P_F, P_I, P_ZA, P_B, P_C, P_H, P_ZB, P_GA, P_GB = range(10)

PROMPT_TILE = 256
PROMPT_CHUNK = 128
DECODE_GROUP = 8
STATE_IN_SLOTS = 3
STATE_OUT_SLOTS = 2
STATE_PARTS = 4
LOAD_BANDS = 4
SUBLANES = 8
CTX_ROW0 = SUBLANES - (CONV_W - 1)
MAX_HALF_CHUNK_LOG_DECAY = 80.0
V7X_VMEM_LIMIT_BYTES = 58 * 1024 * 1024

BF16 = jnp.bfloat16
F32 = jnp.float32
HEADS = [slice(h * HEAD_DIM, (h + 1) * HEAD_DIM) for h in range(N_HEADS)]


def _dot(a, b):
    return jnp.dot(a, b, preferred_element_type=F32)


def _dot_nt(a, b):
    return lax.dot_general(a, b, (((1,), (1,)), ((), ())), preferred_element_type=F32)


def _dot_tn(a, b):
    return lax.dot_general(a, b, (((0,), (0,)), ((), ())), preferred_element_type=F32)


def _sigmoid(x):
    return 1.0 / (1.0 + jnp.exp(-x))


def _silu(x):
    return x * _sigmoid(x)


def _rms_scale(x):
    return lax.rsqrt(jnp.mean(x * x, axis=-1, keepdims=True) + EPS)


def _lower_bound(lb_logits):
    m = jnp.max(lb_logits, axis=0, keepdims=True)
    e = jnp.exp(lb_logits - m)
    return e[0:1, :] / jnp.sum(e, axis=0, keepdims=True)


def _block_cumsum(x, block):
    n = x.shape[0]
    ri = lax.broadcasted_iota(jnp.int32, (n, n), 0)
    ci = lax.broadcasted_iota(jnp.int32, (n, n), 1)
    tri = ri >= ci
    if n != block:
        tri = jnp.logical_and(tri, (ri // block) == (ci // block))
    tri = jnp.where(tri, 1.0, 0.0).astype(BF16)
    hi = x.astype(BF16)
    lo = (x - hi.astype(F32)).astype(BF16)
    return _dot(tri, hi) + _dot(tri, lo)


def _seq_kernel(x_ref, meta_ref, w_in_hbm, npre_ref, npost_ref, lb_ref, hn_ref, cw_ref,
                wa_hbm, wb_hbm, wo_hbm,
                y_ref, st_out_ref, c_out_ref, w_in_out, wa_out, wb_out, wo_out,
                st_scr, u_scr, q_scr, k_scr, v_scr, g_scr, o_scr,
                qt_scr, kt_scr, qin_scr, kst_scr, vb_scr, dec_scr, st_meta_scr, ctx_meta_scr,
                w_in_ref, wa_ref, wb_ref, wo_ref, stage_scr, load_sem, store_sem,
                *, tile, chunk):
    T, C = tile, chunk
    n_chunks = T // C
    t = pl.program_id(1)
    first_step = jnp.logical_and(pl.program_id(0) == 0, t == 0)
    last_step = jnp.logical_and(pl.program_id(0) == pl.num_programs(0) - 1, t == pl.num_programs(1) - 1)
    lb = _lower_bound(lb_ref[...])

    col = [pl.ds(j * D_MODEL, D_MODEL) for j in range(w_in_hbm.shape[1] // D_MODEL)]
    groups = [(w_in_hbm.at[:, c], w_in_ref.at[:, c], w_in_out.at[:, c]) for c in col]
    groups += [(wa_hbm, wa_ref, wa_out), (wb_hbm, wb_ref, wb_out), (wo_hbm, wo_ref, wo_out)]

    def load_copies(i):
        band = groups[i][0].shape[0] // LOAD_BANDS
        return [pltpu.make_async_copy(groups[i][0].at[pl.ds(b * band, band), :],
                                      stage_scr.at[i % 2, pl.ds(b * band, band), :],
                                      load_sem.at[i % 2, b]) for b in range(LOAD_BANDS)]

    def store_copy(i):
        return pltpu.make_async_copy(groups[i][1], groups[i][2], store_sem.at[i])

    @pl.when(first_step)
    def _stream_weights():
        for copy in load_copies(0):
            copy.start()
        for i in range(len(groups)):
            if i + 1 < len(groups):
                for copy in load_copies(i + 1):
                    copy.start()
            for copy in load_copies(i):
                copy.wait()
            groups[i][1][...] = stage_scr[i % 2].astype(BF16)
        for i in range(len(groups)):
            store_copy(i).start()

    def project(xn, j):
        return _dot(xn, w_in_ref[:, j * D_MODEL:(j + 1) * D_MODEL])

    @pl.when(first_step)
    def _meta_prefix():
        xm = meta_ref[...]
        n_meta = xm.shape[0]
        xn_m = (xm * _rms_scale(xm) * npre_ref[...]).astype(BF16)
        f_m = lb + (1.0 - lb) * _sigmoid(project(xn_m, P_F))
        g_m = _block_cumsum(jnp.log(f_m), n_meta)
        k_end = ((1.0 - f_m) * jnp.exp(g_m[n_meta - 1:n_meta, :] - g_m)).astype(BF16)
        v_m = project(xn_m, P_I).astype(BF16)
        for h, sl in enumerate(HEADS):
            st_meta_scr[h] = _dot_tn(k_end[:, sl], v_m[:, sl])
        u_m = project(xn_m, P_C) * project(xn_m, P_H)
        ctx_meta_scr[CTX_ROW0:SUBLANES, :] = u_m[n_meta - (CONV_W - 1):n_meta, :]
        dec_scr[...] = jnp.zeros(dec_scr.shape, F32)

    @pl.when(t == 0)
    def _init():
        st_scr[...] = st_meta_scr[...]
        u_scr[CTX_ROW0:SUBLANES, :] = ctx_meta_scr[CTX_ROW0:SUBLANES, :]

    x = x_ref[0]
    xn = (x * _rms_scale(x) * npre_ref[...]).astype(BF16)

    def proj(j):
        return project(xn, j)

    f = lb + (1.0 - lb) * _sigmoid(proj(P_F))
    k_scr[...] = 1.0 - f
    q_scr[...] = _silu(proj(P_Q))
    vb_scr[...] = proj(P_I).astype(BF16)
    g_scr[...] = _block_cumsum(jnp.log(f), C)

    g_floor = None
    for c in range(n_chunks):
        rows = slice(c * C, (c + 1) * C)
        gc = g_scr[rows, :]
        g_last = gc[C - 1:C, :]
        g_mid = gc[C // 2 - 1:C // 2, :]
        qc = q_scr[rows, :]
        kc = k_scr[rows, :]
        qt_scr[rows, :] = (qc * jnp.exp(gc - g_mid)).astype(BF16)
        kt_scr[rows, :] = (kc * jnp.exp(g_mid - gc)).astype(BF16)
        qin_scr[rows, :] = (qc * jnp.exp(gc)).astype(BF16)
        kst_scr[rows, :] = (kc * jnp.exp(g_last - gc)).astype(BF16)
        dec_scr[c:c + 1, :] = jnp.exp(g_last)
        half_floor = jnp.minimum(g_mid, g_last - g_mid)
        g_floor = half_floor if g_floor is None else jnp.minimum(g_floor, half_floor)
    stable = jnp.min(g_floor) >= -MAX_HALF_CHUNK_LOG_DECAY

    dec_cols = dec_scr[...].T
    keep = jnp.logical_and(
        lax.broadcasted_iota(jnp.int32, (C, C), 0) >= lax.broadcasted_iota(jnp.int32, (C, C), 1), stable)
    for c in range(n_chunks):
        rows = slice(c * C, (c + 1) * C)
        scores = [jnp.where(keep, _dot_nt(qt_scr[rows, sl], kt_scr[rows, sl]), 0.0).astype(BF16)
                  for sl in HEADS]
        for h, sl in enumerate(HEADS):
            vb = vb_scr[rows, sl]
            st = st_scr[h]
            o_scr[rows, sl] = _dot(jnp.concatenate([qin_scr[rows, sl], scores[h]], axis=1),
                                   jnp.concatenate([st.astype(BF16), vb], axis=0))
            decay = jnp.broadcast_to(dec_cols[sl, c:c + 1], (HEAD_DIM, HEAD_DIM))
            st_scr[h] = decay * st + _dot_tn(kst_scr[rows, sl], vb)

    @pl.when(jnp.logical_not(stable))
    def _intra_exact():
        same_head = (lax.broadcasted_iota(jnp.int32, (D_MODEL, D_MODEL), 0) // HEAD_DIM ==
                     lax.broadcasted_iota(jnp.int32, (D_MODEL, D_MODEL), 1) // HEAD_DIM)
        head_sum = jnp.where(same_head, 1.0, 0.0).astype(BF16)
        tpos = lax.broadcasted_iota(jnp.int32, (C, 1), 0)
        v_scr[...] = proj(P_I)

        def chunk_step(c, carry):
            r0 = pl.multiple_of(c * C, C)
            rows = pl.ds(r0, C)
            gc = g_scr[rows, :]
            qc = q_scr[rows, :]

            def src_step(i, carry2):
                src = pl.ds(r0 + i, 1)
                p = qc * jnp.exp(jnp.minimum(gc - g_scr[src, :], 0.0)) * k_scr[src, :]
                p = jnp.where(tpos >= i, p, 0.0).astype(BF16)
                o_scr[rows, :] += _dot(p, head_sum) * v_scr[src, :]
                return carry2

            return lax.fori_loop(0, C, src_step, carry)

        lax.fori_loop(0, n_chunks, chunk_step, 0)

    hn = hn_ref[...]
    silu_za = _silu(proj(P_ZA))
    gated = []
    for sl in HEADS:
        o_h = o_scr[:, sl]
        gated.append((o_h * _rms_scale(o_h) * hn * silu_za[:, sl]).astype(BF16))
    y_a = _dot(jnp.concatenate(gated, axis=1), wa_ref[...])

    u = proj(P_C) * proj(P_H)
    u_scr[SUBLANES:SUBLANES + T, :] = u
    cw = cw_ref[...]
    conv = cw[CONV_W - 1:CONV_W, :] * u
    for j in range(CONV_W - 1):
        conv = conv + cw[j:j + 1, :] * u_scr[CTX_ROW0 + j:CTX_ROW0 + j + T, :]
    y_b = _dot((proj(P_B) * conv * _silu(proj(P_ZB))).astype(BF16), wb_ref[...])
    new_ctx = u[T - (CONV_W - 1):T, :]
    u_scr[CTX_ROW0:SUBLANES, :] = new_ctx

    merged = _sigmoid(proj(P_GA)) * y_a + _sigmoid(proj(P_GB)) * y_b
    out = _dot(merged.astype(BF16), wo_ref[...])
    y_ref[0] = x_ref[0] + out * _rms_scale(out) * npost_ref[...]

    @pl.when(t == pl.num_programs(1) - 1)
    def _finish():
        c_out_ref[0] = new_ctx
        st_out_ref[0] = st_scr[...]

    @pl.when(last_step)
    def _weights_written():
        for i in range(len(groups)):
            store_copy(i).wait()


def _const_spec(shape):
    zeros = (0,) * len(shape)
    return pl.BlockSpec(shape, lambda b, t: zeros, pipeline_mode=pl.Buffered(1))


def _run_prompt(x, meta, weights, *, tile, chunk):
    n, length, _ = x.shape
    assert length % tile == 0 and tile % chunk == 0 and chunk % (2 * SUBLANES) == 0 and tile >= CONV_W - 1
    assert meta.shape[0] % (2 * SUBLANES) == 0 and meta.shape[0] >= CONV_W - 1
    assert tile // chunk <= HEAD_DIM
    w_in, npre, npost, lb_logits, hn, cw, wa, wb, wo = weights
    kern = functools.partial(_seq_kernel, tile=tile, chunk=chunk)
    hbm = pl.BlockSpec(memory_space=pl.ANY)
    n_groups = w_in.shape[1] // D_MODEL + 3
    assert w_in.shape[1] % D_MODEL == 0 and wa.shape == wb.shape == wo.shape == (w_in.shape[0], D_MODEL)
    state_shape = (1, N_HEADS, HEAD_DIM, HEAD_DIM)
    ctx_shape = (1, CONV_W - 1, D_MODEL)
    tile_f32 = pltpu.VMEM((tile, D_MODEL), F32)
    tile_bf16 = pltpu.VMEM((tile, D_MODEL), BF16)
    return pl.pallas_call(
        kern,
        grid=(n, length // tile),
        in_specs=[
            pl.BlockSpec((1, tile, D_MODEL), lambda b, t: (b, t, 0)),
            _const_spec(meta.shape),
            hbm,
            _const_spec(npre.shape),
            _const_spec(npost.shape),
            _const_spec(lb_logits.shape),
            _const_spec(hn.shape),
            _const_spec(cw.shape),
            hbm, hbm, hbm,
        ],
        out_specs=[
            pl.BlockSpec((1, tile, D_MODEL), lambda b, t: (b, t, 0)),
            pl.BlockSpec(state_shape, lambda b, t: (b, 0, 0, 0)),
            pl.BlockSpec(ctx_shape, lambda b, t: (b, 0, 0)),
            hbm, hbm, hbm, hbm,
        ],
        out_shape=[
            jax.ShapeDtypeStruct(x.shape, F32),
            jax.ShapeDtypeStruct((n,) + state_shape[1:], F32),
            jax.ShapeDtypeStruct((n,) + ctx_shape[1:], F32),
            jax.ShapeDtypeStruct(w_in.shape, BF16),
            jax.ShapeDtypeStruct(wa.shape, BF16),
            jax.ShapeDtypeStruct(wb.shape, BF16),
            jax.ShapeDtypeStruct(wo.shape, BF16),
        ],
        scratch_shapes=[
            pltpu.VMEM((N_HEADS, HEAD_DIM, HEAD_DIM), F32),
            pltpu.VMEM((tile + SUBLANES, D_MODEL), F32),
            tile_f32, tile_f32, tile_f32, tile_f32, tile_f32,
            tile_bf16, tile_bf16, tile_bf16, tile_bf16, tile_bf16,
            pltpu.VMEM((HEAD_DIM, D_MODEL), F32),
            pltpu.VMEM((N_HEADS, HEAD_DIM, HEAD_DIM), F32),
            pltpu.VMEM((SUBLANES, D_MODEL), F32),
            pltpu.VMEM(w_in.shape, BF16),
            pltpu.VMEM(wa.shape, BF16), pltpu.VMEM(wb.shape, BF16), pltpu.VMEM(wo.shape, BF16),
            pltpu.VMEM((2,) + wa.shape, F32),
            pltpu.SemaphoreType.DMA((2, LOAD_BANDS)),
            pltpu.SemaphoreType.DMA((n_groups,)),
        ],
        compiler_params=pltpu.CompilerParams(
            dimension_semantics=("arbitrary", "arbitrary"),
            vmem_limit_bytes=V7X_VMEM_LIMIT_BYTES),
        name="prompt_sweep",
    )(x, meta, w_in, npre, npost, lb_logits, hn, cw, wa, wb, wo)


def _decode_kernel(x_ref, st_hbm, ctx_ref, w_in_ref, npre_ref, npost_ref, lb_ref, hn_ref, cw_ref,
                   wa_ref, wb_ref, wo_ref,
                   y_ref, st_out_hbm, ctx_out_ref,
                   ft_scr, q_scr, v_scr, o_scr, za_scr, pb_scr, ga_scr, gb_scr,
                   in_buf, out_buf, in_sem, out_sem):
    G = DECODE_GROUP
    i = pl.program_id(0)
    n_steps = pl.num_programs(0)
    n_rows = x_ref.shape[0]
    part_rows = G // STATE_PARTS

    def in_copies(step):
        slot = step % STATE_IN_SLOTS
        return [pltpu.make_async_copy(st_hbm.at[pl.ds(step * G + p * part_rows, part_rows)],
                                      in_buf.at[slot, pl.ds(p * part_rows, part_rows)],
                                      in_sem.at[slot, p]) for p in range(STATE_PARTS)]

    def out_copies(step):
        slot = step % STATE_OUT_SLOTS
        return [pltpu.make_async_copy(out_buf.at[slot, pl.ds(p * part_rows, part_rows)],
                                      st_out_hbm.at[pl.ds(step * G + p * part_rows, part_rows)],
                                      out_sem.at[slot, p]) for p in range(STATE_PARTS)]

    @pl.when(i == 0)
    def _prime():
        for step in range(STATE_IN_SLOTS - 1):
            for copy in in_copies(step):
                copy.start()

    @pl.when(i == 0)
    def _project():
        x = x_ref[:, 0, :]
        xn = (x * _rms_scale(x) * npre_ref[...]).astype(BF16)

        def proj(j):
            return _dot(xn, w_in_ref[:, j * D_MODEL:(j + 1) * D_MODEL])

        lb = _lower_bound(lb_ref[...])
        f = lb + (1.0 - lb) * _sigmoid(proj(P_F))
        ft_scr[...] = f.T
        q_scr[...] = _silu(proj(P_Q))
        v_scr[...] = proj(P_I)
        za_scr[...] = _silu(proj(P_ZA))
        u = proj(P_C) * proj(P_H)
        cw = cw_ref[...]
        conv = cw[CONV_W - 1:CONV_W, :] * u
        for j in range(CONV_W - 1):
            ctx_j = ctx_ref[:, j, :]
            conv = conv + cw[j:j + 1, :] * ctx_j
            if j > 0:
                ctx_out_ref[:, j - 1, :] = ctx_j
        ctx_out_ref[:, CONV_W - 2, :] = u
        pb_scr[...] = proj(P_B) * conv * _silu(proj(P_ZB))
        ga_scr[...] = _sigmoid(proj(P_GA))
        gb_scr[...] = _sigmoid(proj(P_GB))

    @pl.when(i + STATE_IN_SLOTS - 1 < n_steps)
    def _read_ahead():
        for copy in in_copies(i + STATE_IN_SLOTS - 1):
            copy.start()

    @pl.when(i >= STATE_OUT_SLOTS)
    def _slot_written_back():
        for copy in out_copies(i - STATE_OUT_SLOTS):
            copy.wait()

    for copy in in_copies(i):
        copy.wait()
    st_ref = in_buf.at[i % STATE_IN_SLOTS]
    st_out_ref = out_buf.at[i % STATE_OUT_SLOTS]

    shift = (n_rows - i * G) % n_rows
    f_cols = pltpu.roll(ft_scr[...], shift, 1)
    r0 = pl.multiple_of(i * G, G)
    v_rows = v_scr[pl.ds(r0, G), :]
    q_rows = q_scr[pl.ds(r0, G), :].astype(BF16)
    row_id = lax.broadcasted_iota(jnp.int32, (G, HEAD_DIM), 0)
    o_heads = [jnp.zeros((G, HEAD_DIM), F32)] * N_HEADS
    for j in range(G):
        for h, sl in enumerate(HEADS):
            f_b = jnp.broadcast_to(f_cols[sl, j:j + 1], (HEAD_DIM, HEAD_DIM))
            s_new = f_b * st_ref[j, h] + (1.0 - f_b) * v_rows[j:j + 1, sl]
            st_out_ref[j, h] = s_new
            read = _dot(q_rows[:, sl], s_new.astype(BF16))
            o_heads[h] = jnp.where(row_id == j, read, o_heads[h])
    o_scr[pl.ds(r0, G), :] = jnp.concatenate(o_heads, axis=1)
    for copy in out_copies(i):
        copy.start()

    @pl.when(i == pl.num_programs(0) - 1)
    def _output():
        hn = hn_ref[...]
        for sl in HEADS:
            o_h = o_scr[:, sl]
            o_scr[:, sl] = o_h * _rms_scale(o_h) * hn
        y_a = _dot((o_scr[...] * za_scr[...]).astype(BF16), wa_ref[...])
        y_b = _dot(pb_scr[...].astype(BF16), wb_ref[...])
        merged = ga_scr[...] * y_a + gb_scr[...] * y_b
        out = _dot(merged.astype(BF16), wo_ref[...])
        y_ref[:, 0, :] = x_ref[:, 0, :] + out * _rms_scale(out) * npost_ref[...]
        for back in range(STATE_OUT_SLOTS):
            for copy in out_copies(i - back):
                copy.wait()


def _run_decode(x, state, ctx, weights):
    n = x.shape[0]
    assert n % DECODE_GROUP == 0 and n == 128 and DECODE_GROUP % STATE_PARTS == 0
    assert n // DECODE_GROUP >= max(STATE_IN_SLOTS, STATE_OUT_SLOTS)
    w_in, npre, npost, lb_logits, hn, cw, wa, wb, wo = weights

    def const(shape):
        zeros = (0,) * len(shape)
        return pl.BlockSpec(shape, lambda i: zeros, pipeline_mode=pl.Buffered(1))

    hbm = pl.BlockSpec(memory_space=pl.ANY)
    in_slots = pltpu.VMEM((STATE_IN_SLOTS, DECODE_GROUP) + state.shape[1:], F32)
    out_slots = pltpu.VMEM((STATE_OUT_SLOTS, DECODE_GROUP) + state.shape[1:], F32)
    rows_f32 = pltpu.VMEM((n, D_MODEL), F32)
    cols_f32 = pltpu.VMEM((D_MODEL, n), F32)
    return pl.pallas_call(
        _decode_kernel,
        grid=(n // DECODE_GROUP,),
        in_specs=[const(x.shape), hbm, const(ctx.shape), const(w_in.shape), const(npre.shape),
                  const(npost.shape), const(lb_logits.shape), const(hn.shape), const(cw.shape),
                  const(wa.shape), const(wb.shape), const(wo.shape)],
        out_specs=[const(x.shape), hbm, const(ctx.shape)],
        out_shape=[jax.ShapeDtypeStruct(x.shape, F32),
                   jax.ShapeDtypeStruct(state.shape, F32),
                   jax.ShapeDtypeStruct(ctx.shape, F32)],
        scratch_shapes=[cols_f32, rows_f32, rows_f32, rows_f32, rows_f32, rows_f32, rows_f32, rows_f32,
                        in_slots, out_slots,
                        pltpu.SemaphoreType.DMA((STATE_IN_SLOTS, STATE_PARTS)),
                        pltpu.SemaphoreType.DMA((STATE_OUT_SLOTS, STATE_PARTS))],
        compiler_params=pltpu.CompilerParams(
            dimension_semantics=("arbitrary",),
            vmem_limit_bytes=V7X_VMEM_LIMIT_BYTES),
        name="decode_step",
    )(x, state, ctx, w_in, npre, npost, lb_logits, hn, cw, wa, wb, wo)


def kernel(x_prompt, x_sample, state_hgrn, state_conv, meta_tokens, w_in, norm_pre, norm_post, lb_logits,
           hgrn_norm, conv_w, w_a, w_b, w_o):
    depth = w_in.shape[0]
    assert depth == 1, "single-layer trunk"
    assert x_sample.shape[1] == 1, "one new token per decode row"

    weights = (w_in[0], norm_pre, norm_post, lb_logits, hgrn_norm, conv_w[0], w_a[0], w_b[0], w_o[0])

    y_prompt, hgrn_p, conv_p, w_in_bf, wa_bf, wb_bf, wo_bf = _run_prompt(
        x_prompt, meta_tokens.astype(x_prompt.dtype), weights, tile=PROMPT_TILE, chunk=PROMPT_CHUNK)
    weights = (w_in_bf,) + weights[1:6] + (wa_bf, wb_bf, wo_bf)

    y_s, hgrn_s, conv_s = _run_decode(x_sample, state_hgrn[0], state_conv[0], weights)

    return (y_prompt, y_s, hgrn_p[None], hgrn_s[None], conv_p[None], conv_s[None])
```

```python
import functools

import jax
import jax.numpy as jnp
from jax import lax
from jax.experimental import pallas as pl
from jax.experimental.pallas import tpu as pltpu

D_MODEL = 1024
N_HEADS = 8
HEAD_DIM = D_MODEL // N_HEADS
CONV_W = 3
EPS = 1e-6
P_Q, P_F, P_I, P_ZA, P_B, P_C, P_H, P_ZB, P_GA, P_GB = range(10)

PROMPT_TILE = 256
PROMPT_CHUNK = 128
DECODE_GROUP = 8
STATE_IN_SLOTS = 3
STATE_OUT_SLOTS = 2
STATE_PARTS = 4
LOAD_BANDS = 4
SUBLANES = 8
CTX_ROW0 = SUBLANES - (CONV_W - 1)
MAX_HALF_CHUNK_LOG_DECAY = 80.0
V7X_VMEM_LIMIT_BYTES = 58 * 1024 * 1024

BF16 = jnp.bfloat16
F32 = jnp.float32
HEADS = [slice(h * HEAD_DIM, (h + 1) * HEAD_DIM) for h in range(N_HEADS)]


def _dot(a, b):
    return jnp.dot(a, b, preferred_element_type=F32)


def _dot_nt(a, b):
    return lax.dot_general(a, b, (((1,), (1,)), ((), ())), preferred_element_type=F32)


def _dot_tn(a, b):
    return lax.dot_general(a, b, (((0,), (0,)), ((), ())), preferred_element_type=F32)


def _sigmoid(x):
    return 1.0 / (1.0 + jnp.exp(-x))


def _silu(x):
    return x * _sigmoid(x)


def _rms_scale(x):
    return lax.rsqrt(jnp.mean(x * x, axis=-1, keepdims=True) + EPS)


def _lower_bound(lb_logits):
    m = jnp.max(lb_logits, axis=0, keepdims=True)
    e = jnp.exp(lb_logits - m)
    return e[0:1, :] / jnp.sum(e, axis=0, keepdims=True)


def _block_cumsum(x, block):
    n = x.shape[0]
    ri = lax.broadcasted_iota(jnp.int32, (n, n), 0)
    ci = lax.broadcasted_iota(jnp.int32, (n, n), 1)
    tri = ri >= ci
    if n != block:
        tri = jnp.logical_and(tri, (ri // block) == (ci // block))
    tri = jnp.where(tri, 1.0, 0.0).astype(BF16)
    hi = x.astype(BF16)
    lo = (x - hi.astype(F32)).astype(BF16)
    return _dot(tri, hi) + _dot(tri, lo)


def _seq_kernel(x_ref, meta_ref, w_in_hbm, npre_ref, npost_ref, lb_ref, hn_ref, cw_ref,
                wa_hbm, wb_hbm, wo_hbm,
                y_ref, st_out_ref, c_out_ref, w_in_out, wa_out, wb_out, wo_out,
                st_scr, u_scr, q_scr, k_scr, v_scr, g_scr, o_scr,
                qt_scr, kt_scr, qin_scr, kst_scr, vb_scr, dec_scr, st_meta_scr, ctx_meta_scr,
                w_in_ref, wa_ref, wb_ref, wo_ref, stage_scr, load_sem, store_sem,
                *, tile, chunk):
    T, C = tile, chunk
    n_chunks = T // C
    t = pl.program_id(1)
    first_step = jnp.logical_and(pl.program_id(0) == 0, t == 0)
    last_step = jnp.logical_and(pl.program_id(0) == pl.num_programs(0) - 1, t == pl.num_programs(1) - 1)
    lb = _lower_bound(lb_ref[...])

    col = [pl.ds(j * D_MODEL, D_MODEL) for j in range(w_in_hbm.shape[1] // D_MODEL)]
    groups = [(w_in_hbm.at[:, c], w_in_ref.at[:, c], w_in_out.at[:, c]) for c in col]
    groups += [(wa_hbm, wa_ref, wa_out), (wb_hbm, wb_ref, wb_out), (wo_hbm, wo_ref, wo_out)]

    def load_copies(i):
        band = groups[i][0].shape[0] // LOAD_BANDS
        return [pltpu.make_async_copy(groups[i][0].at[pl.ds(b * band, band), :],
                                      stage_scr.at[i % 2, pl.ds(b * band, band), :],
                                      load_sem.at[i % 2, b]) for b in range(LOAD_BANDS)]

    def store_copy(i):
        return pltpu.make_async_copy(groups[i][1], groups[i][2], store_sem.at[i])

    @pl.when(first_step)
    def _stream_weights():
        for copy in load_copies(0):
            copy.start()
        for i in range(len(groups)):
            if i + 1 < len(groups):
                for copy in load_copies(i + 1):
                    copy.start()
            for copy in load_copies(i):
                copy.wait()
            groups[i][1][...] = stage_scr[i % 2].astype(BF16)
        for i in range(len(groups)):
            store_copy(i).start()

    def project(xn, j):
        return _dot(xn, w_in_ref[:, j * D_MODEL:(j + 1) * D_MODEL])

    @pl.when(first_step)
    def _meta_prefix():
        xm = meta_ref[...]
        n_meta = xm.shape[0]
        xn_m = (xm * _rms_scale(xm) * npre_ref[...]).astype(BF16)
        f_m = lb + (1.0 - lb) * _sigmoid(project(xn_m, P_F))
        g_m = _block_cumsum(jnp.log(f_m), n_meta)
        k_end = ((1.0 - f_m) * jnp.exp(g_m[n_meta - 1:n_meta, :] - g_m)).astype(BF16)
        v_m = project(xn_m, P_I).astype(BF16)
        for h, sl in enumerate(HEADS):
            st_meta_scr[h] = _dot_tn(k_end[:, sl], v_m[:, sl])
        u_m = project(xn_m, P_C) * project(xn_m, P_H)
        ctx_meta_scr[CTX_ROW0:SUBLANES, :] = u_m[n_meta - (CONV_W - 1):n_meta, :]
        dec_scr[...] = jnp.zeros(dec_scr.shape, F32)

    @pl.when(t == 0)
    def _init():
        st_scr[...] = st_meta_scr[...]
        u_scr[CTX_ROW0:SUBLANES, :] = ctx_meta_scr[CTX_ROW0:SUBLANES, :]

    x = x_ref[0]
    xn = (x * _rms_scale(x) * npre_ref[...]).astype(BF16)

    def proj(j):
        return project(xn, j)

    f = lb + (1.0 - lb) * _sigmoid(proj(P_F))
    k_scr[...] = 1.0 - f
    q_scr[...] = _silu(proj(P_Q))
    vb_scr[...] = proj(P_I).astype(BF16)
    g_scr[...] = _block_cumsum(jnp.log(f), C)

    g_floor = None
    for c in range(n_chunks):
        rows = slice(c * C, (c + 1) * C)
        gc = g_scr[rows, :]
        g_last = gc[C - 1:C, :]
        g_mid = gc[C // 2 - 1:C // 2, :]
        qc = q_scr[rows, :]
        kc = k_scr[rows, :]
        qt_scr[rows, :] = (qc * jnp.exp(gc - g_mid)).astype(BF16)
        kt_scr[rows, :] = (kc * jnp.exp(g_mid - gc)).astype(BF16)
        qin_scr[rows, :] = (qc * jnp.exp(gc)).astype(BF16)
        kst_scr[rows, :] = (kc * jnp.exp(g_last - gc)).astype(BF16)
        dec_scr[c:c + 1, :] = jnp.exp(g_last)
        half_floor = jnp.minimum(g_mid, g_last - g_mid)
        g_floor = half_floor if g_floor is None else jnp.minimum(g_floor, half_floor)
    stable = jnp.min(g_floor) >= -MAX_HALF_CHUNK_LOG_DECAY

    dec_cols = dec_scr[...].T
    keep = jnp.logical_and(
        lax.broadcasted_iota(jnp.int32, (C, C), 0) >= lax.broadcasted_iota(jnp.int32, (C, C), 1), stable)
    for c in range(n_chunks):
        rows = slice(c * C, (c + 1) * C)
        scores = [jnp.where(keep, _dot_nt(qt_scr[rows, sl], kt_scr[rows, sl]), 0.0).astype(BF16)
                  for sl in HEADS]
        for h, sl in enumerate(HEADS):
            vb = vb_scr[rows, sl]
            st = st_scr[h]
            o_scr[rows, sl] = _dot(jnp.concatenate([qin_scr[rows, sl], scores[h]], axis=1),
                                   jnp.concatenate([st.astype(BF16), vb], axis=0))
            decay = jnp.broadcast_to(dec_cols[sl, c:c + 1], (HEAD_DIM, HEAD_DIM))
            st_scr[h] = decay * st + _dot_tn(kst_scr[rows, sl], vb)

    @pl.when(jnp.logical_not(stable))
    def _intra_exact():
        same_head = (lax.broadcasted_iota(jnp.int32, (D_MODEL, D_MODEL), 0) // HEAD_DIM ==
                     lax.broadcasted_iota(jnp.int32, (D_MODEL, D_MODEL), 1) // HEAD_DIM)
        head_sum = jnp.where(same_head, 1.0, 0.0).astype(BF16)
        tpos = lax.broadcasted_iota(jnp.int32, (C, 1), 0)
        v_scr[...] = proj(P_I)

        def chunk_step(c, carry):
            r0 = pl.multiple_of(c * C, C)
            rows = pl.ds(r0, C)
            gc = g_scr[rows, :]
            qc = q_scr[rows, :]

            def src_step(i, carry2):
                src = pl.ds(r0 + i, 1)
                p = qc * jnp.exp(jnp.minimum(gc - g_scr[src, :], 0.0)) * k_scr[src, :]
                p = jnp.where(tpos >= i, p, 0.0).astype(BF16)
                o_scr[rows, :] += _dot(p, head_sum) * v_scr[src, :]
                return carry2

            return lax.fori_loop(0, C, src_step, carry)

        lax.fori_loop(0, n_chunks, chunk_step, 0)

    hn = hn_ref[...]
    silu_za = _silu(proj(P_ZA))
    gated = []
    for sl in HEADS:
        o_h = o_scr[:, sl]
        gated.append((o_h * _rms_scale(o_h) * hn * silu_za[:, sl]).astype(BF16))
    y_a = _dot(jnp.concatenate(gated, axis=1), wa_ref[...])

    u = proj(P_C) * proj(P_H)
    u_scr[SUBLANES:SUBLANES + T, :] = u
    cw = cw_ref[...]
    conv = cw[CONV_W - 1:CONV_W, :] * u
    for j in range(CONV_W - 1):
        conv = conv + cw[j:j + 1, :] * u_scr[CTX_ROW0 + j:CTX_ROW0 + j + T, :]
    y_b = _dot((proj(P_B) * conv * _silu(proj(P_ZB))).astype(BF16), wb_ref[...])
    new_ctx = u[T - (CONV_W - 1):T, :]
    u_scr[CTX_ROW0:SUBLANES, :] = new_ctx

    merged = _sigmoid(proj(P_GA)) * y_a + _sigmoid(proj(P_GB)) * y_b
    out = _dot(merged.astype(BF16), wo_ref[...])
    y_ref[0] = x_ref[0] + out * _rms_scale(out) * npost_ref[...]

    @pl.when(t == pl.num_programs(1) - 1)
    def _finish():
        c_out_ref[0] = new_ctx
        st_out_ref[0] = st_scr[...]

    @pl.when(last_step)
    def _weights_written():
        for i in range(len(groups)):
            store_copy(i).wait()


def _const_spec(shape):
    zeros = (0,) * len(shape)
    return pl.BlockSpec(shape, lambda b, t: zeros, pipeline_mode=pl.Buffered(1))


def _run_prompt(x, meta, weights, *, tile, chunk):
    n, length, _ = x.shape
    assert length % tile == 0 and tile % chunk == 0 and chunk % (2 * SUBLANES) == 0 and tile >= CONV_W - 1
    assert meta.shape[0] % (2 * SUBLANES) == 0 and meta.shape[0] >= CONV_W - 1
    assert tile // chunk <= HEAD_DIM
    w_in, npre, npost, lb_logits, hn, cw, wa, wb, wo = weights
    kern = functools.partial(_seq_kernel, tile=tile, chunk=chunk)
    hbm = pl.BlockSpec(memory_space=pl.ANY)
    n_groups = w_in.shape[1] // D_MODEL + 3
    assert w_in.shape[1] % D_MODEL == 0 and wa.shape == wb.shape == wo.shape == (w_in.shape[0], D_MODEL)
    state_shape = (1, N_HEADS, HEAD_DIM, HEAD_DIM)
    ctx_shape = (1, CONV_W - 1, D_MODEL)
    tile_f32 = pltpu.VMEM((tile, D_MODEL), F32)
    tile_bf16 = pltpu.VMEM((tile, D_MODEL), BF16)
    return pl.pallas_call(
        kern,
        grid=(n, length // tile),
        in_specs=[
            pl.BlockSpec((1, tile, D_MODEL), lambda b, t: (b, t, 0)),
            _const_spec(meta.shape),
            hbm,
            _const_spec(npre.shape),
            _const_spec(npost.shape),
            _const_spec(lb_logits.shape),
            _const_spec(hn.shape),
            _const_spec(cw.shape),
            hbm, hbm, hbm,
        ],
        out_specs=[
            pl.BlockSpec((1, tile, D_MODEL), lambda b, t: (b, t, 0)),
            pl.BlockSpec(state_shape, lambda b, t: (b, 0, 0, 0)),
            pl.BlockSpec(ctx_shape, lambda b, t: (b, 0, 0)),
            hbm, hbm, hbm, hbm,
        ],
        out_shape=[
            jax.ShapeDtypeStruct(x.shape, F32),
            jax.ShapeDtypeStruct((n,) + state_shape[1:], F32),
            jax.ShapeDtypeStruct((n,) + ctx_shape[1:], F32),
            jax.ShapeDtypeStruct(w_in.shape, BF16),
            jax.ShapeDtypeStruct(wa.shape, BF16),
            jax.ShapeDtypeStruct(wb.shape, BF16),
            jax.ShapeDtypeStruct(wo.shape, BF16),
        ],
        scratch_shapes=[
            pltpu.VMEM((N_HEADS, HEAD_DIM, HEAD_DIM), F32),
            pltpu.VMEM((tile + SUBLANES, D_MODEL), F32),
            tile_f32, tile_f32, tile_f32, tile_f32, tile_f32,
            tile_bf16, tile_bf16, tile_bf16, tile_bf16, tile_bf16,
            pltpu.VMEM((HEAD_DIM, D_MODEL), F32),
            pltpu.VMEM((N_HEADS, HEAD_DIM, HEAD_DIM), F32),
            pltpu.VMEM((SUBLANES, D_MODEL), F32),
            pltpu.VMEM(w_in.shape, BF16),
            pltpu.VMEM(wa.shape, BF16), pltpu.VMEM(wb.shape, BF16), pltpu.VMEM(wo.shape, BF16),
            pltpu.VMEM((2,) + wa.shape, F32),
            pltpu.SemaphoreType.DMA((2, LOAD_BANDS)),
            pltpu.SemaphoreType.DMA((n_groups,)),
        ],
        compiler_params=pltpu.CompilerParams(
            dimension_semantics=("arbitrary", "arbitrary"),
            vmem_limit_bytes=V7X_VMEM_LIMIT_BYTES),
        name="prompt_sweep",
    )(x, meta, w_in, npre, npost, lb_logits, hn, cw, wa, wb, wo)


def _decode_kernel(x_ref, st_hbm, ctx_ref, w_in_hbm, npre_ref, npost_ref, lb_ref, hn_ref, cw_ref,
                   wa_hbm, wb_hbm, wo_hbm,
                   y_ref, st_out_hbm, ctx_out_ref,
                   ft_scr, q_scr, v_scr, o_scr, za_scr, pb_scr, ga_scr, gb_scr,
                   in_buf, out_buf, in_sem, out_sem, w_in_ref, wa_ref, wb_ref, wo_ref, w_sem):
    G = DECODE_GROUP
    i = pl.program_id(0)
    n_steps = pl.num_programs(0)
    n_rows = x_ref.shape[0]
    part_rows = G // STATE_PARTS

    def in_copies(step):
        slot = step % STATE_IN_SLOTS
        return [pltpu.make_async_copy(st_hbm.at[pl.ds(step * G + p * part_rows, part_rows)],
                                      in_buf.at[slot, pl.ds(p * part_rows, part_rows)],
                                      in_sem.at[slot, p]) for p in range(STATE_PARTS)]

    def out_copies(step):
        slot = step % STATE_OUT_SLOTS
        return [pltpu.make_async_copy(out_buf.at[slot, pl.ds(p * part_rows, part_rows)],
                                      st_out_hbm.at[pl.ds(step * G + p * part_rows, part_rows)],
                                      out_sem.at[slot, p]) for p in range(STATE_PARTS)]

    n_proj = w_in_hbm.shape[1] // D_MODEL
    w_parts = [(w_in_hbm.at[:, pl.ds(j * D_MODEL, D_MODEL)], w_in_ref.at[:, pl.ds(j * D_MODEL, D_MODEL)])
               for j in range(n_proj)] + [(wa_hbm, wa_ref), (wb_hbm, wb_ref), (wo_hbm, wo_ref)]
    W_A, W_B, W_O = n_proj, n_proj + 1, n_proj + 2

    def w_copy(k):
        return pltpu.make_async_copy(w_parts[k][0], w_parts[k][1], w_sem.at[k])

    @pl.when(i == 0)
    def _prime():
        for step in range(STATE_IN_SLOTS - 1):
            for copy in in_copies(step):
                copy.start()
        for k in (P_F, P_Q, P_I, P_ZA, P_C, P_H, P_B, P_ZB, P_GA, P_GB, W_A, W_B, W_O):
            w_copy(k).start()

    @pl.when(i == 0)
    def _project():
        x = x_ref[:, 0, :]
        xn = (x * _rms_scale(x) * npre_ref[...]).astype(BF16)

        def proj(j):
            w_copy(j).wait()
            return _dot(xn, w_in_ref[:, j * D_MODEL:(j + 1) * D_MODEL])

        lb = _lower_bound(lb_ref[...])
        f = lb + (1.0 - lb) * _sigmoid(proj(P_F))
        ft_scr[...] = f.T
        q_scr[...] = _silu(proj(P_Q))
        v_scr[...] = proj(P_I)
        za_scr[...] = _silu(proj(P_ZA))
        u = proj(P_C) * proj(P_H)
        cw = cw_ref[...]
        conv = cw[CONV_W - 1:CONV_W, :] * u
        for j in range(CONV_W - 1):
            ctx_j = ctx_ref[:, j, :]
            conv = conv + cw[j:j + 1, :] * ctx_j
            if j > 0:
                ctx_out_ref[:, j - 1, :] = ctx_j
        ctx_out_ref[:, CONV_W - 2, :] = u
        pb_scr[...] = proj(P_B) * conv * _silu(proj(P_ZB))
        ga_scr[...] = _sigmoid(proj(P_GA))
        gb_scr[...] = _sigmoid(proj(P_GB))

    @pl.when(i + STATE_IN_SLOTS - 1 < n_steps)
    def _read_ahead():
        for copy in in_copies(i + STATE_IN_SLOTS - 1):
            copy.start()

    @pl.when(i >= STATE_OUT_SLOTS)
    def _slot_written_back():
        for copy in out_copies(i - STATE_OUT_SLOTS):
            copy.wait()

    for copy in in_copies(i):
        copy.wait()
    st_ref = in_buf.at[i % STATE_IN_SLOTS]
    st_out_ref = out_buf.at[i % STATE_OUT_SLOTS]

    shift = (n_rows - i * G) % n_rows
    f_cols = pltpu.roll(ft_scr[...], shift, 1)
    r0 = pl.multiple_of(i * G, G)
    v_rows = v_scr[pl.ds(r0, G), :]
    q_rows = q_scr[pl.ds(r0, G), :].astype(BF16)
    row_id = lax.broadcasted_iota(jnp.int32, (G, HEAD_DIM), 0)
    o_heads = [jnp.zeros((G, HEAD_DIM), F32)] * N_HEADS
    for j in range(G):
        for h, sl in enumerate(HEADS):
            f_b = jnp.broadcast_to(f_cols[sl, j:j + 1], (HEAD_DIM, HEAD_DIM))
            s_new = f_b * st_ref[j, h] + (1.0 - f_b) * v_rows[j:j + 1, sl]
            st_out_ref[j, h] = s_new
            read = _dot(q_rows[:, sl], s_new.astype(BF16))
            o_heads[h] = jnp.where(row_id == j, read, o_heads[h])
    o_scr[pl.ds(r0, G), :] = jnp.concatenate(o_heads, axis=1)
    for copy in out_copies(i):
        copy.start()

    @pl.when(i == pl.num_programs(0) - 1)
    def _output():
        hn = hn_ref[...]
        for sl in HEADS:
            o_h = o_scr[:, sl]
            o_scr[:, sl] = o_h * _rms_scale(o_h) * hn
        for k in (W_A, W_B, W_O):
            w_copy(k).wait()
        y_a = _dot((o_scr[...] * za_scr[...]).astype(BF16), wa_ref[...])
        y_b = _dot(pb_scr[...].astype(BF16), wb_ref[...])
        merged = ga_scr[...] * y_a + gb_scr[...] * y_b
        out = _dot(merged.astype(BF16), wo_ref[...])
        y_ref[:, 0, :] = x_ref[:, 0, :] + out * _rms_scale(out) * npost_ref[...]
        for back in range(STATE_OUT_SLOTS):
            for copy in out_copies(i - back):
                copy.wait()


def _run_decode(x, state, ctx, weights):
    n = x.shape[0]
    assert n % DECODE_GROUP == 0 and n == 128 and DECODE_GROUP % STATE_PARTS == 0
    assert n // DECODE_GROUP >= max(STATE_IN_SLOTS, STATE_OUT_SLOTS)
    w_in, npre, npost, lb_logits, hn, cw, wa, wb, wo = weights

    def const(shape):
        zeros = (0,) * len(shape)
        return pl.BlockSpec(shape, lambda i: zeros, pipeline_mode=pl.Buffered(1))

    hbm = pl.BlockSpec(memory_space=pl.ANY)
    in_slots = pltpu.VMEM((STATE_IN_SLOTS, DECODE_GROUP) + state.shape[1:], F32)
    out_slots = pltpu.VMEM((STATE_OUT_SLOTS, DECODE_GROUP) + state.shape[1:], F32)
    rows_f32 = pltpu.VMEM((n, D_MODEL), F32)
    cols_f32 = pltpu.VMEM((D_MODEL, n), F32)
    return pl.pallas_call(
        _decode_kernel,
        grid=(n // DECODE_GROUP,),
        in_specs=[const(x.shape), hbm, const(ctx.shape), hbm, const(npre.shape),
                  const(npost.shape), const(lb_logits.shape), const(hn.shape), const(cw.shape),
                  hbm, hbm, hbm],
        out_specs=[const(x.shape), hbm, const(ctx.shape)],
        out_shape=[jax.ShapeDtypeStruct(x.shape, F32),
                   jax.ShapeDtypeStruct(state.shape, F32),
                   jax.ShapeDtypeStruct(ctx.shape, F32)],
        scratch_shapes=[cols_f32, rows_f32, rows_f32, rows_f32, rows_f32, rows_f32, rows_f32, rows_f32,
                        in_slots, out_slots,
                        pltpu.SemaphoreType.DMA((STATE_IN_SLOTS, STATE_PARTS)),
                        pltpu.SemaphoreType.DMA((STATE_OUT_SLOTS, STATE_PARTS)),
                        pltpu.VMEM(w_in.shape, BF16), pltpu.VMEM(wa.shape, BF16),
                        pltpu.VMEM(wb.shape, BF16), pltpu.VMEM(wo.shape, BF16),
                        pltpu.SemaphoreType.DMA((w_in.shape[1] // D_MODEL + 3,))],
        compiler_params=pltpu.CompilerParams(
            dimension_semantics=("arbitrary",),
            vmem_limit_bytes=V7X_VMEM_LIMIT_BYTES),
        name="decode_step",
    )(x, state, ctx, w_in, npre, npost, lb_logits, hn, cw, wa, wb, wo)


def kernel(x_prompt, x_sample, state_hgrn, state_conv, meta_tokens, w_in, norm_pre, norm_post, lb_logits,
           hgrn_norm, conv_w, w_a, w_b, w_o):
    depth = w_in.shape[0]
    assert depth == 1, "single-layer trunk"
    assert x_sample.shape[1] == 1, "one new token per decode row"

    weights = (w_in[0], norm_pre, norm_post, lb_logits, hgrn_norm, conv_w[0], w_a[0], w_b[0], w_o[0])

    y_prompt, hgrn_p, conv_p, w_in_bf, wa_bf, wb_bf, wo_bf = _run_prompt(
        x_prompt, meta_tokens.astype(x_prompt.dtype), weights, tile=PROMPT_TILE, chunk=PROMPT_CHUNK)
    weights = (w_in_bf,) + weights[1:6] + (wa_bf, wb_bf, wo_bf)

    y_s, hgrn_s, conv_s = _run_decode(x_sample, state_hgrn[0], state_conv[0], weights)

    return (y_prompt, y_s, hgrn_p[None], hgrn_s[None], conv_p[None], conv_s[None])
```

```python
import functools

import jax
import jax.numpy as jnp
from jax import lax
from jax.experimental import pallas as pl
from jax.experimental.pallas import tpu as pltpu

D_MODEL = 1024
N_HEADS = 8
HEAD_DIM = D_MODEL // N_HEADS
CONV_W = 3
EPS = 1e-6
P_Q, P_F, P_I, P_ZA, P_B, P_C, P_H, P_ZB, P_GA, P_GB = range(10)

PROMPT_TILE = 256
PROMPT_CHUNK = 128
DECODE_GROUP = 8
STATE_IN_SLOTS = 3
STATE_OUT_SLOTS = 2
STATE_PARTS = 8
LOAD_BANDS = 4
SUBLANES = 8
CTX_ROW0 = SUBLANES - (CONV_W - 1)
MAX_HALF_CHUNK_LOG_DECAY = 80.0
V7X_VMEM_LIMIT_BYTES = 58 * 1024 * 1024

BF16 = jnp.bfloat16
F32 = jnp.float32
HEADS = [slice(h * HEAD_DIM, (h + 1) * HEAD_DIM) for h in range(N_HEADS)]


def _dot(a, b):
    return jnp.dot(a, b, preferred_element_type=F32)


def _dot_nt(a, b):
    return lax.dot_general(a, b, (((1,), (1,)), ((), ())), preferred_element_type=F32)


def _dot_tn(a, b):
    return lax.dot_general(a, b, (((0,), (0,)), ((), ())), preferred_element_type=F32)


def _sigmoid(x):
    return 1.0 / (1.0 + jnp.exp(-x))


def _silu(x):
    return x * _sigmoid(x)


def _rms_scale(x):
    return lax.rsqrt(jnp.mean(x * x, axis=-1, keepdims=True) + EPS)


def _lower_bound(lb_logits):
    m = jnp.max(lb_logits, axis=0, keepdims=True)
    e = jnp.exp(lb_logits - m)
    return e[0:1, :] / jnp.sum(e, axis=0, keepdims=True)


def _block_cumsum(x, block):
    n = x.shape[0]
    ri = lax.broadcasted_iota(jnp.int32, (n, n), 0)
    ci = lax.broadcasted_iota(jnp.int32, (n, n), 1)
    tri = ri >= ci
    if n != block:
        tri = jnp.logical_and(tri, (ri // block) == (ci // block))
    tri = jnp.where(tri, 1.0, 0.0).astype(BF16)
    hi = x.astype(BF16)
    lo = (x - hi.astype(F32)).astype(BF16)
    return _dot(tri, hi) + _dot(tri, lo)


def _seq_kernel(x_ref, meta_ref, w_in_hbm, npre_ref, npost_ref, lb_ref, hn_ref, cw_ref,
                wa_hbm, wb_hbm, wo_hbm,
                y_ref, st_out_ref, c_out_ref, w_in_out, wa_out, wb_out, wo_out,
                st_scr, u_scr, q_scr, k_scr, v_scr, g_scr, o_scr,
                qt_scr, kt_scr, qin_scr, kst_scr, vb_scr, dec_scr, st_meta_scr, ctx_meta_scr,
                w_in_ref, wa_ref, wb_ref, wo_ref, stage_scr, load_sem, store_sem,
                *, tile, chunk):
    T, C = tile, chunk
    n_chunks = T // C
    t = pl.program_id(1)
    first_step = jnp.logical_and(pl.program_id(0) == 0, t == 0)
    last_step = jnp.logical_and(pl.program_id(0) == pl.num_programs(0) - 1, t == pl.num_programs(1) - 1)
    lb = _lower_bound(lb_ref[...])

    col = [pl.ds(j * D_MODEL, D_MODEL) for j in range(w_in_hbm.shape[1] // D_MODEL)]
    groups = [(w_in_hbm.at[:, c], w_in_ref.at[:, c], w_in_out.at[:, c]) for c in col]
    groups += [(wa_hbm, wa_ref, wa_out), (wb_hbm, wb_ref, wb_out), (wo_hbm, wo_ref, wo_out)]

    def load_copies(i):
        band = groups[i][0].shape[0] // LOAD_BANDS
        return [pltpu.make_async_copy(groups[i][0].at[pl.ds(b * band, band), :],
                                      stage_scr.at[i % 2, pl.ds(b * band, band), :],
                                      load_sem.at[i % 2, b]) for b in range(LOAD_BANDS)]

    def store_copy(i):
        return pltpu.make_async_copy(groups[i][1], groups[i][2], store_sem.at[i])

    @pl.when(first_step)
    def _stream_weights():
        for copy in load_copies(0):
            copy.start()
        for i in range(len(groups)):
            if i + 1 < len(groups):
                for copy in load_copies(i + 1):
                    copy.start()
            for copy in load_copies(i):
                copy.wait()
            groups[i][1][...] = stage_scr[i % 2].astype(BF16)
        for i in range(len(groups)):
            store_copy(i).start()

    def project(xn, j):
        return _dot(xn, w_in_ref[:, j * D_MODEL:(j + 1) * D_MODEL])

    @pl.when(first_step)
    def _meta_prefix():
        xm = meta_ref[...]
        n_meta = xm.shape[0]
        xn_m = (xm * _rms_scale(xm) * npre_ref[...]).astype(BF16)
        f_m = lb + (1.0 - lb) * _sigmoid(project(xn_m, P_F))
        g_m = _block_cumsum(jnp.log(f_m), n_meta)
        k_end = ((1.0 - f_m) * jnp.exp(g_m[n_meta - 1:n_meta, :] - g_m)).astype(BF16)
        v_m = project(xn_m, P_I).astype(BF16)
        for h, sl in enumerate(HEADS):
            st_meta_scr[h] = _dot_tn(k_end[:, sl], v_m[:, sl])
        u_m = project(xn_m, P_C) * project(xn_m, P_H)
        ctx_meta_scr[CTX_ROW0:SUBLANES, :] = u_m[n_meta - (CONV_W - 1):n_meta, :]
        dec_scr[...] = jnp.zeros(dec_scr.shape, F32)

    @pl.when(t == 0)
    def _init():
        st_scr[...] = st_meta_scr[...]
        u_scr[CTX_ROW0:SUBLANES, :] = ctx_meta_scr[CTX_ROW0:SUBLANES, :]

    x = x_ref[0]
    xn = (x * _rms_scale(x) * npre_ref[...]).astype(BF16)

    def proj(j):
        return project(xn, j)

    f = lb + (1.0 - lb) * _sigmoid(proj(P_F))
    k_scr[...] = 1.0 - f
    q_scr[...] = _silu(proj(P_Q))
    vb_scr[...] = proj(P_I).astype(BF16)
    g_scr[...] = _block_cumsum(jnp.log(f), C)

    g_floor = None
    for c in range(n_chunks):
        rows = slice(c * C, (c + 1) * C)
        gc = g_scr[rows, :]
        g_last = gc[C - 1:C, :]
        g_mid = gc[C // 2 - 1:C // 2, :]
        qc = q_scr[rows, :]
        kc = k_scr[rows, :]
        qt_scr[rows, :] = (qc * jnp.exp(gc - g_mid)).astype(BF16)
        kt_scr[rows, :] = (kc * jnp.exp(g_mid - gc)).astype(BF16)
        qin_scr[rows, :] = (qc * jnp.exp(gc)).astype(BF16)
        kst_scr[rows, :] = (kc * jnp.exp(g_last - gc)).astype(BF16)
        dec_scr[c:c + 1, :] = jnp.exp(g_last)
        half_floor = jnp.minimum(g_mid, g_last - g_mid)
        g_floor = half_floor if g_floor is None else jnp.minimum(g_floor, half_floor)
    stable = jnp.min(g_floor) >= -MAX_HALF_CHUNK_LOG_DECAY

    dec_cols = dec_scr[...].T
    keep = jnp.logical_and(
        lax.broadcasted_iota(jnp.int32, (C, C), 0) >= lax.broadcasted_iota(jnp.int32, (C, C), 1), stable)
    for c in range(n_chunks):
        rows = slice(c * C, (c + 1) * C)
        scores = [jnp.where(keep, _dot_nt(qt_scr[rows, sl], kt_scr[rows, sl]), 0.0).astype(BF16)
                  for sl in HEADS]
        for h, sl in enumerate(HEADS):
            vb = vb_scr[rows, sl]
            st = st_scr[h]
            o_scr[rows, sl] = _dot(jnp.concatenate([qin_scr[rows, sl], scores[h]], axis=1),
                                   jnp.concatenate([st.astype(BF16), vb], axis=0))
            decay = jnp.broadcast_to(dec_cols[sl, c:c + 1], (HEAD_DIM, HEAD_DIM))
            st_scr[h] = decay * st + _dot_tn(kst_scr[rows, sl], vb)

    @pl.when(jnp.logical_not(stable))
    def _intra_exact():
        same_head = (lax.broadcasted_iota(jnp.int32, (D_MODEL, D_MODEL), 0) // HEAD_DIM ==
                     lax.broadcasted_iota(jnp.int32, (D_MODEL, D_MODEL), 1) // HEAD_DIM)
        head_sum = jnp.where(same_head, 1.0, 0.0).astype(BF16)
        tpos = lax.broadcasted_iota(jnp.int32, (C, 1), 0)
        v_scr[...] = proj(P_I)

        def chunk_step(c, carry):
            r0 = pl.multiple_of(c * C, C)
            rows = pl.ds(r0, C)
            gc = g_scr[rows, :]
            qc = q_scr[rows, :]

            def src_step(i, carry2):
                src = pl.ds(r0 + i, 1)
                p = qc * jnp.exp(jnp.minimum(gc - g_scr[src, :], 0.0)) * k_scr[src, :]
                p = jnp.where(tpos >= i, p, 0.0).astype(BF16)
                o_scr[rows, :] += _dot(p, head_sum) * v_scr[src, :]
                return carry2

            return lax.fori_loop(0, C, src_step, carry)

        lax.fori_loop(0, n_chunks, chunk_step, 0)

    hn = hn_ref[...]
    silu_za = _silu(proj(P_ZA))
    gated = []
    for sl in HEADS:
        o_h = o_scr[:, sl]
        gated.append((o_h * _rms_scale(o_h) * hn * silu_za[:, sl]).astype(BF16))
    y_a = _dot(jnp.concatenate(gated, axis=1), wa_ref[...])

    u = proj(P_C) * proj(P_H)
    u_scr[SUBLANES:SUBLANES + T, :] = u
    cw = cw_ref[...]
    conv = cw[CONV_W - 1:CONV_W, :] * u
    for j in range(CONV_W - 1):
        conv = conv + cw[j:j + 1, :] * u_scr[CTX_ROW0 + j:CTX_ROW0 + j + T, :]
    y_b = _dot((proj(P_B) * conv * _silu(proj(P_ZB))).astype(BF16), wb_ref[...])
    new_ctx = u[T - (CONV_W - 1):T, :]
    u_scr[CTX_ROW0:SUBLANES, :] = new_ctx

    merged = _sigmoid(proj(P_GA)) * y_a + _sigmoid(proj(P_GB)) * y_b
    out = _dot(merged.astype(BF16), wo_ref[...])
    y_ref[0] = x_ref[0] + out * _rms_scale(out) * npost_ref[...]

    @pl.when(t == pl.num_programs(1) - 1)
    def _finish():
        c_out_ref[0] = new_ctx
        st_out_ref[0] = st_scr[...]

    @pl.when(last_step)
    def _weights_written():
        for i in range(len(groups)):
            store_copy(i).wait()


def _const_spec(shape):
    zeros = (0,) * len(shape)
    return pl.BlockSpec(shape, lambda b, t: zeros, pipeline_mode=pl.Buffered(1))


def _run_prompt(x, meta, weights, *, tile, chunk):
    n, length, _ = x.shape
    assert length % tile == 0 and tile % chunk == 0 and chunk % (2 * SUBLANES) == 0 and tile >= CONV_W - 1
    assert meta.shape[0] % (2 * SUBLANES) == 0 and meta.shape[0] >= CONV_W - 1
    assert tile // chunk <= HEAD_DIM
    w_in, npre, npost, lb_logits, hn, cw, wa, wb, wo = weights
    kern = functools.partial(_seq_kernel, tile=tile, chunk=chunk)
    hbm = pl.BlockSpec(memory_space=pl.ANY)
    n_groups = w_in.shape[1] // D_MODEL + 3
    assert w_in.shape[1] % D_MODEL == 0 and wa.shape == wb.shape == wo.shape == (w_in.shape[0], D_MODEL)
    state_shape = (1, N_HEADS, HEAD_DIM, HEAD_DIM)
    ctx_shape = (1, CONV_W - 1, D_MODEL)
    tile_f32 = pltpu.VMEM((tile, D_MODEL), F32)
    tile_bf16 = pltpu.VMEM((tile, D_MODEL), BF16)
    return pl.pallas_call(
        kern,
        grid=(n, length // tile),
        in_specs=[
            pl.BlockSpec((1, tile, D_MODEL), lambda b, t: (b, t, 0)),
            _const_spec(meta.shape),
            hbm,
            _const_spec(npre.shape),
            _const_spec(npost.shape),
            _const_spec(lb_logits.shape),
            _const_spec(hn.shape),
            _const_spec(cw.shape),
            hbm, hbm, hbm,
        ],
        out_specs=[
            pl.BlockSpec((1, tile, D_MODEL), lambda b, t: (b, t, 0)),
            pl.BlockSpec(state_shape, lambda b, t: (b, 0, 0, 0)),
            pl.BlockSpec(ctx_shape, lambda b, t: (b, 0, 0)),
            hbm, hbm, hbm, hbm,
        ],
        out_shape=[
            jax.ShapeDtypeStruct(x.shape, F32),
            jax.ShapeDtypeStruct((n,) + state_shape[1:], F32),
            jax.ShapeDtypeStruct((n,) + ctx_shape[1:], F32),
            jax.ShapeDtypeStruct(w_in.shape, BF16),
            jax.ShapeDtypeStruct(wa.shape, BF16),
            jax.ShapeDtypeStruct(wb.shape, BF16),
            jax.ShapeDtypeStruct(wo.shape, BF16),
        ],
        scratch_shapes=[
            pltpu.VMEM((N_HEADS, HEAD_DIM, HEAD_DIM), F32),
            pltpu.VMEM((tile + SUBLANES, D_MODEL), F32),
            tile_f32, tile_f32, tile_f32, tile_f32, tile_f32,
            tile_bf16, tile_bf16, tile_bf16, tile_bf16, tile_bf16,
            pltpu.VMEM((HEAD_DIM, D_MODEL), F32),
            pltpu.VMEM((N_HEADS, HEAD_DIM, HEAD_DIM), F32),
            pltpu.VMEM((SUBLANES, D_MODEL), F32),
            pltpu.VMEM(w_in.shape, BF16),
            pltpu.VMEM(wa.shape, BF16), pltpu.VMEM(wb.shape, BF16), pltpu.VMEM(wo.shape, BF16),
            pltpu.VMEM((2,) + wa.shape, F32),
            pltpu.SemaphoreType.DMA((2, LOAD_BANDS)),
            pltpu.SemaphoreType.DMA((n_groups,)),
        ],
        compiler_params=pltpu.CompilerParams(
            dimension_semantics=("arbitrary", "arbitrary"),
            vmem_limit_bytes=V7X_VMEM_LIMIT_BYTES),
        name="prompt_sweep",
    )(x, meta, w_in, npre, npost, lb_logits, hn, cw, wa, wb, wo)


def _decode_kernel(x_ref, st_hbm, ctx_ref, w_in_hbm, npre_ref, npost_ref, lb_ref, hn_ref, cw_ref,
                   wa_hbm, wb_hbm, wo_hbm,
                   y_ref, st_out_hbm, ctx_out_ref,
                   ft_scr, q_scr, v_scr, o_scr, za_scr, pb_scr, ga_scr, gb_scr,
                   in_buf, out_buf, in_sem, out_sem, w_in_ref, wa_ref, wb_ref, wo_ref, w_sem):
    G = DECODE_GROUP
    i = pl.program_id(0)
    n_steps = pl.num_programs(0)
    n_rows = x_ref.shape[0]
    part_rows = G // STATE_PARTS

    def in_copies(step):
        slot = step % STATE_IN_SLOTS
        return [pltpu.make_async_copy(st_hbm.at[pl.ds(step * G + p * part_rows, part_rows)],
                                      in_buf.at[slot, pl.ds(p * part_rows, part_rows)],
                                      in_sem.at[slot, p]) for p in range(STATE_PARTS)]

    def out_copies(step):
        slot = step % STATE_OUT_SLOTS
        return [pltpu.make_async_copy(out_buf.at[slot, pl.ds(p * part_rows, part_rows)],
                                      st_out_hbm.at[pl.ds(step * G + p * part_rows, part_rows)],
                                      out_sem.at[slot, p]) for p in range(STATE_PARTS)]

    n_proj = w_in_hbm.shape[1] // D_MODEL
    w_parts = [(w_in_hbm.at[:, pl.ds(j * D_MODEL, D_MODEL)], w_in_ref.at[:, pl.ds(j * D_MODEL, D_MODEL)])
               for j in range(n_proj)] + [(wa_hbm, wa_ref), (wb_hbm, wb_ref), (wo_hbm, wo_ref)]
    W_A, W_B, W_O = n_proj, n_proj + 1, n_proj + 2

    def w_copy(k):
        return pltpu.make_async_copy(w_parts[k][0], w_parts[k][1], w_sem.at[k])

    @pl.when(i == 0)
    def _prime():
        for step in range(STATE_IN_SLOTS - 1):
            for copy in in_copies(step):
                copy.start()
        for k in (P_F, P_Q, P_I, P_ZA, P_C, P_H, P_B, P_ZB, P_GA, P_GB, W_A, W_B, W_O):
            w_copy(k).start()

    @pl.when(i == 0)
    def _project():
        x = x_ref[:, 0, :]
        xn = (x * _rms_scale(x) * npre_ref[...]).astype(BF16)

        def proj(j):
            w_copy(j).wait()
            return _dot(xn, w_in_ref[:, j * D_MODEL:(j + 1) * D_MODEL])

        lb = _lower_bound(lb_ref[...])
        f = lb + (1.0 - lb) * _sigmoid(proj(P_F))
        ft_scr[...] = f.T
        q_scr[...] = _silu(proj(P_Q))
        v_scr[...] = proj(P_I)
        za_scr[...] = _silu(proj(P_ZA))
        u = proj(P_C) * proj(P_H)
        cw = cw_ref[...]
        conv = cw[CONV_W - 1:CONV_W, :] * u
        for j in range(CONV_W - 1):
            ctx_j = ctx_ref[:, j, :]
            conv = conv + cw[j:j + 1, :] * ctx_j
            if j > 0:
                ctx_out_ref[:, j - 1, :] = ctx_j
        ctx_out_ref[:, CONV_W - 2, :] = u
        pb_scr[...] = proj(P_B) * conv * _silu(proj(P_ZB))
        ga_scr[...] = _sigmoid(proj(P_GA))
        gb_scr[...] = _sigmoid(proj(P_GB))

    @pl.when(i + STATE_IN_SLOTS - 1 < n_steps)
    def _read_ahead():
        for copy in in_copies(i + STATE_IN_SLOTS - 1):
            copy.start()

    @pl.when(i >= STATE_OUT_SLOTS)
    def _slot_written_back():
        for copy in out_copies(i - STATE_OUT_SLOTS):
            copy.wait()

    for copy in in_copies(i):
        copy.wait()
    st_ref = in_buf.at[i % STATE_IN_SLOTS]
    st_out_ref = out_buf.at[i % STATE_OUT_SLOTS]

    shift = (n_rows - i * G) % n_rows
    f_cols = pltpu.roll(ft_scr[...], shift, 1)
    r0 = pl.multiple_of(i * G, G)
    v_rows = v_scr[pl.ds(r0, G), :]
    q_rows = q_scr[pl.ds(r0, G), :].astype(BF16)
    row_id = lax.broadcasted_iota(jnp.int32, (G, HEAD_DIM), 0)
    o_heads = [jnp.zeros((G, HEAD_DIM), F32)] * N_HEADS
    for j in range(G):
        for h, sl in enumerate(HEADS):
            f_b = jnp.broadcast_to(f_cols[sl, j:j + 1], (HEAD_DIM, HEAD_DIM))
            s_new = f_b * st_ref[j, h] + (1.0 - f_b) * v_rows[j:j + 1, sl]
            st_out_ref[j, h] = s_new
            read = _dot(q_rows[:, sl], s_new.astype(BF16))
            o_heads[h] = jnp.where(row_id == j, read, o_heads[h])
    o_scr[pl.ds(r0, G), :] = jnp.concatenate(o_heads, axis=1)
    for copy in out_copies(i):
        copy.start()

    @pl.when(i == pl.num_programs(0) - 1)
    def _output():
        hn = hn_ref[...]
        for sl in HEADS:
            o_h = o_scr[:, sl]
            o_scr[:, sl] = o_h * _rms_scale(o_h) * hn
        for k in (W_A, W_B, W_O):
            w_copy(k).wait()
        y_a = _dot((o_scr[...] * za_scr[...]).astype(BF16), wa_ref[...])
        y_b = _dot(pb_scr[...].astype(BF16), wb_ref[...])
        merged = ga_scr[...] * y_a + gb_scr[...] * y_b
        out = _dot(merged.astype(BF16), wo_ref[...])
        y_ref[:, 0, :] = x_ref[:, 0, :] + out * _rms_scale(out) * npost_ref[...]
        for back in range(STATE_OUT_SLOTS):
            for copy in out_copies(i - back):
                copy.wait()


def _run_decode(x, state, ctx, weights):
    n = x.shape[0]
    assert n % DECODE_GROUP == 0 and n == 128 and DECODE_GROUP % STATE_PARTS == 0
    assert n // DECODE_GROUP >= max(STATE_IN_SLOTS, STATE_OUT_SLOTS)
    w_in, npre, npost, lb_logits, hn, cw, wa, wb, wo = weights

    def const(shape):
        zeros = (0,) * len(shape)
        return pl.BlockSpec(shape, lambda i: zeros, pipeline_mode=pl.Buffered(1))

    hbm = pl.BlockSpec(memory_space=pl.ANY)
    in_slots = pltpu.VMEM((STATE_IN_SLOTS, DECODE_GROUP) + state.shape[1:], F32)
    out_slots = pltpu.VMEM((STATE_OUT_SLOTS, DECODE_GROUP) + state.shape[1:], F32)
    rows_f32 = pltpu.VMEM((n, D_MODEL), F32)
    cols_f32 = pltpu.VMEM((D_MODEL, n), F32)
    return pl.pallas_call(
        _decode_kernel,
        grid=(n // DECODE_GROUP,),
        in_specs=[const(x.shape), hbm, const(ctx.shape), hbm, const(npre.shape),
                  const(npost.shape), const(lb_logits.shape), const(hn.shape), const(cw.shape),
                  hbm, hbm, hbm],
        out_specs=[const(x.shape), hbm, const(ctx.shape)],
        out_shape=[jax.ShapeDtypeStruct(x.shape, F32),
                   jax.ShapeDtypeStruct(state.shape, F32),
                   jax.ShapeDtypeStruct(ctx.shape, F32)],
        scratch_shapes=[cols_f32, rows_f32, rows_f32, rows_f32, rows_f32, rows_f32, rows_f32, rows_f32,
                        in_slots, out_slots,
                        pltpu.SemaphoreType.DMA((STATE_IN_SLOTS, STATE_PARTS)),
                        pltpu.SemaphoreType.DMA((STATE_OUT_SLOTS, STATE_PARTS)),
                        pltpu.VMEM(w_in.shape, BF16), pltpu.VMEM(wa.shape, BF16),
                        pltpu.VMEM(wb.shape, BF16), pltpu.VMEM(wo.shape, BF16),
                        pltpu.SemaphoreType.DMA((w_in.shape[1] // D_MODEL + 3,))],
        compiler_params=pltpu.CompilerParams(
            dimension_semantics=("arbitrary",),
            vmem_limit_bytes=V7X_VMEM_LIMIT_BYTES),
        name="decode_step",
    )(x, state, ctx, w_in, npre, npost, lb_logits, hn, cw, wa, wb, wo)


def kernel(x_prompt, x_sample, state_hgrn, state_conv, meta_tokens, w_in, norm_pre, norm_post, lb_logits,
           hgrn_norm, conv_w, w_a, w_b, w_o):
    depth = w_in.shape[0]
    assert depth == 1, "single-layer trunk"
    assert x_sample.shape[1] == 1, "one new token per decode row"

    weights = (w_in[0], norm_pre, norm_post, lb_logits, hgrn_norm, conv_w[0], w_a[0], w_b[0], w_o[0])

    y_prompt, hgrn_p, conv_p, w_in_bf, wa_bf, wb_bf, wo_bf = _run_prompt(
        x_prompt, meta_tokens.astype(x_prompt.dtype), weights, tile=PROMPT_TILE, chunk=PROMPT_CHUNK)
    weights = (w_in_bf,) + weights[1:6] + (wa_bf, wb_bf, wo_bf)

    y_s, hgrn_s, conv_s = _run_decode(x_sample, state_hgrn[0], state_conv[0], weights)

    return (y_prompt, y_s, hgrn_p[None], hgrn_s[None], conv_p[None], conv_s[None])
```

```python
import functools

import jax
import jax.numpy as jnp
from jax import lax
from jax.experimental import pallas as pl
from jax.experimental.pallas import tpu as pltpu

D_MODEL = 1024
N_HEADS = 8
HEAD_DIM = D_MODEL // N_HEADS
CONV_W = 3
EPS = 1e-6
P_Q, P_F, P_I, P_ZA, P_B, P_C, P_H, P_ZB, P_GA, P_GB = range(10)

PROMPT_TILE = 256
PROMPT_CHUNK = 128
DECODE_GROUP = 8
STATE_IN_SLOTS = 3
STATE_OUT_SLOTS = 2
STATE_PARTS = 8
LOAD_BANDS = 4
SUBLANES = 8
CTX_ROW0 = SUBLANES - (CONV_W - 1)
MAX_HALF_CHUNK_LOG_DECAY = 80.0
V7X_VMEM_LIMIT_BYTES = 58 * 1024 * 1024

BF16 = jnp.bfloat16
F32 = jnp.float32
HEADS = [slice(h * HEAD_DIM, (h + 1) * HEAD_DIM) for h in range(N_HEADS)]


def _dot(a, b):
    return jnp.dot(a, b, preferred_element_type=F32)


def _dot_nt(a, b):
    return lax.dot_general(a, b, (((1,), (1,)), ((), ())), preferred_element_type=F32)


def _dot_tn(a, b):
    return lax.dot_general(a, b, (((0,), (0,)), ((), ())), preferred_element_type=F32)


def _sigmoid(x):
    return 1.0 / (1.0 + jnp.exp(-x))


def _silu(x):
    return x * _sigmoid(x)


def _rms_scale(x):
    return lax.rsqrt(jnp.mean(x * x, axis=-1, keepdims=True) + EPS)


def _lower_bound(lb_logits):
    m = jnp.max(lb_logits, axis=0, keepdims=True)
    e = jnp.exp(lb_logits - m)
    return e[0:1, :] / jnp.sum(e, axis=0, keepdims=True)


def _block_cumsum(x, block):
    n = x.shape[0]
    ri = lax.broadcasted_iota(jnp.int32, (n, n), 0)
    ci = lax.broadcasted_iota(jnp.int32, (n, n), 1)
    tri = ri >= ci
    if n != block:
        tri = jnp.logical_and(tri, (ri // block) == (ci // block))
    tri = jnp.where(tri, 1.0, 0.0).astype(BF16)
    hi = x.astype(BF16)
    lo = (x - hi.astype(F32)).astype(BF16)
    return _dot(tri, hi) + _dot(tri, lo)


def _seq_kernel(x_ref, meta_ref, w_in_hbm, npre_ref, npost_ref, lb_ref, hn_ref, cw_ref,
                wa_hbm, wb_hbm, wo_hbm,
                y_ref, st_out_ref, c_out_ref, w_in_out, wa_out, wb_out, wo_out,
                st_scr, u_scr, q_scr, k_scr, v_scr, g_scr, o_scr,
                qt_scr, kt_scr, qin_scr, kst_scr, vb_scr, dec_scr, st_meta_scr, ctx_meta_scr,
                w_in_ref, wa_ref, wb_ref, wo_ref, stage_scr, load_sem, store_sem,
                *, tile, chunk):
    T, C = tile, chunk
    n_chunks = T // C
    t = pl.program_id(1)
    first_step = jnp.logical_and(pl.program_id(0) == 0, t == 0)
    last_step = jnp.logical_and(pl.program_id(0) == pl.num_programs(0) - 1, t == pl.num_programs(1) - 1)
    lb = _lower_bound(lb_ref[...])

    col = [pl.ds(j * D_MODEL, D_MODEL) for j in range(w_in_hbm.shape[1] // D_MODEL)]
    groups = [(w_in_hbm.at[:, c], w_in_ref.at[:, c], w_in_out.at[:, c]) for c in col]
    groups += [(wa_hbm, wa_ref, wa_out), (wb_hbm, wb_ref, wb_out), (wo_hbm, wo_ref, wo_out)]

    def load_copies(p):
        src = groups[fetch_order[p]][0]
        band = src.shape[0] // LOAD_BANDS
        return [pltpu.make_async_copy(src.at[pl.ds(b * band, band), :],
                                      stage_scr.at[p % 2, pl.ds(b * band, band), :],
                                      load_sem.at[p % 2, b]) for b in range(LOAD_BANDS)]

    def store_copy(i):
        return pltpu.make_async_copy(groups[i][1], groups[i][2], store_sem.at[i])

    W_A, W_B, W_O = len(col), len(col) + 1, len(col) + 2
    fetch_order = [P_F, P_I, P_C, P_H, P_Q, P_ZA, W_A, P_B, P_ZB, W_B, P_GA, P_GB, W_O]
    n_eager = 4
    assert sorted(fetch_order) == list(range(len(groups)))

    def fetch(k):
        p = fetch_order.index(k)
        for copy in load_copies(p):
            copy.wait()
        groups[k][1][...] = stage_scr[p % 2].astype(BF16)
        if p + 2 < len(fetch_order):
            for copy in load_copies(p + 2):
                copy.start()

    @pl.when(first_step)
    def _stream_first_weights():
        for p in range(2):
            for copy in load_copies(p):
                copy.start()
        for k in fetch_order[:n_eager]:
            fetch(k)

    def project(xn, j):
        return _dot(xn, w_in_ref[:, j * D_MODEL:(j + 1) * D_MODEL])

    @pl.when(first_step)
    def _meta_prefix():
        xm = meta_ref[...]
        n_meta = xm.shape[0]
        xn_m = (xm * _rms_scale(xm) * npre_ref[...]).astype(BF16)
        f_m = lb + (1.0 - lb) * _sigmoid(project(xn_m, P_F))
        g_m = _block_cumsum(jnp.log(f_m), n_meta)
        k_end = ((1.0 - f_m) * jnp.exp(g_m[n_meta - 1:n_meta, :] - g_m)).astype(BF16)
        v_m = project(xn_m, P_I).astype(BF16)
        for h, sl in enumerate(HEADS):
            st_meta_scr[h] = _dot_tn(k_end[:, sl], v_m[:, sl])
        u_m = project(xn_m, P_C) * project(xn_m, P_H)
        ctx_meta_scr[CTX_ROW0:SUBLANES, :] = u_m[n_meta - (CONV_W - 1):n_meta, :]
        dec_scr[...] = jnp.zeros(dec_scr.shape, F32)

    @pl.when(t == 0)
    def _init():
        st_scr[...] = st_meta_scr[...]
        u_scr[CTX_ROW0:SUBLANES, :] = ctx_meta_scr[CTX_ROW0:SUBLANES, :]

    def run_tile(lazy_weights):
        fetched = set()
        x = x_ref[0]
        xn = (x * _rms_scale(x) * npre_ref[...]).astype(BF16)

        def proj(j):
            if lazy_weights and j in fetch_order[n_eager:] and j not in fetched:
                fetched.add(j)
                fetch(j)
            return project(xn, j)

        def late_weight(k, ref):
            if lazy_weights:
                fetch(k)
            return ref[...]

        f = lb + (1.0 - lb) * _sigmoid(proj(P_F))
        k_scr[...] = 1.0 - f
        q_scr[...] = _silu(proj(P_Q))
        vb_scr[...] = proj(P_I).astype(BF16)
        g_scr[...] = _block_cumsum(jnp.log(f), C)

        g_floor = None
        for c in range(n_chunks):
            rows = slice(c * C, (c + 1) * C)
            gc = g_scr[rows, :]
            g_last = gc[C - 1:C, :]
            g_mid = gc[C // 2 - 1:C // 2, :]
            qc = q_scr[rows, :]
            kc = k_scr[rows, :]
            qt_scr[rows, :] = (qc * jnp.exp(gc - g_mid)).astype(BF16)
            kt_scr[rows, :] = (kc * jnp.exp(g_mid - gc)).astype(BF16)
            qin_scr[rows, :] = (qc * jnp.exp(gc)).astype(BF16)
            kst_scr[rows, :] = (kc * jnp.exp(g_last - gc)).astype(BF16)
            dec_scr[c:c + 1, :] = jnp.exp(g_last)
            half_floor = jnp.minimum(g_mid, g_last - g_mid)
            g_floor = half_floor if g_floor is None else jnp.minimum(g_floor, half_floor)
        stable = jnp.min(g_floor) >= -MAX_HALF_CHUNK_LOG_DECAY

        dec_cols = dec_scr[...].T
        keep = jnp.logical_and(
            lax.broadcasted_iota(jnp.int32, (C, C), 0) >= lax.broadcasted_iota(jnp.int32, (C, C), 1), stable)
        for c in range(n_chunks):
            rows = slice(c * C, (c + 1) * C)
            scores = [jnp.where(keep, _dot_nt(qt_scr[rows, sl], kt_scr[rows, sl]), 0.0).astype(BF16)
                      for sl in HEADS]
            for h, sl in enumerate(HEADS):
                vb = vb_scr[rows, sl]
                st = st_scr[h]
                o_scr[rows, sl] = _dot(jnp.concatenate([qin_scr[rows, sl], scores[h]], axis=1),
                                       jnp.concatenate([st.astype(BF16), vb], axis=0))
                decay = jnp.broadcast_to(dec_cols[sl, c:c + 1], (HEAD_DIM, HEAD_DIM))
                st_scr[h] = decay * st + _dot_tn(kst_scr[rows, sl], vb)

        @pl.when(jnp.logical_not(stable))
        def _intra_exact():
            same_head = (lax.broadcasted_iota(jnp.int32, (D_MODEL, D_MODEL), 0) // HEAD_DIM ==
                         lax.broadcasted_iota(jnp.int32, (D_MODEL, D_MODEL), 1) // HEAD_DIM)
            head_sum = jnp.where(same_head, 1.0, 0.0).astype(BF16)
            tpos = lax.broadcasted_iota(jnp.int32, (C, 1), 0)
            v_scr[...] = proj(P_I)

            def chunk_step(c, carry):
                r0 = pl.multiple_of(c * C, C)
                rows = pl.ds(r0, C)
                gc = g_scr[rows, :]
                qc = q_scr[rows, :]

                def src_step(i, carry2):
                    src = pl.ds(r0 + i, 1)
                    p = qc * jnp.exp(jnp.minimum(gc - g_scr[src, :], 0.0)) * k_scr[src, :]
                    p = jnp.where(tpos >= i, p, 0.0).astype(BF16)
                    o_scr[rows, :] += _dot(p, head_sum) * v_scr[src, :]
                    return carry2

                return lax.fori_loop(0, C, src_step, carry)

            lax.fori_loop(0, n_chunks, chunk_step, 0)

        hn = hn_ref[...]
        silu_za = _silu(proj(P_ZA))
        gated = []
        for sl in HEADS:
            o_h = o_scr[:, sl]
            gated.append((o_h * _rms_scale(o_h) * hn * silu_za[:, sl]).astype(BF16))
        y_a = _dot(jnp.concatenate(gated, axis=1), late_weight(W_A, wa_ref))

        u = proj(P_C) * proj(P_H)
        u_scr[SUBLANES:SUBLANES + T, :] = u
        cw = cw_ref[...]
        conv = cw[CONV_W - 1:CONV_W, :] * u
        for j in range(CONV_W - 1):
            conv = conv + cw[j:j + 1, :] * u_scr[CTX_ROW0 + j:CTX_ROW0 + j + T, :]
        y_b = _dot((proj(P_B) * conv * _silu(proj(P_ZB))).astype(BF16), late_weight(W_B, wb_ref))
        u_scr[CTX_ROW0:SUBLANES, :] = u[T - (CONV_W - 1):T, :]

        merged = _sigmoid(proj(P_GA)) * y_a + _sigmoid(proj(P_GB)) * y_b
        out = _dot(merged.astype(BF16), late_weight(W_O, wo_ref))
        y_ref[0] = x_ref[0] + out * _rms_scale(out) * npost_ref[...]

    @pl.when(first_step)
    def _first_tile():
        run_tile(lazy_weights=True)
        for k in range(len(groups)):
            store_copy(k).start()

    @pl.when(jnp.logical_not(first_step))
    def _tile():
        run_tile(lazy_weights=False)

    @pl.when(t == pl.num_programs(1) - 1)
    def _finish():
        c_out_ref[0] = u_scr[CTX_ROW0:SUBLANES, :]
        st_out_ref[0] = st_scr[...]

    @pl.when(last_step)
    def _weights_written():
        for i in range(len(groups)):
            store_copy(i).wait()


def _const_spec(shape):
    zeros = (0,) * len(shape)
    return pl.BlockSpec(shape, lambda b, t: zeros, pipeline_mode=pl.Buffered(1))


def _run_prompt(x, meta, weights, *, tile, chunk):
    n, length, _ = x.shape
    assert length % tile == 0 and tile % chunk == 0 and chunk % (2 * SUBLANES) == 0 and tile >= CONV_W - 1
    assert meta.shape[0] % (2 * SUBLANES) == 0 and meta.shape[0] >= CONV_W - 1
    assert tile // chunk <= HEAD_DIM
    w_in, npre, npost, lb_logits, hn, cw, wa, wb, wo = weights
    kern = functools.partial(_seq_kernel, tile=tile, chunk=chunk)
    hbm = pl.BlockSpec(memory_space=pl.ANY)
    n_groups = w_in.shape[1] // D_MODEL + 3
    assert w_in.shape[1] % D_MODEL == 0 and wa.shape == wb.shape == wo.shape == (w_in.shape[0], D_MODEL)
    state_shape = (1, N_HEADS, HEAD_DIM, HEAD_DIM)
    ctx_shape = (1, CONV_W - 1, D_MODEL)
    tile_f32 = pltpu.VMEM((tile, D_MODEL), F32)
    tile_bf16 = pltpu.VMEM((tile, D_MODEL), BF16)
    return pl.pallas_call(
        kern,
        grid=(n, length // tile),
        in_specs=[
            pl.BlockSpec((1, tile, D_MODEL), lambda b, t: (b, t, 0)),
            _const_spec(meta.shape),
            hbm,
            _const_spec(npre.shape),
            _const_spec(npost.shape),
            _const_spec(lb_logits.shape),
            _const_spec(hn.shape),
            _const_spec(cw.shape),
            hbm, hbm, hbm,
        ],
        out_specs=[
            pl.BlockSpec((1, tile, D_MODEL), lambda b, t: (b, t, 0)),
            pl.BlockSpec(state_shape, lambda b, t: (b, 0, 0, 0)),
            pl.BlockSpec(ctx_shape, lambda b, t: (b, 0, 0)),
            hbm, hbm, hbm, hbm,
        ],
        out_shape=[
            jax.ShapeDtypeStruct(x.shape, F32),
            jax.ShapeDtypeStruct((n,) + state_shape[1:], F32),
            jax.ShapeDtypeStruct((n,) + ctx_shape[1:], F32),
            jax.ShapeDtypeStruct(w_in.shape, BF16),
            jax.ShapeDtypeStruct(wa.shape, BF16),
            jax.ShapeDtypeStruct(wb.shape, BF16),
            jax.ShapeDtypeStruct(wo.shape, BF16),
        ],
        scratch_shapes=[
            pltpu.VMEM((N_HEADS, HEAD_DIM, HEAD_DIM), F32),
            pltpu.VMEM((tile + SUBLANES, D_MODEL), F32),
            tile_f32, tile_f32, tile_f32, tile_f32, tile_f32,
            tile_bf16, tile_bf16, tile_bf16, tile_bf16, tile_bf16,
            pltpu.VMEM((HEAD_DIM, D_MODEL), F32),
            pltpu.VMEM((N_HEADS, HEAD_DIM, HEAD_DIM), F32),
            pltpu.VMEM((SUBLANES, D_MODEL), F32),
            pltpu.VMEM(w_in.shape, BF16),
            pltpu.VMEM(wa.shape, BF16), pltpu.VMEM(wb.shape, BF16), pltpu.VMEM(wo.shape, BF16),
            pltpu.VMEM((2,) + wa.shape, F32),
            pltpu.SemaphoreType.DMA((2, LOAD_BANDS)),
            pltpu.SemaphoreType.DMA((n_groups,)),
        ],
        compiler_params=pltpu.CompilerParams(
            dimension_semantics=("arbitrary", "arbitrary"),
            vmem_limit_bytes=V7X_VMEM_LIMIT_BYTES),
        name="prompt_sweep",
    )(x, meta, w_in, npre, npost, lb_logits, hn, cw, wa, wb, wo)


def _decode_kernel(x_ref, st_hbm, ctx_ref, w_in_hbm, npre_ref, npost_ref, lb_ref, hn_ref, cw_ref,
                   wa_hbm, wb_hbm, wo_hbm,
                   y_ref, st_out_hbm, ctx_out_ref,
                   ft_scr, q_scr, v_scr, o_scr, za_scr, pb_scr, ga_scr, gb_scr,
                   in_buf, out_buf, in_sem, out_sem, w_in_ref, wa_ref, wb_ref, wo_ref, w_sem):
    G = DECODE_GROUP
    i = pl.program_id(0)
    n_steps = pl.num_programs(0)
    n_rows = x_ref.shape[0]
    part_rows = G // STATE_PARTS

    def in_copies(step):
        slot = step % STATE_IN_SLOTS
        return [pltpu.make_async_copy(st_hbm.at[pl.ds(step * G + p * part_rows, part_rows)],
                                      in_buf.at[slot, pl.ds(p * part_rows, part_rows)],
                                      in_sem.at[slot, p]) for p in range(STATE_PARTS)]

    def out_copies(step):
        slot = step % STATE_OUT_SLOTS
        return [pltpu.make_async_copy(out_buf.at[slot, pl.ds(p * part_rows, part_rows)],
                                      st_out_hbm.at[pl.ds(step * G + p * part_rows, part_rows)],
                                      out_sem.at[slot, p]) for p in range(STATE_PARTS)]

    n_proj = w_in_hbm.shape[1] // D_MODEL
    w_parts = [(w_in_hbm.at[:, pl.ds(j * D_MODEL, D_MODEL)], w_in_ref.at[:, pl.ds(j * D_MODEL, D_MODEL)])
               for j in range(n_proj)] + [(wa_hbm, wa_ref), (wb_hbm, wb_ref), (wo_hbm, wo_ref)]
    W_A, W_B, W_O = n_proj, n_proj + 1, n_proj + 2

    def w_copy(k):
        return pltpu.make_async_copy(w_parts[k][0], w_parts[k][1], w_sem.at[k])

    @pl.when(i == 0)
    def _prime():
        for step in range(STATE_IN_SLOTS - 1):
            for copy in in_copies(step):
                copy.start()
        for k in (P_F, P_Q, P_I, P_ZA, P_C, P_H, P_B, P_ZB, P_GA, P_GB, W_A, W_B, W_O):
            w_copy(k).start()

    @pl.when(i == 0)
    def _project():
        x = x_ref[:, 0, :]
        xn = (x * _rms_scale(x) * npre_ref[...]).astype(BF16)

        def proj(j):
            w_copy(j).wait()
            return _dot(xn, w_in_ref[:, j * D_MODEL:(j + 1) * D_MODEL])

        lb = _lower_bound(lb_ref[...])
        f = lb + (1.0 - lb) * _sigmoid(proj(P_F))
        ft_scr[...] = f.T
        q_scr[...] = _silu(proj(P_Q))
        v_scr[...] = proj(P_I)
        za_scr[...] = _silu(proj(P_ZA))
        u = proj(P_C) * proj(P_H)
        cw = cw_ref[...]
        conv = cw[CONV_W - 1:CONV_W, :] * u
        for j in range(CONV_W - 1):
            ctx_j = ctx_ref[:, j, :]
            conv = conv + cw[j:j + 1, :] * ctx_j
            if j > 0:
                ctx_out_ref[:, j - 1, :] = ctx_j
        ctx_out_ref[:, CONV_W - 2, :] = u
        pb_scr[...] = proj(P_B) * conv * _silu(proj(P_ZB))
        ga_scr[...] = _sigmoid(proj(P_GA))
        gb_scr[...] = _sigmoid(proj(P_GB))

    @pl.when(i + STATE_IN_SLOTS - 1 < n_steps)
    def _read_ahead():
        for copy in in_copies(i + STATE_IN_SLOTS - 1):
            copy.start()

    @pl.when(i >= STATE_OUT_SLOTS)
    def _slot_written_back():
        for copy in out_copies(i - STATE_OUT_SLOTS):
            copy.wait()

    for copy in in_copies(i):
        copy.wait()
    st_ref = in_buf.at[i % STATE_IN_SLOTS]
    st_out_ref = out_buf.at[i % STATE_OUT_SLOTS]

    shift = (n_rows - i * G) % n_rows
    f_cols = pltpu.roll(ft_scr[...], shift, 1)
    r0 = pl.multiple_of(i * G, G)
    v_rows = v_scr[pl.ds(r0, G), :]
    q_rows = q_scr[pl.ds(r0, G), :].astype(BF16)
    row_id = lax.broadcasted_iota(jnp.int32, (G, HEAD_DIM), 0)
    o_heads = [jnp.zeros((G, HEAD_DIM), F32)] * N_HEADS
    for j in range(G):
        for h, sl in enumerate(HEADS):
            f_b = jnp.broadcast_to(f_cols[sl, j:j + 1], (HEAD_DIM, HEAD_DIM))
            s_new = f_b * st_ref[j, h] + (1.0 - f_b) * v_rows[j:j + 1, sl]
            st_out_ref[j, h] = s_new
            read = _dot(q_rows[:, sl], s_new.astype(BF16))
            o_heads[h] = jnp.where(row_id == j, read, o_heads[h])
    o_scr[pl.ds(r0, G), :] = jnp.concatenate(o_heads, axis=1)
    for copy in out_copies(i):
        copy.start()

    @pl.when(i == pl.num_programs(0) - 1)
    def _output():
        hn = hn_ref[...]
        for sl in HEADS:
            o_h = o_scr[:, sl]
            o_scr[:, sl] = o_h * _rms_scale(o_h) * hn
        for k in (W_A, W_B, W_O):
            w_copy(k).wait()
        y_a = _dot((o_scr[...] * za_scr[...]).astype(BF16), wa_ref[...])
        y_b = _dot(pb_scr[...].astype(BF16), wb_ref[...])
        merged = ga_scr[...] * y_a + gb_scr[...] * y_b
        out = _dot(merged.astype(BF16), wo_ref[...])
        y_ref[:, 0, :] = x_ref[:, 0, :] + out * _rms_scale(out) * npost_ref[...]
        for back in range(STATE_OUT_SLOTS):
            for copy in out_copies(i - back):
                copy.wait()


def _run_decode(x, state, ctx, weights):
    n = x.shape[0]
    assert n % DECODE_GROUP == 0 and n == 128 and DECODE_GROUP % STATE_PARTS == 0
    assert n // DECODE_GROUP >= max(STATE_IN_SLOTS, STATE_OUT_SLOTS)
    w_in, npre, npost, lb_logits, hn, cw, wa, wb, wo = weights

    def const(shape):
        zeros = (0,) * len(shape)
        return pl.BlockSpec(shape, lambda i: zeros, pipeline_mode=pl.Buffered(1))

    hbm = pl.BlockSpec(memory_space=pl.ANY)
    in_slots = pltpu.VMEM((STATE_IN_SLOTS, DECODE_GROUP) + state.shape[1:], F32)
    out_slots = pltpu.VMEM((STATE_OUT_SLOTS, DECODE_GROUP) + state.shape[1:], F32)
    rows_f32 = pltpu.VMEM((n, D_MODEL), F32)
    cols_f32 = pltpu.VMEM((D_MODEL, n), F32)
    return pl.pallas_call(
        _decode_kernel,
        grid=(n // DECODE_GROUP,),
        in_specs=[const(x.shape), hbm, const(ctx.shape), hbm, const(npre.shape),
                  const(npost.shape), const(lb_logits.shape), const(hn.shape), const(cw.shape),
                  hbm, hbm, hbm],
        out_specs=[const(x.shape), hbm, const(ctx.shape)],
        out_shape=[jax.ShapeDtypeStruct(x.shape, F32),
                   jax.ShapeDtypeStruct(state.shape, F32),
                   jax.ShapeDtypeStruct(ctx.shape, F32)],
        scratch_shapes=[cols_f32, rows_f32, rows_f32, rows_f32, rows_f32, rows_f32, rows_f32, rows_f32,
                        in_slots, out_slots,
                        pltpu.SemaphoreType.DMA((STATE_IN_SLOTS, STATE_PARTS)),
                        pltpu.SemaphoreType.DMA((STATE_OUT_SLOTS, STATE_PARTS)),
                        pltpu.VMEM(w_in.shape, BF16), pltpu.VMEM(wa.shape, BF16),
                        pltpu.VMEM(wb.shape, BF16), pltpu.VMEM(wo.shape, BF16),
                        pltpu.SemaphoreType.DMA((w_in.shape[1] // D_MODEL + 3,))],
        compiler_params=pltpu.CompilerParams(
            dimension_semantics=("arbitrary",),
            vmem_limit_bytes=V7X_VMEM_LIMIT_BYTES),
        name="decode_step",
    )(x, state, ctx, w_in, npre, npost, lb_logits, hn, cw, wa, wb, wo)


def kernel(x_prompt, x_sample, state_hgrn, state_conv, meta_tokens, w_in, norm_pre, norm_post, lb_logits,
           hgrn_norm, conv_w, w_a, w_b, w_o):
    depth = w_in.shape[0]
    assert depth == 1, "single-layer trunk"
    assert x_sample.shape[1] == 1, "one new token per decode row"

    weights = (w_in[0], norm_pre, norm_post, lb_logits, hgrn_norm, conv_w[0], w_a[0], w_b[0], w_o[0])

    y_prompt, hgrn_p, conv_p, w_in_bf, wa_bf, wb_bf, wo_bf = _run_prompt(
        x_prompt, meta_tokens.astype(x_prompt.dtype), weights, tile=PROMPT_TILE, chunk=PROMPT_CHUNK)
    weights = (w_in_bf,) + weights[1:6] + (wa_bf, wb_bf, wo_bf)

    y_s, hgrn_s, conv_s = _run_decode(x_sample, state_hgrn[0], state_conv[0], weights)

    return (y_prompt, y_s, hgrn_p[None], hgrn_s[None], conv_p[None], conv_s[None])
```

```python
import functools

import jax
import jax.numpy as jnp
from jax import lax
from jax.experimental import pallas as pl
from jax.experimental.pallas import tpu as pltpu

D_MODEL = 1024
N_HEADS = 8
HEAD_DIM = D_MODEL // N_HEADS
CONV_W = 3
EPS = 1e-6
P_Q, P_F, P_I, P_ZA, P_B, P_C, P_H, P_ZB, P_GA, P_GB = range(10)

PROMPT_TILE = 256
PROMPT_CHUNK = 128
DECODE_GROUP = 8
STATE_IN_SLOTS = 3
STATE_OUT_SLOTS = 2
STATE_PARTS = 8
LOAD_BANDS = 4
SUBLANES = 8
CTX_ROW0 = SUBLANES - (CONV_W - 1)
MAX_HALF_CHUNK_LOG_DECAY = 80.0
V7X_VMEM_LIMIT_BYTES = 58 * 1024 * 1024

BF16 = jnp.bfloat16
F32 = jnp.float32
HEADS = [slice(h * HEAD_DIM, (h + 1) * HEAD_DIM) for h in range(N_HEADS)]


def _dot(a, b):
    return jnp.dot(a, b, preferred_element_type=F32)


def _dot_nt(a, b):
    return lax.dot_general(a, b, (((1,), (1,)), ((), ())), preferred_element_type=F32)


def _dot_tn(a, b):
    return lax.dot_general(a, b, (((0,), (0,)), ((), ())), preferred_element_type=F32)


def _sigmoid(x):
    return 1.0 / (1.0 + jnp.exp(-x))


def _silu(x):
    return x * _sigmoid(x)


def _rms_scale(x):
    return lax.rsqrt(jnp.mean(x * x, axis=-1, keepdims=True) + EPS)


def _lower_bound(lb_logits):
    m = jnp.max(lb_logits, axis=0, keepdims=True)
    e = jnp.exp(lb_logits - m)
    return e[0:1, :] / jnp.sum(e, axis=0, keepdims=True)


def _block_cumsum(x, block):
    n = x.shape[0]
    ri = lax.broadcasted_iota(jnp.int32, (n, n), 0)
    ci = lax.broadcasted_iota(jnp.int32, (n, n), 1)
    tri = ri >= ci
    if n != block:
        tri = jnp.logical_and(tri, (ri // block) == (ci // block))
    tri = jnp.where(tri, 1.0, 0.0).astype(BF16)
    hi = x.astype(BF16)
    lo = (x - hi.astype(F32)).astype(BF16)
    return _dot(tri, hi) + _dot(tri, lo)


def _seq_kernel(x_ref, meta_ref, w_in_hbm, npre_ref, npost_ref, lb_ref, hn_ref, cw_ref,
                wa_hbm, wb_hbm, wo_hbm,
                y_ref, st_out_ref, c_out_ref, w_in_out, wa_out, wb_out, wo_out,
                st_scr, u_scr, q_scr, k_scr, v_scr, g_scr, o_scr,
                qt_scr, kt_scr, qin_scr, kst_scr, vb_scr, dec_scr, st_meta_scr, ctx_meta_scr,
                w_in_ref, wa_ref, wb_ref, wo_ref, stage_scr, load_sem, store_sem,
                *, tile, chunk):
    T, C = tile, chunk
    n_chunks = T // C
    t = pl.program_id(1)
    first_step = jnp.logical_and(pl.program_id(0) == 0, t == 0)
    last_step = jnp.logical_and(pl.program_id(0) == pl.num_programs(0) - 1, t == pl.num_programs(1) - 1)
    lb = _lower_bound(lb_ref[...])

    col = [pl.ds(j * D_MODEL, D_MODEL) for j in range(w_in_hbm.shape[1] // D_MODEL)]
    groups = [(w_in_hbm.at[:, c], w_in_ref.at[:, c], w_in_out.at[:, c]) for c in col]
    groups += [(wa_hbm, wa_ref, wa_out), (wb_hbm, wb_ref, wb_out), (wo_hbm, wo_ref, wo_out)]

    def load_copies(p):
        src = groups[fetch_order[p]][0]
        band = src.shape[0] // LOAD_BANDS
        return [pltpu.make_async_copy(src.at[pl.ds(b * band, band), :],
                                      stage_scr.at[p % 2, pl.ds(b * band, band), :],
                                      load_sem.at[p % 2, b]) for b in range(LOAD_BANDS)]

    def store_copy(i):
        return pltpu.make_async_copy(groups[i][1], groups[i][2], store_sem.at[i])

    W_A, W_B, W_O = len(col), len(col) + 1, len(col) + 2
    fetch_order = [P_F, P_I, P_C, P_H, P_Q, P_ZA, W_A, P_B, P_ZB, W_B, P_GA, P_GB, W_O]
    n_eager = 4
    assert sorted(fetch_order) == list(range(len(groups)))

    def fetch(k):
        p = fetch_order.index(k)
        for copy in load_copies(p):
            copy.wait()
        groups[k][1][...] = stage_scr[p % 2].astype(BF16)
        if p + 2 < len(fetch_order):
            for copy in load_copies(p + 2):
                copy.start()

    @pl.when(first_step)
    def _stream_first_weights():
        for p in range(2):
            for copy in load_copies(p):
                copy.start()

    def project(xn, j):
        return _dot(xn, w_in_ref[:, j * D_MODEL:(j + 1) * D_MODEL])

    @pl.when(first_step)
    def _meta_prefix():
        xm = meta_ref[...]
        n_meta = xm.shape[0]
        xn_m = (xm * _rms_scale(xm) * npre_ref[...]).astype(BF16)

        def meta_proj(j):
            fetch(j)
            return project(xn_m, j)

        f_m = lb + (1.0 - lb) * _sigmoid(meta_proj(P_F))
        g_m = _block_cumsum(jnp.log(f_m), n_meta)
        k_end = ((1.0 - f_m) * jnp.exp(g_m[n_meta - 1:n_meta, :] - g_m)).astype(BF16)
        v_m = meta_proj(P_I).astype(BF16)
        for h, sl in enumerate(HEADS):
            st_meta_scr[h] = _dot_tn(k_end[:, sl], v_m[:, sl])
        u_m = meta_proj(P_C) * meta_proj(P_H)
        ctx_meta_scr[CTX_ROW0:SUBLANES, :] = u_m[n_meta - (CONV_W - 1):n_meta, :]
        dec_scr[...] = jnp.zeros(dec_scr.shape, F32)

    @pl.when(t == 0)
    def _init():
        st_scr[...] = st_meta_scr[...]
        u_scr[CTX_ROW0:SUBLANES, :] = ctx_meta_scr[CTX_ROW0:SUBLANES, :]

    def run_tile(lazy_weights):
        fetched = set()
        x = x_ref[0]
        xn = (x * _rms_scale(x) * npre_ref[...]).astype(BF16)

        def proj(j):
            if lazy_weights and j in fetch_order[n_eager:] and j not in fetched:
                fetched.add(j)
                fetch(j)
            return project(xn, j)

        def late_weight(k, ref):
            if lazy_weights:
                fetch(k)
            return ref[...]

        f = lb + (1.0 - lb) * _sigmoid(proj(P_F))
        k_scr[...] = 1.0 - f
        q_scr[...] = _silu(proj(P_Q))
        vb_scr[...] = proj(P_I).astype(BF16)
        g_scr[...] = _block_cumsum(jnp.log(f), C)

        g_floor = None
        for c in range(n_chunks):
            rows = slice(c * C, (c + 1) * C)
            gc = g_scr[rows, :]
            g_last = gc[C - 1:C, :]
            g_mid = gc[C // 2 - 1:C // 2, :]
            qc = q_scr[rows, :]
            kc = k_scr[rows, :]
            qt_scr[rows, :] = (qc * jnp.exp(gc - g_mid)).astype(BF16)
            kt_scr[rows, :] = (kc * jnp.exp(g_mid - gc)).astype(BF16)
            qin_scr[rows, :] = (qc * jnp.exp(gc)).astype(BF16)
            kst_scr[rows, :] = (kc * jnp.exp(g_last - gc)).astype(BF16)
            dec_scr[c:c + 1, :] = jnp.exp(g_last)
            half_floor = jnp.minimum(g_mid, g_last - g_mid)
            g_floor = half_floor if g_floor is None else jnp.minimum(g_floor, half_floor)
        stable = jnp.min(g_floor) >= -MAX_HALF_CHUNK_LOG_DECAY

        dec_cols = dec_scr[...].T
        keep = jnp.logical_and(
            lax.broadcasted_iota(jnp.int32, (C, C), 0) >= lax.broadcasted_iota(jnp.int32, (C, C), 1), stable)
        for c in range(n_chunks):
            rows = slice(c * C, (c + 1) * C)
            scores = [jnp.where(keep, _dot_nt(qt_scr[rows, sl], kt_scr[rows, sl]), 0.0).astype(BF16)
                      for sl in HEADS]
            for h, sl in enumerate(HEADS):
                vb = vb_scr[rows, sl]
                st = st_scr[h]
                o_scr[rows, sl] = _dot(jnp.concatenate([qin_scr[rows, sl], scores[h]], axis=1),
                                       jnp.concatenate([st.astype(BF16), vb], axis=0))
                decay = jnp.broadcast_to(dec_cols[sl, c:c + 1], (HEAD_DIM, HEAD_DIM))
                st_scr[h] = decay * st + _dot_tn(kst_scr[rows, sl], vb)

        @pl.when(jnp.logical_not(stable))
        def _intra_exact():
            same_head = (lax.broadcasted_iota(jnp.int32, (D_MODEL, D_MODEL), 0) // HEAD_DIM ==
                         lax.broadcasted_iota(jnp.int32, (D_MODEL, D_MODEL), 1) // HEAD_DIM)
            head_sum = jnp.where(same_head, 1.0, 0.0).astype(BF16)
            tpos = lax.broadcasted_iota(jnp.int32, (C, 1), 0)
            v_scr[...] = proj(P_I)

            def chunk_step(c, carry):
                r0 = pl.multiple_of(c * C, C)
                rows = pl.ds(r0, C)
                gc = g_scr[rows, :]
                qc = q_scr[rows, :]

                def src_step(i, carry2):
                    src = pl.ds(r0 + i, 1)
                    p = qc * jnp.exp(jnp.minimum(gc - g_scr[src, :], 0.0)) * k_scr[src, :]
                    p = jnp.where(tpos >= i, p, 0.0).astype(BF16)
                    o_scr[rows, :] += _dot(p, head_sum) * v_scr[src, :]
                    return carry2

                return lax.fori_loop(0, C, src_step, carry)

            lax.fori_loop(0, n_chunks, chunk_step, 0)

        hn = hn_ref[...]
        silu_za = _silu(proj(P_ZA))
        gated = []
        for sl in HEADS:
            o_h = o_scr[:, sl]
            gated.append((o_h * _rms_scale(o_h) * hn * silu_za[:, sl]).astype(BF16))
        y_a = _dot(jnp.concatenate(gated, axis=1), late_weight(W_A, wa_ref))

        u = proj(P_C) * proj(P_H)
        u_scr[SUBLANES:SUBLANES + T, :] = u
        cw = cw_ref[...]
        conv = cw[CONV_W - 1:CONV_W, :] * u
        for j in range(CONV_W - 1):
            conv = conv + cw[j:j + 1, :] * u_scr[CTX_ROW0 + j:CTX_ROW0 + j + T, :]
        y_b = _dot((proj(P_B) * conv * _silu(proj(P_ZB))).astype(BF16), late_weight(W_B, wb_ref))
        u_scr[CTX_ROW0:SUBLANES, :] = u[T - (CONV_W - 1):T, :]

        merged = _sigmoid(proj(P_GA)) * y_a + _sigmoid(proj(P_GB)) * y_b
        out = _dot(merged.astype(BF16), late_weight(W_O, wo_ref))
        y_ref[0] = x_ref[0] + out * _rms_scale(out) * npost_ref[...]

    @pl.when(first_step)
    def _first_tile():
        run_tile(lazy_weights=True)
        for k in range(len(groups)):
            store_copy(k).start()

    @pl.when(jnp.logical_not(first_step))
    def _tile():
        run_tile(lazy_weights=False)

    @pl.when(t == pl.num_programs(1) - 1)
    def _finish():
        c_out_ref[0] = u_scr[CTX_ROW0:SUBLANES, :]
        st_out_ref[0] = st_scr[...]

    @pl.when(last_step)
    def _weights_written():
        for i in range(len(groups)):
            store_copy(i).wait()


def _const_spec(shape):
    zeros = (0,) * len(shape)
    return pl.BlockSpec(shape, lambda b, t: zeros, pipeline_mode=pl.Buffered(1))


def _run_prompt(x, meta, weights, *, tile, chunk):
    n, length, _ = x.shape
    assert length % tile == 0 and tile % chunk == 0 and chunk % (2 * SUBLANES) == 0 and tile >= CONV_W - 1
    assert meta.shape[0] % (2 * SUBLANES) == 0 and meta.shape[0] >= CONV_W - 1
    assert tile // chunk <= HEAD_DIM
    w_in, npre, npost, lb_logits, hn, cw, wa, wb, wo = weights
    kern = functools.partial(_seq_kernel, tile=tile, chunk=chunk)
    hbm = pl.BlockSpec(memory_space=pl.ANY)
    n_groups = w_in.shape[1] // D_MODEL + 3
    assert w_in.shape[1] % D_MODEL == 0 and wa.shape == wb.shape == wo.shape == (w_in.shape[0], D_MODEL)
    state_shape = (1, N_HEADS, HEAD_DIM, HEAD_DIM)
    ctx_shape = (1, CONV_W - 1, D_MODEL)
    tile_f32 = pltpu.VMEM((tile, D_MODEL), F32)
    tile_bf16 = pltpu.VMEM((tile, D_MODEL), BF16)
    return pl.pallas_call(
        kern,
        grid=(n, length // tile),
        in_specs=[
            pl.BlockSpec((1, tile, D_MODEL), lambda b, t: (b, t, 0)),
            _const_spec(meta.shape),
            hbm,
            _const_spec(npre.shape),
            _const_spec(npost.shape),
            _const_spec(lb_logits.shape),
            _const_spec(hn.shape),
            _const_spec(cw.shape),
            hbm, hbm, hbm,
        ],
        out_specs=[
            pl.BlockSpec((1, tile, D_MODEL), lambda b, t: (b, t, 0)),
            pl.BlockSpec(state_shape, lambda b, t: (b, 0, 0, 0)),
            pl.BlockSpec(ctx_shape, lambda b, t: (b, 0, 0)),
            hbm, hbm, hbm, hbm,
        ],
        out_shape=[
            jax.ShapeDtypeStruct(x.shape, F32),
            jax.ShapeDtypeStruct((n,) + state_shape[1:], F32),
            jax.ShapeDtypeStruct((n,) + ctx_shape[1:], F32),
            jax.ShapeDtypeStruct(w_in.shape, BF16),
            jax.ShapeDtypeStruct(wa.shape, BF16),
            jax.ShapeDtypeStruct(wb.shape, BF16),
            jax.ShapeDtypeStruct(wo.shape, BF16),
        ],
        scratch_shapes=[
            pltpu.VMEM((N_HEADS, HEAD_DIM, HEAD_DIM), F32),
            pltpu.VMEM((tile + SUBLANES, D_MODEL), F32),
            tile_f32, tile_f32, tile_f32, tile_f32, tile_f32,
            tile_bf16, tile_bf16, tile_bf16, tile_bf16, tile_bf16,
            pltpu.VMEM((HEAD_DIM, D_MODEL), F32),
            pltpu.VMEM((N_HEADS, HEAD_DIM, HEAD_DIM), F32),
            pltpu.VMEM((SUBLANES, D_MODEL), F32),
            pltpu.VMEM(w_in.shape, BF16),
            pltpu.VMEM(wa.shape, BF16), pltpu.VMEM(wb.shape, BF16), pltpu.VMEM(wo.shape, BF16),
            pltpu.VMEM((2,) + wa.shape, F32),
            pltpu.SemaphoreType.DMA((2, LOAD_BANDS)),
            pltpu.SemaphoreType.DMA((n_groups,)),
        ],
        compiler_params=pltpu.CompilerParams(
            dimension_semantics=("arbitrary", "arbitrary"),
            vmem_limit_bytes=V7X_VMEM_LIMIT_BYTES),
        name="prompt_sweep",
    )(x, meta, w_in, npre, npost, lb_logits, hn, cw, wa, wb, wo)


def _decode_kernel(x_ref, st_hbm, ctx_ref, w_in_hbm, npre_ref, npost_ref, lb_ref, hn_ref, cw_ref,
                   wa_hbm, wb_hbm, wo_hbm,
                   y_ref, st_out_hbm, ctx_out_ref,
                   ft_scr, q_scr, v_scr, o_scr, za_scr, pb_scr, ga_scr, gb_scr,
                   in_buf, out_buf, in_sem, out_sem, w_in_ref, wa_ref, wb_ref, wo_ref, w_sem):
    G = DECODE_GROUP
    i = pl.program_id(0)
    n_steps = pl.num_programs(0)
    n_rows = x_ref.shape[0]
    part_rows = G // STATE_PARTS

    def in_copies(step):
        slot = step % STATE_IN_SLOTS
        return [pltpu.make_async_copy(st_hbm.at[pl.ds(step * G + p * part_rows, part_rows)],
                                      in_buf.at[slot, pl.ds(p * part_rows, part_rows)],
                                      in_sem.at[slot, p]) for p in range(STATE_PARTS)]

    def out_copies(step):
        slot = step % STATE_OUT_SLOTS
        return [pltpu.make_async_copy(out_buf.at[slot, pl.ds(p * part_rows, part_rows)],
                                      st_out_hbm.at[pl.ds(step * G + p * part_rows, part_rows)],
                                      out_sem.at[slot, p]) for p in range(STATE_PARTS)]

    n_proj = w_in_hbm.shape[1] // D_MODEL
    w_parts = [(w_in_hbm.at[:, pl.ds(j * D_MODEL, D_MODEL)], w_in_ref.at[:, pl.ds(j * D_MODEL, D_MODEL)])
               for j in range(n_proj)] + [(wa_hbm, wa_ref), (wb_hbm, wb_ref), (wo_hbm, wo_ref)]
    W_A, W_B, W_O = n_proj, n_proj + 1, n_proj + 2

    def w_copy(k):
        return pltpu.make_async_copy(w_parts[k][0], w_parts[k][1], w_sem.at[k])

    @pl.when(i == 0)
    def _prime():
        for step in range(STATE_IN_SLOTS - 1):
            for copy in in_copies(step):
                copy.start()
        for k in (P_F, P_Q, P_I, P_ZA, P_C, P_H, P_B, P_ZB, P_GA, P_GB, W_A, W_B, W_O):
            w_copy(k).start()

    @pl.when(i == 0)
    def _project():
        x = x_ref[:, 0, :]
        xn = (x * _rms_scale(x) * npre_ref[...]).astype(BF16)

        def proj(j):
            w_copy(j).wait()
            return _dot(xn, w_in_ref[:, j * D_MODEL:(j + 1) * D_MODEL])

        lb = _lower_bound(lb_ref[...])
        f = lb + (1.0 - lb) * _sigmoid(proj(P_F))
        ft_scr[...] = f.T
        q_scr[...] = _silu(proj(P_Q))
        v_scr[...] = proj(P_I)
        za_scr[...] = _silu(proj(P_ZA))
        u = proj(P_C) * proj(P_H)
        cw = cw_ref[...]
        conv = cw[CONV_W - 1:CONV_W, :] * u
        for j in range(CONV_W - 1):
            ctx_j = ctx_ref[:, j, :]
            conv = conv + cw[j:j + 1, :] * ctx_j
            if j > 0:
                ctx_out_ref[:, j - 1, :] = ctx_j
        ctx_out_ref[:, CONV_W - 2, :] = u
        pb_scr[...] = proj(P_B) * conv * _silu(proj(P_ZB))
        ga_scr[...] = _sigmoid(proj(P_GA))
        gb_scr[...] = _sigmoid(proj(P_GB))

    @pl.when(i + STATE_IN_SLOTS - 1 < n_steps)
    def _read_ahead():
        for copy in in_copies(i + STATE_IN_SLOTS - 1):
            copy.start()

    @pl.when(i >= STATE_OUT_SLOTS)
    def _slot_written_back():
        for copy in out_copies(i - STATE_OUT_SLOTS):
            copy.wait()

    for copy in in_copies(i):
        copy.wait()
    st_ref = in_buf.at[i % STATE_IN_SLOTS]
    st_out_ref = out_buf.at[i % STATE_OUT_SLOTS]

    shift = (n_rows - i * G) % n_rows
    f_cols = pltpu.roll(ft_scr[...], shift, 1)
    r0 = pl.multiple_of(i * G, G)
    v_rows = v_scr[pl.ds(r0, G), :]
    q_rows = q_scr[pl.ds(r0, G), :].astype(BF16)
    row_id = lax.broadcasted_iota(jnp.int32, (G, HEAD_DIM), 0)
    o_heads = [jnp.zeros((G, HEAD_DIM), F32)] * N_HEADS
    for j in range(G):
        for h, sl in enumerate(HEADS):
            f_b = jnp.broadcast_to(f_cols[sl, j:j + 1], (HEAD_DIM, HEAD_DIM))
            s_new = f_b * st_ref[j, h] + (1.0 - f_b) * v_rows[j:j + 1, sl]
            st_out_ref[j, h] = s_new
            read = _dot(q_rows[:, sl], s_new.astype(BF16))
            o_heads[h] = jnp.where(row_id == j, read, o_heads[h])
    o_scr[pl.ds(r0, G), :] = jnp.concatenate(o_heads, axis=1)
    for copy in out_copies(i):
        copy.start()

    @pl.when(i == pl.num_programs(0) - 1)
    def _output():
        hn = hn_ref[...]
        for sl in HEADS:
            o_h = o_scr[:, sl]
            o_scr[:, sl] = o_h * _rms_scale(o_h) * hn
        for k in (W_A, W_B, W_O):
            w_copy(k).wait()
        y_a = _dot((o_scr[...] * za_scr[...]).astype(BF16), wa_ref[...])
        y_b = _dot(pb_scr[...].astype(BF16), wb_ref[...])
        merged = ga_scr[...] * y_a + gb_scr[...] * y_b
        out = _dot(merged.astype(BF16), wo_ref[...])
        y_ref[:, 0, :] = x_ref[:, 0, :] + out * _rms_scale(out) * npost_ref[...]
        for back in range(STATE_OUT_SLOTS):
            for copy in out_copies(i - back):
                copy.wait()


def _run_decode(x, state, ctx, weights):
    n = x.shape[0]
    assert n % DECODE_GROUP == 0 and n == 128 and DECODE_GROUP % STATE_PARTS == 0
    assert n // DECODE_GROUP >= max(STATE_IN_SLOTS, STATE_OUT_SLOTS)
    w_in, npre, npost, lb_logits, hn, cw, wa, wb, wo = weights

    def const(shape):
        zeros = (0,) * len(shape)
        return pl.BlockSpec(shape, lambda i: zeros, pipeline_mode=pl.Buffered(1))

    hbm = pl.BlockSpec(memory_space=pl.ANY)
    in_slots = pltpu.VMEM((STATE_IN_SLOTS, DECODE_GROUP) + state.shape[1:], F32)
    out_slots = pltpu.VMEM((STATE_OUT_SLOTS, DECODE_GROUP) + state.shape[1:], F32)
    rows_f32 = pltpu.VMEM((n, D_MODEL), F32)
    cols_f32 = pltpu.VMEM((D_MODEL, n), F32)
    return pl.pallas_call(
        _decode_kernel,
        grid=(n // DECODE_GROUP,),
        in_specs=[const(x.shape), hbm, const(ctx.shape), hbm, const(npre.shape),
                  const(npost.shape), const(lb_logits.shape), const(hn.shape), const(cw.shape),
                  hbm, hbm, hbm],
        out_specs=[const(x.shape), hbm, const(ctx.shape)],
        out_shape=[jax.ShapeDtypeStruct(x.shape, F32),
                   jax.ShapeDtypeStruct(state.shape, F32),
                   jax.ShapeDtypeStruct(ctx.shape, F32)],
        scratch_shapes=[cols_f32, rows_f32, rows_f32, rows_f32, rows_f32, rows_f32, rows_f32, rows_f32,
                        in_slots, out_slots,
                        pltpu.SemaphoreType.DMA((STATE_IN_SLOTS, STATE_PARTS)),
                        pltpu.SemaphoreType.DMA((STATE_OUT_SLOTS, STATE_PARTS)),
                        pltpu.VMEM(w_in.shape, BF16), pltpu.VMEM(wa.shape, BF16),
                        pltpu.VMEM(wb.shape, BF16), pltpu.VMEM(wo.shape, BF16),
                        pltpu.SemaphoreType.DMA((w_in.shape[1] // D_MODEL + 3,))],
        compiler_params=pltpu.CompilerParams(
            dimension_semantics=("arbitrary",),
            vmem_limit_bytes=V7X_VMEM_LIMIT_BYTES),
        name="decode_step",
    )(x, state, ctx, w_in, npre, npost, lb_logits, hn, cw, wa, wb, wo)


def kernel(x_prompt, x_sample, state_hgrn, state_conv, meta_tokens, w_in, norm_pre, norm_post, lb_logits,
           hgrn_norm, conv_w, w_a, w_b, w_o):
    depth = w_in.shape[0]
    assert depth == 1, "single-layer trunk"
    assert x_sample.shape[1] == 1, "one new token per decode row"

    weights = (w_in[0], norm_pre, norm_post, lb_logits, hgrn_norm, conv_w[0], w_a[0], w_b[0], w_o[0])

    y_prompt, hgrn_p, conv_p, w_in_bf, wa_bf, wb_bf, wo_bf = _run_prompt(
        x_prompt, meta_tokens.astype(x_prompt.dtype), weights, tile=PROMPT_TILE, chunk=PROMPT_CHUNK)
    weights = (w_in_bf,) + weights[1:6] + (wa_bf, wb_bf, wo_bf)

    y_s, hgrn_s, conv_s = _run_decode(x_sample, state_hgrn[0], state_conv[0], weights)

    return (y_prompt, y_s, hgrn_p[None], hgrn_s[None], conv_p[None], conv_s[None])
```

```python
import functools

import jax
import jax.numpy as jnp
from jax import lax
from jax.experimental import pallas as pl
from jax.experimental.pallas import tpu as pltpu

D_MODEL = 1024
N_HEADS = 8
HEAD_DIM = D_MODEL // N_HEADS
CONV_W = 3
EPS = 1e-6
P_Q, P_F, P_I, P_ZA, P_B, P_C, P_H, P_ZB, P_GA, P_GB = range(10)

PROMPT_TILE = 256
PROMPT_CHUNK = 128
DECODE_GROUP = 8
STATE_IN_SLOTS = 3
STATE_OUT_SLOTS = 2
STATE_PARTS = 8
LOAD_BANDS = 4
SUBLANES = 8
CTX_ROW0 = SUBLANES - (CONV_W - 1)
MAX_HALF_CHUNK_LOG_DECAY = 80.0
V7X_VMEM_LIMIT_BYTES = 58 * 1024 * 1024

BF16 = jnp.bfloat16
F32 = jnp.float32
HEADS = [slice(h * HEAD_DIM, (h + 1) * HEAD_DIM) for h in range(N_HEADS)]


def _dot(a, b):
    return jnp.dot(a, b, preferred_element_type=F32)


def _dot_nt(a, b):
    return lax.dot_general(a, b, (((1,), (1,)), ((), ())), preferred_element_type=F32)


def _dot_tn(a, b):
    return lax.dot_general(a, b, (((0,), (0,)), ((), ())), preferred_element_type=F32)


def _sigmoid(x):
    return 1.0 / (1.0 + jnp.exp(-x))


def _silu(x):
    return x * _sigmoid(x)


def _rms_scale(x):
    return lax.rsqrt(jnp.mean(x * x, axis=-1, keepdims=True) + EPS)


def _lower_bound(lb_logits):
    m = jnp.max(lb_logits, axis=0, keepdims=True)
    e = jnp.exp(lb_logits - m)
    return e[0:1, :] / jnp.sum(e, axis=0, keepdims=True)


def _block_cumsum(x, block):
    n = x.shape[0]
    ri = lax.broadcasted_iota(jnp.int32, (n, n), 0)
    ci = lax.broadcasted_iota(jnp.int32, (n, n), 1)
    tri = ri >= ci
    if n != block:
        tri = jnp.logical_and(tri, (ri // block) == (ci // block))
    tri = jnp.where(tri, 1.0, 0.0).astype(BF16)
    hi = x.astype(BF16)
    lo = (x - hi.astype(F32)).astype(BF16)
    return _dot(tri, hi) + _dot(tri, lo)


def _seq_kernel(x_ref, meta_ref, w_in_hbm, npre_ref, npost_ref, lb_ref, hn_ref, cw_ref,
                wa_hbm, wb_hbm, wo_hbm,
                y_ref, st_out_ref, c_out_ref, w_in_out, wa_out, wb_out, wo_out,
                st_scr, u_scr, q_scr, k_scr, v_scr, g_scr, o_scr,
                qt_scr, kt_scr, qin_scr, kst_scr, vb_scr, dec_scr, st_meta_scr, ctx_meta_scr,
                w_in_ref, wa_ref, wb_ref, wo_ref, stage_scr, load_sem, store_sem,
                *, tile, chunk):
    T, C = tile, chunk
    n_chunks = T // C
    t = pl.program_id(1)
    first_step = jnp.logical_and(pl.program_id(0) == 0, t == 0)
    last_step = jnp.logical_and(pl.program_id(0) == pl.num_programs(0) - 1, t == pl.num_programs(1) - 1)
    lb = _lower_bound(lb_ref[...])

    col = [pl.ds(j * D_MODEL, D_MODEL) for j in range(w_in_hbm.shape[1] // D_MODEL)]
    groups = [(w_in_hbm.at[:, c], w_in_ref.at[:, c], w_in_out.at[:, c]) for c in col]
    groups += [(wa_hbm, wa_ref, wa_out), (wb_hbm, wb_ref, wb_out), (wo_hbm, wo_ref, wo_out)]

    def load_copies(p):
        src = groups[fetch_order[p]][0]
        band = src.shape[0] // LOAD_BANDS
        return [pltpu.make_async_copy(src.at[pl.ds(b * band, band), :],
                                      stage_scr.at[p % 2, pl.ds(b * band, band), :],
                                      load_sem.at[p % 2, b]) for b in range(LOAD_BANDS)]

    def store_copy(i):
        return pltpu.make_async_copy(groups[i][1], groups[i][2], store_sem.at[i])

    W_A, W_B, W_O = len(col), len(col) + 1, len(col) + 2
    fetch_order = [P_F, P_I, P_C, P_H, P_Q, P_ZA, W_A, P_B, P_ZB, W_B, P_GA, P_GB, W_O]
    n_eager = 4
    assert sorted(fetch_order) == list(range(len(groups)))

    def fetch(k):
        p = fetch_order.index(k)
        for copy in load_copies(p):
            copy.wait()
        groups[k][1][...] = stage_scr[p % 2].astype(BF16)
        if p + 2 < len(fetch_order):
            for copy in load_copies(p + 2):
                copy.start()

    @pl.when(first_step)
    def _stream_first_weights():
        for p in range(2):
            for copy in load_copies(p):
                copy.start()

    def project(xn, j):
        return _dot(xn, w_in_ref[:, j * D_MODEL:(j + 1) * D_MODEL])

    @pl.when(first_step)
    def _meta_prefix():
        xm = meta_ref[...]
        n_meta = xm.shape[0]
        xn_m = (xm * _rms_scale(xm) * npre_ref[...]).astype(BF16)

        def meta_proj(j):
            fetch(j)
            return project(xn_m, j)

        f_m = lb + (1.0 - lb) * _sigmoid(meta_proj(P_F))
        g_m = _block_cumsum(jnp.log(f_m), n_meta)
        k_end = ((1.0 - f_m) * jnp.exp(g_m[n_meta - 1:n_meta, :] - g_m)).astype(BF16)
        v_m = meta_proj(P_I).astype(BF16)
        for h, sl in enumerate(HEADS):
            st_meta_scr[h] = _dot_tn(k_end[:, sl], v_m[:, sl])
        u_m = meta_proj(P_C) * meta_proj(P_H)
        ctx_meta_scr[CTX_ROW0:SUBLANES, :] = u_m[n_meta - (CONV_W - 1):n_meta, :]
        dec_scr[...] = jnp.zeros(dec_scr.shape, F32)

    @pl.when(t == 0)
    def _init():
        st_scr[...] = st_meta_scr[...]
        u_scr[CTX_ROW0:SUBLANES, :] = ctx_meta_scr[CTX_ROW0:SUBLANES, :]

    def run_tile(lazy_weights):
        fetched = set()
        x = x_ref[0]
        xg = (x * npre_ref[...]).astype(BF16)
        row_scale = _rms_scale(x)

        def proj(j):
            if lazy_weights and j in fetch_order[n_eager:] and j not in fetched:
                fetched.add(j)
                fetch(j)
            return row_scale * project(xg, j)

        def late_weight(k, ref):
            if lazy_weights:
                fetch(k)
            return ref[...]

        f = lb + (1.0 - lb) * _sigmoid(proj(P_F))
        k_scr[...] = 1.0 - f
        q_scr[...] = _silu(proj(P_Q))
        vb_scr[...] = proj(P_I).astype(BF16)
        g_scr[...] = _block_cumsum(jnp.log(f), C)

        g_floor = None
        for c in range(n_chunks):
            rows = slice(c * C, (c + 1) * C)
            gc = g_scr[rows, :]
            g_last = gc[C - 1:C, :]
            g_mid = gc[C // 2 - 1:C // 2, :]
            qc = q_scr[rows, :]
            kc = k_scr[rows, :]
            qt_scr[rows, :] = (qc * jnp.exp(gc - g_mid)).astype(BF16)
            kt_scr[rows, :] = (kc * jnp.exp(g_mid - gc)).astype(BF16)
            qin_scr[rows, :] = (qc * jnp.exp(gc)).astype(BF16)
            kst_scr[rows, :] = (kc * jnp.exp(g_last - gc)).astype(BF16)
            dec_scr[c:c + 1, :] = jnp.exp(g_last)
            half_floor = jnp.minimum(g_mid, g_last - g_mid)
            g_floor = half_floor if g_floor is None else jnp.minimum(g_floor, half_floor)
        stable = jnp.min(g_floor) >= -MAX_HALF_CHUNK_LOG_DECAY

        dec_cols = dec_scr[...].T
        keep = jnp.logical_and(
            lax.broadcasted_iota(jnp.int32, (C, C), 0) >= lax.broadcasted_iota(jnp.int32, (C, C), 1), stable)
        for c in range(n_chunks):
            rows = slice(c * C, (c + 1) * C)
            scores = [jnp.where(keep, _dot_nt(qt_scr[rows, sl], kt_scr[rows, sl]), 0.0).astype(BF16)
                      for sl in HEADS]
            for h, sl in enumerate(HEADS):
                vb = vb_scr[rows, sl]
                st = st_scr[h]
                o_scr[rows, sl] = _dot(jnp.concatenate([qin_scr[rows, sl], scores[h]], axis=1),
                                       jnp.concatenate([st.astype(BF16), vb], axis=0))
                decay = jnp.broadcast_to(dec_cols[sl, c:c + 1], (HEAD_DIM, HEAD_DIM))
                st_scr[h] = decay * st + _dot_tn(kst_scr[rows, sl], vb)

        @pl.when(jnp.logical_not(stable))
        def _intra_exact():
            same_head = (lax.broadcasted_iota(jnp.int32, (D_MODEL, D_MODEL), 0) // HEAD_DIM ==
                         lax.broadcasted_iota(jnp.int32, (D_MODEL, D_MODEL), 1) // HEAD_DIM)
            head_sum = jnp.where(same_head, 1.0, 0.0).astype(BF16)
            tpos = lax.broadcasted_iota(jnp.int32, (C, 1), 0)
            v_scr[...] = proj(P_I)

            def chunk_step(c, carry):
                r0 = pl.multiple_of(c * C, C)
                rows = pl.ds(r0, C)
                gc = g_scr[rows, :]
                qc = q_scr[rows, :]

                def src_step(i, carry2):
                    src = pl.ds(r0 + i, 1)
                    p = qc * jnp.exp(jnp.minimum(gc - g_scr[src, :], 0.0)) * k_scr[src, :]
                    p = jnp.where(tpos >= i, p, 0.0).astype(BF16)
                    o_scr[rows, :] += _dot(p, head_sum) * v_scr[src, :]
                    return carry2

                return lax.fori_loop(0, C, src_step, carry)

            lax.fori_loop(0, n_chunks, chunk_step, 0)

        hn = hn_ref[...]
        silu_za = _silu(proj(P_ZA))
        gated = []
        for sl in HEADS:
            o_h = o_scr[:, sl]
            gated.append((o_h * _rms_scale(o_h) * hn * silu_za[:, sl]).astype(BF16))
        y_a = _dot(jnp.concatenate(gated, axis=1), late_weight(W_A, wa_ref))

        u = proj(P_C) * proj(P_H)
        u_scr[SUBLANES:SUBLANES + T, :] = u
        cw = cw_ref[...]
        conv = cw[CONV_W - 1:CONV_W, :] * u
        for j in range(CONV_W - 1):
            conv = conv + cw[j:j + 1, :] * u_scr[CTX_ROW0 + j:CTX_ROW0 + j + T, :]
        y_b = _dot((proj(P_B) * conv * _silu(proj(P_ZB))).astype(BF16), late_weight(W_B, wb_ref))
        u_scr[CTX_ROW0:SUBLANES, :] = u[T - (CONV_W - 1):T, :]

        merged = _sigmoid(proj(P_GA)) * y_a + _sigmoid(proj(P_GB)) * y_b
        out = _dot(merged.astype(BF16), late_weight(W_O, wo_ref))
        y_ref[0] = x_ref[0] + out * _rms_scale(out) * npost_ref[...]

    @pl.when(first_step)
    def _first_tile():
        run_tile(lazy_weights=True)
        for k in range(len(groups)):
            store_copy(k).start()

    @pl.when(jnp.logical_not(first_step))
    def _tile():
        run_tile(lazy_weights=False)

    @pl.when(t == pl.num_programs(1) - 1)
    def _finish():
        c_out_ref[0] = u_scr[CTX_ROW0:SUBLANES, :]
        st_out_ref[0] = st_scr[...]

    @pl.when(last_step)
    def _weights_written():
        for i in range(len(groups)):
            store_copy(i).wait()


def _const_spec(shape):
    zeros = (0,) * len(shape)
    return pl.BlockSpec(shape, lambda b, t: zeros, pipeline_mode=pl.Buffered(1))


def _run_prompt(x, meta, weights, *, tile, chunk):
    n, length, _ = x.shape
    assert length % tile == 0 and tile % chunk == 0 and chunk % (2 * SUBLANES) == 0 and tile >= CONV_W - 1
    assert meta.shape[0] % (2 * SUBLANES) == 0 and meta.shape[0] >= CONV_W - 1
    assert tile // chunk <= HEAD_DIM
    w_in, npre, npost, lb_logits, hn, cw, wa, wb, wo = weights
    kern = functools.partial(_seq_kernel, tile=tile, chunk=chunk)
    hbm = pl.BlockSpec(memory_space=pl.ANY)
    n_groups = w_in.shape[1] // D_MODEL + 3
    assert w_in.shape[1] % D_MODEL == 0 and wa.shape == wb.shape == wo.shape == (w_in.shape[0], D_MODEL)
    state_shape = (1, N_HEADS, HEAD_DIM, HEAD_DIM)
    ctx_shape = (1, CONV_W - 1, D_MODEL)
    tile_f32 = pltpu.VMEM((tile, D_MODEL), F32)
    tile_bf16 = pltpu.VMEM((tile, D_MODEL), BF16)
    return pl.pallas_call(
        kern,
        grid=(n, length // tile),
        in_specs=[
            pl.BlockSpec((1, tile, D_MODEL), lambda b, t: (b, t, 0)),
            _const_spec(meta.shape),
            hbm,
            _const_spec(npre.shape),
            _const_spec(npost.shape),
            _const_spec(lb_logits.shape),
            _const_spec(hn.shape),
            _const_spec(cw.shape),
            hbm, hbm, hbm,
        ],
        out_specs=[
            pl.BlockSpec((1, tile, D_MODEL), lambda b, t: (b, t, 0)),
            pl.BlockSpec(state_shape, lambda b, t: (b, 0, 0, 0)),
            pl.BlockSpec(ctx_shape, lambda b, t: (b, 0, 0)),
            hbm, hbm, hbm, hbm,
        ],
        out_shape=[
            jax.ShapeDtypeStruct(x.shape, F32),
            jax.ShapeDtypeStruct((n,) + state_shape[1:], F32),
            jax.ShapeDtypeStruct((n,) + ctx_shape[1:], F32),
            jax.ShapeDtypeStruct(w_in.shape, BF16),
            jax.ShapeDtypeStruct(wa.shape, BF16),
            jax.ShapeDtypeStruct(wb.shape, BF16),
            jax.ShapeDtypeStruct(wo.shape, BF16),
        ],
        scratch_shapes=[
            pltpu.VMEM((N_HEADS, HEAD_DIM, HEAD_DIM), F32),
            pltpu.VMEM((tile + SUBLANES, D_MODEL), F32),
            tile_f32, tile_f32, tile_f32, tile_f32, tile_f32,
            tile_bf16, tile_bf16, tile_bf16, tile_bf16, tile_bf16,
            pltpu.VMEM((HEAD_DIM, D_MODEL), F32),
            pltpu.VMEM((N_HEADS, HEAD_DIM, HEAD_DIM), F32),
            pltpu.VMEM((SUBLANES, D_MODEL), F32),
            pltpu.VMEM(w_in.shape, BF16),
            pltpu.VMEM(wa.shape, BF16), pltpu.VMEM(wb.shape, BF16), pltpu.VMEM(wo.shape, BF16),
            pltpu.VMEM((2,) + wa.shape, F32),
            pltpu.SemaphoreType.DMA((2, LOAD_BANDS)),
            pltpu.SemaphoreType.DMA((n_groups,)),
        ],
        compiler_params=pltpu.CompilerParams(
            dimension_semantics=("arbitrary", "arbitrary"),
            vmem_limit_bytes=V7X_VMEM_LIMIT_BYTES),
        name="prompt_sweep",
    )(x, meta, w_in, npre, npost, lb_logits, hn, cw, wa, wb, wo)


def _decode_kernel(x_ref, st_hbm, ctx_ref, w_in_hbm, npre_ref, npost_ref, lb_ref, hn_ref, cw_ref,
                   wa_hbm, wb_hbm, wo_hbm,
                   y_ref, st_out_hbm, ctx_out_ref,
                   ft_scr, q_scr, v_scr, o_scr, za_scr, pb_scr, ga_scr, gb_scr,
                   in_buf, out_buf, in_sem, out_sem, w_in_ref, wa_ref, wb_ref, wo_ref, w_sem):
    G = DECODE_GROUP
    i = pl.program_id(0)
    n_steps = pl.num_programs(0)
    n_rows = x_ref.shape[0]
    part_rows = G // STATE_PARTS

    def in_copies(step):
        slot = step % STATE_IN_SLOTS
        return [pltpu.make_async_copy(st_hbm.at[pl.ds(step * G + p * part_rows, part_rows)],
                                      in_buf.at[slot, pl.ds(p * part_rows, part_rows)],
                                      in_sem.at[slot, p]) for p in range(STATE_PARTS)]

    def out_copies(step):
        slot = step % STATE_OUT_SLOTS
        return [pltpu.make_async_copy(out_buf.at[slot, pl.ds(p * part_rows, part_rows)],
                                      st_out_hbm.at[pl.ds(step * G + p * part_rows, part_rows)],
                                      out_sem.at[slot, p]) for p in range(STATE_PARTS)]

    n_proj = w_in_hbm.shape[1] // D_MODEL
    w_parts = [(w_in_hbm.at[:, pl.ds(j * D_MODEL, D_MODEL)], w_in_ref.at[:, pl.ds(j * D_MODEL, D_MODEL)])
               for j in range(n_proj)] + [(wa_hbm, wa_ref), (wb_hbm, wb_ref), (wo_hbm, wo_ref)]
    W_A, W_B, W_O = n_proj, n_proj + 1, n_proj + 2

    def w_copy(k):
        return pltpu.make_async_copy(w_parts[k][0], w_parts[k][1], w_sem.at[k])

    @pl.when(i == 0)
    def _prime():
        for step in range(STATE_IN_SLOTS - 1):
            for copy in in_copies(step):
                copy.start()
        for k in (P_F, P_Q, P_I, P_ZA, P_C, P_H, P_B, P_ZB, P_GA, P_GB, W_A, W_B, W_O):
            w_copy(k).start()

    @pl.when(i == 0)
    def _project():
        x = x_ref[:, 0, :]
        xn = (x * _rms_scale(x) * npre_ref[...]).astype(BF16)

        def proj(j):
            w_copy(j).wait()
            return _dot(xn, w_in_ref[:, j * D_MODEL:(j + 1) * D_MODEL])

        lb = _lower_bound(lb_ref[...])
        f = lb + (1.0 - lb) * _sigmoid(proj(P_F))
        ft_scr[...] = f.T
        q_scr[...] = _silu(proj(P_Q))
        v_scr[...] = proj(P_I)
        za_scr[...] = _silu(proj(P_ZA))
        u = proj(P_C) * proj(P_H)
        cw = cw_ref[...]
        conv = cw[CONV_W - 1:CONV_W, :] * u
        for j in range(CONV_W - 1):
            ctx_j = ctx_ref[:, j, :]
            conv = conv + cw[j:j + 1, :] * ctx_j
            if j > 0:
                ctx_out_ref[:, j - 1, :] = ctx_j
        ctx_out_ref[:, CONV_W - 2, :] = u
        pb_scr[...] = proj(P_B) * conv * _silu(proj(P_ZB))
        ga_scr[...] = _sigmoid(proj(P_GA))
        gb_scr[...] = _sigmoid(proj(P_GB))

    @pl.when(i + STATE_IN_SLOTS - 1 < n_steps)
    def _read_ahead():
        for copy in in_copies(i + STATE_IN_SLOTS - 1):
            copy.start()

    @pl.when(i >= STATE_OUT_SLOTS)
    def _slot_written_back():
        for copy in out_copies(i - STATE_OUT_SLOTS):
            copy.wait()

    for copy in in_copies(i):
        copy.wait()
    st_ref = in_buf.at[i % STATE_IN_SLOTS]
    st_out_ref = out_buf.at[i % STATE_OUT_SLOTS]

    shift = (n_rows - i * G) % n_rows
    f_cols = pltpu.roll(ft_scr[...], shift, 1)
    r0 = pl.multiple_of(i * G, G)
    v_rows = v_scr[pl.ds(r0, G), :]
    q_rows = q_scr[pl.ds(r0, G), :].astype(BF16)
    row_id = lax.broadcasted_iota(jnp.int32, (G, HEAD_DIM), 0)
    o_heads = [jnp.zeros((G, HEAD_DIM), F32)] * N_HEADS
    for j in range(G):
        for h, sl in enumerate(HEADS):
            f_b = jnp.broadcast_to(f_cols[sl, j:j + 1], (HEAD_DIM, HEAD_DIM))
            s_new = f_b * st_ref[j, h] + (1.0 - f_b) * v_rows[j:j + 1, sl]
            st_out_ref[j, h] = s_new
            read = _dot(q_rows[:, sl], s_new.astype(BF16))
            o_heads[h] = jnp.where(row_id == j, read, o_heads[h])
    o_scr[pl.ds(r0, G), :] = jnp.concatenate(o_heads, axis=1)
    for copy in out_copies(i):
        copy.start()

    @pl.when(i == pl.num_programs(0) - 1)
    def _output():
        hn = hn_ref[...]
        for sl in HEADS:
            o_h = o_scr[:, sl]
            o_scr[:, sl] = o_h * _rms_scale(o_h) * hn
        for k in (W_A, W_B, W_O):
            w_copy(k).wait()
        y_a = _dot((o_scr[...] * za_scr[...]).astype(BF16), wa_ref[...])
        y_b = _dot(pb_scr[...].astype(BF16), wb_ref[...])
        merged = ga_scr[...] * y_a + gb_scr[...] * y_b
        out = _dot(merged.astype(BF16), wo_ref[...])
        y_ref[:, 0, :] = x_ref[:, 0, :] + out * _rms_scale(out) * npost_ref[...]
        for back in range(STATE_OUT_SLOTS):
            for copy in out_copies(i - back):
                copy.wait()


def _run_decode(x, state, ctx, weights):
    n = x.shape[0]
    assert n % DECODE_GROUP == 0 and n == 128 and DECODE_GROUP % STATE_PARTS == 0
    assert n // DECODE_GROUP >= max(STATE_IN_SLOTS, STATE_OUT_SLOTS)
    w_in, npre, npost, lb_logits, hn, cw, wa, wb, wo = weights

    def const(shape):
        zeros = (0,) * len(shape)
        return pl.BlockSpec(shape, lambda i: zeros, pipeline_mode=pl.Buffered(1))

    hbm = pl.BlockSpec(memory_space=pl.ANY)
    in_slots = pltpu.VMEM((STATE_IN_SLOTS, DECODE_GROUP) + state.shape[1:], F32)
    out_slots = pltpu.VMEM((STATE_OUT_SLOTS, DECODE_GROUP) + state.shape[1:], F32)
    rows_f32 = pltpu.VMEM((n, D_MODEL), F32)
    cols_f32 = pltpu.VMEM((D_MODEL, n), F32)
    return pl.pallas_call(
        _decode_kernel,
        grid=(n // DECODE_GROUP,),
        in_specs=[const(x.shape), hbm, const(ctx.shape), hbm, const(npre.shape),
                  const(npost.shape), const(lb_logits.shape), const(hn.shape), const(cw.shape),
                  hbm, hbm, hbm],
        out_specs=[const(x.shape), hbm, const(ctx.shape)],
        out_shape=[jax.ShapeDtypeStruct(x.shape, F32),
                   jax.ShapeDtypeStruct(state.shape, F32),
                   jax.ShapeDtypeStruct(ctx.shape, F32)],
        scratch_shapes=[cols_f32, rows_f32, rows_f32, rows_f32, rows_f32, rows_f32, rows_f32, rows_f32,
                        in_slots, out_slots,
                        pltpu.SemaphoreType.DMA((STATE_IN_SLOTS, STATE_PARTS)),
                        pltpu.SemaphoreType.DMA((STATE_OUT_SLOTS, STATE_PARTS)),
                        pltpu.VMEM(w_in.shape, BF16), pltpu.VMEM(wa.shape, BF16),
                        pltpu.VMEM(wb.shape, BF16), pltpu.VMEM(wo.shape, BF16),
                        pltpu.SemaphoreType.DMA((w_in.shape[1] // D_MODEL + 3,))],
        compiler_params=pltpu.CompilerParams(
            dimension_semantics=("arbitrary",),
            vmem_limit_bytes=V7X_VMEM_LIMIT_BYTES),
        name="decode_step",
    )(x, state, ctx, w_in, npre, npost, lb_logits, hn, cw, wa, wb, wo)


def kernel(x_prompt, x_sample, state_hgrn, state_conv, meta_tokens, w_in, norm_pre, norm_post, lb_logits,
           hgrn_norm, conv_w, w_a, w_b, w_o):
    depth = w_in.shape[0]
    assert depth == 1, "single-layer trunk"
    assert x_sample.shape[1] == 1, "one new token per decode row"

    weights = (w_in[0], norm_pre, norm_post, lb_logits, hgrn_norm, conv_w[0], w_a[0], w_b[0], w_o[0])

    y_prompt, hgrn_p, conv_p, w_in_bf, wa_bf, wb_bf, wo_bf = _run_prompt(
        x_prompt, meta_tokens.astype(x_prompt.dtype), weights, tile=PROMPT_TILE, chunk=PROMPT_CHUNK)
    weights = (w_in_bf,) + weights[1:6] + (wa_bf, wb_bf, wo_bf)

    y_s, hgrn_s, conv_s = _run_decode(x_sample, state_hgrn[0], state_conv[0], weights)

    return (y_prompt, y_s, hgrn_p[None], hgrn_s[None], conv_p[None], conv_s[None])
```

```python
import functools

import jax
import jax.numpy as jnp
from jax import lax
from jax.experimental import pallas as pl
from jax.experimental.pallas import tpu as pltpu

D_MODEL = 1024
N_HEADS = 8
HEAD_DIM = D_MODEL // N_HEADS
CONV_W = 3
EPS = 1e-6
P_Q, P_F, P_I, P_ZA, P_B, P_C, P_H, P_ZB, P_GA, P_GB = range(10)

PROMPT_TILE = 256
PROMPT_CHUNK = 128
DECODE_GROUP = 8
STATE_IN_SLOTS = 3
STATE_OUT_SLOTS = 2
STATE_PARTS = 8
LOAD_BANDS = 4
SUBLANES = 8
CTX_ROW0 = SUBLANES - (CONV_W - 1)
MAX_HALF_CHUNK_LOG2_DECAY = 115.0
V7X_VMEM_LIMIT_BYTES = 58 * 1024 * 1024

BF16 = jnp.bfloat16
F32 = jnp.float32
HEADS = [slice(h * HEAD_DIM, (h + 1) * HEAD_DIM) for h in range(N_HEADS)]


def _dot(a, b):
    return jnp.dot(a, b, preferred_element_type=F32)


def _dot_nt(a, b):
    return lax.dot_general(a, b, (((1,), (1,)), ((), ())), preferred_element_type=F32)


def _dot_tn(a, b):
    return lax.dot_general(a, b, (((0,), (0,)), ((), ())), preferred_element_type=F32)


def _sigmoid(x):
    return 1.0 / (1.0 + jnp.exp(-x))


def _silu(x):
    return x * _sigmoid(x)


def _rms_scale(x):
    return lax.rsqrt(jnp.mean(x * x, axis=-1, keepdims=True) + EPS)


def _lower_bound(lb_logits):
    m = jnp.max(lb_logits, axis=0, keepdims=True)
    e = jnp.exp(lb_logits - m)
    return e[0:1, :] / jnp.sum(e, axis=0, keepdims=True)


def _block_cumsum(x, block):
    n = x.shape[0]
    ri = lax.broadcasted_iota(jnp.int32, (n, n), 0)
    ci = lax.broadcasted_iota(jnp.int32, (n, n), 1)
    tri = ri >= ci
    if n != block:
        tri = jnp.logical_and(tri, (ri // block) == (ci // block))
    tri = jnp.where(tri, 1.0, 0.0).astype(BF16)
    hi = x.astype(BF16)
    lo = (x - hi.astype(F32)).astype(BF16)
    return _dot(tri, hi) + _dot(tri, lo)


def _seq_kernel(x_ref, meta_ref, w_in_hbm, npre_ref, npost_ref, lb_ref, hn_ref, cw_ref,
                wa_hbm, wb_hbm, wo_hbm,
                y_ref, st_out_ref, c_out_ref, w_in_out, wa_out, wb_out, wo_out,
                st_scr, u_scr, q_scr, k_scr, v_scr, g_scr, o_scr,
                qt_scr, kt_scr, qin_scr, kst_scr, vb_scr, dec_scr, st_meta_scr, ctx_meta_scr,
                w_in_ref, wa_ref, wb_ref, wo_ref, stage_scr, load_sem, store_sem,
                *, tile, chunk):
    T, C = tile, chunk
    n_chunks = T // C
    t = pl.program_id(1)
    first_step = jnp.logical_and(pl.program_id(0) == 0, t == 0)
    last_step = jnp.logical_and(pl.program_id(0) == pl.num_programs(0) - 1, t == pl.num_programs(1) - 1)
    lb = _lower_bound(lb_ref[...])

    col = [pl.ds(j * D_MODEL, D_MODEL) for j in range(w_in_hbm.shape[1] // D_MODEL)]
    groups = [(w_in_hbm.at[:, c], w_in_ref.at[:, c], w_in_out.at[:, c]) for c in col]
    groups += [(wa_hbm, wa_ref, wa_out), (wb_hbm, wb_ref, wb_out), (wo_hbm, wo_ref, wo_out)]

    def load_copies(p):
        src = groups[fetch_order[p]][0]
        band = src.shape[0] // LOAD_BANDS
        return [pltpu.make_async_copy(src.at[pl.ds(b * band, band), :],
                                      stage_scr.at[p % 2, pl.ds(b * band, band), :],
                                      load_sem.at[p % 2, b]) for b in range(LOAD_BANDS)]

    def store_copy(i):
        return pltpu.make_async_copy(groups[i][1], groups[i][2], store_sem.at[i])

    W_A, W_B, W_O = len(col), len(col) + 1, len(col) + 2
    fetch_order = [P_F, P_I, P_C, P_H, P_Q, P_ZA, W_A, P_B, P_ZB, W_B, P_GA, P_GB, W_O]
    n_eager = 4
    assert sorted(fetch_order) == list(range(len(groups)))

    def fetch(k):
        p = fetch_order.index(k)
        for copy in load_copies(p):
            copy.wait()
        groups[k][1][...] = stage_scr[p % 2].astype(BF16)
        if p + 2 < len(fetch_order):
            for copy in load_copies(p + 2):
                copy.start()

    @pl.when(first_step)
    def _stream_first_weights():
        for p in range(2):
            for copy in load_copies(p):
                copy.start()

    def project(xn, j):
        return _dot(xn, w_in_ref[:, j * D_MODEL:(j + 1) * D_MODEL])

    @pl.when(first_step)
    def _meta_prefix():
        xm = meta_ref[...]
        n_meta = xm.shape[0]
        xn_m = (xm * _rms_scale(xm) * npre_ref[...]).astype(BF16)

        def meta_proj(j):
            fetch(j)
            return project(xn_m, j)

        f_m = lb + (1.0 - lb) * _sigmoid(meta_proj(P_F))
        g_m = _block_cumsum(jnp.log2(f_m), n_meta)
        k_end = ((1.0 - f_m) * jnp.exp2(g_m[n_meta - 1:n_meta, :] - g_m)).astype(BF16)
        v_m = meta_proj(P_I).astype(BF16)
        for h, sl in enumerate(HEADS):
            st_meta_scr[h] = _dot_tn(k_end[:, sl], v_m[:, sl])
        u_m = meta_proj(P_C) * meta_proj(P_H)
        ctx_meta_scr[CTX_ROW0:SUBLANES, :] = u_m[n_meta - (CONV_W - 1):n_meta, :]
        dec_scr[...] = jnp.zeros(dec_scr.shape, F32)

    @pl.when(t == 0)
    def _init():
        st_scr[...] = st_meta_scr[...]
        u_scr[CTX_ROW0:SUBLANES, :] = ctx_meta_scr[CTX_ROW0:SUBLANES, :]

    def run_tile(lazy_weights):
        fetched = set()
        x = x_ref[0]
        xn = (x * _rms_scale(x) * npre_ref[...]).astype(BF16)

        def proj(j):
            if lazy_weights and j in fetch_order[n_eager:] and j not in fetched:
                fetched.add(j)
                fetch(j)
            return project(xn, j)

        def late_weight(k, ref):
            if lazy_weights:
                fetch(k)
            return ref[...]

        f = lb + (1.0 - lb) * _sigmoid(proj(P_F))
        k_scr[...] = 1.0 - f
        q_scr[...] = _silu(proj(P_Q))
        vb_scr[...] = proj(P_I).astype(BF16)
        g_scr[...] = _block_cumsum(jnp.log2(f), C)

        g_floor = None
        for c in range(n_chunks):
            rows = slice(c * C, (c + 1) * C)
            gc = g_scr[rows, :]
            g_last = gc[C - 1:C, :]
            g_mid = gc[C // 2 - 1:C // 2, :]
            qc = q_scr[rows, :]
            kc = k_scr[rows, :]
            qt_scr[rows, :] = (qc * jnp.exp2(gc - g_mid)).astype(BF16)
            kt_scr[rows, :] = (kc * jnp.exp2(g_mid - gc)).astype(BF16)
            qin_scr[rows, :] = (qc * jnp.exp2(gc)).astype(BF16)
            kst_scr[rows, :] = (kc * jnp.exp2(g_last - gc)).astype(BF16)
            dec_scr[c:c + 1, :] = jnp.exp2(g_last)
            half_floor = jnp.minimum(g_mid, g_last - g_mid)
            g_floor = half_floor if g_floor is None else jnp.minimum(g_floor, half_floor)
        stable = jnp.min(g_floor) >= -MAX_HALF_CHUNK_LOG2_DECAY

        dec_cols = dec_scr[...].T
        keep = jnp.logical_and(
            lax.broadcasted_iota(jnp.int32, (C, C), 0) >= lax.broadcasted_iota(jnp.int32, (C, C), 1), stable)
        for c in range(n_chunks):
            rows = slice(c * C, (c + 1) * C)
            scores = [jnp.where(keep, _dot_nt(qt_scr[rows, sl], kt_scr[rows, sl]), 0.0).astype(BF16)
                      for sl in HEADS]
            for h, sl in enumerate(HEADS):
                vb = vb_scr[rows, sl]
                st = st_scr[h]
                o_scr[rows, sl] = _dot(jnp.concatenate([qin_scr[rows, sl], scores[h]], axis=1),
                                       jnp.concatenate([st.astype(BF16), vb], axis=0))
                decay = jnp.broadcast_to(dec_cols[sl, c:c + 1], (HEAD_DIM, HEAD_DIM))
                st_scr[h] = decay * st + _dot_tn(kst_scr[rows, sl], vb)

        @pl.when(jnp.logical_not(stable))
        def _intra_exact():
            same_head = (lax.broadcasted_iota(jnp.int32, (D_MODEL, D_MODEL), 0) // HEAD_DIM ==
                         lax.broadcasted_iota(jnp.int32, (D_MODEL, D_MODEL), 1) // HEAD_DIM)
            head_sum = jnp.where(same_head, 1.0, 0.0).astype(BF16)
            tpos = lax.broadcasted_iota(jnp.int32, (C, 1), 0)
            v_scr[...] = proj(P_I)

            def chunk_step(c, carry):
                r0 = pl.multiple_of(c * C, C)
                rows = pl.ds(r0, C)
                gc = g_scr[rows, :]
                qc = q_scr[rows, :]

                def src_step(i, carry2):
                    src = pl.ds(r0 + i, 1)
                    p = qc * jnp.exp2(jnp.minimum(gc - g_scr[src, :], 0.0)) * k_scr[src, :]
                    p = jnp.where(tpos >= i, p, 0.0).astype(BF16)
                    o_scr[rows, :] += _dot(p, head_sum) * v_scr[src, :]
                    return carry2

                return lax.fori_loop(0, C, src_step, carry)

            lax.fori_loop(0, n_chunks, chunk_step, 0)

        hn = hn_ref[...]
        silu_za = _silu(proj(P_ZA))
        gated = []
        for sl in HEADS:
            o_h = o_scr[:, sl]
            gated.append((o_h * _rms_scale(o_h) * hn * silu_za[:, sl]).astype(BF16))
        y_a = _dot(jnp.concatenate(gated, axis=1), late_weight(W_A, wa_ref))

        u = proj(P_C) * proj(P_H)
        u_scr[SUBLANES:SUBLANES + T, :] = u
        cw = cw_ref[...]
        conv = cw[CONV_W - 1:CONV_W, :] * u
        for j in range(CONV_W - 1):
            conv = conv + cw[j:j + 1, :] * u_scr[CTX_ROW0 + j:CTX_ROW0 + j + T, :]
        y_b = _dot((proj(P_B) * conv * _silu(proj(P_ZB))).astype(BF16), late_weight(W_B, wb_ref))
        u_scr[CTX_ROW0:SUBLANES, :] = u[T - (CONV_W - 1):T, :]

        merged = _sigmoid(proj(P_GA)) * y_a + _sigmoid(proj(P_GB)) * y_b
        out = _dot(merged.astype(BF16), late_weight(W_O, wo_ref))
        y_ref[0] = x_ref[0] + out * _rms_scale(out) * npost_ref[...]

    @pl.when(first_step)
    def _first_tile():
        run_tile(lazy_weights=True)
        for k in range(len(groups)):
            store_copy(k).start()

    @pl.when(jnp.logical_not(first_step))
    def _tile():
        run_tile(lazy_weights=False)

    @pl.when(t == pl.num_programs(1) - 1)
    def _finish():
        c_out_ref[0] = u_scr[CTX_ROW0:SUBLANES, :]
        st_out_ref[0] = st_scr[...]

    @pl.when(last_step)
    def _weights_written():
        for i in range(len(groups)):
            store_copy(i).wait()


def _const_spec(shape):
    zeros = (0,) * len(shape)
    return pl.BlockSpec(shape, lambda b, t: zeros, pipeline_mode=pl.Buffered(1))


def _run_prompt(x, meta, weights, *, tile, chunk):
    n, length, _ = x.shape
    assert length % tile == 0 and tile % chunk == 0 and chunk % (2 * SUBLANES) == 0 and tile >= CONV_W - 1
    assert meta.shape[0] % (2 * SUBLANES) == 0 and meta.shape[0] >= CONV_W - 1
    assert tile // chunk <= HEAD_DIM
    w_in, npre, npost, lb_logits, hn, cw, wa, wb, wo = weights
    kern = functools.partial(_seq_kernel, tile=tile, chunk=chunk)
    hbm = pl.BlockSpec(memory_space=pl.ANY)
    n_groups = w_in.shape[1] // D_MODEL + 3
    assert w_in.shape[1] % D_MODEL == 0 and wa.shape == wb.shape == wo.shape == (w_in.shape[0], D_MODEL)
    state_shape = (1, N_HEADS, HEAD_DIM, HEAD_DIM)
    ctx_shape = (1, CONV_W - 1, D_MODEL)
    tile_f32 = pltpu.VMEM((tile, D_MODEL), F32)
    tile_bf16 = pltpu.VMEM((tile, D_MODEL), BF16)
    return pl.pallas_call(
        kern,
        grid=(n, length // tile),
        in_specs=[
            pl.BlockSpec((1, tile, D_MODEL), lambda b, t: (b, t, 0)),
            _const_spec(meta.shape),
            hbm,
            _const_spec(npre.shape),
            _const_spec(npost.shape),
            _const_spec(lb_logits.shape),
            _const_spec(hn.shape),
            _const_spec(cw.shape),
            hbm, hbm, hbm,
        ],
        out_specs=[
            pl.BlockSpec((1, tile, D_MODEL), lambda b, t: (b, t, 0)),
            pl.BlockSpec(state_shape, lambda b, t: (b, 0, 0, 0)),
            pl.BlockSpec(ctx_shape, lambda b, t: (b, 0, 0)),
            hbm, hbm, hbm, hbm,
        ],
        out_shape=[
            jax.ShapeDtypeStruct(x.shape, F32),
            jax.ShapeDtypeStruct((n,) + state_shape[1:], F32),
            jax.ShapeDtypeStruct((n,) + ctx_shape[1:], F32),
            jax.ShapeDtypeStruct(w_in.shape, BF16),
            jax.ShapeDtypeStruct(wa.shape, BF16),
            jax.ShapeDtypeStruct(wb.shape, BF16),
            jax.ShapeDtypeStruct(wo.shape, BF16),
        ],
        scratch_shapes=[
            pltpu.VMEM((N_HEADS, HEAD_DIM, HEAD_DIM), F32),
            pltpu.VMEM((tile + SUBLANES, D_MODEL), F32),
            tile_f32, tile_f32, tile_f32, tile_f32, tile_f32,
            tile_bf16, tile_bf16, tile_bf16, tile_bf16, tile_bf16,
            pltpu.VMEM((HEAD_DIM, D_MODEL), F32),
            pltpu.VMEM((N_HEADS, HEAD_DIM, HEAD_DIM), F32),
            pltpu.VMEM((SUBLANES, D_MODEL), F32),
            pltpu.VMEM(w_in.shape, BF16),
            pltpu.VMEM(wa.shape, BF16), pltpu.VMEM(wb.shape, BF16), pltpu.VMEM(wo.shape, BF16),
            pltpu.VMEM((2,) + wa.shape, F32),
            pltpu.SemaphoreType.DMA((2, LOAD_BANDS)),
            pltpu.SemaphoreType.DMA((n_groups,)),
        ],
        compiler_params=pltpu.CompilerParams(
            dimension_semantics=("arbitrary", "arbitrary"),
            vmem_limit_bytes=V7X_VMEM_LIMIT_BYTES),
        name="prompt_sweep",
    )(x, meta, w_in, npre, npost, lb_logits, hn, cw, wa, wb, wo)


def _decode_kernel(x_ref, st_hbm, ctx_ref, w_in_hbm, npre_ref, npost_ref, lb_ref, hn_ref, cw_ref,
                   wa_hbm, wb_hbm, wo_hbm,
                   y_ref, st_out_hbm, ctx_out_ref,
                   ft_scr, q_scr, v_scr, o_scr, za_scr, pb_scr, ga_scr, gb_scr,
                   in_buf, out_buf, in_sem, out_sem, w_in_ref, wa_ref, wb_ref, wo_ref, w_sem):
    G = DECODE_GROUP
    i = pl.program_id(0)
    n_steps = pl.num_programs(0)
    n_rows = x_ref.shape[0]
    part_rows = G // STATE_PARTS

    def in_copies(step):
        slot = step % STATE_IN_SLOTS
        return [pltpu.make_async_copy(st_hbm.at[pl.ds(step * G + p * part_rows, part_rows)],
                                      in_buf.at[slot, pl.ds(p * part_rows, part_rows)],
                                      in_sem.at[slot, p]) for p in range(STATE_PARTS)]

    def out_copies(step):
        slot = step % STATE_OUT_SLOTS
        return [pltpu.make_async_copy(out_buf.at[slot, pl.ds(p * part_rows, part_rows)],
                                      st_out_hbm.at[pl.ds(step * G + p * part_rows, part_rows)],
                                      out_sem.at[slot, p]) for p in range(STATE_PARTS)]

    n_proj = w_in_hbm.shape[1] // D_MODEL
    w_parts = [(w_in_hbm.at[:, pl.ds(j * D_MODEL, D_MODEL)], w_in_ref.at[:, pl.ds(j * D_MODEL, D_MODEL)])
               for j in range(n_proj)] + [(wa_hbm, wa_ref), (wb_hbm, wb_ref), (wo_hbm, wo_ref)]
    W_A, W_B, W_O = n_proj, n_proj + 1, n_proj + 2

    def w_copy(k):
        return pltpu.make_async_copy(w_parts[k][0], w_parts[k][1], w_sem.at[k])

    @pl.when(i == 0)
    def _prime():
        for step in range(STATE_IN_SLOTS - 1):
            for copy in in_copies(step):
                copy.start()
        for k in (P_F, P_Q, P_I, P_ZA, P_C, P_H, P_B, P_ZB, P_GA, P_GB, W_A, W_B, W_O):
            w_copy(k).start()

    @pl.when(i == 0)
    def _project():
        x = x_ref[:, 0, :]
        xn = (x * _rms_scale(x) * npre_ref[...]).astype(BF16)

        def proj(j):
            w_copy(j).wait()
            return _dot(xn, w_in_ref[:, j * D_MODEL:(j + 1) * D_MODEL])

        lb = _lower_bound(lb_ref[...])
        f = lb + (1.0 - lb) * _sigmoid(proj(P_F))
        ft_scr[...] = f.T
        q_scr[...] = _silu(proj(P_Q))
        v_scr[...] = proj(P_I)
        za_scr[...] = _silu(proj(P_ZA))
        u = proj(P_C) * proj(P_H)
        cw = cw_ref[...]
        conv = cw[CONV_W - 1:CONV_W, :] * u
        for j in range(CONV_W - 1):
            ctx_j = ctx_ref[:, j, :]
            conv = conv + cw[j:j + 1, :] * ctx_j
            if j > 0:
                ctx_out_ref[:, j - 1, :] = ctx_j
        ctx_out_ref[:, CONV_W - 2, :] = u
        pb_scr[...] = proj(P_B) * conv * _silu(proj(P_ZB))
        ga_scr[...] = _sigmoid(proj(P_GA))
        gb_scr[...] = _sigmoid(proj(P_GB))

    @pl.when(i + STATE_IN_SLOTS - 1 < n_steps)
    def _read_ahead():
        for copy in in_copies(i + STATE_IN_SLOTS - 1):
            copy.start()

    @pl.when(i >= STATE_OUT_SLOTS)
    def _slot_written_back():
        for copy in out_copies(i - STATE_OUT_SLOTS):
            copy.wait()

    for copy in in_copies(i):
        copy.wait()
    st_ref = in_buf.at[i % STATE_IN_SLOTS]
    st_out_ref = out_buf.at[i % STATE_OUT_SLOTS]

    shift = (n_rows - i * G) % n_rows
    f_cols = pltpu.roll(ft_scr[...], shift, 1)
    r0 = pl.multiple_of(i * G, G)
    v_rows = v_scr[pl.ds(r0, G), :]
    q_rows = q_scr[pl.ds(r0, G), :].astype(BF16)
    row_id = lax.broadcasted_iota(jnp.int32, (G, HEAD_DIM), 0)
    o_heads = [jnp.zeros((G, HEAD_DIM), F32)] * N_HEADS
    for j in range(G):
        for h, sl in enumerate(HEADS):
            f_b = jnp.broadcast_to(f_cols[sl, j:j + 1], (HEAD_DIM, HEAD_DIM))
            s_new = f_b * st_ref[j, h] + (1.0 - f_b) * v_rows[j:j + 1, sl]
            st_out_ref[j, h] = s_new
            read = _dot(q_rows[:, sl], s_new.astype(BF16))
            o_heads[h] = jnp.where(row_id == j, read, o_heads[h])
    o_scr[pl.ds(r0, G), :] = jnp.concatenate(o_heads, axis=1)
    for copy in out_copies(i):
        copy.start()

    @pl.when(i == pl.num_programs(0) - 1)
    def _output():
        hn = hn_ref[...]
        for sl in HEADS:
            o_h = o_scr[:, sl]
            o_scr[:, sl] = o_h * _rms_scale(o_h) * hn
        for k in (W_A, W_B, W_O):
            w_copy(k).wait()
        y_a = _dot((o_scr[...] * za_scr[...]).astype(BF16), wa_ref[...])
        y_b = _dot(pb_scr[...].astype(BF16), wb_ref[...])
        merged = ga_scr[...] * y_a + gb_scr[...] * y_b
        out = _dot(merged.astype(BF16), wo_ref[...])
        y_ref[:, 0, :] = x_ref[:, 0, :] + out * _rms_scale(out) * npost_ref[...]
        for back in range(STATE_OUT_SLOTS):
            for copy in out_copies(i - back):
                copy.wait()


def _run_decode(x, state, ctx, weights):
    n = x.shape[0]
    assert n % DECODE_GROUP == 0 and n == 128 and DECODE_GROUP % STATE_PARTS == 0
    assert n // DECODE_GROUP >= max(STATE_IN_SLOTS, STATE_OUT_SLOTS)
    w_in, npre, npost, lb_logits, hn, cw, wa, wb, wo = weights

    def const(shape):
        zeros = (0,) * len(shape)
        return pl.BlockSpec(shape, lambda i: zeros, pipeline_mode=pl.Buffered(1))

    hbm = pl.BlockSpec(memory_space=pl.ANY)
    in_slots = pltpu.VMEM((STATE_IN_SLOTS, DECODE_GROUP) + state.shape[1:], F32)
    out_slots = pltpu.VMEM((STATE_OUT_SLOTS, DECODE_GROUP) + state.shape[1:], F32)
    rows_f32 = pltpu.VMEM((n, D_MODEL), F32)
    cols_f32 = pltpu.VMEM((D_MODEL, n), F32)
    return pl.pallas_call(
        _decode_kernel,
        grid=(n // DECODE_GROUP,),
        in_specs=[const(x.shape), hbm, const(ctx.shape), hbm, const(npre.shape),
                  const(npost.shape), const(lb_logits.shape), const(hn.shape), const(cw.shape),
                  hbm, hbm, hbm],
        out_specs=[const(x.shape), hbm, const(ctx.shape)],
        out_shape=[jax.ShapeDtypeStruct(x.shape, F32),
                   jax.ShapeDtypeStruct(state.shape, F32),
                   jax.ShapeDtypeStruct(ctx.shape, F32)],
        scratch_shapes=[cols_f32, rows_f32, rows_f32, rows_f32, rows_f32, rows_f32, rows_f32, rows_f32,
                        in_slots, out_slots,
                        pltpu.SemaphoreType.DMA((STATE_IN_SLOTS, STATE_PARTS)),
                        pltpu.SemaphoreType.DMA((STATE_OUT_SLOTS, STATE_PARTS)),
                        pltpu.VMEM(w_in.shape, BF16), pltpu.VMEM(wa.shape, BF16),
                        pltpu.VMEM(wb.shape, BF16), pltpu.VMEM(wo.shape, BF16),
                        pltpu.SemaphoreType.DMA((w_in.shape[1] // D_MODEL + 3,))],
        compiler_params=pltpu.CompilerParams(
            dimension_semantics=("arbitrary",),
            vmem_limit_bytes=V7X_VMEM_LIMIT_BYTES),
        name="decode_step",
    )(x, state, ctx, w_in, npre, npost, lb_logits, hn, cw, wa, wb, wo)


def kernel(x_prompt, x_sample, state_hgrn, state_conv, meta_tokens, w_in, norm_pre, norm_post, lb_logits,
           hgrn_norm, conv_w, w_a, w_b, w_o):
    depth = w_in.shape[0]
    assert depth == 1, "single-layer trunk"
    assert x_sample.shape[1] == 1, "one new token per decode row"

    weights = (w_in[0], norm_pre, norm_post, lb_logits, hgrn_norm, conv_w[0], w_a[0], w_b[0], w_o[0])

    y_prompt, hgrn_p, conv_p, w_in_bf, wa_bf, wb_bf, wo_bf = _run_prompt(
        x_prompt, meta_tokens.astype(x_prompt.dtype), weights, tile=PROMPT_TILE, chunk=PROMPT_CHUNK)
    weights = (w_in_bf,) + weights[1:6] + (wa_bf, wb_bf, wo_bf)

    y_s, hgrn_s, conv_s = _run_decode(x_sample, state_hgrn[0], state_conv[0], weights)

    return (y_prompt, y_s, hgrn_p[None], hgrn_s[None], conv_p[None], conv_s[None])
```

```python
import functools

import jax
import jax.numpy as jnp
from jax import lax
from jax.experimental import pallas as pl
from jax.experimental.pallas import tpu as pltpu

D_MODEL = 1024
N_HEADS = 8
HEAD_DIM = D_MODEL // N_HEADS
CONV_W = 3
EPS = 1e-6
NEG_LOG2_E = -1.4426950408889634
P_Q, P_F, P_I, P_ZA, P_B, P_C, P_H, P_ZB, P_GA, P_GB = range(10)

PROMPT_TILE = 256
PROMPT_CHUNK = 128
DECODE_GROUP = 8
STATE_IN_SLOTS = 3
STATE_OUT_SLOTS = 2
STATE_PARTS = 8
LOAD_BANDS = 4
SUBLANES = 8
CTX_ROW0 = SUBLANES - (CONV_W - 1)
MAX_HALF_CHUNK_LOG2_DECAY = 115.0
V7X_VMEM_LIMIT_BYTES = 58 * 1024 * 1024

BF16 = jnp.bfloat16
F32 = jnp.float32
HEADS = [slice(h * HEAD_DIM, (h + 1) * HEAD_DIM) for h in range(N_HEADS)]


def _dot(a, b):
    return jnp.dot(a, b, preferred_element_type=F32)


def _dot_nt(a, b):
    return lax.dot_general(a, b, (((1,), (1,)), ((), ())), preferred_element_type=F32)


def _dot_tn(a, b):
    return lax.dot_general(a, b, (((0,), (0,)), ((), ())), preferred_element_type=F32)


def _sigmoid(x):
    return 1.0 / (1.0 + jnp.exp2(x * NEG_LOG2_E))


def _silu(x):
    return x * _sigmoid(x)


def _rms_scale(x):
    return lax.rsqrt(jnp.mean(x * x, axis=-1, keepdims=True) + EPS)


def _lower_bound(lb_logits):
    m = jnp.max(lb_logits, axis=0, keepdims=True)
    e = jnp.exp(lb_logits - m)
    return e[0:1, :] / jnp.sum(e, axis=0, keepdims=True)


def _block_cumsum(x, block):
    n = x.shape[0]
    ri = lax.broadcasted_iota(jnp.int32, (n, n), 0)
    ci = lax.broadcasted_iota(jnp.int32, (n, n), 1)
    tri = ri >= ci
    if n != block:
        tri = jnp.logical_and(tri, (ri // block) == (ci // block))
    tri = jnp.where(tri, 1.0, 0.0).astype(BF16)
    hi = x.astype(BF16)
    lo = (x - hi.astype(F32)).astype(BF16)
    return _dot(tri, hi) + _dot(tri, lo)


def _seq_kernel(x_ref, meta_ref, w_in_hbm, npre_ref, npost_ref, lb_ref, hn_ref, cw_ref,
                wa_hbm, wb_hbm, wo_hbm,
                y_ref, st_out_ref, c_out_ref, w_in_out, wa_out, wb_out, wo_out,
                st_scr, u_scr, q_scr, k_scr, v_scr, g_scr, o_scr,
                qt_scr, kt_scr, qin_scr, kst_scr, vb_scr, dec_scr, st_meta_scr, ctx_meta_scr,
                w_in_ref, wa_ref, wb_ref, wo_ref, stage_scr, load_sem, store_sem,
                *, tile, chunk):
    T, C = tile, chunk
    n_chunks = T // C
    t = pl.program_id(1)
    first_step = jnp.logical_and(pl.program_id(0) == 0, t == 0)
    last_step = jnp.logical_and(pl.program_id(0) == pl.num_programs(0) - 1, t == pl.num_programs(1) - 1)
    lb = _lower_bound(lb_ref[...])

    col = [pl.ds(j * D_MODEL, D_MODEL) for j in range(w_in_hbm.shape[1] // D_MODEL)]
    groups = [(w_in_hbm.at[:, c], w_in_ref.at[:, c], w_in_out.at[:, c]) for c in col]
    groups += [(wa_hbm, wa_ref, wa_out), (wb_hbm, wb_ref, wb_out), (wo_hbm, wo_ref, wo_out)]

    def load_copies(p):
        src = groups[fetch_order[p]][0]
        band = src.shape[0] // LOAD_BANDS
        return [pltpu.make_async_copy(src.at[pl.ds(b * band, band), :],
                                      stage_scr.at[p % 2, pl.ds(b * band, band), :],
                                      load_sem.at[p % 2, b]) for b in range(LOAD_BANDS)]

    def store_copy(i):
        return pltpu.make_async_copy(groups[i][1], groups[i][2], store_sem.at[i])

    W_A, W_B, W_O = len(col), len(col) + 1, len(col) + 2
    fetch_order = [P_F, P_I, P_C, P_H, P_Q, P_ZA, W_A, P_B, P_ZB, W_B, P_GA, P_GB, W_O]
    n_eager = 4
    assert sorted(fetch_order) == list(range(len(groups)))

    def fetch(k):
        p = fetch_order.index(k)
        for copy in load_copies(p):
            copy.wait()
        groups[k][1][...] = stage_scr[p % 2].astype(BF16)
        if p + 2 < len(fetch_order):
            for copy in load_copies(p + 2):
                copy.start()

    @pl.when(first_step)
    def _stream_first_weights():
        for p in range(2):
            for copy in load_copies(p):
                copy.start()

    def project(xn, j):
        return _dot(xn, w_in_ref[:, j * D_MODEL:(j + 1) * D_MODEL])

    @pl.when(first_step)
    def _meta_prefix():
        xm = meta_ref[...]
        n_meta = xm.shape[0]
        xn_m = (xm * _rms_scale(xm) * npre_ref[...]).astype(BF16)

        def meta_proj(j):
            fetch(j)
            return project(xn_m, j)

        f_m = lb + (1.0 - lb) * _sigmoid(meta_proj(P_F))
        g_m = _block_cumsum(jnp.log2(f_m), n_meta)
        k_end = ((1.0 - f_m) * jnp.exp2(g_m[n_meta - 1:n_meta, :] - g_m)).astype(BF16)
        v_m = meta_proj(P_I).astype(BF16)
        for h, sl in enumerate(HEADS):
            st_meta_scr[h] = _dot_tn(k_end[:, sl], v_m[:, sl])
        u_m = meta_proj(P_C) * meta_proj(P_H)
        ctx_meta_scr[CTX_ROW0:SUBLANES, :] = u_m[n_meta - (CONV_W - 1):n_meta, :]
        dec_scr[...] = jnp.zeros(dec_scr.shape, F32)

    @pl.when(t == 0)
    def _init():
        st_scr[...] = st_meta_scr[...]
        u_scr[CTX_ROW0:SUBLANES, :] = ctx_meta_scr[CTX_ROW0:SUBLANES, :]

    def run_tile(lazy_weights):
        fetched = set()
        x = x_ref[0]
        xn = (x * _rms_scale(x) * npre_ref[...]).astype(BF16)

        def proj(j):
            if lazy_weights and j in fetch_order[n_eager:] and j not in fetched:
                fetched.add(j)
                fetch(j)
            return project(xn, j)

        def late_weight(k, ref):
            if lazy_weights:
                fetch(k)
            return ref[...]

        f = lb + (1.0 - lb) * _sigmoid(proj(P_F))
        k_scr[...] = 1.0 - f
        q_scr[...] = _silu(proj(P_Q))
        vb_scr[...] = proj(P_I).astype(BF16)
        g_scr[...] = _block_cumsum(jnp.log2(f), C)

        g_floor = None
        for c in range(n_chunks):
            rows = slice(c * C, (c + 1) * C)
            gc = g_scr[rows, :]
            g_last = gc[C - 1:C, :]
            g_mid = gc[C // 2 - 1:C // 2, :]
            qc = q_scr[rows, :]
            kc = k_scr[rows, :]
            qt_scr[rows, :] = (qc * jnp.exp2(gc - g_mid)).astype(BF16)
            kt_scr[rows, :] = (kc * jnp.exp2(g_mid - gc)).astype(BF16)
            qin_scr[rows, :] = (qc * jnp.exp2(gc)).astype(BF16)
            kst_scr[rows, :] = (kc * jnp.exp2(g_last - gc)).astype(BF16)
            dec_scr[c:c + 1, :] = jnp.exp2(g_last)
            half_floor = jnp.minimum(g_mid, g_last - g_mid)
            g_floor = half_floor if g_floor is None else jnp.minimum(g_floor, half_floor)
        stable = jnp.min(g_floor) >= -MAX_HALF_CHUNK_LOG2_DECAY

        dec_cols = dec_scr[...].T
        keep = jnp.logical_and(
            lax.broadcasted_iota(jnp.int32, (C, C), 0) >= lax.broadcasted_iota(jnp.int32, (C, C), 1), stable)
        for c in range(n_chunks):
            rows = slice(c * C, (c + 1) * C)
            scores = [jnp.where(keep, _dot_nt(qt_scr[rows, sl], kt_scr[rows, sl]), 0.0).astype(BF16)
                      for sl in HEADS]
            for h, sl in enumerate(HEADS):
                vb = vb_scr[rows, sl]
                st = st_scr[h]
                o_scr[rows, sl] = _dot(jnp.concatenate([qin_scr[rows, sl], scores[h]], axis=1),
                                       jnp.concatenate([st.astype(BF16), vb], axis=0))
                decay = jnp.broadcast_to(dec_cols[sl, c:c + 1], (HEAD_DIM, HEAD_DIM))
                st_scr[h] = decay * st + _dot_tn(kst_scr[rows, sl], vb)

        @pl.when(jnp.logical_not(stable))
        def _intra_exact():
            same_head = (lax.broadcasted_iota(jnp.int32, (D_MODEL, D_MODEL), 0) // HEAD_DIM ==
                         lax.broadcasted_iota(jnp.int32, (D_MODEL, D_MODEL), 1) // HEAD_DIM)
            head_sum = jnp.where(same_head, 1.0, 0.0).astype(BF16)
            tpos = lax.broadcasted_iota(jnp.int32, (C, 1), 0)
            v_scr[...] = proj(P_I)

            def chunk_step(c, carry):
                r0 = pl.multiple_of(c * C, C)
                rows = pl.ds(r0, C)
                gc = g_scr[rows, :]
                qc = q_scr[rows, :]

                def src_step(i, carry2):
                    src = pl.ds(r0 + i, 1)
                    p = qc * jnp.exp2(jnp.minimum(gc - g_scr[src, :], 0.0)) * k_scr[src, :]
                    p = jnp.where(tpos >= i, p, 0.0).astype(BF16)
                    o_scr[rows, :] += _dot(p, head_sum) * v_scr[src, :]
                    return carry2

                return lax.fori_loop(0, C, src_step, carry)

            lax.fori_loop(0, n_chunks, chunk_step, 0)

        hn = hn_ref[...]
        silu_za = _silu(proj(P_ZA))
        gated = []
        for sl in HEADS:
            o_h = o_scr[:, sl]
            gated.append((o_h * _rms_scale(o_h) * hn * silu_za[:, sl]).astype(BF16))
        y_a = _dot(jnp.concatenate(gated, axis=1), late_weight(W_A, wa_ref))

        u = proj(P_C) * proj(P_H)
        u_scr[SUBLANES:SUBLANES + T, :] = u
        cw = cw_ref[...]
        conv = cw[CONV_W - 1:CONV_W, :] * u
        for j in range(CONV_W - 1):
            conv = conv + cw[j:j + 1, :] * u_scr[CTX_ROW0 + j:CTX_ROW0 + j + T, :]
        y_b = _dot((proj(P_B) * conv * _silu(proj(P_ZB))).astype(BF16), late_weight(W_B, wb_ref))
        u_scr[CTX_ROW0:SUBLANES, :] = u[T - (CONV_W - 1):T, :]

        merged = _sigmoid(proj(P_GA)) * y_a + _sigmoid(proj(P_GB)) * y_b
        out = _dot(merged.astype(BF16), late_weight(W_O, wo_ref))
        y_ref[0] = x_ref[0] + out * _rms_scale(out) * npost_ref[...]

    @pl.when(first_step)
    def _first_tile():
        run_tile(lazy_weights=True)
        for k in range(len(groups)):
            store_copy(k).start()

    @pl.when(jnp.logical_not(first_step))
    def _tile():
        run_tile(lazy_weights=False)

    @pl.when(t == pl.num_programs(1) - 1)
    def _finish():
        c_out_ref[0] = u_scr[CTX_ROW0:SUBLANES, :]
        st_out_ref[0] = st_scr[...]

    @pl.when(last_step)
    def _weights_written():
        for i in range(len(groups)):
            store_copy(i).wait()


def _const_spec(shape):
    zeros = (0,) * len(shape)
    return pl.BlockSpec(shape, lambda b, t: zeros, pipeline_mode=pl.Buffered(1))


def _run_prompt(x, meta, weights, *, tile, chunk):
    n, length, _ = x.shape
    assert length % tile == 0 and tile % chunk == 0 and chunk % (2 * SUBLANES) == 0 and tile >= CONV_W - 1
    assert meta.shape[0] % (2 * SUBLANES) == 0 and meta.shape[0] >= CONV_W - 1
    assert tile // chunk <= HEAD_DIM
    w_in, npre, npost, lb_logits, hn, cw, wa, wb, wo = weights
    kern = functools.partial(_seq_kernel, tile=tile, chunk=chunk)
    hbm = pl.BlockSpec(memory_space=pl.ANY)
    n_groups = w_in.shape[1] // D_MODEL + 3
    assert w_in.shape[1] % D_MODEL == 0 and wa.shape == wb.shape == wo.shape == (w_in.shape[0], D_MODEL)
    state_shape = (1, N_HEADS, HEAD_DIM, HEAD_DIM)
    ctx_shape = (1, CONV_W - 1, D_MODEL)
    tile_f32 = pltpu.VMEM((tile, D_MODEL), F32)
    tile_bf16 = pltpu.VMEM((tile, D_MODEL), BF16)
    return pl.pallas_call(
        kern,
        grid=(n, length // tile),
        in_specs=[
            pl.BlockSpec((1, tile, D_MODEL), lambda b, t: (b, t, 0)),
            _const_spec(meta.shape),
            hbm,
            _const_spec(npre.shape),
            _const_spec(npost.shape),
            _const_spec(lb_logits.shape),
            _const_spec(hn.shape),
            _const_spec(cw.shape),
            hbm, hbm, hbm,
        ],
        out_specs=[
            pl.BlockSpec((1, tile, D_MODEL), lambda b, t: (b, t, 0)),
            pl.BlockSpec(state_shape, lambda b, t: (b, 0, 0, 0)),
            pl.BlockSpec(ctx_shape, lambda b, t: (b, 0, 0)),
            hbm, hbm, hbm, hbm,
        ],
        out_shape=[
            jax.ShapeDtypeStruct(x.shape, F32),
            jax.ShapeDtypeStruct((n,) + state_shape[1:], F32),
            jax.ShapeDtypeStruct((n,) + ctx_shape[1:], F32),
            jax.ShapeDtypeStruct(w_in.shape, BF16),
            jax.ShapeDtypeStruct(wa.shape, BF16),
            jax.ShapeDtypeStruct(wb.shape, BF16),
            jax.ShapeDtypeStruct(wo.shape, BF16),
        ],
        scratch_shapes=[
            pltpu.VMEM((N_HEADS, HEAD_DIM, HEAD_DIM), F32),
            pltpu.VMEM((tile + SUBLANES, D_MODEL), F32),
            tile_f32, tile_f32, tile_f32, tile_f32, tile_f32,
            tile_bf16, tile_bf16, tile_bf16, tile_bf16, tile_bf16,
            pltpu.VMEM((HEAD_DIM, D_MODEL), F32),
            pltpu.VMEM((N_HEADS, HEAD_DIM, HEAD_DIM), F32),
            pltpu.VMEM((SUBLANES, D_MODEL), F32),
            pltpu.VMEM(w_in.shape, BF16),
            pltpu.VMEM(wa.shape, BF16), pltpu.VMEM(wb.shape, BF16), pltpu.VMEM(wo.shape, BF16),
            pltpu.VMEM((2,) + wa.shape, F32),
            pltpu.SemaphoreType.DMA((2, LOAD_BANDS)),
            pltpu.SemaphoreType.DMA((n_groups,)),
        ],
        compiler_params=pltpu.CompilerParams(
            dimension_semantics=("arbitrary", "arbitrary"),
            vmem_limit_bytes=V7X_VMEM_LIMIT_BYTES),
        name="prompt_sweep",
    )(x, meta, w_in, npre, npost, lb_logits, hn, cw, wa, wb, wo)


def _decode_kernel(x_ref, st_hbm, ctx_ref, w_in_hbm, npre_ref, npost_ref, lb_ref, hn_ref, cw_ref,
                   wa_hbm, wb_hbm, wo_hbm,
                   y_ref, st_out_hbm, ctx_out_ref,
                   ft_scr, q_scr, v_scr, o_scr, za_scr, pb_scr, ga_scr, gb_scr,
                   in_buf, out_buf, in_sem, out_sem, w_in_ref, wa_ref, wb_ref, wo_ref, w_sem):
    G = DECODE_GROUP
    i = pl.program_id(0)
    n_steps = pl.num_programs(0)
    n_rows = x_ref.shape[0]
    part_rows = G // STATE_PARTS

    def in_copies(step):
        slot = step % STATE_IN_SLOTS
        return [pltpu.make_async_copy(st_hbm.at[pl.ds(step * G + p * part_rows, part_rows)],
                                      in_buf.at[slot, pl.ds(p * part_rows, part_rows)],
                                      in_sem.at[slot, p]) for p in range(STATE_PARTS)]

    def out_copies(step):
        slot = step % STATE_OUT_SLOTS
        return [pltpu.make_async_copy(out_buf.at[slot, pl.ds(p * part_rows, part_rows)],
                                      st_out_hbm.at[pl.ds(step * G + p * part_rows, part_rows)],
                                      out_sem.at[slot, p]) for p in range(STATE_PARTS)]

    n_proj = w_in_hbm.shape[1] // D_MODEL
    w_parts = [(w_in_hbm.at[:, pl.ds(j * D_MODEL, D_MODEL)], w_in_ref.at[:, pl.ds(j * D_MODEL, D_MODEL)])
               for j in range(n_proj)] + [(wa_hbm, wa_ref), (wb_hbm, wb_ref), (wo_hbm, wo_ref)]
    W_A, W_B, W_O = n_proj, n_proj + 1, n_proj + 2

    def w_copy(k):
        return pltpu.make_async_copy(w_parts[k][0], w_parts[k][1], w_sem.at[k])

    @pl.when(i == 0)
    def _prime():
        for step in range(STATE_IN_SLOTS - 1):
            for copy in in_copies(step):
                copy.start()
        for k in (P_F, P_Q, P_I, P_ZA, P_C, P_H, P_B, P_ZB, P_GA, P_GB, W_A, W_B, W_O):
            w_copy(k).start()

    @pl.when(i == 0)
    def _project():
        x = x_ref[:, 0, :]
        xn = (x * _rms_scale(x) * npre_ref[...]).astype(BF16)

        def proj(j):
            w_copy(j).wait()
            return _dot(xn, w_in_ref[:, j * D_MODEL:(j + 1) * D_MODEL])

        lb = _lower_bound(lb_ref[...])
        f = lb + (1.0 - lb) * _sigmoid(proj(P_F))
        ft_scr[...] = f.T
        q_scr[...] = _silu(proj(P_Q))
        v_scr[...] = proj(P_I)
        za_scr[...] = _silu(proj(P_ZA))
        u = proj(P_C) * proj(P_H)
        cw = cw_ref[...]
        conv = cw[CONV_W - 1:CONV_W, :] * u
        for j in range(CONV_W - 1):
            ctx_j = ctx_ref[:, j, :]
            conv = conv + cw[j:j + 1, :] * ctx_j
            if j > 0:
                ctx_out_ref[:, j - 1, :] = ctx_j
        ctx_out_ref[:, CONV_W - 2, :] = u
        pb_scr[...] = proj(P_B) * conv * _silu(proj(P_ZB))
        ga_scr[...] = _sigmoid(proj(P_GA))
        gb_scr[...] = _sigmoid(proj(P_GB))

    @pl.when(i + STATE_IN_SLOTS - 1 < n_steps)
    def _read_ahead():
        for copy in in_copies(i + STATE_IN_SLOTS - 1):
            copy.start()

    @pl.when(i >= STATE_OUT_SLOTS)
    def _slot_written_back():
        for copy in out_copies(i - STATE_OUT_SLOTS):
            copy.wait()

    for copy in in_copies(i):
        copy.wait()
    st_ref = in_buf.at[i % STATE_IN_SLOTS]
    st_out_ref = out_buf.at[i % STATE_OUT_SLOTS]

    shift = (n_rows - i * G) % n_rows
    f_cols = pltpu.roll(ft_scr[...], shift, 1)
    r0 = pl.multiple_of(i * G, G)
    v_rows = v_scr[pl.ds(r0, G), :]
    q_rows = q_scr[pl.ds(r0, G), :].astype(BF16)
    row_id = lax.broadcasted_iota(jnp.int32, (G, HEAD_DIM), 0)
    o_heads = [jnp.zeros((G, HEAD_DIM), F32)] * N_HEADS
    for j in range(G):
        for h, sl in enumerate(HEADS):
            f_b = jnp.broadcast_to(f_cols[sl, j:j + 1], (HEAD_DIM, HEAD_DIM))
            s_new = f_b * st_ref[j, h] + (1.0 - f_b) * v_rows[j:j + 1, sl]
            st_out_ref[j, h] = s_new
            read = _dot(q_rows[:, sl], s_new.astype(BF16))
            o_heads[h] = jnp.where(row_id == j, read, o_heads[h])
    o_scr[pl.ds(r0, G), :] = jnp.concatenate(o_heads, axis=1)
    for copy in out_copies(i):
        copy.start()

    @pl.when(i == pl.num_programs(0) - 1)
    def _output():
        hn = hn_ref[...]
        for sl in HEADS:
            o_h = o_scr[:, sl]
            o_scr[:, sl] = o_h * _rms_scale(o_h) * hn
        for k in (W_A, W_B, W_O):
            w_copy(k).wait()
        y_a = _dot((o_scr[...] * za_scr[...]).astype(BF16), wa_ref[...])
        y_b = _dot(pb_scr[...].astype(BF16), wb_ref[...])
        merged = ga_scr[...] * y_a + gb_scr[...] * y_b
        out = _dot(merged.astype(BF16), wo_ref[...])
        y_ref[:, 0, :] = x_ref[:, 0, :] + out * _rms_scale(out) * npost_ref[...]
        for back in range(STATE_OUT_SLOTS):
            for copy in out_copies(i - back):
                copy.wait()


def _run_decode(x, state, ctx, weights):
    n = x.shape[0]
    assert n % DECODE_GROUP == 0 and n == 128 and DECODE_GROUP % STATE_PARTS == 0
    assert n // DECODE_GROUP >= max(STATE_IN_SLOTS, STATE_OUT_SLOTS)
    w_in, npre, npost, lb_logits, hn, cw, wa, wb, wo = weights

    def const(shape):
        zeros = (0,) * len(shape)
        return pl.BlockSpec(shape, lambda i: zeros, pipeline_mode=pl.Buffered(1))

    hbm = pl.BlockSpec(memory_space=pl.ANY)
    in_slots = pltpu.VMEM((STATE_IN_SLOTS, DECODE_GROUP) + state.shape[1:], F32)
    out_slots = pltpu.VMEM((STATE_OUT_SLOTS, DECODE_GROUP) + state.shape[1:], F32)
    rows_f32 = pltpu.VMEM((n, D_MODEL), F32)
    cols_f32 = pltpu.VMEM((D_MODEL, n), F32)
    return pl.pallas_call(
        _decode_kernel,
        grid=(n // DECODE_GROUP,),
        in_specs=[const(x.shape), hbm, const(ctx.shape), hbm, const(npre.shape),
                  const(npost.shape), const(lb_logits.shape), const(hn.shape), const(cw.shape),
                  hbm, hbm, hbm],
        out_specs=[const(x.shape), hbm, const(ctx.shape)],
        out_shape=[jax.ShapeDtypeStruct(x.shape, F32),
                   jax.ShapeDtypeStruct(state.shape, F32),
                   jax.ShapeDtypeStruct(ctx.shape, F32)],
        scratch_shapes=[cols_f32, rows_f32, rows_f32, rows_f32, rows_f32, rows_f32, rows_f32, rows_f32,
                        in_slots, out_slots,
                        pltpu.SemaphoreType.DMA((STATE_IN_SLOTS, STATE_PARTS)),
                        pltpu.SemaphoreType.DMA((STATE_OUT_SLOTS, STATE_PARTS)),
                        pltpu.VMEM(w_in.shape, BF16), pltpu.VMEM(wa.shape, BF16),
                        pltpu.VMEM(wb.shape, BF16), pltpu.VMEM(wo.shape, BF16),
                        pltpu.SemaphoreType.DMA((w_in.shape[1] // D_MODEL + 3,))],
        compiler_params=pltpu.CompilerParams(
            dimension_semantics=("arbitrary",),
            vmem_limit_bytes=V7X_VMEM_LIMIT_BYTES),
        name="decode_step",
    )(x, state, ctx, w_in, npre, npost, lb_logits, hn, cw, wa, wb, wo)


def kernel(x_prompt, x_sample, state_hgrn, state_conv, meta_tokens, w_in, norm_pre, norm_post, lb_logits,
           hgrn_norm, conv_w, w_a, w_b, w_o):
    depth = w_in.shape[0]
    assert depth == 1, "single-layer trunk"
    assert x_sample.shape[1] == 1, "one new token per decode row"

    weights = (w_in[0], norm_pre, norm_post, lb_logits, hgrn_norm, conv_w[0], w_a[0], w_b[0], w_o[0])

    y_prompt, hgrn_p, conv_p, w_in_bf, wa_bf, wb_bf, wo_bf = _run_prompt(
        x_prompt, meta_tokens.astype(x_prompt.dtype), weights, tile=PROMPT_TILE, chunk=PROMPT_CHUNK)
    weights = (w_in_bf,) + weights[1:6] + (wa_bf, wb_bf, wo_bf)

    y_s, hgrn_s, conv_s = _run_decode(x_sample, state_hgrn[0], state_conv[0], weights)

    return (y_prompt, y_s, hgrn_p[None], hgrn_s[None], conv_p[None], conv_s[None])
```

```python
import functools

import jax
import jax.numpy as jnp
from jax import lax
from jax.experimental import pallas as pl
from jax.experimental.pallas import tpu as pltpu

D_MODEL = 1024
N_HEADS = 8
HEAD_DIM = D_MODEL // N_HEADS
CONV_W = 3
EPS = 1e-6
NEG_LOG2_E = -1.4426950408889634
P_Q, P_F, P_I, P_ZA, P_B, P_C, P_H, P_ZB, P_GA, P_GB = range(10)

PROMPT_TILE = 256
PROMPT_CHUNK = 128
DECODE_GROUP = 8
STATE_IN_SLOTS = 3
STATE_OUT_SLOTS = 2
STATE_PARTS = 8
LOAD_BANDS = 4
SUBLANES = 8
CTX_ROW0 = SUBLANES - (CONV_W - 1)
MAX_HALF_CHUNK_LOG2_DECAY = 115.0
V7X_VMEM_LIMIT_BYTES = 58 * 1024 * 1024

BF16 = jnp.bfloat16
F32 = jnp.float32
HEADS = [slice(h * HEAD_DIM, (h + 1) * HEAD_DIM) for h in range(N_HEADS)]


def _dot(a, b):
    return jnp.dot(a, b, preferred_element_type=F32)


def _dot_nt(a, b):
    return lax.dot_general(a, b, (((1,), (1,)), ((), ())), preferred_element_type=F32)


def _dot_tn(a, b):
    return lax.dot_general(a, b, (((0,), (0,)), ((), ())), preferred_element_type=F32)


def _sigmoid(x):
    return 1.0 / (1.0 + jnp.exp2(x * NEG_LOG2_E))


def _silu(x):
    return x * _sigmoid(x)


def _rms_scale(x):
    return lax.rsqrt(jnp.mean(x * x, axis=-1, keepdims=True) + EPS)


def _lower_bound(lb_logits):
    m = jnp.max(lb_logits, axis=0, keepdims=True)
    e = jnp.exp(lb_logits - m)
    return e[0:1, :] / jnp.sum(e, axis=0, keepdims=True)


def _block_cumsum(x, block):
    n = x.shape[0]
    ri = lax.broadcasted_iota(jnp.int32, (n, n), 0)
    ci = lax.broadcasted_iota(jnp.int32, (n, n), 1)
    tri = ri >= ci
    if n != block:
        tri = jnp.logical_and(tri, (ri // block) == (ci // block))
    tri = jnp.where(tri, 1.0, 0.0).astype(BF16)
    hi = x.astype(BF16)
    lo = (x - hi.astype(F32)).astype(BF16)
    return _dot(jnp.concatenate([tri, tri], axis=1), jnp.concatenate([hi, lo], axis=0))


def _seq_kernel(x_ref, meta_ref, w_in_hbm, npre_ref, npost_ref, lb_ref, hn_ref, cw_ref,
                wa_hbm, wb_hbm, wo_hbm,
                y_ref, st_out_ref, c_out_ref, w_in_out, wa_out, wb_out, wo_out,
                st_scr, u_scr, q_scr, k_scr, v_scr, g_scr, o_scr,
                qt_scr, kt_scr, qin_scr, kst_scr, vb_scr, dec_scr, st_meta_scr, ctx_meta_scr,
                w_in_ref, wa_ref, wb_ref, wo_ref, stage_scr, load_sem, store_sem,
                *, tile, chunk):
    T, C = tile, chunk
    n_chunks = T // C
    t = pl.program_id(1)
    first_step = jnp.logical_and(pl.program_id(0) == 0, t == 0)
    last_step = jnp.logical_and(pl.program_id(0) == pl.num_programs(0) - 1, t == pl.num_programs(1) - 1)
    lb = _lower_bound(lb_ref[...])

    col = [pl.ds(j * D_MODEL, D_MODEL) for j in range(w_in_hbm.shape[1] // D_MODEL)]
    groups = [(w_in_hbm.at[:, c], w_in_ref.at[:, c], w_in_out.at[:, c]) for c in col]
    groups += [(wa_hbm, wa_ref, wa_out), (wb_hbm, wb_ref, wb_out), (wo_hbm, wo_ref, wo_out)]

    def load_copies(p):
        src = groups[fetch_order[p]][0]
        band = src.shape[0] // LOAD_BANDS
        return [pltpu.make_async_copy(src.at[pl.ds(b * band, band), :],
                                      stage_scr.at[p % 2, pl.ds(b * band, band), :],
                                      load_sem.at[p % 2, b]) for b in range(LOAD_BANDS)]

    def store_copy(i):
        return pltpu.make_async_copy(groups[i][1], groups[i][2], store_sem.at[i])

    W_A, W_B, W_O = len(col), len(col) + 1, len(col) + 2
    fetch_order = [P_F, P_I, P_C, P_H, P_Q, P_ZA, W_A, P_B, P_ZB, W_B, P_GA, P_GB, W_O]
    n_eager = 4
    assert sorted(fetch_order) == list(range(len(groups)))

    def fetch(k):
        p = fetch_order.index(k)
        for copy in load_copies(p):
            copy.wait()
        groups[k][1][...] = stage_scr[p % 2].astype(BF16)
        if p + 2 < len(fetch_order):
            for copy in load_copies(p + 2):
                copy.start()

    @pl.when(first_step)
    def _stream_first_weights():
        for p in range(2):
            for copy in load_copies(p):
                copy.start()

    def project(xn, j):
        return _dot(xn, w_in_ref[:, j * D_MODEL:(j + 1) * D_MODEL])

    @pl.when(first_step)
    def _meta_prefix():
        xm = meta_ref[...]
        n_meta = xm.shape[0]
        xn_m = (xm * _rms_scale(xm) * npre_ref[...]).astype(BF16)

        def meta_proj(j):
            fetch(j)
            return project(xn_m, j)

        f_m = lb + (1.0 - lb) * _sigmoid(meta_proj(P_F))
        g_m = _block_cumsum(jnp.log2(f_m), n_meta)
        k_end = ((1.0 - f_m) * jnp.exp2(g_m[n_meta - 1:n_meta, :] - g_m)).astype(BF16)
        v_m = meta_proj(P_I).astype(BF16)
        for h, sl in enumerate(HEADS):
            st_meta_scr[h] = _dot_tn(k_end[:, sl], v_m[:, sl])
        u_m = meta_proj(P_C) * meta_proj(P_H)
        ctx_meta_scr[CTX_ROW0:SUBLANES, :] = u_m[n_meta - (CONV_W - 1):n_meta, :]
        dec_scr[...] = jnp.zeros(dec_scr.shape, F32)

    @pl.when(t == 0)
    def _init():
        st_scr[...] = st_meta_scr[...]
        u_scr[CTX_ROW0:SUBLANES, :] = ctx_meta_scr[CTX_ROW0:SUBLANES, :]

    def run_tile(lazy_weights):
        fetched = set()
        x = x_ref[0]
        xn = (x * _rms_scale(x) * npre_ref[...]).astype(BF16)

        def proj(j):
            if lazy_weights and j in fetch_order[n_eager:] and j not in fetched:
                fetched.add(j)
                fetch(j)
            return project(xn, j)

        def late_weight(k, ref):
            if lazy_weights:
                fetch(k)
            return ref[...]

        f = lb + (1.0 - lb) * _sigmoid(proj(P_F))
        k_scr[...] = 1.0 - f
        q_scr[...] = _silu(proj(P_Q))
        vb_scr[...] = proj(P_I).astype(BF16)
        g_scr[...] = _block_cumsum(jnp.log2(f), C)

        g_floor = None
        for c in range(n_chunks):
            rows = slice(c * C, (c + 1) * C)
            gc = g_scr[rows, :]
            g_last = gc[C - 1:C, :]
            g_mid = gc[C // 2 - 1:C // 2, :]
            qc = q_scr[rows, :]
            kc = k_scr[rows, :]
            qt_scr[rows, :] = (qc * jnp.exp2(gc - g_mid)).astype(BF16)
            kt_scr[rows, :] = (kc * jnp.exp2(g_mid - gc)).astype(BF16)
            qin_scr[rows, :] = (qc * jnp.exp2(gc)).astype(BF16)
            kst_scr[rows, :] = (kc * jnp.exp2(g_last - gc)).astype(BF16)
            dec_scr[c:c + 1, :] = jnp.exp2(g_last)
            half_floor = jnp.minimum(g_mid, g_last - g_mid)
            g_floor = half_floor if g_floor is None else jnp.minimum(g_floor, half_floor)
        stable = jnp.min(g_floor) >= -MAX_HALF_CHUNK_LOG2_DECAY

        dec_cols = dec_scr[...].T
        keep = jnp.logical_and(
            lax.broadcasted_iota(jnp.int32, (C, C), 0) >= lax.broadcasted_iota(jnp.int32, (C, C), 1), stable)
        for c in range(n_chunks):
            rows = slice(c * C, (c + 1) * C)
            scores = [jnp.where(keep, _dot_nt(qt_scr[rows, sl], kt_scr[rows, sl]), 0.0).astype(BF16)
                      for sl in HEADS]
            for h, sl in enumerate(HEADS):
                vb = vb_scr[rows, sl]
                st = st_scr[h]
                o_scr[rows, sl] = _dot(jnp.concatenate([qin_scr[rows, sl], scores[h]], axis=1),
                                       jnp.concatenate([st.astype(BF16), vb], axis=0))
                decay = jnp.broadcast_to(dec_cols[sl, c:c + 1], (HEAD_DIM, HEAD_DIM))
                st_scr[h] = decay * st + _dot_tn(kst_scr[rows, sl], vb)

        @pl.when(jnp.logical_not(stable))
        def _intra_exact():
            same_head = (lax.broadcasted_iota(jnp.int32, (D_MODEL, D_MODEL), 0) // HEAD_DIM ==
                         lax.broadcasted_iota(jnp.int32, (D_MODEL, D_MODEL), 1) // HEAD_DIM)
            head_sum = jnp.where(same_head, 1.0, 0.0).astype(BF16)
            tpos = lax.broadcasted_iota(jnp.int32, (C, 1), 0)
            v_scr[...] = proj(P_I)

            def chunk_step(c, carry):
                r0 = pl.multiple_of(c * C, C)
                rows = pl.ds(r0, C)
                gc = g_scr[rows, :]
                qc = q_scr[rows, :]

                def src_step(i, carry2):
                    src = pl.ds(r0 + i, 1)
                    p = qc * jnp.exp2(jnp.minimum(gc - g_scr[src, :], 0.0)) * k_scr[src, :]
                    p = jnp.where(tpos >= i, p, 0.0).astype(BF16)
                    o_scr[rows, :] += _dot(p, head_sum) * v_scr[src, :]
                    return carry2

                return lax.fori_loop(0, C, src_step, carry)

            lax.fori_loop(0, n_chunks, chunk_step, 0)

        hn = hn_ref[...] * (HEAD_DIM ** 0.5)
        silu_za = _silu(proj(P_ZA))
        gated = []
        for sl in HEADS:
            o_h = o_scr[:, sl]
            inv_norm = lax.rsqrt(jnp.sum(o_h * o_h, axis=-1, keepdims=True) + HEAD_DIM * EPS)
            gated.append((o_h * inv_norm * hn * silu_za[:, sl]).astype(BF16))
        y_a = _dot(jnp.concatenate(gated, axis=1), late_weight(W_A, wa_ref))

        u = proj(P_C) * proj(P_H)
        u_scr[SUBLANES:SUBLANES + T, :] = u
        cw = cw_ref[...]
        conv = cw[CONV_W - 1:CONV_W, :] * u
        for j in range(CONV_W - 1):
            conv = conv + cw[j:j + 1, :] * u_scr[CTX_ROW0 + j:CTX_ROW0 + j + T, :]
        y_b = _dot((proj(P_B) * conv * _silu(proj(P_ZB))).astype(BF16), late_weight(W_B, wb_ref))
        u_scr[CTX_ROW0:SUBLANES, :] = u[T - (CONV_W - 1):T, :]

        merged = _sigmoid(proj(P_GA)) * y_a + _sigmoid(proj(P_GB)) * y_b
        out = _dot(merged.astype(BF16), late_weight(W_O, wo_ref))
        y_ref[0] = x_ref[0] + out * _rms_scale(out) * npost_ref[...]

    @pl.when(first_step)
    def _first_tile():
        run_tile(lazy_weights=True)
        for k in range(len(groups)):
            store_copy(k).start()

    @pl.when(jnp.logical_not(first_step))
    def _tile():
        run_tile(lazy_weights=False)

    @pl.when(t == pl.num_programs(1) - 1)
    def _finish():
        c_out_ref[0] = u_scr[CTX_ROW0:SUBLANES, :]
        st_out_ref[0] = st_scr[...]

    @pl.when(last_step)
    def _weights_written():
        for i in range(len(groups)):
            store_copy(i).wait()


def _const_spec(shape):
    zeros = (0,) * len(shape)
    return pl.BlockSpec(shape, lambda b, t: zeros, pipeline_mode=pl.Buffered(1))


def _run_prompt(x, meta, weights, *, tile, chunk):
    n, length, _ = x.shape
    assert length % tile == 0 and tile % chunk == 0 and chunk % (2 * SUBLANES) == 0 and tile >= CONV_W - 1
    assert meta.shape[0] % (2 * SUBLANES) == 0 and meta.shape[0] >= CONV_W - 1
    assert tile // chunk <= HEAD_DIM
    w_in, npre, npost, lb_logits, hn, cw, wa, wb, wo = weights
    kern = functools.partial(_seq_kernel, tile=tile, chunk=chunk)
    hbm = pl.BlockSpec(memory_space=pl.ANY)
    n_groups = w_in.shape[1] // D_MODEL + 3
    assert w_in.shape[1] % D_MODEL == 0 and wa.shape == wb.shape == wo.shape == (w_in.shape[0], D_MODEL)
    state_shape = (1, N_HEADS, HEAD_DIM, HEAD_DIM)
    ctx_shape = (1, CONV_W - 1, D_MODEL)
    tile_f32 = pltpu.VMEM((tile, D_MODEL), F32)
    tile_bf16 = pltpu.VMEM((tile, D_MODEL), BF16)
    return pl.pallas_call(
        kern,
        grid=(n, length // tile),
        in_specs=[
            pl.BlockSpec((1, tile, D_MODEL), lambda b, t: (b, t, 0)),
            _const_spec(meta.shape),
            hbm,
            _const_spec(npre.shape),
            _const_spec(npost.shape),
            _const_spec(lb_logits.shape),
            _const_spec(hn.shape),
            _const_spec(cw.shape),
            hbm, hbm, hbm,
        ],
        out_specs=[
            pl.BlockSpec((1, tile, D_MODEL), lambda b, t: (b, t, 0)),
            pl.BlockSpec(state_shape, lambda b, t: (b, 0, 0, 0)),
            pl.BlockSpec(ctx_shape, lambda b, t: (b, 0, 0)),
            hbm, hbm, hbm, hbm,
        ],
        out_shape=[
            jax.ShapeDtypeStruct(x.shape, F32),
            jax.ShapeDtypeStruct((n,) + state_shape[1:], F32),
            jax.ShapeDtypeStruct((n,) + ctx_shape[1:], F32),
            jax.ShapeDtypeStruct(w_in.shape, BF16),
            jax.ShapeDtypeStruct(wa.shape, BF16),
            jax.ShapeDtypeStruct(wb.shape, BF16),
            jax.ShapeDtypeStruct(wo.shape, BF16),
        ],
        scratch_shapes=[
            pltpu.VMEM((N_HEADS, HEAD_DIM, HEAD_DIM), F32),
            pltpu.VMEM((tile + SUBLANES, D_MODEL), F32),
            tile_f32, tile_f32, tile_f32, tile_f32, tile_f32,
            tile_bf16, tile_bf16, tile_bf16, tile_bf16, tile_bf16,
            pltpu.VMEM((HEAD_DIM, D_MODEL), F32),
            pltpu.VMEM((N_HEADS, HEAD_DIM, HEAD_DIM), F32),
            pltpu.VMEM((SUBLANES, D_MODEL), F32),
            pltpu.VMEM(w_in.shape, BF16),
            pltpu.VMEM(wa.shape, BF16), pltpu.VMEM(wb.shape, BF16), pltpu.VMEM(wo.shape, BF16),
            pltpu.VMEM((2,) + wa.shape, F32),
            pltpu.SemaphoreType.DMA((2, LOAD_BANDS)),
            pltpu.SemaphoreType.DMA((n_groups,)),
        ],
        compiler_params=pltpu.CompilerParams(
            dimension_semantics=("arbitrary", "arbitrary"),
            vmem_limit_bytes=V7X_VMEM_LIMIT_BYTES),
        name="prompt_sweep",
    )(x, meta, w_in, npre, npost, lb_logits, hn, cw, wa, wb, wo)


def _decode_kernel(x_ref, st_hbm, ctx_ref, w_in_hbm, npre_ref, npost_ref, lb_ref, hn_ref, cw_ref,
                   wa_hbm, wb_hbm, wo_hbm,
                   y_ref, st_out_hbm, ctx_out_ref,
                   ft_scr, q_scr, v_scr, o_scr, za_scr, pb_scr, ga_scr, gb_scr,
                   in_buf, out_buf, in_sem, out_sem, w_in_ref, wa_ref, wb_ref, wo_ref, w_sem):
    G = DECODE_GROUP
    i = pl.program_id(0)
    n_steps = pl.num_programs(0)
    n_rows = x_ref.shape[0]
    part_rows = G // STATE_PARTS

    def in_copies(step):
        slot = step % STATE_IN_SLOTS
        return [pltpu.make_async_copy(st_hbm.at[pl.ds(step * G + p * part_rows, part_rows)],
                                      in_buf.at[slot, pl.ds(p * part_rows, part_rows)],
                                      in_sem.at[slot, p]) for p in range(STATE_PARTS)]

    def out_copies(step):
        slot = step % STATE_OUT_SLOTS
        return [pltpu.make_async_copy(out_buf.at[slot, pl.ds(p * part_rows, part_rows)],
                                      st_out_hbm.at[pl.ds(step * G + p * part_rows, part_rows)],
                                      out_sem.at[slot, p]) for p in range(STATE_PARTS)]

    n_proj = w_in_hbm.shape[1] // D_MODEL
    w_parts = [(w_in_hbm.at[:, pl.ds(j * D_MODEL, D_MODEL)], w_in_ref.at[:, pl.ds(j * D_MODEL, D_MODEL)])
               for j in range(n_proj)] + [(wa_hbm, wa_ref), (wb_hbm, wb_ref), (wo_hbm, wo_ref)]
    W_A, W_B, W_O = n_proj, n_proj + 1, n_proj + 2

    def w_copy(k):
        return pltpu.make_async_copy(w_parts[k][0], w_parts[k][1], w_sem.at[k])

    @pl.when(i == 0)
    def _prime():
        for step in range(STATE_IN_SLOTS - 1):
            for copy in in_copies(step):
                copy.start()
        for k in (P_F, P_Q, P_I, P_ZA, P_C, P_H, P_B, P_ZB, P_GA, P_GB, W_A, W_B, W_O):
            w_copy(k).start()

    @pl.when(i == 0)
    def _project():
        x = x_ref[:, 0, :]
        xn = (x * _rms_scale(x) * npre_ref[...]).astype(BF16)

        def proj(j):
            w_copy(j).wait()
            return _dot(xn, w_in_ref[:, j * D_MODEL:(j + 1) * D_MODEL])

        lb = _lower_bound(lb_ref[...])
        f = lb + (1.0 - lb) * _sigmoid(proj(P_F))
        ft_scr[...] = f.T
        q_scr[...] = _silu(proj(P_Q))
        v_scr[...] = proj(P_I)
        za_scr[...] = _silu(proj(P_ZA))
        u = proj(P_C) * proj(P_H)
        cw = cw_ref[...]
        conv = cw[CONV_W - 1:CONV_W, :] * u
        for j in range(CONV_W - 1):
            ctx_j = ctx_ref[:, j, :]
            conv = conv + cw[j:j + 1, :] * ctx_j
            if j > 0:
                ctx_out_ref[:, j - 1, :] = ctx_j
        ctx_out_ref[:, CONV_W - 2, :] = u
        pb_scr[...] = proj(P_B) * conv * _silu(proj(P_ZB))
        ga_scr[...] = _sigmoid(proj(P_GA))
        gb_scr[...] = _sigmoid(proj(P_GB))

    @pl.when(i + STATE_IN_SLOTS - 1 < n_steps)
    def _read_ahead():
        for copy in in_copies(i + STATE_IN_SLOTS - 1):
            copy.start()

    @pl.when(i >= STATE_OUT_SLOTS)
    def _slot_written_back():
        for copy in out_copies(i - STATE_OUT_SLOTS):
            copy.wait()

    for copy in in_copies(i):
        copy.wait()
    st_ref = in_buf.at[i % STATE_IN_SLOTS]
    st_out_ref = out_buf.at[i % STATE_OUT_SLOTS]

    shift = (n_rows - i * G) % n_rows
    f_cols = pltpu.roll(ft_scr[...], shift, 1)
    r0 = pl.multiple_of(i * G, G)
    v_rows = v_scr[pl.ds(r0, G), :]
    q_rows = q_scr[pl.ds(r0, G), :].astype(BF16)
    row_id = lax.broadcasted_iota(jnp.int32, (G, HEAD_DIM), 0)
    o_heads = [jnp.zeros((G, HEAD_DIM), F32)] * N_HEADS
    for j in range(G):
        for h, sl in enumerate(HEADS):
            f_b = jnp.broadcast_to(f_cols[sl, j:j + 1], (HEAD_DIM, HEAD_DIM))
            s_new = f_b * st_ref[j, h] + (1.0 - f_b) * v_rows[j:j + 1, sl]
            st_out_ref[j, h] = s_new
            read = _dot(q_rows[:, sl], s_new.astype(BF16))
            o_heads[h] = jnp.where(row_id == j, read, o_heads[h])
    o_scr[pl.ds(r0, G), :] = jnp.concatenate(o_heads, axis=1)
    for copy in out_copies(i):
        copy.start()

    @pl.when(i == pl.num_programs(0) - 1)
    def _output():
        hn = hn_ref[...]
        for sl in HEADS:
            o_h = o_scr[:, sl]
            o_scr[:, sl] = o_h * _rms_scale(o_h) * hn
        for k in (W_A, W_B, W_O):
            w_copy(k).wait()
        y_a = _dot((o_scr[...] * za_scr[...]).astype(BF16), wa_ref[...])
        y_b = _dot(pb_scr[...].astype(BF16), wb_ref[...])
        merged = ga_scr[...] * y_a + gb_scr[...] * y_b
        out = _dot(merged.astype(BF16), wo_ref[...])
        y_ref[:, 0, :] = x_ref[:, 0, :] + out * _rms_scale(out) * npost_ref[...]
        for back in range(STATE_OUT_SLOTS):
            for copy in out_copies(i - back):
                copy.wait()


def _run_decode(x, state, ctx, weights):
    n = x.shape[0]
    assert n % DECODE_GROUP == 0 and n == 128 and DECODE_GROUP % STATE_PARTS == 0
    assert n // DECODE_GROUP >= max(STATE_IN_SLOTS, STATE_OUT_SLOTS)
    w_in, npre, npost, lb_logits, hn, cw, wa, wb, wo = weights

    def const(shape):
        zeros = (0,) * len(shape)
        return pl.BlockSpec(shape, lambda i: zeros, pipeline_mode=pl.Buffered(1))

    hbm = pl.BlockSpec(memory_space=pl.ANY)
    in_slots = pltpu.VMEM((STATE_IN_SLOTS, DECODE_GROUP) + state.shape[1:], F32)
    out_slots = pltpu.VMEM((STATE_OUT_SLOTS, DECODE_GROUP) + state.shape[1:], F32)
    rows_f32 = pltpu.VMEM((n, D_MODEL), F32)
    cols_f32 = pltpu.VMEM((D_MODEL, n), F32)
    return pl.pallas_call(
        _decode_kernel,
        grid=(n // DECODE_GROUP,),
        in_specs=[const(x.shape), hbm, const(ctx.shape), hbm, const(npre.shape),
                  const(npost.shape), const(lb_logits.shape), const(hn.shape), const(cw.shape),
                  hbm, hbm, hbm],
        out_specs=[const(x.shape), hbm, const(ctx.shape)],
        out_shape=[jax.ShapeDtypeStruct(x.shape, F32),
                   jax.ShapeDtypeStruct(state.shape, F32),
                   jax.ShapeDtypeStruct(ctx.shape, F32)],
        scratch_shapes=[cols_f32, rows_f32, rows_f32, rows_f32, rows_f32, rows_f32, rows_f32, rows_f32,
                        in_slots, out_slots,
                        pltpu.SemaphoreType.DMA((STATE_IN_SLOTS, STATE_PARTS)),
                        pltpu.SemaphoreType.DMA((STATE_OUT_SLOTS, STATE_PARTS)),
                        pltpu.VMEM(w_in.shape, BF16), pltpu.VMEM(wa.shape, BF16),
                        pltpu.VMEM(wb.shape, BF16), pltpu.VMEM(wo.shape, BF16),
                        pltpu.SemaphoreType.DMA((w_in.shape[1] // D_MODEL + 3,))],
        compiler_params=pltpu.CompilerParams(
            dimension_semantics=("arbitrary",),
            vmem_limit_bytes=V7X_VMEM_LIMIT_BYTES),
        name="decode_step",
    )(x, state, ctx, w_in, npre, npost, lb_logits, hn, cw, wa, wb, wo)


def kernel(x_prompt, x_sample, state_hgrn, state_conv, meta_tokens, w_in, norm_pre, norm_post, lb_logits,
           hgrn_norm, conv_w, w_a, w_b, w_o):
    depth = w_in.shape[0]
    assert depth == 1, "single-layer trunk"
    assert x_sample.shape[1] == 1, "one new token per decode row"

    weights = (w_in[0], norm_pre, norm_post, lb_logits, hgrn_norm, conv_w[0], w_a[0], w_b[0], w_o[0])

    y_prompt, hgrn_p, conv_p, w_in_bf, wa_bf, wb_bf, wo_bf = _run_prompt(
        x_prompt, meta_tokens.astype(x_prompt.dtype), weights, tile=PROMPT_TILE, chunk=PROMPT_CHUNK)
    weights = (w_in_bf,) + weights[1:6] + (wa_bf, wb_bf, wo_bf)

    y_s, hgrn_s, conv_s = _run_decode(x_sample, state_hgrn[0], state_conv[0], weights)

    return (y_prompt, y_s, hgrn_p[None], hgrn_s[None], conv_p[None], conv_s[None])
```

```python
import functools

import jax
import jax.numpy as jnp
from jax import lax
from jax.experimental import pallas as pl
from jax.experimental.pallas import tpu as pltpu

D_MODEL = 1024
N_HEADS = 8
HEAD_DIM = D_MODEL // N_HEADS
CONV_W = 3
EPS = 1e-6
NEG_LOG2_E = -1.4426950408889634
P_Q, P_F, P_I, P_ZA, P_B, P_C, P_H, P_ZB, P_GA, P_GB = range(10)

PROMPT_TILE = 256
PROMPT_CHUNK = 128
DECODE_GROUP = 8
STATE_IN_SLOTS = 3
STATE_OUT_SLOTS = 2
STATE_PARTS = 8
LOAD_BANDS = 4
SUBLANES = 8
CTX_ROW0 = SUBLANES - (CONV_W - 1)
MAX_HALF_CHUNK_LOG2_DECAY = 115.0
V7X_VMEM_LIMIT_BYTES = 58 * 1024 * 1024

BF16 = jnp.bfloat16
F32 = jnp.float32
HEADS = [slice(h * HEAD_DIM, (h + 1) * HEAD_DIM) for h in range(N_HEADS)]


def _dot(a, b):
    return jnp.dot(a, b, preferred_element_type=F32)


def _dot_nt(a, b):
    return lax.dot_general(a, b, (((1,), (1,)), ((), ())), preferred_element_type=F32)


def _dot_tn(a, b):
    return lax.dot_general(a, b, (((0,), (0,)), ((), ())), preferred_element_type=F32)


def _sigmoid(x):
    return 1.0 / (1.0 + jnp.exp2(x * NEG_LOG2_E))


def _silu(x):
    return x * _sigmoid(x)


def _rms_scale(x):
    return lax.rsqrt(jnp.mean(x * x, axis=-1, keepdims=True) + EPS)


def _lower_bound(lb_logits):
    m = jnp.max(lb_logits, axis=0, keepdims=True)
    e = jnp.exp(lb_logits - m)
    return e[0:1, :] / jnp.sum(e, axis=0, keepdims=True)


def _block_cumsum(x, block):
    n = x.shape[0]
    ri = lax.broadcasted_iota(jnp.int32, (n, n), 0)
    ci = lax.broadcasted_iota(jnp.int32, (n, n), 1)
    tri = ri >= ci
    if n != block:
        tri = jnp.logical_and(tri, (ri // block) == (ci // block))
    tri = jnp.where(tri, 1.0, 0.0).astype(BF16)
    hi = x.astype(BF16)
    lo = (x - hi.astype(F32)).astype(BF16)
    return _dot(tri, hi) + _dot(tri, lo)


def _seq_kernel(x_ref, meta_ref, w_in_hbm, npre_ref, npost_ref, lb_ref, hn_ref, cw_ref,
                wa_hbm, wb_hbm, wo_hbm,
                y_ref, st_out_ref, c_out_ref, w_in_out, wa_out, wb_out, wo_out,
                st_scr, u_scr, q_scr, k_scr, v_scr, g_scr, o_scr,
                qt_scr, kt_scr, qin_scr, kst_scr, vb_scr, dec_scr, st_meta_scr, ctx_meta_scr,
                w_in_ref, wa_ref, wb_ref, wo_ref, stage_scr, load_sem, store_sem,
                *, tile, chunk):
    T, C = tile, chunk
    n_chunks = T // C
    t = pl.program_id(1)
    first_step = jnp.logical_and(pl.program_id(0) == 0, t == 0)
    last_step = jnp.logical_and(pl.program_id(0) == pl.num_programs(0) - 1, t == pl.num_programs(1) - 1)
    lb = _lower_bound(lb_ref[...])

    col = [pl.ds(j * D_MODEL, D_MODEL) for j in range(w_in_hbm.shape[1] // D_MODEL)]
    groups = [(w_in_hbm.at[:, c], w_in_ref.at[:, c], w_in_out.at[:, c]) for c in col]
    groups += [(wa_hbm, wa_ref, wa_out), (wb_hbm, wb_ref, wb_out), (wo_hbm, wo_ref, wo_out)]

    def load_copies(p):
        src = groups[fetch_order[p]][0]
        band = src.shape[0] // LOAD_BANDS
        return [pltpu.make_async_copy(src.at[pl.ds(b * band, band), :],
                                      stage_scr.at[p % 2, pl.ds(b * band, band), :],
                                      load_sem.at[p % 2, b]) for b in range(LOAD_BANDS)]

    def store_copy(i):
        return pltpu.make_async_copy(groups[i][1], groups[i][2], store_sem.at[i])

    W_A, W_B, W_O = len(col), len(col) + 1, len(col) + 2
    fetch_order = [P_F, P_I, P_C, P_H, P_Q, P_ZA, W_A, P_B, P_ZB, W_B, P_GA, P_GB, W_O]
    n_eager = 4
    assert sorted(fetch_order) == list(range(len(groups)))

    def fetch(k):
        p = fetch_order.index(k)
        for copy in load_copies(p):
            copy.wait()
        groups[k][1][...] = stage_scr[p % 2].astype(BF16)
        if p + 2 < len(fetch_order):
            for copy in load_copies(p + 2):
                copy.start()

    @pl.when(first_step)
    def _stream_first_weights():
        for p in range(2):
            for copy in load_copies(p):
                copy.start()

    def project(xn, j):
        return _dot(xn, w_in_ref[:, j * D_MODEL:(j + 1) * D_MODEL])

    @pl.when(first_step)
    def _meta_prefix():
        xm = meta_ref[...]
        n_meta = xm.shape[0]
        xn_m = (xm * _rms_scale(xm) * npre_ref[...]).astype(BF16)

        def meta_proj(j):
            fetch(j)
            return project(xn_m, j)

        f_m = lb + (1.0 - lb) * _sigmoid(meta_proj(P_F))
        g_m = _block_cumsum(jnp.log2(f_m), n_meta)
        k_end = ((1.0 - f_m) * jnp.exp2(g_m[n_meta - 1:n_meta, :] - g_m)).astype(BF16)
        v_m = meta_proj(P_I).astype(BF16)
        for h, sl in enumerate(HEADS):
            st_meta_scr[h] = _dot_tn(k_end[:, sl], v_m[:, sl])
        u_m = meta_proj(P_C) * meta_proj(P_H)
        ctx_meta_scr[CTX_ROW0:SUBLANES, :] = u_m[n_meta - (CONV_W - 1):n_meta, :]
        dec_scr[...] = jnp.zeros(dec_scr.shape, F32)

    @pl.when(t == 0)
    def _init():
        st_scr[...] = st_meta_scr[...]
        u_scr[CTX_ROW0:SUBLANES, :] = ctx_meta_scr[CTX_ROW0:SUBLANES, :]

    def run_tile(lazy_weights):
        fetched = set()
        x = x_ref[0]
        xn = (x * _rms_scale(x) * npre_ref[...]).astype(BF16)

        def proj(j):
            if lazy_weights and j in fetch_order[n_eager:] and j not in fetched:
                fetched.add(j)
                fetch(j)
            return project(xn, j)

        def late_weight(k, ref):
            if lazy_weights:
                fetch(k)
            return ref[...]

        f = lb + (1.0 - lb) * _sigmoid(proj(P_F))
        k_scr[...] = 1.0 - f
        q_scr[...] = _silu(proj(P_Q))
        vb_scr[...] = proj(P_I).astype(BF16)
        g_scr[...] = _block_cumsum(jnp.log2(f), C)

        g_floor = None
        for c in range(n_chunks):
            rows = slice(c * C, (c + 1) * C)
            gc = g_scr[rows, :]
            g_last = gc[C - 1:C, :]
            g_mid = gc[C // 2 - 1:C // 2, :]
            qc = q_scr[rows, :]
            kc = k_scr[rows, :]
            qt_scr[rows, :] = (qc * jnp.exp2(gc - g_mid)).astype(BF16)
            kt_scr[rows, :] = (kc * jnp.exp2(g_mid - gc)).astype(BF16)
            qin_scr[rows, :] = (qc * jnp.exp2(gc)).astype(BF16)
            kst_scr[rows, :] = (kc * jnp.exp2(g_last - gc)).astype(BF16)
            dec_scr[c:c + 1, :] = jnp.exp2(g_last)
            half_floor = jnp.minimum(g_mid, g_last - g_mid)
            g_floor = half_floor if g_floor is None else jnp.minimum(g_floor, half_floor)
        stable = jnp.min(g_floor) >= -MAX_HALF_CHUNK_LOG2_DECAY

        dec_cols = dec_scr[...].T
        keep = jnp.logical_and(
            lax.broadcasted_iota(jnp.int32, (C, C), 0) >= lax.broadcasted_iota(jnp.int32, (C, C), 1), stable)
        for c in range(n_chunks):
            rows = slice(c * C, (c + 1) * C)
            scores = [jnp.where(keep, _dot_nt(qt_scr[rows, sl], kt_scr[rows, sl]), 0.0).astype(BF16)
                      for sl in HEADS]
            for h, sl in enumerate(HEADS):
                vb = vb_scr[rows, sl]
                st = st_scr[h]
                o_scr[rows, sl] = _dot(jnp.concatenate([qin_scr[rows, sl], scores[h]], axis=1),
                                       jnp.concatenate([st.astype(BF16), vb], axis=0))
                decay = jnp.broadcast_to(dec_cols[sl, c:c + 1], (HEAD_DIM, HEAD_DIM))
                st_scr[h] = decay * st + _dot_tn(kst_scr[rows, sl], vb)

        @pl.when(jnp.logical_not(stable))
        def _intra_exact():
            same_head = (lax.broadcasted_iota(jnp.int32, (D_MODEL, D_MODEL), 0) // HEAD_DIM ==
                         lax.broadcasted_iota(jnp.int32, (D_MODEL, D_MODEL), 1) // HEAD_DIM)
            head_sum = jnp.where(same_head, 1.0, 0.0).astype(BF16)
            tpos = lax.broadcasted_iota(jnp.int32, (C, 1), 0)
            v_scr[...] = proj(P_I)

            def chunk_step(c, carry):
                r0 = pl.multiple_of(c * C, C)
                rows = pl.ds(r0, C)
                gc = g_scr[rows, :]
                qc = q_scr[rows, :]

                def src_step(i, carry2):
                    src = pl.ds(r0 + i, 1)
                    p = qc * jnp.exp2(jnp.minimum(gc - g_scr[src, :], 0.0)) * k_scr[src, :]
                    p = jnp.where(tpos >= i, p, 0.0).astype(BF16)
                    o_scr[rows, :] += _dot(p, head_sum) * v_scr[src, :]
                    return carry2

                return lax.fori_loop(0, C, src_step, carry)

            lax.fori_loop(0, n_chunks, chunk_step, 0)

        hn = hn_ref[...]
        silu_za = _silu(proj(P_ZA))
        gated = []
        for sl in HEADS:
            o_h = o_scr[:, sl]
            gated.append((o_h * _rms_scale(o_h) * hn * silu_za[:, sl]).astype(BF16))
        y_a = _dot(jnp.concatenate(gated, axis=1), late_weight(W_A, wa_ref))

        u = proj(P_C) * proj(P_H)
        u_scr[SUBLANES:SUBLANES + T, :] = u
        cw = cw_ref[...]
        conv = cw[CONV_W - 1:CONV_W, :] * u
        for j in range(CONV_W - 1):
            conv = conv + cw[j:j + 1, :] * u_scr[CTX_ROW0 + j:CTX_ROW0 + j + T, :]
        y_b = _dot((proj(P_B) * conv * _silu(proj(P_ZB))).astype(BF16), late_weight(W_B, wb_ref))
        u_scr[CTX_ROW0:SUBLANES, :] = u[T - (CONV_W - 1):T, :]

        merged = _sigmoid(proj(P_GA)) * y_a + _sigmoid(proj(P_GB)) * y_b
        out = _dot(merged.astype(BF16), late_weight(W_O, wo_ref))
        y_ref[0] = x_ref[0] + out * _rms_scale(out) * npost_ref[...]

    @pl.when(first_step)
    def _first_tile():
        run_tile(lazy_weights=True)
        for k in range(len(groups)):
            store_copy(k).start(priority=1)

    @pl.when(jnp.logical_not(first_step))
    def _tile():
        run_tile(lazy_weights=False)

    @pl.when(t == pl.num_programs(1) - 1)
    def _finish():
        c_out_ref[0] = u_scr[CTX_ROW0:SUBLANES, :]
        st_out_ref[0] = st_scr[...]

    @pl.when(last_step)
    def _weights_written():
        for i in range(len(groups)):
            store_copy(i).wait()


def _const_spec(shape):
    zeros = (0,) * len(shape)
    return pl.BlockSpec(shape, lambda b, t: zeros, pipeline_mode=pl.Buffered(1))


def _run_prompt(x, meta, weights, *, tile, chunk):
    n, length, _ = x.shape
    assert length % tile == 0 and tile % chunk == 0 and chunk % (2 * SUBLANES) == 0 and tile >= CONV_W - 1
    assert meta.shape[0] % (2 * SUBLANES) == 0 and meta.shape[0] >= CONV_W - 1
    assert tile // chunk <= HEAD_DIM
    w_in, npre, npost, lb_logits, hn, cw, wa, wb, wo = weights
    kern = functools.partial(_seq_kernel, tile=tile, chunk=chunk)
    hbm = pl.BlockSpec(memory_space=pl.ANY)
    n_groups = w_in.shape[1] // D_MODEL + 3
    assert w_in.shape[1] % D_MODEL == 0 and wa.shape == wb.shape == wo.shape == (w_in.shape[0], D_MODEL)
    state_shape = (1, N_HEADS, HEAD_DIM, HEAD_DIM)
    ctx_shape = (1, CONV_W - 1, D_MODEL)
    tile_f32 = pltpu.VMEM((tile, D_MODEL), F32)
    tile_bf16 = pltpu.VMEM((tile, D_MODEL), BF16)
    return pl.pallas_call(
        kern,
        grid=(n, length // tile),
        in_specs=[
            pl.BlockSpec((1, tile, D_MODEL), lambda b, t: (b, t, 0)),
            _const_spec(meta.shape),
            hbm,
            _const_spec(npre.shape),
            _const_spec(npost.shape),
            _const_spec(lb_logits.shape),
            _const_spec(hn.shape),
            _const_spec(cw.shape),
            hbm, hbm, hbm,
        ],
        out_specs=[
            pl.BlockSpec((1, tile, D_MODEL), lambda b, t: (b, t, 0)),
            pl.BlockSpec(state_shape, lambda b, t: (b, 0, 0, 0)),
            pl.BlockSpec(ctx_shape, lambda b, t: (b, 0, 0)),
            hbm, hbm, hbm, hbm,
        ],
        out_shape=[
            jax.ShapeDtypeStruct(x.shape, F32),
            jax.ShapeDtypeStruct((n,) + state_shape[1:], F32),
            jax.ShapeDtypeStruct((n,) + ctx_shape[1:], F32),
            jax.ShapeDtypeStruct(w_in.shape, BF16),
            jax.ShapeDtypeStruct(wa.shape, BF16),
            jax.ShapeDtypeStruct(wb.shape, BF16),
            jax.ShapeDtypeStruct(wo.shape, BF16),
        ],
        scratch_shapes=[
            pltpu.VMEM((N_HEADS, HEAD_DIM, HEAD_DIM), F32),
            pltpu.VMEM((tile + SUBLANES, D_MODEL), F32),
            tile_f32, tile_f32, tile_f32, tile_f32, tile_f32,
            tile_bf16, tile_bf16, tile_bf16, tile_bf16, tile_bf16,
            pltpu.VMEM((HEAD_DIM, D_MODEL), F32),
            pltpu.VMEM((N_HEADS, HEAD_DIM, HEAD_DIM), F32),
            pltpu.VMEM((SUBLANES, D_MODEL), F32),
            pltpu.VMEM(w_in.shape, BF16),
            pltpu.VMEM(wa.shape, BF16), pltpu.VMEM(wb.shape, BF16), pltpu.VMEM(wo.shape, BF16),
            pltpu.VMEM((2,) + wa.shape, F32),
            pltpu.SemaphoreType.DMA((2, LOAD_BANDS)),
            pltpu.SemaphoreType.DMA((n_groups,)),
        ],
        compiler_params=pltpu.CompilerParams(
            dimension_semantics=("arbitrary", "arbitrary"),
            vmem_limit_bytes=V7X_VMEM_LIMIT_BYTES),
        name="prompt_sweep",
    )(x, meta, w_in, npre, npost, lb_logits, hn, cw, wa, wb, wo)


def _decode_kernel(x_ref, st_hbm, ctx_ref, w_in_hbm, npre_ref, npost_ref, lb_ref, hn_ref, cw_ref,
                   wa_hbm, wb_hbm, wo_hbm,
                   y_ref, st_out_hbm, ctx_out_ref,
                   ft_scr, q_scr, v_scr, o_scr, za_scr, pb_scr, ga_scr, gb_scr,
                   in_buf, out_buf, in_sem, out_sem, w_in_ref, wa_ref, wb_ref, wo_ref, w_sem):
    G = DECODE_GROUP
    i = pl.program_id(0)
    n_steps = pl.num_programs(0)
    n_rows = x_ref.shape[0]
    part_rows = G // STATE_PARTS

    def in_copies(step):
        slot = step % STATE_IN_SLOTS
        return [pltpu.make_async_copy(st_hbm.at[pl.ds(step * G + p * part_rows, part_rows)],
                                      in_buf.at[slot, pl.ds(p * part_rows, part_rows)],
                                      in_sem.at[slot, p]) for p in range(STATE_PARTS)]

    def out_copies(step):
        slot = step % STATE_OUT_SLOTS
        return [pltpu.make_async_copy(out_buf.at[slot, pl.ds(p * part_rows, part_rows)],
                                      st_out_hbm.at[pl.ds(step * G + p * part_rows, part_rows)],
                                      out_sem.at[slot, p]) for p in range(STATE_PARTS)]

    n_proj = w_in_hbm.shape[1] // D_MODEL
    w_parts = [(w_in_hbm.at[:, pl.ds(j * D_MODEL, D_MODEL)], w_in_ref.at[:, pl.ds(j * D_MODEL, D_MODEL)])
               for j in range(n_proj)] + [(wa_hbm, wa_ref), (wb_hbm, wb_ref), (wo_hbm, wo_ref)]
    W_A, W_B, W_O = n_proj, n_proj + 1, n_proj + 2

    def w_copy(k):
        return pltpu.make_async_copy(w_parts[k][0], w_parts[k][1], w_sem.at[k])

    @pl.when(i == 0)
    def _prime():
        for step in range(STATE_IN_SLOTS - 1):
            for copy in in_copies(step):
                copy.start()
        for k in (P_F, P_Q, P_I, P_ZA, P_C, P_H, P_B, P_ZB, P_GA, P_GB, W_A, W_B, W_O):
            w_copy(k).start(priority=1)

    @pl.when(i == 0)
    def _project():
        x = x_ref[:, 0, :]
        xn = (x * _rms_scale(x) * npre_ref[...]).astype(BF16)

        def proj(j):
            w_copy(j).wait()
            return _dot(xn, w_in_ref[:, j * D_MODEL:(j + 1) * D_MODEL])

        lb = _lower_bound(lb_ref[...])
        f = lb + (1.0 - lb) * _sigmoid(proj(P_F))
        ft_scr[...] = f.T
        q_scr[...] = _silu(proj(P_Q))
        v_scr[...] = proj(P_I)
        za_scr[...] = _silu(proj(P_ZA))
        u = proj(P_C) * proj(P_H)
        cw = cw_ref[...]
        conv = cw[CONV_W - 1:CONV_W, :] * u
        for j in range(CONV_W - 1):
            ctx_j = ctx_ref[:, j, :]
            conv = conv + cw[j:j + 1, :] * ctx_j
            if j > 0:
                ctx_out_ref[:, j - 1, :] = ctx_j
        ctx_out_ref[:, CONV_W - 2, :] = u
        pb_scr[...] = proj(P_B) * conv * _silu(proj(P_ZB))
        ga_scr[...] = _sigmoid(proj(P_GA))
        gb_scr[...] = _sigmoid(proj(P_GB))

    @pl.when(i + STATE_IN_SLOTS - 1 < n_steps)
    def _read_ahead():
        for copy in in_copies(i + STATE_IN_SLOTS - 1):
            copy.start()

    @pl.when(i >= STATE_OUT_SLOTS)
    def _slot_written_back():
        for copy in out_copies(i - STATE_OUT_SLOTS):
            copy.wait()

    for copy in in_copies(i):
        copy.wait()
    st_ref = in_buf.at[i % STATE_IN_SLOTS]
    st_out_ref = out_buf.at[i % STATE_OUT_SLOTS]

    shift = (n_rows - i * G) % n_rows
    f_cols = pltpu.roll(ft_scr[...], shift, 1)
    r0 = pl.multiple_of(i * G, G)
    v_rows = v_scr[pl.ds(r0, G), :]
    q_rows = q_scr[pl.ds(r0, G), :].astype(BF16)
    row_id = lax.broadcasted_iota(jnp.int32, (G, HEAD_DIM), 0)
    o_heads = [jnp.zeros((G, HEAD_DIM), F32)] * N_HEADS
    for j in range(G):
        for h, sl in enumerate(HEADS):
            f_b = jnp.broadcast_to(f_cols[sl, j:j + 1], (HEAD_DIM, HEAD_DIM))
            s_new = f_b * st_ref[j, h] + (1.0 - f_b) * v_rows[j:j + 1, sl]
            st_out_ref[j, h] = s_new
            read = _dot(q_rows[:, sl], s_new.astype(BF16))
            o_heads[h] = jnp.where(row_id == j, read, o_heads[h])
    o_scr[pl.ds(r0, G), :] = jnp.concatenate(o_heads, axis=1)
    for copy in out_copies(i):
        copy.start(priority=1)

    @pl.when(i == pl.num_programs(0) - 1)
    def _output():
        hn = hn_ref[...]
        for sl in HEADS:
            o_h = o_scr[:, sl]
            o_scr[:, sl] = o_h * _rms_scale(o_h) * hn
        for k in (W_A, W_B, W_O):
            w_copy(k).wait()
        y_a = _dot((o_scr[...] * za_scr[...]).astype(BF16), wa_ref[...])
        y_b = _dot(pb_scr[...].astype(BF16), wb_ref[...])
        merged = ga_scr[...] * y_a + gb_scr[...] * y_b
        out = _dot(merged.astype(BF16), wo_ref[...])
        y_ref[:, 0, :] = x_ref[:, 0, :] + out * _rms_scale(out) * npost_ref[...]
        for back in range(STATE_OUT_SLOTS):
            for copy in out_copies(i - back):
                copy.wait()


def _run_decode(x, state, ctx, weights):
    n = x.shape[0]
    assert n % DECODE_GROUP == 0 and n == 128 and DECODE_GROUP % STATE_PARTS == 0
    assert n // DECODE_GROUP >= max(STATE_IN_SLOTS, STATE_OUT_SLOTS)
    w_in, npre, npost, lb_logits, hn, cw, wa, wb, wo = weights

    def const(shape):
        zeros = (0,) * len(shape)
        return pl.BlockSpec(shape, lambda i: zeros, pipeline_mode=pl.Buffered(1))

    hbm = pl.BlockSpec(memory_space=pl.ANY)
    in_slots = pltpu.VMEM((STATE_IN_SLOTS, DECODE_GROUP) + state.shape[1:], F32)
    out_slots = pltpu.VMEM((STATE_OUT_SLOTS, DECODE_GROUP) + state.shape[1:], F32)
    rows_f32 = pltpu.VMEM((n, D_MODEL), F32)
    cols_f32 = pltpu.VMEM((D_MODEL, n), F32)
    return pl.pallas_call(
        _decode_kernel,
        grid=(n // DECODE_GROUP,),
        in_specs=[const(x.shape), hbm, const(ctx.shape), hbm, const(npre.shape),
                  const(npost.shape), const(lb_logits.shape), const(hn.shape), const(cw.shape),
                  hbm, hbm, hbm],
        out_specs=[const(x.shape), hbm, const(ctx.shape)],
        out_shape=[jax.ShapeDtypeStruct(x.shape, F32),
                   jax.ShapeDtypeStruct(state.shape, F32),
                   jax.ShapeDtypeStruct(ctx.shape, F32)],
        scratch_shapes=[cols_f32, rows_f32, rows_f32, rows_f32, rows_f32, rows_f32, rows_f32, rows_f32,
                        in_slots, out_slots,
                        pltpu.SemaphoreType.DMA((STATE_IN_SLOTS, STATE_PARTS)),
                        pltpu.SemaphoreType.DMA((STATE_OUT_SLOTS, STATE_PARTS)),
                        pltpu.VMEM(w_in.shape, BF16), pltpu.VMEM(wa.shape, BF16),
                        pltpu.VMEM(wb.shape, BF16), pltpu.VMEM(wo.shape, BF16),
                        pltpu.SemaphoreType.DMA((w_in.shape[1] // D_MODEL + 3,))],
        compiler_params=pltpu.CompilerParams(
            dimension_semantics=("arbitrary",),
            vmem_limit_bytes=V7X_VMEM_LIMIT_BYTES),
        name="decode_step",
    )(x, state, ctx, w_in, npre, npost, lb_logits, hn, cw, wa, wb, wo)


def kernel(x_prompt, x_sample, state_hgrn, state_conv, meta_tokens, w_in, norm_pre, norm_post, lb_logits,
           hgrn_norm, conv_w, w_a, w_b, w_o):
    depth = w_in.shape[0]
    assert depth == 1, "single-layer trunk"
    assert x_sample.shape[1] == 1, "one new token per decode row"

    weights = (w_in[0], norm_pre, norm_post, lb_logits, hgrn_norm, conv_w[0], w_a[0], w_b[0], w_o[0])

    y_prompt, hgrn_p, conv_p, w_in_bf, wa_bf, wb_bf, wo_bf = _run_prompt(
        x_prompt, meta_tokens.astype(x_prompt.dtype), weights, tile=PROMPT_TILE, chunk=PROMPT_CHUNK)
    weights = (w_in_bf,) + weights[1:6] + (wa_bf, wb_bf, wo_bf)

    y_s, hgrn_s, conv_s = _run_decode(x_sample, state_hgrn[0], state_conv[0], weights)

    return (y_prompt, y_s, hgrn_p[None], hgrn_s[None], conv_p[None], conv_s[None])
```

```python
import functools

import jax
import jax.numpy as jnp
from jax import lax
from jax.experimental import pallas as pl
from jax.experimental.pallas import tpu as pltpu

D_MODEL = 1024
N_HEADS = 8
HEAD_DIM = D_MODEL // N_HEADS
CONV_W = 3
EPS = 1e-6
NEG_LOG2_E = -1.4426950408889634
P_Q, P_F, P_I, P_ZA, P_B, P_C, P_H, P_ZB, P_GA, P_GB = range(10)

PROMPT_TILE = 256
PROMPT_CHUNK = 128
DECODE_GROUP = 8
STATE_IN_SLOTS = 3
STATE_OUT_SLOTS = 2
STATE_PARTS = 8
LOAD_BANDS = 4
SUBLANES = 8
CTX_ROW0 = SUBLANES - (CONV_W - 1)
MAX_HALF_CHUNK_LOG2_DECAY = 115.0
V7X_VMEM_LIMIT_BYTES = 58 * 1024 * 1024

BF16 = jnp.bfloat16
F32 = jnp.float32
HEADS = [slice(h * HEAD_DIM, (h + 1) * HEAD_DIM) for h in range(N_HEADS)]


def _dot(a, b):
    return jnp.dot(a, b, preferred_element_type=F32)


def _dot_nt(a, b):
    return lax.dot_general(a, b, (((1,), (1,)), ((), ())), preferred_element_type=F32)


def _dot_tn(a, b):
    return lax.dot_general(a, b, (((0,), (0,)), ((), ())), preferred_element_type=F32)


def _sigmoid(x):
    return pl.reciprocal(1.0 + jnp.exp2(x * NEG_LOG2_E), approx=True)


def _silu(x):
    return x * _sigmoid(x)


def _rms_scale(x):
    return lax.rsqrt(jnp.mean(x * x, axis=-1, keepdims=True) + EPS)


def _lower_bound(lb_logits):
    m = jnp.max(lb_logits, axis=0, keepdims=True)
    e = jnp.exp(lb_logits - m)
    return e[0:1, :] / jnp.sum(e, axis=0, keepdims=True)


def _block_cumsum(x, block):
    n = x.shape[0]
    ri = lax.broadcasted_iota(jnp.int32, (n, n), 0)
    ci = lax.broadcasted_iota(jnp.int32, (n, n), 1)
    tri = ri >= ci
    if n != block:
        tri = jnp.logical_and(tri, (ri // block) == (ci // block))
    tri = jnp.where(tri, 1.0, 0.0).astype(BF16)
    hi = x.astype(BF16)
    lo = (x - hi.astype(F32)).astype(BF16)
    return _dot(tri, hi) + _dot(tri, lo)


def _seq_kernel(x_ref, meta_ref, w_in_hbm, npre_ref, npost_ref, lb_ref, hn_ref, cw_ref,
                wa_hbm, wb_hbm, wo_hbm,
                y_ref, st_out_ref, c_out_ref, w_in_out, wa_out, wb_out, wo_out,
                st_scr, u_scr, q_scr, k_scr, v_scr, g_scr, o_scr,
                qt_scr, kt_scr, qin_scr, kst_scr, vb_scr, dec_scr, st_meta_scr, ctx_meta_scr,
                w_in_ref, wa_ref, wb_ref, wo_ref, stage_scr, load_sem, store_sem,
                *, tile, chunk):
    T, C = tile, chunk
    n_chunks = T // C
    t = pl.program_id(1)
    first_step = jnp.logical_and(pl.program_id(0) == 0, t == 0)
    last_step = jnp.logical_and(pl.program_id(0) == pl.num_programs(0) - 1, t == pl.num_programs(1) - 1)
    lb = _lower_bound(lb_ref[...])

    col = [pl.ds(j * D_MODEL, D_MODEL) for j in range(w_in_hbm.shape[1] // D_MODEL)]
    groups = [(w_in_hbm.at[:, c], w_in_ref.at[:, c], w_in_out.at[:, c]) for c in col]
    groups += [(wa_hbm, wa_ref, wa_out), (wb_hbm, wb_ref, wb_out), (wo_hbm, wo_ref, wo_out)]

    def load_copies(p):
        src = groups[fetch_order[p]][0]
        band = src.shape[0] // LOAD_BANDS
        return [pltpu.make_async_copy(src.at[pl.ds(b * band, band), :],
                                      stage_scr.at[p % 2, pl.ds(b * band, band), :],
                                      load_sem.at[p % 2, b]) for b in range(LOAD_BANDS)]

    def store_copy(i):
        return pltpu.make_async_copy(groups[i][1], groups[i][2], store_sem.at[i])

    W_A, W_B, W_O = len(col), len(col) + 1, len(col) + 2
    fetch_order = [P_F, P_I, P_C, P_H, P_Q, P_ZA, W_A, P_B, P_ZB, W_B, P_GA, P_GB, W_O]
    n_eager = 4
    assert sorted(fetch_order) == list(range(len(groups)))

    def fetch(k):
        p = fetch_order.index(k)
        for copy in load_copies(p):
            copy.wait()
        groups[k][1][...] = stage_scr[p % 2].astype(BF16)
        if p + 2 < len(fetch_order):
            for copy in load_copies(p + 2):
                copy.start()

    @pl.when(first_step)
    def _stream_first_weights():
        for p in range(2):
            for copy in load_copies(p):
                copy.start()

    def project(xn, j):
        return _dot(xn, w_in_ref[:, j * D_MODEL:(j + 1) * D_MODEL])

    @pl.when(first_step)
    def _meta_prefix():
        xm = meta_ref[...]
        n_meta = xm.shape[0]
        xn_m = (xm * _rms_scale(xm) * npre_ref[...]).astype(BF16)

        def meta_proj(j):
            fetch(j)
            return project(xn_m, j)

        f_m = lb + (1.0 - lb) * _sigmoid(meta_proj(P_F))
        g_m = _block_cumsum(jnp.log2(f_m), n_meta)
        k_end = ((1.0 - f_m) * jnp.exp2(g_m[n_meta - 1:n_meta, :] - g_m)).astype(BF16)
        v_m = meta_proj(P_I).astype(BF16)
        for h, sl in enumerate(HEADS):
            st_meta_scr[h] = _dot_tn(k_end[:, sl], v_m[:, sl])
        u_m = meta_proj(P_C) * meta_proj(P_H)
        ctx_meta_scr[CTX_ROW0:SUBLANES, :] = u_m[n_meta - (CONV_W - 1):n_meta, :]
        dec_scr[...] = jnp.zeros(dec_scr.shape, F32)

    @pl.when(t == 0)
    def _init():
        st_scr[...] = st_meta_scr[...]
        u_scr[CTX_ROW0:SUBLANES, :] = ctx_meta_scr[CTX_ROW0:SUBLANES, :]

    def run_tile(lazy_weights):
        fetched = set()
        x = x_ref[0]
        xn = (x * _rms_scale(x) * npre_ref[...]).astype(BF16)

        def proj(j):
            if lazy_weights and j in fetch_order[n_eager:] and j not in fetched:
                fetched.add(j)
                fetch(j)
            return project(xn, j)

        def late_weight(k, ref):
            if lazy_weights:
                fetch(k)
            return ref[...]

        f = lb + (1.0 - lb) * _sigmoid(proj(P_F))
        k_scr[...] = 1.0 - f
        q_scr[...] = _silu(proj(P_Q))
        vb_scr[...] = proj(P_I).astype(BF16)
        g_scr[...] = _block_cumsum(jnp.log2(f), C)

        g_floor = None
        for c in range(n_chunks):
            rows = slice(c * C, (c + 1) * C)
            gc = g_scr[rows, :]
            g_last = gc[C - 1:C, :]
            g_mid = gc[C // 2 - 1:C // 2, :]
            qc = q_scr[rows, :]
            kc = k_scr[rows, :]
            qt_scr[rows, :] = (qc * jnp.exp2(gc - g_mid)).astype(BF16)
            kt_scr[rows, :] = (kc * jnp.exp2(g_mid - gc)).astype(BF16)
            qin_scr[rows, :] = (qc * jnp.exp2(gc)).astype(BF16)
            kst_scr[rows, :] = (kc * jnp.exp2(g_last - gc)).astype(BF16)
            dec_scr[c:c + 1, :] = jnp.exp2(g_last)
            half_floor = jnp.minimum(g_mid, g_last - g_mid)
            g_floor = half_floor if g_floor is None else jnp.minimum(g_floor, half_floor)
        stable = jnp.min(g_floor) >= -MAX_HALF_CHUNK_LOG2_DECAY

        dec_cols = dec_scr[...].T
        keep = jnp.logical_and(
            lax.broadcasted_iota(jnp.int32, (C, C), 0) >= lax.broadcasted_iota(jnp.int32, (C, C), 1), stable)
        for c in range(n_chunks):
            rows = slice(c * C, (c + 1) * C)
            scores = [jnp.where(keep, _dot_nt(qt_scr[rows, sl], kt_scr[rows, sl]), 0.0).astype(BF16)
                      for sl in HEADS]
            for h, sl in enumerate(HEADS):
                vb = vb_scr[rows, sl]
                st = st_scr[h]
                o_scr[rows, sl] = _dot(jnp.concatenate([qin_scr[rows, sl], scores[h]], axis=1),
                                       jnp.concatenate([st.astype(BF16), vb], axis=0))
                decay = jnp.broadcast_to(dec_cols[sl, c:c + 1], (HEAD_DIM, HEAD_DIM))
                st_scr[h] = decay * st + _dot_tn(kst_scr[rows, sl], vb)

        @pl.when(jnp.logical_not(stable))
        def _intra_exact():
            same_head = (lax.broadcasted_iota(jnp.int32, (D_MODEL, D_MODEL), 0) // HEAD_DIM ==
                         lax.broadcasted_iota(jnp.int32, (D_MODEL, D_MODEL), 1) // HEAD_DIM)
            head_sum = jnp.where(same_head, 1.0, 0.0).astype(BF16)
            tpos = lax.broadcasted_iota(jnp.int32, (C, 1), 0)
            v_scr[...] = proj(P_I)

            def chunk_step(c, carry):
                r0 = pl.multiple_of(c * C, C)
                rows = pl.ds(r0, C)
                gc = g_scr[rows, :]
                qc = q_scr[rows, :]

                def src_step(i, carry2):
                    src = pl.ds(r0 + i, 1)
                    p = qc * jnp.exp2(jnp.minimum(gc - g_scr[src, :], 0.0)) * k_scr[src, :]
                    p = jnp.where(tpos >= i, p, 0.0).astype(BF16)
                    o_scr[rows, :] += _dot(p, head_sum) * v_scr[src, :]
                    return carry2

                return lax.fori_loop(0, C, src_step, carry)

            lax.fori_loop(0, n_chunks, chunk_step, 0)

        hn = hn_ref[...]
        silu_za = _silu(proj(P_ZA))
        gated = []
        for sl in HEADS:
            o_h = o_scr[:, sl]
            gated.append((o_h * _rms_scale(o_h) * hn * silu_za[:, sl]).astype(BF16))
        y_a = _dot(jnp.concatenate(gated, axis=1), late_weight(W_A, wa_ref))

        u = proj(P_C) * proj(P_H)
        u_scr[SUBLANES:SUBLANES + T, :] = u
        cw = cw_ref[...]
        conv = cw[CONV_W - 1:CONV_W, :] * u
        for j in range(CONV_W - 1):
            conv = conv + cw[j:j + 1, :] * u_scr[CTX_ROW0 + j:CTX_ROW0 + j + T, :]
        y_b = _dot((proj(P_B) * conv * _silu(proj(P_ZB))).astype(BF16), late_weight(W_B, wb_ref))
        u_scr[CTX_ROW0:SUBLANES, :] = u[T - (CONV_W - 1):T, :]

        merged = _sigmoid(proj(P_GA)) * y_a + _sigmoid(proj(P_GB)) * y_b
        out = _dot(merged.astype(BF16), late_weight(W_O, wo_ref))
        y_ref[0] = x_ref[0] + out * _rms_scale(out) * npost_ref[...]

    @pl.when(first_step)
    def _first_tile():
        run_tile(lazy_weights=True)
        for k in range(len(groups)):
            store_copy(k).start()

    @pl.when(jnp.logical_not(first_step))
    def _tile():
        run_tile(lazy_weights=False)

    @pl.when(t == pl.num_programs(1) - 1)
    def _finish():
        c_out_ref[0] = u_scr[CTX_ROW0:SUBLANES, :]
        st_out_ref[0] = st_scr[...]

    @pl.when(last_step)
    def _weights_written():
        for i in range(len(groups)):
            store_copy(i).wait()


def _const_spec(shape):
    zeros = (0,) * len(shape)
    return pl.BlockSpec(shape, lambda b, t: zeros, pipeline_mode=pl.Buffered(1))


def _run_prompt(x, meta, weights, *, tile, chunk):
    n, length, _ = x.shape
    assert length % tile == 0 and tile % chunk == 0 and chunk % (2 * SUBLANES) == 0 and tile >= CONV_W - 1
    assert meta.shape[0] % (2 * SUBLANES) == 0 and meta.shape[0] >= CONV_W - 1
    assert tile // chunk <= HEAD_DIM
    w_in, npre, npost, lb_logits, hn, cw, wa, wb, wo = weights
    kern = functools.partial(_seq_kernel, tile=tile, chunk=chunk)
    hbm = pl.BlockSpec(memory_space=pl.ANY)
    n_groups = w_in.shape[1] // D_MODEL + 3
    assert w_in.shape[1] % D_MODEL == 0 and wa.shape == wb.shape == wo.shape == (w_in.shape[0], D_MODEL)
    state_shape = (1, N_HEADS, HEAD_DIM, HEAD_DIM)
    ctx_shape = (1, CONV_W - 1, D_MODEL)
    tile_f32 = pltpu.VMEM((tile, D_MODEL), F32)
    tile_bf16 = pltpu.VMEM((tile, D_MODEL), BF16)
    return pl.pallas_call(
        kern,
        grid=(n, length // tile),
        in_specs=[
            pl.BlockSpec((1, tile, D_MODEL), lambda b, t: (b, t, 0)),
            _const_spec(meta.shape),
            hbm,
            _const_spec(npre.shape),
            _const_spec(npost.shape),
            _const_spec(lb_logits.shape),
            _const_spec(hn.shape),
            _const_spec(cw.shape),
            hbm, hbm, hbm,
        ],
        out_specs=[
            pl.BlockSpec((1, tile, D_MODEL), lambda b, t: (b, t, 0)),
            pl.BlockSpec(state_shape, lambda b, t: (b, 0, 0, 0)),
            pl.BlockSpec(ctx_shape, lambda b, t: (b, 0, 0)),
            hbm, hbm, hbm, hbm,
        ],
        out_shape=[
            jax.ShapeDtypeStruct(x.shape, F32),
            jax.ShapeDtypeStruct((n,) + state_shape[1:], F32),
            jax.ShapeDtypeStruct((n,) + ctx_shape[1:], F32),
            jax.ShapeDtypeStruct(w_in.shape, BF16),
            jax.ShapeDtypeStruct(wa.shape, BF16),
            jax.ShapeDtypeStruct(wb.shape, BF16),
            jax.ShapeDtypeStruct(wo.shape, BF16),
        ],
        scratch_shapes=[
            pltpu.VMEM((N_HEADS, HEAD_DIM, HEAD_DIM), F32),
            pltpu.VMEM((tile + SUBLANES, D_MODEL), F32),
            tile_f32, tile_f32, tile_f32, tile_f32, tile_f32,
            tile_bf16, tile_bf16, tile_bf16, tile_bf16, tile_bf16,
            pltpu.VMEM((HEAD_DIM, D_MODEL), F32),
            pltpu.VMEM((N_HEADS, HEAD_DIM, HEAD_DIM), F32),
            pltpu.VMEM((SUBLANES, D_MODEL), F32),
            pltpu.VMEM(w_in.shape, BF16),
            pltpu.VMEM(wa.shape, BF16), pltpu.VMEM(wb.shape, BF16), pltpu.VMEM(wo.shape, BF16),
            pltpu.VMEM((2,) + wa.shape, F32),
            pltpu.SemaphoreType.DMA((2, LOAD_BANDS)),
            pltpu.SemaphoreType.DMA((n_groups,)),
        ],
        compiler_params=pltpu.CompilerParams(
            dimension_semantics=("arbitrary", "arbitrary"),
            vmem_limit_bytes=V7X_VMEM_LIMIT_BYTES),
        name="prompt_sweep",
    )(x, meta, w_in, npre, npost, lb_logits, hn, cw, wa, wb, wo)


def _decode_kernel(x_ref, st_hbm, ctx_ref, w_in_hbm, npre_ref, npost_ref, lb_ref, hn_ref, cw_ref,
                   wa_hbm, wb_hbm, wo_hbm,
                   y_ref, st_out_hbm, ctx_out_ref,
                   ft_scr, q_scr, v_scr, o_scr, za_scr, pb_scr, ga_scr, gb_scr,
                   in_buf, out_buf, in_sem, out_sem, w_in_ref, wa_ref, wb_ref, wo_ref, w_sem):
    G = DECODE_GROUP
    i = pl.program_id(0)
    n_steps = pl.num_programs(0)
    n_rows = x_ref.shape[0]
    part_rows = G // STATE_PARTS

    def in_copies(step):
        slot = step % STATE_IN_SLOTS
        return [pltpu.make_async_copy(st_hbm.at[pl.ds(step * G + p * part_rows, part_rows)],
                                      in_buf.at[slot, pl.ds(p * part_rows, part_rows)],
                                      in_sem.at[slot, p]) for p in range(STATE_PARTS)]

    def out_copies(step):
        slot = step % STATE_OUT_SLOTS
        return [pltpu.make_async_copy(out_buf.at[slot, pl.ds(p * part_rows, part_rows)],
                                      st_out_hbm.at[pl.ds(step * G + p * part_rows, part_rows)],
                                      out_sem.at[slot, p]) for p in range(STATE_PARTS)]

    n_proj = w_in_hbm.shape[1] // D_MODEL
    w_parts = [(w_in_hbm.at[:, pl.ds(j * D_MODEL, D_MODEL)], w_in_ref.at[:, pl.ds(j * D_MODEL, D_MODEL)])
               for j in range(n_proj)] + [(wa_hbm, wa_ref), (wb_hbm, wb_ref), (wo_hbm, wo_ref)]
    W_A, W_B, W_O = n_proj, n_proj + 1, n_proj + 2

    def w_copy(k):
        return pltpu.make_async_copy(w_parts[k][0], w_parts[k][1], w_sem.at[k])

    @pl.when(i == 0)
    def _prime():
        for step in range(STATE_IN_SLOTS - 1):
            for copy in in_copies(step):
                copy.start()
        for k in (P_F, P_Q, P_I, P_ZA, P_C, P_H, P_B, P_ZB, P_GA, P_GB, W_A, W_B, W_O):
            w_copy(k).start()

    @pl.when(i == 0)
    def _project():
        x = x_ref[:, 0, :]
        xn = (x * _rms_scale(x) * npre_ref[...]).astype(BF16)

        def proj(j):
            w_copy(j).wait()
            return _dot(xn, w_in_ref[:, j * D_MODEL:(j + 1) * D_MODEL])

        lb = _lower_bound(lb_ref[...])
        f = lb + (1.0 - lb) * _sigmoid(proj(P_F))
        ft_scr[...] = f.T
        q_scr[...] = _silu(proj(P_Q))
        v_scr[...] = proj(P_I)
        za_scr[...] = _silu(proj(P_ZA))
        u = proj(P_C) * proj(P_H)
        cw = cw_ref[...]
        conv = cw[CONV_W - 1:CONV_W, :] * u
        for j in range(CONV_W - 1):
            ctx_j = ctx_ref[:, j, :]
            conv = conv + cw[j:j + 1, :] * ctx_j
            if j > 0:
                ctx_out_ref[:, j - 1, :] = ctx_j
        ctx_out_ref[:, CONV_W - 2, :] = u
        pb_scr[...] = proj(P_B) * conv * _silu(proj(P_ZB))
        ga_scr[...] = _sigmoid(proj(P_GA))
        gb_scr[...] = _sigmoid(proj(P_GB))

    @pl.when(i + STATE_IN_SLOTS - 1 < n_steps)
    def _read_ahead():
        for copy in in_copies(i + STATE_IN_SLOTS - 1):
            copy.start()

    @pl.when(i >= STATE_OUT_SLOTS)
    def _slot_written_back():
        for copy in out_copies(i - STATE_OUT_SLOTS):
            copy.wait()

    for copy in in_copies(i):
        copy.wait()
    st_ref = in_buf.at[i % STATE_IN_SLOTS]
    st_out_ref = out_buf.at[i % STATE_OUT_SLOTS]

    shift = (n_rows - i * G) % n_rows
    f_cols = pltpu.roll(ft_scr[...], shift, 1)
    r0 = pl.multiple_of(i * G, G)
    v_rows = v_scr[pl.ds(r0, G), :]
    q_rows = q_scr[pl.ds(r0, G), :].astype(BF16)
    row_id = lax.broadcasted_iota(jnp.int32, (G, HEAD_DIM), 0)
    o_heads = [jnp.zeros((G, HEAD_DIM), F32)] * N_HEADS
    for j in range(G):
        for h, sl in enumerate(HEADS):
            f_b = jnp.broadcast_to(f_cols[sl, j:j + 1], (HEAD_DIM, HEAD_DIM))
            s_new = f_b * st_ref[j, h] + (1.0 - f_b) * v_rows[j:j + 1, sl]
            st_out_ref[j, h] = s_new
            read = _dot(q_rows[:, sl], s_new.astype(BF16))
            o_heads[h] = jnp.where(row_id == j, read, o_heads[h])
    o_scr[pl.ds(r0, G), :] = jnp.concatenate(o_heads, axis=1)
    for copy in out_copies(i):
        copy.start()

    @pl.when(i == pl.num_programs(0) - 1)
    def _output():
        hn = hn_ref[...]
        for sl in HEADS:
            o_h = o_scr[:, sl]
            o_scr[:, sl] = o_h * _rms_scale(o_h) * hn
        for k in (W_A, W_B, W_O):
            w_copy(k).wait()
        y_a = _dot((o_scr[...] * za_scr[...]).astype(BF16), wa_ref[...])
        y_b = _dot(pb_scr[...].astype(BF16), wb_ref[...])
        merged = ga_scr[...] * y_a + gb_scr[...] * y_b
        out = _dot(merged.astype(BF16), wo_ref[...])
        y_ref[:, 0, :] = x_ref[:, 0, :] + out * _rms_scale(out) * npost_ref[...]
        for back in range(STATE_OUT_SLOTS):
            for copy in out_copies(i - back):
                copy.wait()


def _run_decode(x, state, ctx, weights):
    n = x.shape[0]
    assert n % DECODE_GROUP == 0 and n == 128 and DECODE_GROUP % STATE_PARTS == 0
    assert n // DECODE_GROUP >= max(STATE_IN_SLOTS, STATE_OUT_SLOTS)
    w_in, npre, npost, lb_logits, hn, cw, wa, wb, wo = weights

    def const(shape):
        zeros = (0,) * len(shape)
        return pl.BlockSpec(shape, lambda i: zeros, pipeline_mode=pl.Buffered(1))

    hbm = pl.BlockSpec(memory_space=pl.ANY)
    in_slots = pltpu.VMEM((STATE_IN_SLOTS, DECODE_GROUP) + state.shape[1:], F32)
    out_slots = pltpu.VMEM((STATE_OUT_SLOTS, DECODE_GROUP) + state.shape[1:], F32)
    rows_f32 = pltpu.VMEM((n, D_MODEL), F32)
    cols_f32 = pltpu.VMEM((D_MODEL, n), F32)
    return pl.pallas_call(
        _decode_kernel,
        grid=(n // DECODE_GROUP,),
        in_specs=[const(x.shape), hbm, const(ctx.shape), hbm, const(npre.shape),
                  const(npost.shape), const(lb_logits.shape), const(hn.shape), const(cw.shape),
                  hbm, hbm, hbm],
        out_specs=[const(x.shape), hbm, const(ctx.shape)],
        out_shape=[jax.ShapeDtypeStruct(x.shape, F32),
                   jax.ShapeDtypeStruct(state.shape, F32),
                   jax.ShapeDtypeStruct(ctx.shape, F32)],
        scratch_shapes=[cols_f32, rows_f32, rows_f32, rows_f32, rows_f32, rows_f32, rows_f32, rows_f32,
                        in_slots, out_slots,
                        pltpu.SemaphoreType.DMA((STATE_IN_SLOTS, STATE_PARTS)),
                        pltpu.SemaphoreType.DMA((STATE_OUT_SLOTS, STATE_PARTS)),
                        pltpu.VMEM(w_in.shape, BF16), pltpu.VMEM(wa.shape, BF16),
                        pltpu.VMEM(wb.shape, BF16), pltpu.VMEM(wo.shape, BF16),
                        pltpu.SemaphoreType.DMA((w_in.shape[1] // D_MODEL + 3,))],
        compiler_params=pltpu.CompilerParams(
            dimension_semantics=("arbitrary",),
            vmem_limit_bytes=V7X_VMEM_LIMIT_BYTES),
        name="decode_step",
    )(x, state, ctx, w_in, npre, npost, lb_logits, hn, cw, wa, wb, wo)


def kernel(x_prompt, x_sample, state_hgrn, state_conv, meta_tokens, w_in, norm_pre, norm_post, lb_logits,
           hgrn_norm, conv_w, w_a, w_b, w_o):
    depth = w_in.shape[0]
    assert depth == 1, "single-layer trunk"
    assert x_sample.shape[1] == 1, "one new token per decode row"

    weights = (w_in[0], norm_pre, norm_post, lb_logits, hgrn_norm, conv_w[0], w_a[0], w_b[0], w_o[0])

    y_prompt, hgrn_p, conv_p, w_in_bf, wa_bf, wb_bf, wo_bf = _run_prompt(
        x_prompt, meta_tokens.astype(x_prompt.dtype), weights, tile=PROMPT_TILE, chunk=PROMPT_CHUNK)
    weights = (w_in_bf,) + weights[1:6] + (wa_bf, wb_bf, wo_bf)

    y_s, hgrn_s, conv_s = _run_decode(x_sample, state_hgrn[0], state_conv[0], weights)

    return (y_prompt, y_s, hgrn_p[None], hgrn_s[None], conv_p[None], conv_s[None])
```

```python
import functools

import jax
import jax.numpy as jnp
from jax import lax
from jax.experimental import pallas as pl
from jax.experimental.pallas import tpu as pltpu

D_MODEL = 1024
N_HEADS = 8
HEAD_DIM = D_MODEL // N_HEADS
CONV_W = 3
EPS = 1e-6
NEG_LOG2_E = -1.4426950408889634
P_Q, P_F, P_I, P_ZA, P_B, P_C, P_H, P_ZB, P_GA, P_GB = range(10)

PROMPT_TILE = 256
PROMPT_CHUNK = 128
DECODE_GROUP = 8
STATE_IN_SLOTS = 3
STATE_OUT_SLOTS = 2
STATE_PARTS = 8
LOAD_BANDS = 4
SUBLANES = 8
CTX_ROW0 = SUBLANES - (CONV_W - 1)
MAX_HALF_CHUNK_LOG2_DECAY = 115.0
V7X_VMEM_LIMIT_BYTES = 58 * 1024 * 1024

BF16 = jnp.bfloat16
F32 = jnp.float32
HEADS = [slice(h * HEAD_DIM, (h + 1) * HEAD_DIM) for h in range(N_HEADS)]


def _dot(a, b):
    return jnp.dot(a, b, preferred_element_type=F32)


def _dot_nt(a, b):
    return lax.dot_general(a, b, (((1,), (1,)), ((), ())), preferred_element_type=F32)


def _dot_tn(a, b):
    return lax.dot_general(a, b, (((0,), (0,)), ((), ())), preferred_element_type=F32)


def _sigmoid(x):
    return 1.0 / (1.0 + jnp.exp2(x * NEG_LOG2_E))


def _silu(x):
    return x * _sigmoid(x)


def _rms_scale(x):
    return lax.rsqrt(jnp.mean(x * x, axis=-1, keepdims=True) + EPS)


def _lower_bound(lb_logits):
    m = jnp.max(lb_logits, axis=0, keepdims=True)
    e = jnp.exp(lb_logits - m)
    return e[0:1, :] / jnp.sum(e, axis=0, keepdims=True)


def _block_cumsum(x, block):
    n = x.shape[0]
    ri = lax.broadcasted_iota(jnp.int32, (n, n), 0)
    ci = lax.broadcasted_iota(jnp.int32, (n, n), 1)
    tri = ri >= ci
    if n != block:
        tri = jnp.logical_and(tri, (ri // block) == (ci // block))
    tri = jnp.where(tri, 1.0, 0.0).astype(BF16)
    hi = x.astype(BF16)
    lo = (x - hi.astype(F32)).astype(BF16)
    return _dot(tri, hi) + _dot(tri, lo)


def _seq_kernel(x_ref, meta_ref, w_in_hbm, npre_ref, npost_ref, lb_ref, hn_ref, cw_ref,
                wa_hbm, wb_hbm, wo_hbm,
                y_ref, st_out_ref, c_out_ref, w_in_out, wa_out, wb_out, wo_out,
                st_scr, u_scr, q_scr, k_scr, v_scr, g_scr, o_scr,
                qt_scr, kt_scr, qin_scr, kst_scr, vb_scr, dec_scr, st_meta_scr, ctx_meta_scr,
                w_in_ref, wa_ref, wb_ref, wo_ref, stage_scr, load_sem, store_sem,
                *, tile, chunk):
    T, C = tile, chunk
    n_chunks = T // C
    t = pl.program_id(1)
    first_step = jnp.logical_and(pl.program_id(0) == 0, t == 0)
    last_step = jnp.logical_and(pl.program_id(0) == pl.num_programs(0) - 1, t == pl.num_programs(1) - 1)
    lb = _lower_bound(lb_ref[...])

    col = [pl.ds(j * D_MODEL, D_MODEL) for j in range(w_in_hbm.shape[1] // D_MODEL)]
    groups = [(w_in_hbm.at[:, c], w_in_ref.at[:, c], w_in_out.at[:, c]) for c in col]
    groups += [(wa_hbm, wa_ref, wa_out), (wb_hbm, wb_ref, wb_out), (wo_hbm, wo_ref, wo_out)]

    def load_copies(p):
        src = groups[fetch_order[p]][0]
        band = src.shape[0] // LOAD_BANDS
        return [pltpu.make_async_copy(src.at[pl.ds(b * band, band), :],
                                      stage_scr.at[p % 2, pl.ds(b * band, band), :],
                                      load_sem.at[p % 2, b]) for b in range(LOAD_BANDS)]

    def store_copy(i):
        return pltpu.make_async_copy(groups[i][1], groups[i][2], store_sem.at[i])

    W_A, W_B, W_O = len(col), len(col) + 1, len(col) + 2
    fetch_order = [P_F, P_I, P_C, P_H, P_Q, P_ZA, W_A, P_B, P_ZB, W_B, P_GA, P_GB, W_O]
    n_eager = 4
    assert sorted(fetch_order) == list(range(len(groups)))

    def fetch(k):
        p = fetch_order.index(k)
        for copy in load_copies(p):
            copy.wait()
        groups[k][1][...] = stage_scr[p % 2].astype(BF16)
        if p + 2 < len(fetch_order):
            for copy in load_copies(p + 2):
                copy.start()

    @pl.when(first_step)
    def _stream_first_weights():
        for p in range(2):
            for copy in load_copies(p):
                copy.start()

    def project(xn, j):
        return _dot(xn, w_in_ref[:, j * D_MODEL:(j + 1) * D_MODEL])

    @pl.when(first_step)
    def _meta_prefix():
        xm = meta_ref[...]
        n_meta = xm.shape[0]
        xn_m = (xm * _rms_scale(xm) * npre_ref[...]).astype(BF16)

        def meta_proj(j):
            fetch(j)
            return project(xn_m, j)

        f_m = lb + (1.0 - lb) * _sigmoid(meta_proj(P_F))
        g_m = _block_cumsum(jnp.log2(f_m), n_meta)
        k_end = ((1.0 - f_m) * jnp.exp2(g_m[n_meta - 1:n_meta, :] - g_m)).astype(BF16)
        v_m = meta_proj(P_I).astype(BF16)
        for h, sl in enumerate(HEADS):
            st_meta_scr[h] = _dot_tn(k_end[:, sl], v_m[:, sl])
        u_m = meta_proj(P_C) * meta_proj(P_H)
        ctx_meta_scr[CTX_ROW0:SUBLANES, :] = u_m[n_meta - (CONV_W - 1):n_meta, :]
        dec_scr[...] = jnp.zeros(dec_scr.shape, F32)

    @pl.when(t == 0)
    def _init():
        st_scr[...] = st_meta_scr[...]
        u_scr[CTX_ROW0:SUBLANES, :] = ctx_meta_scr[CTX_ROW0:SUBLANES, :]

    def run_tile(lazy_weights):
        fetched = set()
        x = x_ref[0]
        xn = (x * _rms_scale(x) * npre_ref[...]).astype(BF16)

        def proj(j):
            if lazy_weights and j in fetch_order[n_eager:] and j not in fetched:
                fetched.add(j)
                fetch(j)
            return project(xn, j)

        def late_weight(k, ref):
            if lazy_weights:
                fetch(k)
            return ref[...]

        f = lb + (1.0 - lb) * _sigmoid(proj(P_F))
        k_scr[...] = 1.0 - f
        q_scr[...] = _silu(proj(P_Q))
        vb_scr[...] = proj(P_I).astype(BF16)
        g_scr[...] = _block_cumsum(jnp.log2(f), C)

        g_floor = None
        for c in range(n_chunks):
            rows = slice(c * C, (c + 1) * C)
            gc = g_scr[rows, :]
            g_last = gc[C - 1:C, :]
            g_mid = gc[C // 2 - 1:C // 2, :]
            qc = q_scr[rows, :]
            kc = k_scr[rows, :]
            qt_scr[rows, :] = (qc * jnp.exp2(gc - g_mid)).astype(BF16)
            kt_scr[:, rows] = (kc * jnp.exp2(g_mid - gc)).T.astype(BF16)
            qin_scr[rows, :] = (qc * jnp.exp2(gc)).astype(BF16)
            kst_scr[rows, :] = (kc * jnp.exp2(g_last - gc)).astype(BF16)
            dec_scr[c:c + 1, :] = jnp.exp2(g_last)
            half_floor = jnp.minimum(g_mid, g_last - g_mid)
            g_floor = half_floor if g_floor is None else jnp.minimum(g_floor, half_floor)
        stable = jnp.min(g_floor) >= -MAX_HALF_CHUNK_LOG2_DECAY

        dec_cols = dec_scr[...].T
        keep = jnp.logical_and(
            lax.broadcasted_iota(jnp.int32, (C, C), 0) >= lax.broadcasted_iota(jnp.int32, (C, C), 1), stable)
        for c in range(n_chunks):
            rows = slice(c * C, (c + 1) * C)
            scores = [jnp.where(keep, _dot(qt_scr[rows, sl], kt_scr[sl, rows]), 0.0).astype(BF16)
                      for sl in HEADS]
            for h, sl in enumerate(HEADS):
                vb = vb_scr[rows, sl]
                st = st_scr[h]
                o_scr[rows, sl] = _dot(jnp.concatenate([qin_scr[rows, sl], scores[h]], axis=1),
                                       jnp.concatenate([st.astype(BF16), vb], axis=0))
                decay = jnp.broadcast_to(dec_cols[sl, c:c + 1], (HEAD_DIM, HEAD_DIM))
                st_scr[h] = decay * st + _dot_tn(kst_scr[rows, sl], vb)

        @pl.when(jnp.logical_not(stable))
        def _intra_exact():
            same_head = (lax.broadcasted_iota(jnp.int32, (D_MODEL, D_MODEL), 0) // HEAD_DIM ==
                         lax.broadcasted_iota(jnp.int32, (D_MODEL, D_MODEL), 1) // HEAD_DIM)
            head_sum = jnp.where(same_head, 1.0, 0.0).astype(BF16)
            tpos = lax.broadcasted_iota(jnp.int32, (C, 1), 0)
            v_scr[...] = proj(P_I)

            def chunk_step(c, carry):
                r0 = pl.multiple_of(c * C, C)
                rows = pl.ds(r0, C)
                gc = g_scr[rows, :]
                qc = q_scr[rows, :]

                def src_step(i, carry2):
                    src = pl.ds(r0 + i, 1)
                    p = qc * jnp.exp2(jnp.minimum(gc - g_scr[src, :], 0.0)) * k_scr[src, :]
                    p = jnp.where(tpos >= i, p, 0.0).astype(BF16)
                    o_scr[rows, :] += _dot(p, head_sum) * v_scr[src, :]
                    return carry2

                return lax.fori_loop(0, C, src_step, carry)

            lax.fori_loop(0, n_chunks, chunk_step, 0)

        hn = hn_ref[...]
        silu_za = _silu(proj(P_ZA))
        gated = []
        for sl in HEADS:
            o_h = o_scr[:, sl]
            gated.append((o_h * _rms_scale(o_h) * hn * silu_za[:, sl]).astype(BF16))
        y_a = _dot(jnp.concatenate(gated, axis=1), late_weight(W_A, wa_ref))

        u = proj(P_C) * proj(P_H)
        u_scr[SUBLANES:SUBLANES + T, :] = u
        cw = cw_ref[...]
        conv = cw[CONV_W - 1:CONV_W, :] * u
        for j in range(CONV_W - 1):
            conv = conv + cw[j:j + 1, :] * u_scr[CTX_ROW0 + j:CTX_ROW0 + j + T, :]
        y_b = _dot((proj(P_B) * conv * _silu(proj(P_ZB))).astype(BF16), late_weight(W_B, wb_ref))
        u_scr[CTX_ROW0:SUBLANES, :] = u[T - (CONV_W - 1):T, :]

        merged = _sigmoid(proj(P_GA)) * y_a + _sigmoid(proj(P_GB)) * y_b
        out = _dot(merged.astype(BF16), late_weight(W_O, wo_ref))
        y_ref[0] = x_ref[0] + out * _rms_scale(out) * npost_ref[...]

    @pl.when(first_step)
    def _first_tile():
        run_tile(lazy_weights=True)
        for k in range(len(groups)):
            store_copy(k).start()

    @pl.when(jnp.logical_not(first_step))
    def _tile():
        run_tile(lazy_weights=False)

    @pl.when(t == pl.num_programs(1) - 1)
    def _finish():
        c_out_ref[0] = u_scr[CTX_ROW0:SUBLANES, :]
        st_out_ref[0] = st_scr[...]

    @pl.when(last_step)
    def _weights_written():
        for i in range(len(groups)):
            store_copy(i).wait()


def _const_spec(shape):
    zeros = (0,) * len(shape)
    return pl.BlockSpec(shape, lambda b, t: zeros, pipeline_mode=pl.Buffered(1))


def _run_prompt(x, meta, weights, *, tile, chunk):
    n, length, _ = x.shape
    assert length % tile == 0 and tile % chunk == 0 and chunk % (2 * SUBLANES) == 0 and tile >= CONV_W - 1
    assert meta.shape[0] % (2 * SUBLANES) == 0 and meta.shape[0] >= CONV_W - 1
    assert tile // chunk <= HEAD_DIM
    w_in, npre, npost, lb_logits, hn, cw, wa, wb, wo = weights
    kern = functools.partial(_seq_kernel, tile=tile, chunk=chunk)
    hbm = pl.BlockSpec(memory_space=pl.ANY)
    n_groups = w_in.shape[1] // D_MODEL + 3
    assert w_in.shape[1] % D_MODEL == 0 and wa.shape == wb.shape == wo.shape == (w_in.shape[0], D_MODEL)
    state_shape = (1, N_HEADS, HEAD_DIM, HEAD_DIM)
    ctx_shape = (1, CONV_W - 1, D_MODEL)
    tile_f32 = pltpu.VMEM((tile, D_MODEL), F32)
    tile_bf16 = pltpu.VMEM((tile, D_MODEL), BF16)
    return pl.pallas_call(
        kern,
        grid=(n, length // tile),
        in_specs=[
            pl.BlockSpec((1, tile, D_MODEL), lambda b, t: (b, t, 0)),
            _const_spec(meta.shape),
            hbm,
            _const_spec(npre.shape),
            _const_spec(npost.shape),
            _const_spec(lb_logits.shape),
            _const_spec(hn.shape),
            _const_spec(cw.shape),
            hbm, hbm, hbm,
        ],
        out_specs=[
            pl.BlockSpec((1, tile, D_MODEL), lambda b, t: (b, t, 0)),
            pl.BlockSpec(state_shape, lambda b, t: (b, 0, 0, 0)),
            pl.BlockSpec(ctx_shape, lambda b, t: (b, 0, 0)),
            hbm, hbm, hbm, hbm,
        ],
        out_shape=[
            jax.ShapeDtypeStruct(x.shape, F32),
            jax.ShapeDtypeStruct((n,) + state_shape[1:], F32),
            jax.ShapeDtypeStruct((n,) + ctx_shape[1:], F32),
            jax.ShapeDtypeStruct(w_in.shape, BF16),
            jax.ShapeDtypeStruct(wa.shape, BF16),
            jax.ShapeDtypeStruct(wb.shape, BF16),
            jax.ShapeDtypeStruct(wo.shape, BF16),
        ],
        scratch_shapes=[
            pltpu.VMEM((N_HEADS, HEAD_DIM, HEAD_DIM), F32),
            pltpu.VMEM((tile + SUBLANES, D_MODEL), F32),
            tile_f32, tile_f32, tile_f32, tile_f32, tile_f32,
            tile_bf16, pltpu.VMEM((D_MODEL, tile), BF16), tile_bf16, tile_bf16, tile_bf16,
            pltpu.VMEM((HEAD_DIM, D_MODEL), F32),
            pltpu.VMEM((N_HEADS, HEAD_DIM, HEAD_DIM), F32),
            pltpu.VMEM((SUBLANES, D_MODEL), F32),
            pltpu.VMEM(w_in.shape, BF16),
            pltpu.VMEM(wa.shape, BF16), pltpu.VMEM(wb.shape, BF16), pltpu.VMEM(wo.shape, BF16),
            pltpu.VMEM((2,) + wa.shape, F32),
            pltpu.SemaphoreType.DMA((2, LOAD_BANDS)),
            pltpu.SemaphoreType.DMA((n_groups,)),
        ],
        compiler_params=pltpu.CompilerParams(
            dimension_semantics=("arbitrary", "arbitrary"),
            vmem_limit_bytes=V7X_VMEM_LIMIT_BYTES),
        name="prompt_sweep",
    )(x, meta, w_in, npre, npost, lb_logits, hn, cw, wa, wb, wo)


def _decode_kernel(x_ref, st_hbm, ctx_ref, w_in_hbm, npre_ref, npost_ref, lb_ref, hn_ref, cw_ref,
                   wa_hbm, wb_hbm, wo_hbm,
                   y_ref, st_out_hbm, ctx_out_ref,
                   ft_scr, q_scr, v_scr, o_scr, za_scr, pb_scr, ga_scr, gb_scr,
                   in_buf, out_buf, in_sem, out_sem, w_in_ref, wa_ref, wb_ref, wo_ref, w_sem):
    G = DECODE_GROUP
    i = pl.program_id(0)
    n_steps = pl.num_programs(0)
    n_rows = x_ref.shape[0]
    part_rows = G // STATE_PARTS

    def in_copies(step):
        slot = step % STATE_IN_SLOTS
        return [pltpu.make_async_copy(st_hbm.at[pl.ds(step * G + p * part_rows, part_rows)],
                                      in_buf.at[slot, pl.ds(p * part_rows, part_rows)],
                                      in_sem.at[slot, p]) for p in range(STATE_PARTS)]

    def out_copies(step):
        slot = step % STATE_OUT_SLOTS
        return [pltpu.make_async_copy(out_buf.at[slot, pl.ds(p * part_rows, part_rows)],
                                      st_out_hbm.at[pl.ds(step * G + p * part_rows, part_rows)],
                                      out_sem.at[slot, p]) for p in range(STATE_PARTS)]

    n_proj = w_in_hbm.shape[1] // D_MODEL
    w_parts = [(w_in_hbm.at[:, pl.ds(j * D_MODEL, D_MODEL)], w_in_ref.at[:, pl.ds(j * D_MODEL, D_MODEL)])
               for j in range(n_proj)] + [(wa_hbm, wa_ref), (wb_hbm, wb_ref), (wo_hbm, wo_ref)]
    W_A, W_B, W_O = n_proj, n_proj + 1, n_proj + 2

    def w_copy(k):
        return pltpu.make_async_copy(w_parts[k][0], w_parts[k][1], w_sem.at[k])

    @pl.when(i == 0)
    def _prime():
        for step in range(STATE_IN_SLOTS - 1):
            for copy in in_copies(step):
                copy.start()
        for k in (P_F, P_Q, P_I, P_ZA, P_C, P_H, P_B, P_ZB, P_GA, P_GB, W_A, W_B, W_O):
            w_copy(k).start()

    @pl.when(i == 0)
    def _project():
        x = x_ref[:, 0, :]
        xn = (x * _rms_scale(x) * npre_ref[...]).astype(BF16)

        def proj(j):
            w_copy(j).wait()
            return _dot(xn, w_in_ref[:, j * D_MODEL:(j + 1) * D_MODEL])

        lb = _lower_bound(lb_ref[...])
        f = lb + (1.0 - lb) * _sigmoid(proj(P_F))
        ft_scr[...] = f.T
        q_scr[...] = _silu(proj(P_Q))
        v_scr[...] = proj(P_I)
        za_scr[...] = _silu(proj(P_ZA))
        u = proj(P_C) * proj(P_H)
        cw = cw_ref[...]
        conv = cw[CONV_W - 1:CONV_W, :] * u
        for j in range(CONV_W - 1):
            ctx_j = ctx_ref[:, j, :]
            conv = conv + cw[j:j + 1, :] * ctx_j
            if j > 0:
                ctx_out_ref[:, j - 1, :] = ctx_j
        ctx_out_ref[:, CONV_W - 2, :] = u
        pb_scr[...] = proj(P_B) * conv * _silu(proj(P_ZB))
        ga_scr[...] = _sigmoid(proj(P_GA))
        gb_scr[...] = _sigmoid(proj(P_GB))

    @pl.when(i + STATE_IN_SLOTS - 1 < n_steps)
    def _read_ahead():
        for copy in in_copies(i + STATE_IN_SLOTS - 1):
            copy.start()

    @pl.when(i >= STATE_OUT_SLOTS)
    def _slot_written_back():
        for copy in out_copies(i - STATE_OUT_SLOTS):
            copy.wait()

    for copy in in_copies(i):
        copy.wait()
    st_ref = in_buf.at[i % STATE_IN_SLOTS]
    st_out_ref = out_buf.at[i % STATE_OUT_SLOTS]

    shift = (n_rows - i * G) % n_rows
    f_cols = pltpu.roll(ft_scr[...], shift, 1)
    r0 = pl.multiple_of(i * G, G)
    v_rows = v_scr[pl.ds(r0, G), :]
    q_rows = q_scr[pl.ds(r0, G), :].astype(BF16)
    row_id = lax.broadcasted_iota(jnp.int32, (G, HEAD_DIM), 0)
    o_heads = [jnp.zeros((G, HEAD_DIM), F32)] * N_HEADS
    for j in range(G):
        for h, sl in enumerate(HEADS):
            f_b = jnp.broadcast_to(f_cols[sl, j:j + 1], (HEAD_DIM, HEAD_DIM))
            s_new = f_b * st_ref[j, h] + (1.0 - f_b) * v_rows[j:j + 1, sl]
            st_out_ref[j, h] = s_new
            read = _dot(q_rows[:, sl], s_new.astype(BF16))
            o_heads[h] = jnp.where(row_id == j, read, o_heads[h])
    o_scr[pl.ds(r0, G), :] = jnp.concatenate(o_heads, axis=1)
    for copy in out_copies(i):
        copy.start()

    @pl.when(i == pl.num_programs(0) - 1)
    def _output():
        hn = hn_ref[...]
        for sl in HEADS:
            o_h = o_scr[:, sl]
            o_scr[:, sl] = o_h * _rms_scale(o_h) * hn
        for k in (W_A, W_B, W_O):
            w_copy(k).wait()
        y_a = _dot((o_scr[...] * za_scr[...]).astype(BF16), wa_ref[...])
        y_b = _dot(pb_scr[...].astype(BF16), wb_ref[...])
        merged = ga_scr[...] * y_a + gb_scr[...] * y_b
        out = _dot(merged.astype(BF16), wo_ref[...])
        y_ref[:, 0, :] = x_ref[:, 0, :] + out * _rms_scale(out) * npost_ref[...]
        for back in range(STATE_OUT_SLOTS):
            for copy in out_copies(i - back):
                copy.wait()


def _run_decode(x, state, ctx, weights):
    n = x.shape[0]
    assert n % DECODE_GROUP == 0 and n == 128 and DECODE_GROUP % STATE_PARTS == 0
    assert n // DECODE_GROUP >= max(STATE_IN_SLOTS, STATE_OUT_SLOTS)
    w_in, npre, npost, lb_logits, hn, cw, wa, wb, wo = weights

    def const(shape):
        zeros = (0,) * len(shape)
        return pl.BlockSpec(shape, lambda i: zeros, pipeline_mode=pl.Buffered(1))

    hbm = pl.BlockSpec(memory_space=pl.ANY)
    in_slots = pltpu.VMEM((STATE_IN_SLOTS, DECODE_GROUP) + state.shape[1:], F32)
    out_slots = pltpu.VMEM((STATE_OUT_SLOTS, DECODE_GROUP) + state.shape[1:], F32)
    rows_f32 = pltpu.VMEM((n, D_MODEL), F32)
    cols_f32 = pltpu.VMEM((D_MODEL, n), F32)
    return pl.pallas_call(
        _decode_kernel,
        grid=(n // DECODE_GROUP,),
        in_specs=[const(x.shape), hbm, const(ctx.shape), hbm, const(npre.shape),
                  const(npost.shape), const(lb_logits.shape), const(hn.shape), const(cw.shape),
                  hbm, hbm, hbm],
        out_specs=[const(x.shape), hbm, const(ctx.shape)],
        out_shape=[jax.ShapeDtypeStruct(x.shape, F32),
                   jax.ShapeDtypeStruct(state.shape, F32),
                   jax.ShapeDtypeStruct(ctx.shape, F32)],
        scratch_shapes=[cols_f32, rows_f32, rows_f32, rows_f32, rows_f32, rows_f32, rows_f32, rows_f32,
                        in_slots, out_slots,
                        pltpu.SemaphoreType.DMA((STATE_IN_SLOTS, STATE_PARTS)),
                        pltpu.SemaphoreType.DMA((STATE_OUT_SLOTS, STATE_PARTS)),
                        pltpu.VMEM(w_in.shape, BF16), pltpu.VMEM(wa.shape, BF16),
                        pltpu.VMEM(wb.shape, BF16), pltpu.VMEM(wo.shape, BF16),
                        pltpu.SemaphoreType.DMA((w_in.shape[1] // D_MODEL + 3,))],
        compiler_params=pltpu.CompilerParams(
            dimension_semantics=("arbitrary",),
            vmem_limit_bytes=V7X_VMEM_LIMIT_BYTES),
        name="decode_step",
    )(x, state, ctx, w_in, npre, npost, lb_logits, hn, cw, wa, wb, wo)


def kernel(x_prompt, x_sample, state_hgrn, state_conv, meta_tokens, w_in, norm_pre, norm_post, lb_logits,
           hgrn_norm, conv_w, w_a, w_b, w_o):
    depth = w_in.shape[0]
    assert depth == 1, "single-layer trunk"
    assert x_sample.shape[1] == 1, "one new token per decode row"

    weights = (w_in[0], norm_pre, norm_post, lb_logits, hgrn_norm, conv_w[0], w_a[0], w_b[0], w_o[0])

    y_prompt, hgrn_p, conv_p, w_in_bf, wa_bf, wb_bf, wo_bf = _run_prompt(
        x_prompt, meta_tokens.astype(x_prompt.dtype), weights, tile=PROMPT_TILE, chunk=PROMPT_CHUNK)
    weights = (w_in_bf,) + weights[1:6] + (wa_bf, wb_bf, wo_bf)

    y_s, hgrn_s, conv_s = _run_decode(x_sample, state_hgrn[0], state_conv[0], weights)

    return (y_prompt, y_s, hgrn_p[None], hgrn_s[None], conv_p[None], conv_s[None])
```

```python
import functools

import jax
import jax.numpy as jnp
from jax import lax
from jax.experimental import pallas as pl
from jax.experimental.pallas import tpu as pltpu

D_MODEL = 1024
N_HEADS = 8
HEAD_DIM = D_MODEL // N_HEADS
CONV_W = 3
EPS = 1e-6
NEG_LOG2_E = -1.4426950408889634
P_Q, P_F, P_I, P_ZA, P_B, P_C, P_H, P_ZB, P_GA, P_GB = range(10)

PROMPT_TILE = 256
PROMPT_CHUNK = 128
DECODE_GROUP = 8
STATE_IN_SLOTS = 3
STATE_OUT_SLOTS = 2
STATE_PARTS = 8
LOAD_BANDS = 4
SUBLANES = 8
CTX_ROW0 = SUBLANES - (CONV_W - 1)
MAX_HALF_CHUNK_LOG2_DECAY = 115.0
V7X_VMEM_LIMIT_BYTES = 58 * 1024 * 1024

BF16 = jnp.bfloat16
F32 = jnp.float32
HEADS = [slice(h * HEAD_DIM, (h + 1) * HEAD_DIM) for h in range(N_HEADS)]


def _dot(a, b):
    return jnp.dot(a, b, preferred_element_type=F32)


def _dot_nt(a, b):
    return lax.dot_general(a, b, (((1,), (1,)), ((), ())), preferred_element_type=F32)


def _dot_tn(a, b):
    return lax.dot_general(a, b, (((0,), (0,)), ((), ())), preferred_element_type=F32)


def _sigmoid(x):
    return 1.0 / (1.0 + jnp.exp2(x * NEG_LOG2_E))


def _silu(x):
    return x * _sigmoid(x)


def _rms_scale(x):
    return lax.rsqrt(jnp.mean(x * x, axis=-1, keepdims=True) + EPS)


def _lower_bound(lb_logits):
    m = jnp.max(lb_logits, axis=0, keepdims=True)
    e = jnp.exp(lb_logits - m)
    return e[0:1, :] / jnp.sum(e, axis=0, keepdims=True)


def _block_cumsum(x, block):
    n = x.shape[0]
    ri = lax.broadcasted_iota(jnp.int32, (n, n), 0)
    ci = lax.broadcasted_iota(jnp.int32, (n, n), 1)
    tri = ri >= ci
    if n != block:
        tri = jnp.logical_and(tri, (ri // block) == (ci // block))
    tri = jnp.where(tri, 1.0, 0.0).astype(BF16)
    hi = x.astype(BF16)
    lo = (x - hi.astype(F32)).astype(BF16)
    return _dot(tri, hi) + _dot(tri, lo)


def _seq_kernel(x_ref, meta_ref, w_in_hbm, npre_ref, npost_ref, lb_ref, hn_ref, cw_ref,
                wa_hbm, wb_hbm, wo_hbm,
                y_ref, st_out_ref, c_out_ref, w_in_out, wa_out, wb_out, wo_out,
                st_scr, u_scr, q_scr, k_scr, v_scr, g_scr, o_scr,
                qt_scr, kt_scr, qin_scr, kst_scr, vb_scr, dec_scr, st_meta_scr, ctx_meta_scr,
                w_in_ref, wa_ref, wb_ref, wo_ref, stage_scr, load_sem, store_sem,
                *, tile, chunk):
    T, C = tile, chunk
    n_chunks = T // C
    t = pl.program_id(1)
    first_step = jnp.logical_and(pl.program_id(0) == 0, t == 0)
    last_step = jnp.logical_and(pl.program_id(0) == pl.num_programs(0) - 1, t == pl.num_programs(1) - 1)
    lb = _lower_bound(lb_ref[...])

    col = [pl.ds(j * D_MODEL, D_MODEL) for j in range(w_in_hbm.shape[1] // D_MODEL)]
    groups = [(w_in_hbm.at[:, c], w_in_ref.at[:, c], w_in_out.at[:, c]) for c in col]
    groups += [(wa_hbm, wa_ref, wa_out), (wb_hbm, wb_ref, wb_out), (wo_hbm, wo_ref, wo_out)]

    def load_copies(p):
        src = groups[fetch_order[p]][0]
        band = src.shape[0] // LOAD_BANDS
        return [pltpu.make_async_copy(src.at[pl.ds(b * band, band), :],
                                      stage_scr.at[p % 2, pl.ds(b * band, band), :],
                                      load_sem.at[p % 2, b]) for b in range(LOAD_BANDS)]

    def store_copy(i):
        return pltpu.make_async_copy(groups[i][1], groups[i][2], store_sem.at[i])

    W_A, W_B, W_O = len(col), len(col) + 1, len(col) + 2
    fetch_order = [P_F, P_I, P_C, P_H, P_Q, P_ZA, W_A, P_B, P_ZB, W_B, P_GA, P_GB, W_O]
    n_eager = 4
    assert sorted(fetch_order) == list(range(len(groups)))

    def fetch(k):
        p = fetch_order.index(k)
        for copy in load_copies(p):
            copy.wait()
        groups[k][1][...] = stage_scr[p % 2].astype(BF16)
        if p + 2 < len(fetch_order):
            for copy in load_copies(p + 2):
                copy.start()

    @pl.when(first_step)
    def _stream_first_weights():
        for p in range(2):
            for copy in load_copies(p):
                copy.start()

    def project(xn, j):
        return _dot(xn, w_in_ref[:, j * D_MODEL:(j + 1) * D_MODEL])

    @pl.when(first_step)
    def _meta_prefix():
        xm = meta_ref[...]
        n_meta = xm.shape[0]
        xn_m = (xm * _rms_scale(xm) * npre_ref[...]).astype(BF16)

        def meta_proj(j):
            fetch(j)
            return project(xn_m, j)

        f_m = lb + (1.0 - lb) * _sigmoid(meta_proj(P_F))
        g_m = _block_cumsum(jnp.log2(f_m), n_meta)
        k_end = ((1.0 - f_m) * jnp.exp2(g_m[n_meta - 1:n_meta, :] - g_m)).astype(BF16)
        v_m = meta_proj(P_I).astype(BF16)
        for h, sl in enumerate(HEADS):
            st_meta_scr[h] = _dot_tn(k_end[:, sl], v_m[:, sl])
        u_m = meta_proj(P_C) * meta_proj(P_H)
        ctx_meta_scr[CTX_ROW0:SUBLANES, :] = u_m[n_meta - (CONV_W - 1):n_meta, :]
        dec_scr[...] = jnp.zeros(dec_scr.shape, F32)

    @pl.when(t == 0)
    def _init():
        st_scr[...] = st_meta_scr[...]
        u_scr[CTX_ROW0:SUBLANES, :] = ctx_meta_scr[CTX_ROW0:SUBLANES, :]

    def run_tile(lazy_weights):
        fetched = set()
        x = x_ref[0]
        xn = (x * _rms_scale(x) * npre_ref[...]).astype(BF16)

        def proj(j):
            if lazy_weights and j in fetch_order[n_eager:] and j not in fetched:
                fetched.add(j)
                fetch(j)
            return project(xn, j)

        def late_weight(k, ref):
            if lazy_weights:
                fetch(k)
            return ref[...]

        f = lb + (1.0 - lb) * _sigmoid(proj(P_F))
        k_scr[...] = 1.0 - f
        q_scr[...] = _silu(proj(P_Q))
        vb_scr[...] = proj(P_I).astype(BF16)
        g_scr[...] = _block_cumsum(jnp.log2(f), C)

        g_floor = None
        for c in range(n_chunks):
            rows = slice(c * C, (c + 1) * C)
            gc = g_scr[rows, :]
            g_last = gc[C - 1:C, :]
            g_mid = gc[C // 2 - 1:C // 2, :]
            qc = q_scr[rows, :]
            kc = k_scr[rows, :]
            qt_scr[rows, :] = (qc * jnp.exp2(gc - g_mid)).astype(BF16)
            kt_scr[:, rows] = (kc * jnp.exp2(g_mid - gc)).T.astype(BF16)
            qin_scr[rows, :] = (qc * jnp.exp2(gc)).astype(BF16)
            kst_scr[:, rows] = (kc * jnp.exp2(g_last - gc)).T.astype(BF16)
            dec_scr[c:c + 1, :] = jnp.exp2(g_last)
            half_floor = jnp.minimum(g_mid, g_last - g_mid)
            g_floor = half_floor if g_floor is None else jnp.minimum(g_floor, half_floor)
        stable = jnp.min(g_floor) >= -MAX_HALF_CHUNK_LOG2_DECAY

        dec_cols = dec_scr[...].T
        keep = jnp.logical_and(
            lax.broadcasted_iota(jnp.int32, (C, C), 0) >= lax.broadcasted_iota(jnp.int32, (C, C), 1), stable)
        for c in range(n_chunks):
            rows = slice(c * C, (c + 1) * C)
            scores = [jnp.where(keep, _dot(qt_scr[rows, sl], kt_scr[sl, rows]), 0.0).astype(BF16)
                      for sl in HEADS]
            for h, sl in enumerate(HEADS):
                vb = vb_scr[rows, sl]
                st = st_scr[h]
                o_scr[rows, sl] = _dot(jnp.concatenate([qin_scr[rows, sl], scores[h]], axis=1),
                                       jnp.concatenate([st.astype(BF16), vb], axis=0))
                decay = jnp.broadcast_to(dec_cols[sl, c:c + 1], (HEAD_DIM, HEAD_DIM))
                st_scr[h] = decay * st + _dot(kst_scr[sl, rows], vb)

        @pl.when(jnp.logical_not(stable))
        def _intra_exact():
            same_head = (lax.broadcasted_iota(jnp.int32, (D_MODEL, D_MODEL), 0) // HEAD_DIM ==
                         lax.broadcasted_iota(jnp.int32, (D_MODEL, D_MODEL), 1) // HEAD_DIM)
            head_sum = jnp.where(same_head, 1.0, 0.0).astype(BF16)
            tpos = lax.broadcasted_iota(jnp.int32, (C, 1), 0)
            v_scr[...] = proj(P_I)

            def chunk_step(c, carry):
                r0 = pl.multiple_of(c * C, C)
                rows = pl.ds(r0, C)
                gc = g_scr[rows, :]
                qc = q_scr[rows, :]

                def src_step(i, carry2):
                    src = pl.ds(r0 + i, 1)
                    p = qc * jnp.exp2(jnp.minimum(gc - g_scr[src, :], 0.0)) * k_scr[src, :]
                    p = jnp.where(tpos >= i, p, 0.0).astype(BF16)
                    o_scr[rows, :] += _dot(p, head_sum) * v_scr[src, :]
                    return carry2

                return lax.fori_loop(0, C, src_step, carry)

            lax.fori_loop(0, n_chunks, chunk_step, 0)

        hn = hn_ref[...]
        silu_za = _silu(proj(P_ZA))
        gated = []
        for sl in HEADS:
            o_h = o_scr[:, sl]
            gated.append((o_h * _rms_scale(o_h) * hn * silu_za[:, sl]).astype(BF16))
        y_a = _dot(jnp.concatenate(gated, axis=1), late_weight(W_A, wa_ref))

        u = proj(P_C) * proj(P_H)
        u_scr[SUBLANES:SUBLANES + T, :] = u
        cw = cw_ref[...]
        conv = cw[CONV_W - 1:CONV_W, :] * u
        for j in range(CONV_W - 1):
            conv = conv + cw[j:j + 1, :] * u_scr[CTX_ROW0 + j:CTX_ROW0 + j + T, :]
        y_b = _dot((proj(P_B) * conv * _silu(proj(P_ZB))).astype(BF16), late_weight(W_B, wb_ref))
        u_scr[CTX_ROW0:SUBLANES, :] = u[T - (CONV_W - 1):T, :]

        merged = _sigmoid(proj(P_GA)) * y_a + _sigmoid(proj(P_GB)) * y_b
        out = _dot(merged.astype(BF16), late_weight(W_O, wo_ref))
        y_ref[0] = x_ref[0] + out * _rms_scale(out) * npost_ref[...]

    @pl.when(first_step)
    def _first_tile():
        run_tile(lazy_weights=True)
        for k in range(len(groups)):
            store_copy(k).start()

    @pl.when(jnp.logical_not(first_step))
    def _tile():
        run_tile(lazy_weights=False)

    @pl.when(t == pl.num_programs(1) - 1)
    def _finish():
        c_out_ref[0] = u_scr[CTX_ROW0:SUBLANES, :]
        st_out_ref[0] = st_scr[...]

    @pl.when(last_step)
    def _weights_written():
        for i in range(len(groups)):
            store_copy(i).wait()


def _const_spec(shape):
    zeros = (0,) * len(shape)
    return pl.BlockSpec(shape, lambda b, t: zeros, pipeline_mode=pl.Buffered(1))


def _run_prompt(x, meta, weights, *, tile, chunk):
    n, length, _ = x.shape
    assert length % tile == 0 and tile % chunk == 0 and chunk % (2 * SUBLANES) == 0 and tile >= CONV_W - 1
    assert meta.shape[0] % (2 * SUBLANES) == 0 and meta.shape[0] >= CONV_W - 1
    assert tile // chunk <= HEAD_DIM
    w_in, npre, npost, lb_logits, hn, cw, wa, wb, wo = weights
    kern = functools.partial(_seq_kernel, tile=tile, chunk=chunk)
    hbm = pl.BlockSpec(memory_space=pl.ANY)
    n_groups = w_in.shape[1] // D_MODEL + 3
    assert w_in.shape[1] % D_MODEL == 0 and wa.shape == wb.shape == wo.shape == (w_in.shape[0], D_MODEL)
    state_shape = (1, N_HEADS, HEAD_DIM, HEAD_DIM)
    ctx_shape = (1, CONV_W - 1, D_MODEL)
    tile_f32 = pltpu.VMEM((tile, D_MODEL), F32)
    tile_bf16 = pltpu.VMEM((tile, D_MODEL), BF16)
    return pl.pallas_call(
        kern,
        grid=(n, length // tile),
        in_specs=[
            pl.BlockSpec((1, tile, D_MODEL), lambda b, t: (b, t, 0)),
            _const_spec(meta.shape),
            hbm,
            _const_spec(npre.shape),
            _const_spec(npost.shape),
            _const_spec(lb_logits.shape),
            _const_spec(hn.shape),
            _const_spec(cw.shape),
            hbm, hbm, hbm,
        ],
        out_specs=[
            pl.BlockSpec((1, tile, D_MODEL), lambda b, t: (b, t, 0)),
            pl.BlockSpec(state_shape, lambda b, t: (b, 0, 0, 0)),
            pl.BlockSpec(ctx_shape, lambda b, t: (b, 0, 0)),
            hbm, hbm, hbm, hbm,
        ],
        out_shape=[
            jax.ShapeDtypeStruct(x.shape, F32),
            jax.ShapeDtypeStruct((n,) + state_shape[1:], F32),
            jax.ShapeDtypeStruct((n,) + ctx_shape[1:], F32),
            jax.ShapeDtypeStruct(w_in.shape, BF16),
            jax.ShapeDtypeStruct(wa.shape, BF16),
            jax.ShapeDtypeStruct(wb.shape, BF16),
            jax.ShapeDtypeStruct(wo.shape, BF16),
        ],
        scratch_shapes=[
            pltpu.VMEM((N_HEADS, HEAD_DIM, HEAD_DIM), F32),
            pltpu.VMEM((tile + SUBLANES, D_MODEL), F32),
            tile_f32, tile_f32, tile_f32, tile_f32, tile_f32,
            tile_bf16, pltpu.VMEM((D_MODEL, tile), BF16), tile_bf16,
            pltpu.VMEM((D_MODEL, tile), BF16), tile_bf16,
            pltpu.VMEM((HEAD_DIM, D_MODEL), F32),
            pltpu.VMEM((N_HEADS, HEAD_DIM, HEAD_DIM), F32),
            pltpu.VMEM((SUBLANES, D_MODEL), F32),
            pltpu.VMEM(w_in.shape, BF16),
            pltpu.VMEM(wa.shape, BF16), pltpu.VMEM(wb.shape, BF16), pltpu.VMEM(wo.shape, BF16),
            pltpu.VMEM((2,) + wa.shape, F32),
            pltpu.SemaphoreType.DMA((2, LOAD_BANDS)),
            pltpu.SemaphoreType.DMA((n_groups,)),
        ],
        compiler_params=pltpu.CompilerParams(
            dimension_semantics=("arbitrary", "arbitrary"),
            vmem_limit_bytes=V7X_VMEM_LIMIT_BYTES),
        name="prompt_sweep",
    )(x, meta, w_in, npre, npost, lb_logits, hn, cw, wa, wb, wo)


def _decode_kernel(x_ref, st_hbm, ctx_ref, w_in_hbm, npre_ref, npost_ref, lb_ref, hn_ref, cw_ref,
                   wa_hbm, wb_hbm, wo_hbm,
                   y_ref, st_out_hbm, ctx_out_ref,
                   ft_scr, q_scr, v_scr, o_scr, za_scr, pb_scr, ga_scr, gb_scr,
                   in_buf, out_buf, in_sem, out_sem, w_in_ref, wa_ref, wb_ref, wo_ref, w_sem):
    G = DECODE_GROUP
    i = pl.program_id(0)
    n_steps = pl.num_programs(0)
    n_rows = x_ref.shape[0]
    part_rows = G // STATE_PARTS

    def in_copies(step):
        slot = step % STATE_IN_SLOTS
        return [pltpu.make_async_copy(st_hbm.at[pl.ds(step * G + p * part_rows, part_rows)],
                                      in_buf.at[slot, pl.ds(p * part_rows, part_rows)],
                                      in_sem.at[slot, p]) for p in range(STATE_PARTS)]

    def out_copies(step):
        slot = step % STATE_OUT_SLOTS
        return [pltpu.make_async_copy(out_buf.at[slot, pl.ds(p * part_rows, part_rows)],
                                      st_out_hbm.at[pl.ds(step * G + p * part_rows, part_rows)],
                                      out_sem.at[slot, p]) for p in range(STATE_PARTS)]

    n_proj = w_in_hbm.shape[1] // D_MODEL
    w_parts = [(w_in_hbm.at[:, pl.ds(j * D_MODEL, D_MODEL)], w_in_ref.at[:, pl.ds(j * D_MODEL, D_MODEL)])
               for j in range(n_proj)] + [(wa_hbm, wa_ref), (wb_hbm, wb_ref), (wo_hbm, wo_ref)]
    W_A, W_B, W_O = n_proj, n_proj + 1, n_proj + 2

    def w_copy(k):
        return pltpu.make_async_copy(w_parts[k][0], w_parts[k][1], w_sem.at[k])

    @pl.when(i == 0)
    def _prime():
        for step in range(STATE_IN_SLOTS - 1):
            for copy in in_copies(step):
                copy.start()
        for k in (P_F, P_Q, P_I, P_ZA, P_C, P_H, P_B, P_ZB, P_GA, P_GB, W_A, W_B, W_O):
            w_copy(k).start()

    @pl.when(i == 0)
    def _project():
        x = x_ref[:, 0, :]
        xn = (x * _rms_scale(x) * npre_ref[...]).astype(BF16)

        def proj(j):
            w_copy(j).wait()
            return _dot(xn, w_in_ref[:, j * D_MODEL:(j + 1) * D_MODEL])

        lb = _lower_bound(lb_ref[...])
        f = lb + (1.0 - lb) * _sigmoid(proj(P_F))
        ft_scr[...] = f.T
        q_scr[...] = _silu(proj(P_Q))
        v_scr[...] = proj(P_I)
        za_scr[...] = _silu(proj(P_ZA))
        u = proj(P_C) * proj(P_H)
        cw = cw_ref[...]
        conv = cw[CONV_W - 1:CONV_W, :] * u
        for j in range(CONV_W - 1):
            ctx_j = ctx_ref[:, j, :]
            conv = conv + cw[j:j + 1, :] * ctx_j
            if j > 0:
                ctx_out_ref[:, j - 1, :] = ctx_j
        ctx_out_ref[:, CONV_W - 2, :] = u
        pb_scr[...] = proj(P_B) * conv * _silu(proj(P_ZB))
        ga_scr[...] = _sigmoid(proj(P_GA))
        gb_scr[...] = _sigmoid(proj(P_GB))

    @pl.when(i + STATE_IN_SLOTS - 1 < n_steps)
    def _read_ahead():
        for copy in in_copies(i + STATE_IN_SLOTS - 1):
            copy.start()

    @pl.when(i >= STATE_OUT_SLOTS)
    def _slot_written_back():
        for copy in out_copies(i - STATE_OUT_SLOTS):
            copy.wait()

    for copy in in_copies(i):
        copy.wait()
    st_ref = in_buf.at[i % STATE_IN_SLOTS]
    st_out_ref = out_buf.at[i % STATE_OUT_SLOTS]

    shift = (n_rows - i * G) % n_rows
    f_cols = pltpu.roll(ft_scr[...], shift, 1)
    r0 = pl.multiple_of(i * G, G)
    v_rows = v_scr[pl.ds(r0, G), :]
    q_rows = q_scr[pl.ds(r0, G), :].astype(BF16)
    row_id = lax.broadcasted_iota(jnp.int32, (G, HEAD_DIM), 0)
    o_heads = [jnp.zeros((G, HEAD_DIM), F32)] * N_HEADS
    for j in range(G):
        for h, sl in enumerate(HEADS):
            f_b = jnp.broadcast_to(f_cols[sl, j:j + 1], (HEAD_DIM, HEAD_DIM))
            s_new = f_b * st_ref[j, h] + (1.0 - f_b) * v_rows[j:j + 1, sl]
            st_out_ref[j, h] = s_new
            read = _dot(q_rows[:, sl], s_new.astype(BF16))
            o_heads[h] = jnp.where(row_id == j, read, o_heads[h])
    o_scr[pl.ds(r0, G), :] = jnp.concatenate(o_heads, axis=1)
    for copy in out_copies(i):
        copy.start()

    @pl.when(i == pl.num_programs(0) - 1)
    def _output():
        hn = hn_ref[...]
        for sl in HEADS:
            o_h = o_scr[:, sl]
            o_scr[:, sl] = o_h * _rms_scale(o_h) * hn
        for k in (W_A, W_B, W_O):
            w_copy(k).wait()
        y_a = _dot((o_scr[...] * za_scr[...]).astype(BF16), wa_ref[...])
        y_b = _dot(pb_scr[...].astype(BF16), wb_ref[...])
        merged = ga_scr[...] * y_a + gb_scr[...] * y_b
        out = _dot(merged.astype(BF16), wo_ref[...])
        y_ref[:, 0, :] = x_ref[:, 0, :] + out * _rms_scale(out) * npost_ref[...]
        for back in range(STATE_OUT_SLOTS):
            for copy in out_copies(i - back):
                copy.wait()


def _run_decode(x, state, ctx, weights):
    n = x.shape[0]
    assert n % DECODE_GROUP == 0 and n == 128 and DECODE_GROUP % STATE_PARTS == 0
    assert n // DECODE_GROUP >= max(STATE_IN_SLOTS, STATE_OUT_SLOTS)
    w_in, npre, npost, lb_logits, hn, cw, wa, wb, wo = weights

    def const(shape):
        zeros = (0,) * len(shape)
        return pl.BlockSpec(shape, lambda i: zeros, pipeline_mode=pl.Buffered(1))

    hbm = pl.BlockSpec(memory_space=pl.ANY)
    in_slots = pltpu.VMEM((STATE_IN_SLOTS, DECODE_GROUP) + state.shape[1:], F32)
    out_slots = pltpu.VMEM((STATE_OUT_SLOTS, DECODE_GROUP) + state.shape[1:], F32)
    rows_f32 = pltpu.VMEM((n, D_MODEL), F32)
    cols_f32 = pltpu.VMEM((D_MODEL, n), F32)
    return pl.pallas_call(
        _decode_kernel,
        grid=(n // DECODE_GROUP,),
        in_specs=[const(x.shape), hbm, const(ctx.shape), hbm, const(npre.shape),
                  const(npost.shape), const(lb_logits.shape), const(hn.shape), const(cw.shape),
                  hbm, hbm, hbm],
        out_specs=[const(x.shape), hbm, const(ctx.shape)],
        out_shape=[jax.ShapeDtypeStruct(x.shape, F32),
                   jax.ShapeDtypeStruct(state.shape, F32),
                   jax.ShapeDtypeStruct(ctx.shape, F32)],
        scratch_shapes=[cols_f32, rows_f32, rows_f32, rows_f32, rows_f32, rows_f32, rows_f32, rows_f32,
                        in_slots, out_slots,
                        pltpu.SemaphoreType.DMA((STATE_IN_SLOTS, STATE_PARTS)),
                        pltpu.SemaphoreType.DMA((STATE_OUT_SLOTS, STATE_PARTS)),
                        pltpu.VMEM(w_in.shape, BF16), pltpu.VMEM(wa.shape, BF16),
                        pltpu.VMEM(wb.shape, BF16), pltpu.VMEM(wo.shape, BF16),
                        pltpu.SemaphoreType.DMA((w_in.shape[1] // D_MODEL + 3,))],
        compiler_params=pltpu.CompilerParams(
            dimension_semantics=("arbitrary",),
            vmem_limit_bytes=V7X_VMEM_LIMIT_BYTES),
        name="decode_step",
    )(x, state, ctx, w_in, npre, npost, lb_logits, hn, cw, wa, wb, wo)


def kernel(x_prompt, x_sample, state_hgrn, state_conv, meta_tokens, w_in, norm_pre, norm_post, lb_logits,
           hgrn_norm, conv_w, w_a, w_b, w_o):
    depth = w_in.shape[0]
    assert depth == 1, "single-layer trunk"
    assert x_sample.shape[1] == 1, "one new token per decode row"

    weights = (w_in[0], norm_pre, norm_post, lb_logits, hgrn_norm, conv_w[0], w_a[0], w_b[0], w_o[0])

    y_prompt, hgrn_p, conv_p, w_in_bf, wa_bf, wb_bf, wo_bf = _run_prompt(
        x_prompt, meta_tokens.astype(x_prompt.dtype), weights, tile=PROMPT_TILE, chunk=PROMPT_CHUNK)
    weights = (w_in_bf,) + weights[1:6] + (wa_bf, wb_bf, wo_bf)

    y_s, hgrn_s, conv_s = _run_decode(x_sample, state_hgrn[0], state_conv[0], weights)

    return (y_prompt, y_s, hgrn_p[None], hgrn_s[None], conv_p[None], conv_s[None])
```
